```python
import math
import jax
import jax.numpy as jnp
from jax import lax
import numpy as np

D_MODEL = 1024
BATCH = 8
SEQ = 4096
DEPTH = 4

GRID_W = 64
CTX_LEN = 256
EPS = 1e-6
NEG_INF = -1e30

NA_HEADS = 8
NA_HEAD_DIM = 64
NA_WIN_ROWS = 8
NA_WIN_COLS = 16
NA_WIDTH = NA_HEADS * NA_HEAD_DIM

ML_HEADS = 4
ML_HEAD_DIM = 128
ML_WIDTH = ML_HEADS * ML_HEAD_DIM
ML_CHUNK = 128
ML_CONV = 3
ML_GATES = 4 * ML_HEADS

EVEN_IN = 3 * NA_WIDTH + 4 * ML_WIDTH + ML_GATES
EVEN_SPLITS = (NA_WIDTH, 2 * NA_WIDTH, 3 * NA_WIDTH,
               3 * NA_WIDTH + 2 * ML_WIDTH, 3 * NA_WIDTH + 3 * ML_WIDTH,
               3 * NA_WIDTH + 4 * ML_WIDTH)
EVEN_OUT = NA_WIDTH + ML_WIDTH

MLA_HEADS = 16
MLA_NOPE = 64
MLA_ROPE = 32
MLA_V = 64
MLA_QK_DIM = MLA_NOPE + MLA_ROPE
MLA_Q_RANK = 256
MLA_KV_RANK = 128
ATTN_BLOCK = 128
ROPE_BASE = 10000.0

FFN_HIDDEN = 2816
FFN_CONV = 3

kernel_name = "hybrid_na_mlstm_mla_dit_block"


def rms_norm(x, g):
    xf = x.astype(jnp.float32)
    y = xf * lax.rsqrt(jnp.mean(xf * xf, axis=-1, keepdims=True) + EPS)
    return (y * g.astype(jnp.float32)).astype(x.dtype)


def modulate(h, shift, scale):
    return h * (1 + scale) + shift


def dwconv_centred(x, w, b):
    k = w.shape[0]
    p = k // 2
    t = x.shape[1]
    xp = jnp.pad(x, ((0, 0), (p, p), (0, 0)))
    y = b + xp[:, 0:t] * w[0]
    for j in range(1, k):
        y = y + xp[:, j:j + t] * w[j]
    return y


def _rotate(x, pos):
    d = x.shape[-1]
    half = d // 2
    inv = ROPE_BASE ** (-jnp.arange(half, dtype=jnp.float32) / half)
    ang = pos.astype(jnp.float32)[:, None] * inv
    ang = ang.reshape((ang.shape[0],) + (1,) * (x.ndim - 3) + (half,))
    cos = jnp.cos(ang).astype(x.dtype)
    sin = jnp.sin(ang).astype(x.dtype)
    x1, x2 = x[..., :half], x[..., half:]
    return jnp.concatenate([x1 * cos - x2 * sin, x1 * sin + x2 * cos], axis=-1)


def rope_2d(x, row, col):
    a = x.shape[-1] // 2
    return jnp.concatenate([_rotate(x[..., :a], row), _rotate(x[..., a:], col)], axis=-1)


def dense_attention(q, k, v):
    s = jnp.einsum('bqhd,bkhd->bhqk', q, k).astype(jnp.float32) * (q.shape[-1] ** -0.5)
    p = jax.nn.softmax(s, axis=-1).astype(v.dtype)
    o = jnp.einsum('bhqk,bkhd->bqhd', p, v)
    return o.reshape(o.shape[0], o.shape[1], -1)


def blocked_attention(q, k, v):
    b, t, h, d = q.shape
    nb = t // ATTN_BLOCK
    qb = jnp.moveaxis(q.reshape(b, nb, ATTN_BLOCK, h, d), 1, 0)
    scale = d ** -0.5

    def one_block(qi):
        s = jnp.einsum('bqhd,bkhd->bhqk', qi, k).astype(jnp.float32) * scale
        p = jax.nn.softmax(s, axis=-1).astype(v.dtype)
        return jnp.einsum('bhqk,bkhd->bqhd', p, v)

    o = lax.map(one_block, qb)
    return jnp.moveaxis(o, 0, 1).reshape(b, t, -1)


def neighbourhood_attention(q, k, v, k_c, v_c, rpb):
    b, t, h, d = q.shape
    rows = t // GRID_W
    kh = min(NA_WIN_ROWS, rows)
    kw = NA_WIN_COLS
    scale = d ** -0.5
    qg = q.reshape(b, rows, GRID_W, h, d)
    kg = k.reshape(b, rows, GRID_W, h, d)
    vg = v.reshape(b, rows, GRID_W, h, d)
    col = jnp.arange(GRID_W)
    cs = jnp.clip(col - kw // 2, 0, GRID_W - kw)
    in_win = (col[None, :] >= cs[:, None]) & (col[None, :] < cs[:, None] + kw)
    dc_idx = jnp.clip(col[None, :] - col[:, None], -(kw - 1), kw - 1) + (NA_WIN_COLS - 1)
    rpb_cols = rpb.astype(jnp.float32)[:, :, dc_idx]

    def row_block(args):
        r, q_row = args
        rs = jnp.clip(r - kh // 2, 0, rows - kh)
        k_band = lax.dynamic_slice_in_dim(kg, rs, kh, axis=1).reshape(b, kh * GRID_W, h, d)
        v_band = lax.dynamic_slice_in_dim(vg, rs, kh, axis=1).reshape(b, kh * GRID_W, h, d)
        dr_idx = rs + jnp.arange(kh) - r + (NA_WIN_ROWS - 1)
        bias = jnp.take(rpb_cols, dr_idx, axis=1)
        bias = jnp.where(in_win[None, None], bias, NEG_INF)
        bias = bias.transpose(0, 2, 1, 3).reshape(h, GRID_W, kh * GRID_W)
        s_lat = jnp.einsum('bqhd,bkhd->bhqk', q_row, k_band).astype(jnp.float32) * scale + bias
        s_ctx = jnp.einsum('bqhd,bkhd->bhqk', q_row, k_c).astype(jnp.float32) * scale
        p = jax.nn.softmax(jnp.concatenate([s_lat, s_ctx], axis=-1), axis=-1).astype(v.dtype)
        nl = kh * GRID_W
        return (jnp.einsum('bhqk,bkhd->bqhd', p[..., :nl], v_band)
                + jnp.einsum('bhqk,bkhd->bqhd', p[..., nl:], v_c))

    o = lax.map(row_block, (jnp.arange(rows), jnp.moveaxis(qg, 1, 0)))
    return jnp.moveaxis(o, 0, 1).reshape(b, t, h * d)


def mlstm_chunked(q, k, v, i_pre, logf, state):
    b, h, t, d = q.shape
    nc = t // ML_CHUNK

    def to_chunks(a):
        return jnp.moveaxis(a.reshape(a.shape[:2] + (nc, ML_CHUNK) + a.shape[3:]), 2, 0)

    xs = tuple(to_chunks(a) for a in (q, k, v, i_pre, logf))
    lower = jnp.tril(jnp.ones((ML_CHUNK, ML_CHUNK), dtype=bool))

    def step(carry, chunk):
        c_st, n_st, m_st = carry
        qc, kc, vc, ic, fc = chunk
        bcum = jnp.cumsum(fc, axis=-1)
        d_in = jnp.where(lower, bcum[..., :, None] - bcum[..., None, :] + ic[..., None, :], -jnp.inf)
        m_inter = bcum + m_st[..., None]
        m_t = jnp.maximum(m_inter, jnp.max(d_in, axis=-1))
        w = jnp.exp(d_in - m_t[..., None])
        a = jnp.exp(m_inter - m_t)
        qk = jnp.einsum('bhtd,bhsd->bhts', qc, kc) * w
        num = (a[..., None] * jnp.einsum('bhvd,bhtd->bhtv', c_st, qc)
               + jnp.einsum('bhts,bhsv->bhtv', qk, vc))
        den = a * jnp.einsum('bhd,bhtd->bht', n_st, qc) + jnp.sum(qk, axis=-1)
        h_out = num / jnp.maximum(jnp.abs(den), jnp.exp(-m_t))[..., None]
        b_last = bcum[..., -1]
        g = b_last[..., None] - bcum + ic
        m_new = jnp.maximum(b_last + m_st, jnp.max(g, axis=-1))
        decay = jnp.exp(b_last + m_st - m_new)
        wk = jnp.exp(g - m_new[..., None])
        c_new = decay[..., None, None] * c_st + jnp.einsum('bhs,bhsv,bhsd->bhvd', wk, vc, kc)
        n_new = decay[..., None] * n_st + jnp.einsum('bhs,bhsd->bhd', wk, kc)
        return (c_new, n_new, m_new), h_out

    state, hs = lax.scan(step, state, xs)
    return jnp.moveaxis(hs, 0, 2).reshape(b, h, t, d), state


def _ml_heads(a):
    b, t, _ = a.shape
    return a.reshape(b, t, ML_HEADS, ML_HEAD_DIM).transpose(0, 2, 1, 3).astype(jnp.float32)


def _rev(a):
    return jnp.flip(a, axis=2)


def _mlstm_inputs(qk, v, g, conv_w, conv_b):
    qk = jax.nn.silu(dwconv_centred(qk, conv_w, conv_b))
    q, k = jnp.split(qk, 2, axis=-1)
    q = _ml_heads(q) * (ML_HEAD_DIM ** -0.5)
    g = g.astype(jnp.float32).transpose(0, 2, 1)
    i_f, f_f, i_b, f_b = jnp.split(g, 4, axis=1)
    return (_ml_heads(q.transpose(0, 2, 1, 3).reshape(q.shape[0], q.shape[2], ML_WIDTH)),
            _ml_heads(k), _ml_heads(v),
            i_f, jax.nn.log_sigmoid(f_f), i_b, jax.nn.log_sigmoid(f_b))


def _mlstm_output(h, o_pre, g):
    b, nh, t, d = h.shape
    o = jax.nn.sigmoid(o_pre.astype(jnp.float32)).reshape(b, t, nh, d).transpose(0, 2, 1, 3)
    h = o * h
    mu = jnp.mean(h, axis=-1, keepdims=True)
    var = jnp.mean(jnp.square(h - mu), axis=-1, keepdims=True)
    h = (h - mu) * lax.rsqrt(var + EPS)
    h = h.transpose(0, 2, 1, 3).reshape(b, t, nh * d) * g.astype(jnp.float32)
    return h.astype(o_pre.dtype)


def _zero_state(b):
    return (jnp.zeros((b, ML_HEADS, ML_HEAD_DIM, ML_HEAD_DIM), jnp.float32),
            jnp.zeros((b, ML_HEADS, ML_HEAD_DIM), jnp.float32),
            jnp.zeros((b, ML_HEADS), jnp.float32))


def even_mixer(hx, hc, w_in, gate_b, conv_w, conv_b, rpb, ml_norm_g, w_out, ctx_out):
    naq_x, nak_x, nav_x, mlqk_x, mlv_x, mlo_x, mlg_x = jnp.split(hx @ w_in, list(EVEN_SPLITS), axis=-1)
    naq_c, nak_c, nav_c, mlqk_c, mlv_c, mlo_c, mlg_c = jnp.split(hc @ w_in, list(EVEN_SPLITS), axis=-1)

    def heads(a):
        return a.reshape(a.shape[0], a.shape[1], NA_HEADS, NA_HEAD_DIM)

    na_x = neighbourhood_attention(heads(naq_x), heads(nak_x), heads(nav_x),
                                   heads(nak_c), heads(nav_c), rpb)

    q_x, k_x, v_x, if_x, lf_x, ib_x, lb_x = _mlstm_inputs(mlqk_x, mlv_x, mlg_x + gate_b, conv_w, conv_b)
    q_c, k_c, v_c, if_c, lf_c, ib_c, lb_c = _mlstm_inputs(mlqk_c, mlv_c, mlg_c + gate_b, conv_w, conv_b)
    zero = _zero_state(hx.shape[0])
    h_cf, st_f = mlstm_chunked(q_c, k_c, v_c, if_c, lf_c, zero)
    h_xf, _ = mlstm_chunked(q_x, k_x, v_x, if_x, lf_x, st_f)
    h_cb, st_b = mlstm_chunked(_rev(q_c), _rev(k_c), _rev(v_c), _rev(ib_c), _rev(lb_c), zero)
    h_xb, _ = mlstm_chunked(_rev(q_x), _rev(k_x), _rev(v_x), _rev(ib_x), _rev(lb_x), st_b)
    ml_x = _mlstm_output(h_xf + _rev(h_xb), mlo_x, ml_norm_g)
    y_x = jnp.concatenate([na_x, ml_x], axis=-1) @ w_out
    if not ctx_out:
        return y_x, None
    na_c = dense_attention(heads(naq_c), heads(nak_c), heads(nav_c))
    ml_c = _mlstm_output(h_cf + _rev(h_cb), mlo_c, ml_norm_g)
    y_c = jnp.concatenate([na_c, ml_c], axis=-1) @ w_out
    return y_x, y_c


def _mla_q(h, w_dq, q_norm_g, w_uq, row, col):
    b, t, _ = h.shape
    q = (rms_norm(h @ w_dq, q_norm_g) @ w_uq).reshape(b, t, MLA_HEADS, MLA_QK_DIM)
    if row is None:
        return q
    return jnp.concatenate([q[..., :MLA_NOPE], rope_2d(q[..., MLA_NOPE:], row, col)], axis=-1)


def _mla_kv(h, w_dkv, kv_norm_g, w_ukv, row, col):
    b, t, _ = h.shape
    ckv = h @ w_dkv
    c_kv, k_pe = ckv[..., :MLA_KV_RANK], ckv[..., MLA_KV_RANK:]
    if row is not None:
        k_pe = rope_2d(k_pe, row, col)
    kv = (rms_norm(c_kv, kv_norm_g) @ w_ukv).reshape(b, t, MLA_HEADS, MLA_NOPE + MLA_V)
    k = jnp.concatenate([kv[..., :MLA_NOPE],
                         jnp.broadcast_to(k_pe[:, :, None, :], (b, t, MLA_HEADS, MLA_ROPE))], axis=-1)
    return k, kv[..., MLA_NOPE:]


def odd_mixer(hx, hc, w_dq, q_norm_g, w_uq, w_dkv, kv_norm_g, w_ukv, w_o, row, col, ctx_out):
    q_x = _mla_q(hx, w_dq, q_norm_g, w_uq, row, col)
    k_x, v_x = _mla_kv(hx, w_dkv, kv_norm_g, w_ukv, row, col)
    k_c, v_c = _mla_kv(hc, w_dkv, kv_norm_g, w_ukv, None, None)
    y_x = blocked_attention(q_x, jnp.concatenate([k_x, k_c], axis=1),
                            jnp.concatenate([v_x, v_c], axis=1)) @ w_o
    if not ctx_out:
        return y_x, None
    q_c = _mla_q(hc, w_dq, q_norm_g, w_uq, None, None)
    return y_x, dense_attention(q_c, k_c, v_c) @ w_o


def conv_ffn(h, w_up, conv_w, conv_b, w_down):
    u = dwconv_centred(h @ w_up, conv_w, conv_b)
    a, g = jnp.split(u, 2, axis=-1)
    return (a * jax.nn.silu(g)) @ w_down


def setup_inputs(seed: int = 0) -> dict:
    key = jax.random.key(seed)
    ks = jax.random.split(key, 32)
    n_even = (DEPTH + 1) // 2
    n_odd = DEPTH // 2
    f32 = jnp.float32

    def nrm(k, shape, scale):
        return jax.random.normal(k, shape, f32) * scale

    gk = jax.random.split(ks[15], 4)
    forget_b = jnp.linspace(3.0, 6.0, ML_HEADS, dtype=f32)
    gate_b = jnp.concatenate([
        nrm(gk[0], (n_even, ML_HEADS), 0.1),
        forget_b + nrm(gk[1], (n_even, ML_HEADS), 0.1),
        nrm(gk[2], (n_even, ML_HEADS), 0.1),
        forget_b + nrm(gk[3], (n_even, ML_HEADS), 0.1)], axis=-1)
    return {
        "x": nrm(ks[0], (BATCH, SEQ, D_MODEL), 1.0),
        "c": nrm(ks[1], (BATCH, D_MODEL), 1.0),
        "ctx": nrm(ks[2], (BATCH, CTX_LEN, D_MODEL), 1.0),
        "c_ctx": nrm(ks[3], (D_MODEL,), 1.0),
        "ada_w": nrm(ks[4], (DEPTH, D_MODEL, 6 * D_MODEL), D_MODEL ** -0.5),
        "ada_b": nrm(ks[5], (DEPTH, 6 * D_MODEL), 0.02),
        "norm_g": 1.0 + nrm(ks[6], (DEPTH, 4, D_MODEL), 0.05),
        "ffn_w_up": nrm(ks[7], (DEPTH, D_MODEL, 2 * FFN_HIDDEN), D_MODEL ** -0.5),
        "ffn_conv_w": nrm(ks[8], (DEPTH, FFN_CONV, 2 * FFN_HIDDEN), FFN_CONV ** -0.5),
        "ffn_conv_b": nrm(ks[9], (DEPTH, 2 * FFN_HIDDEN), 0.02),
        "ffn_w_down": nrm(ks[10], (DEPTH, FFN_HIDDEN, D_MODEL), FFN_HIDDEN ** -0.5),
        "ev_w_in": nrm(ks[11], (n_even, D_MODEL, EVEN_IN), D_MODEL ** -0.5),
        "ev_gate_b": gate_b,
        "ev_conv_w": nrm(ks[12], (n_even, ML_CONV, 2 * ML_WIDTH), ML_CONV ** -0.5),
        "ev_conv_b": nrm(ks[13], (n_even, 2 * ML_WIDTH), 0.02),
        "ev_rpb": nrm(ks[14], (n_even, NA_HEADS, 2 * NA_WIN_ROWS - 1, 2 * NA_WIN_COLS - 1), 0.1),
        "ev_ml_norm_g": 1.0 + nrm(ks[16], (n_even, ML_WIDTH), 0.05),
        "ev_w_out": nrm(ks[17], (n_even, EVEN_OUT, D_MODEL), EVEN_OUT ** -0.5),
        "od_w_dq": nrm(ks[18], (n_odd, D_MODEL, MLA_Q_RANK), D_MODEL ** -0.5),
        "od_q_norm_g": 1.0 + nrm(ks[19], (n_odd, MLA_Q_RANK), 0.05),
        "od_w_uq": nrm(ks[20], (n_odd, MLA_Q_RANK, MLA_HEADS * MLA_QK_DIM), MLA_Q_RANK ** -0.5),
        "od_w_dkv": nrm(ks[21], (n_odd, D_MODEL, MLA_KV_RANK + MLA_ROPE), D_MODEL ** -0.5),
        "od_kv_norm_g": 1.0 + nrm(ks[22], (n_odd, MLA_KV_RANK), 0.05),
        "od_w_ukv": nrm(ks[23], (n_odd, MLA_KV_RANK, MLA_HEADS * (MLA_NOPE + MLA_V)), MLA_KV_RANK ** -0.5),
        "od_w_o": nrm(ks[24], (n_odd, MLA_HEADS * MLA_V, D_MODEL), (MLA_HEADS * MLA_V) ** -0.5),
    }


def reference(x, c, ctx, c_ctx, ada_w, ada_b, norm_g, ffn_w_up, ffn_conv_w, ffn_conv_b, ffn_w_down,
              ev_w_in, ev_gate_b, ev_conv_w, ev_conv_b, ev_rpb, ev_ml_norm_g, ev_w_out,
              od_w_dq, od_q_norm_g, od_w_uq, od_w_dkv, od_kv_norm_g, od_w_ukv, od_w_o):
    t = x.shape[1]
    pos = jnp.arange(t)
    row = pos // GRID_W
    col = pos % GRID_W
    silu_c = jax.nn.silu(c)
    silu_cc = jax.nn.silu(c_ctx)
    xc = ctx
    for l in range(DEPTH):
        ctx_out = l < DEPTH - 1
        mod_x = (silu_c @ ada_w[l] + ada_b[l])[:, None, :]
        mod_c = silu_cc @ ada_w[l] + ada_b[l]
        sh1, sc1, g1, sh2, sc2, g2 = jnp.split(mod_x, 6, axis=-1)
        csh1, csc1, cg1, csh2, csc2, cg2 = jnp.split(mod_c, 6, axis=-1)
        hx = modulate(rms_norm(x, norm_g[l, 0]), sh1, sc1)
        hc = modulate(rms_norm(xc, norm_g[l, 0]), csh1, csc1)
        if l % 2 == 0:
            e = l // 2
            yx, yc = even_mixer(hx, hc, ev_w_in[e], ev_gate_b[e], ev_conv_w[e], ev_conv_b[e],
                                ev_rpb[e], ev_ml_norm_g[e], ev_w_out[e], ctx_out)
        else:
            o = l // 2
            yx, yc = odd_mixer(hx, hc, od_w_dq[o], od_q_norm_g[o], od_w_uq[o], od_w_dkv[o],
                               od_kv_norm_g[o], od_w_ukv[o], od_w_o[o], row, col, ctx_out)
        x = x + g1 * rms_norm(yx, norm_g[l, 1])
        hx = modulate(rms_norm(x, norm_g[l, 2]), sh2, sc2)
        x = x + g2 * rms_norm(conv_ffn(hx, ffn_w_up[l], ffn_conv_w[l], ffn_conv_b[l], ffn_w_down[l]),
                              norm_g[l, 3])
        if ctx_out:
            xc = xc + cg1 * rms_norm(yc, norm_g[l, 1])
            hc = modulate(rms_norm(xc, norm_g[l, 2]), csh2, csc2)
            xc = xc + cg2 * rms_norm(conv_ffn(hc, ffn_w_up[l], ffn_conv_w[l], ffn_conv_b[l], ffn_w_down[l]),
                                     norm_g[l, 3])
    return x
```

```python
import functools
import math

import jax
import jax.numpy as jnp
import numpy as np
from jax import lax
from jax.experimental import pallas as pl
from jax.experimental.pallas import tpu as pltpu

F32 = jnp.float32
BF16 = jnp.bfloat16

EPS = 1e-6
NEG = -1e30

GRID_W = 64
NA_HEADS = 8
NA_HD = 64
NA_WIN_ROWS = 8
NA_WIN_COLS = 16
NA_WIDTH = NA_HEADS * NA_HD

ML_HEADS = 4
ML_HD = 128
ML_WIDTH = ML_HEADS * ML_HD
ML_CHUNK = 128
ML_GATES = 4 * ML_HEADS

MLA_HEADS = 16
MLA_NOPE = 64
MLA_ROPE = 32
MLA_V = 64
MLA_QK = MLA_NOPE + MLA_ROPE
MLA_Q_RANK = 256
MLA_KV_RANK = 128
ROPE_BASE = 10000.0

FFN_HIDDEN = 2816

LANES = 128
SUBLANES_BF16 = 16
TM = 256
VMEM_LIMIT = 48 * 1024 * 1024


def _cparams(sem):
    return pltpu.CompilerParams(dimension_semantics=sem, vmem_limit_bytes=VMEM_LIMIT)


def _resident(shape):
    nd = len(shape)
    return pl.BlockSpec(shape, lambda *_: (0,) * nd, pipeline_mode=pl.Buffered(1))


def _rms(xf, g):
    ms = jnp.mean(xf * xf, axis=-1, keepdims=True)
    return xf * lax.rsqrt(ms + EPS) * g


def _dot(a, b):
    return jnp.dot(a, b, preferred_element_type=F32)


def _dot_nt(a, b):
    return lax.dot_general(a, b, (((1,), (1,)), ((), ())), preferred_element_type=F32)


def _ada_kernel(c_ref, w_ref, b_ref, o_ref):
    c = c_ref[...]
    s = c * jax.nn.sigmoid(c)
    o_ref[0] = _dot(s.astype(BF16), w_ref[0].astype(BF16)) + b_ref[0]


def _ada_call(cvec, ada_w, ada_b):
    depth, d, n = ada_w.shape
    rows = cvec.shape[0]
    tn = 1024
    return pl.pallas_call(
        _ada_kernel,
        out_shape=jax.ShapeDtypeStruct((depth, rows, n), F32),
        grid=(depth, n // tn),
        in_specs=[
            pl.BlockSpec((rows, d), lambda l, j: (0, 0)),
            pl.BlockSpec((1, d, tn), lambda l, j: (l, 0, j)),
            pl.BlockSpec((1, 1, tn), lambda l, j: (l, 0, j)),
        ],
        out_specs=pl.BlockSpec((1, rows, tn), lambda l, j: (l, 0, j)),
        compiler_params=_cparams(("parallel", "parallel")),
        name="ada_mod",
    )(cvec, ada_w, ada_b.reshape(depth, 1, n))


def _x_spec(d):
    return pl.BlockSpec((1, TM, d), lambda b, i: (b, i, 0))


def _mod_spec(d6, nctx_t):
    return pl.BlockSpec((1, 1, 1, d6), lambda b, i: (b, jnp.where(i >= nctx_t, 1, 0), 0, 0))


def _modulated(x, m, g, d, which):
    o = 3 * d * which
    return _rms(x, g) * (1.0 + m[:, o + d:o + 2 * d]) + m[:, o:o + d]


def _even_in_kernel(x_ref, mod_ref, ng_ref, w_ref, wg_ref, gb_ref,
                    naq_ref, nak_ref, nav_ref, mlqk_ref, mlv_ref, mlo_ref, g_ref, *, d):
    m = mod_ref[0, 0]
    hb = _modulated(x_ref[0], m, ng_ref[...], d, 0).astype(BF16)

    def seg(lo, hi):
        return _dot(hb, w_ref[:, lo:hi])

    w = NA_WIDTH
    naq_ref[0] = (seg(0, w) * (NA_HD ** -0.5)).astype(BF16)
    nak_ref[0] = seg(w, 2 * w).astype(BF16)
    nav_ref[0] = seg(2 * w, 3 * w).astype(BF16)
    o = 3 * w
    mlqk_ref[0, :, 0:ML_WIDTH] = seg(o, o + ML_WIDTH)
    mlqk_ref[0, :, ML_WIDTH:2 * ML_WIDTH] = seg(o + ML_WIDTH, o + 2 * ML_WIDTH)
    mlv_ref[0] = seg(o + 2 * ML_WIDTH, o + 3 * ML_WIDTH).astype(BF16)
    mlo_ref[0] = seg(o + 3 * ML_WIDTH, o + 4 * ML_WIDTH)
    g_ref[0] = _dot(hb, wg_ref[...]) + gb_ref[...]


def _even_in_call(x_all, modsel, ng, w_main, w_gate, gate_b, nctx_t):
    b, ttot, d = x_all.shape
    nt = ttot // TM
    n_main = w_main.shape[1]
    row = lambda c: pl.BlockSpec((1, TM, c), lambda bb, i: (bb, i, 0))
    sds = lambda c, dt: jax.ShapeDtypeStruct((b, ttot, c), dt)
    return pl.pallas_call(
        functools.partial(_even_in_kernel, d=d),
        out_shape=(sds(NA_WIDTH, BF16), sds(NA_WIDTH, BF16), sds(NA_WIDTH, BF16),
                   sds(2 * ML_WIDTH, F32), sds(ML_WIDTH, BF16), sds(ML_WIDTH, F32),
                   sds(LANES, F32)),
        grid=(b, nt),
        in_specs=[_x_spec(d), _mod_spec(6 * d, nctx_t), _resident((1, d)),
                  _resident((d, n_main)), _resident((d, LANES)), _resident((1, LANES))],
        out_specs=(row(NA_WIDTH), row(NA_WIDTH), row(NA_WIDTH), row(2 * ML_WIDTH),
                   row(ML_WIDTH), row(ML_WIDTH), row(LANES)),
        compiler_params=_cparams(("parallel", "parallel")),
        name="even_in_proj",
    )(x_all, modsel, ng, w_main, w_gate, gate_b)


def _pair_attention(qp, k_parts, v_parts, bias_parts):
    mq = qp.shape[0]
    lane = lax.broadcasted_iota(jnp.int32, (mq, LANES), 1)
    outs = []
    for hh in range(2):
        keep = (lane >= NA_HD) if hh else (lane < NA_HD)
        qh = jnp.where(keep, qp, jnp.zeros_like(qp))
        s = []
        for kk, bias in zip(k_parts, bias_parts):
            sp = _dot_nt(qh, kk)
            if bias is not None:
                sp = sp + bias[hh]
            s.append(sp)
        m = s[0].max(axis=-1, keepdims=True)
        for sp in s[1:]:
            m = jnp.maximum(m, sp.max(axis=-1, keepdims=True))
        l = jnp.zeros_like(m)
        o = jnp.zeros((mq, LANES), F32)
        for sp, vv in zip(s, v_parts):
            p = jnp.exp(sp - m)
            l = l + p.sum(axis=-1, keepdims=True)
            o = o + _dot(p.astype(BF16), vv)
        outs.append(o / l)
    return jnp.where(lane < NA_HD, outs[0], outs[1])


def _na_kernel(q_ref, k_ref, v_ref, bias_ref, o_ref, *, ctx, rows):
    r = pl.program_id(1)
    rs = jnp.clip(r - NA_WIN_ROWS // 2, 0, rows - NA_WIN_ROWS)
    start = pl.multiple_of(ctx + rs * GRID_W, GRID_W)
    band = NA_WIN_ROWS * GRID_W
    for p in range(NA_HEADS // 2):
        cs = slice(p * LANES, (p + 1) * LANES)
        kb = k_ref[0, pl.ds(start, band), cs]
        vb = v_ref[0, pl.ds(start, band), cs]
        kc = k_ref[0, 0:ctx, cs]
        vc = v_ref[0, 0:ctx, cs]
        bias = (bias_ref[0, 2 * p], bias_ref[0, 2 * p + 1])
        o = _pair_attention(q_ref[0, :, cs], (kb, kc), (vb, vc), (bias, None))
        o_ref[0, :, cs] = o.astype(BF16)


def _na_call(naq, nak, nav, bias_tab, ctx, t):
    b, ttot, w = naq.shape
    rows = t // GRID_W
    assert rows >= NA_WIN_ROWS and ctx % GRID_W == 0
    q_off = ctx // GRID_W

    def delta(bb, r):
        rs = jnp.clip(r - NA_WIN_ROWS // 2, 0, rows - NA_WIN_ROWS)
        return (rs - r + NA_WIN_ROWS - 1, 0, 0, 0)

    return pl.pallas_call(
        functools.partial(_na_kernel, ctx=ctx, rows=rows),
        out_shape=jax.ShapeDtypeStruct((b, t, w), BF16),
        grid=(b, rows),
        in_specs=[
            pl.BlockSpec((1, GRID_W, w), lambda bb, r: (bb, q_off + r, 0)),
            pl.BlockSpec((1, ttot, w), lambda bb, r: (bb, 0, 0)),
            pl.BlockSpec((1, ttot, w), lambda bb, r: (bb, 0, 0)),
            pl.BlockSpec((1, NA_HEADS, GRID_W, NA_WIN_ROWS * GRID_W), delta),
        ],
        out_specs=pl.BlockSpec((1, GRID_W, w), lambda bb, r: (bb, r, 0)),
        compiler_params=_cparams(("parallel", "arbitrary")),
        name="na_attention",
    )(naq, nak, nav, bias_tab)


def _na_ctx_kernel(q_ref, k_ref, v_ref, o_ref):
    for p in range(NA_HEADS // 2):
        cs = slice(p * LANES, (p + 1) * LANES)
        o = _pair_attention(q_ref[0, :, cs], (k_ref[0, :, cs],), (v_ref[0, :, cs],), (None,))
        o_ref[0, :, cs] = o.astype(BF16)


def _na_ctx_call(naq, nak, nav, ctx):
    b, _, w = naq.shape
    spec = pl.BlockSpec((1, ctx, w), lambda bb: (bb, 0, 0))
    return pl.pallas_call(
        _na_ctx_kernel,
        out_shape=jax.ShapeDtypeStruct((b, ctx, w), BF16),
        grid=(b,),
        in_specs=[spec, spec, spec],
        out_specs=spec,
        compiler_params=_cparams(("parallel",)),
        name="na_ctx_attention",
    )(naq, nak, nav)


def _na_bias_table(rpb):
    kw = NA_WIN_COLS
    col = jnp.arange(GRID_W)
    cs = jnp.clip(col - kw // 2, 0, GRID_W - kw)
    in_win = (col[None, :] >= cs[:, None]) & (col[None, :] < cs[:, None] + kw)
    dc_idx = jnp.clip(col[None, :] - col[:, None], -(kw - 1), kw - 1) + (NA_WIN_COLS - 1)
    rpb_cols = rpb.astype(F32)[:, :, dc_idx]
    rpb_cols = jnp.where(in_win[None, None], rpb_cols, NEG)
    dr = jnp.arange(NA_WIN_ROWS)[:, None] + jnp.arange(NA_WIN_ROWS)[None, :]
    tab = rpb_cols[:, dr]
    tab = tab.transpose(1, 0, 3, 2, 4)
    return tab.reshape(NA_WIN_ROWS, NA_HEADS, GRID_W, NA_WIN_ROWS * GRID_W)


def _halo_specs(c, rows_h, ttot):
    per = TM // rows_h
    last = ttot // rows_h - 1
    prev = pl.BlockSpec((1, rows_h, c), lambda b, i: (b, jnp.maximum(i * per - 1, 0), 0))
    nxt = pl.BlockSpec((1, rows_h, c), lambda b, i: (b, jnp.minimum((i + 1) * per, last), 0))
    return prev, nxt


def _halo_valid(nctx_t, nt):
    i = pl.program_id(1)
    pv = jnp.where((i == 0) | (i == nctx_t), 0.0, 1.0).astype(F32)
    nv = jnp.where((i == nctx_t - 1) | (i == nt - 1), 0.0, 1.0).astype(F32)
    return pv, nv


def _conv3(u_ref, h, cw, cb, lo, hi):
    return (cb
            + u_ref[h - 1:h - 1 + TM, lo:hi] * cw[0:1]
            + u_ref[h:h + TM, lo:hi] * cw[1:2]
            + u_ref[h + 1:h + 1 + TM, lo:hi] * cw[2:3])


def _ml_conv_kernel(x_ref, xp_ref, xn_ref, cw_ref, cb_ref, sc_ref, o_ref, u_ref, *, nctx_t, nt):
    pv, nv = _halo_valid(nctx_t, nt)
    h = 8
    u_ref[0:h] = xp_ref[0] * pv
    u_ref[h:h + TM] = x_ref[0]
    u_ref[h + TM:h + TM + h] = xn_ref[0] * nv
    c = x_ref.shape[-1]
    y = _conv3(u_ref, h, cw_ref[...], cb_ref[...], 0, c)
    y = y * jax.nn.sigmoid(y) * sc_ref[...]
    o_ref[0] = y.astype(BF16)


def _ml_conv_call(mlqk, conv_w, conv_b, scale, nctx_t):
    b, ttot, c = mlqk.shape
    nt = ttot // TM
    prev, nxt = _halo_specs(c, 8, ttot)
    return pl.pallas_call(
        functools.partial(_ml_conv_kernel, nctx_t=nctx_t, nt=nt),
        out_shape=jax.ShapeDtypeStruct((b, ttot, c), BF16),
        grid=(b, nt),
        in_specs=[_x_spec(c), prev, nxt, _resident((3, c)), _resident((1, c)), _resident((1, c))],
        out_specs=_x_spec(c),
        scratch_shapes=[pltpu.VMEM((TM + 16, c), F32)],
        compiler_params=_cparams(("parallel", "parallel")),
        name="mlstm_conv",
    )(mlqk, mlqk, mlqk, conv_w, conv_b, scale)


def _log_sigmoid(x):
    return jnp.minimum(x, 0.0) - jnp.log(1.0 + jnp.exp(-jnp.abs(x)))


def _split_dot(a, b, data_is_rhs):
    data = b if data_is_rhs else a
    hi = data.astype(BF16)
    lo = (data - hi.astype(F32)).astype(BF16)
    if data_is_rhs:
        return _dot(a, hi) + _dot(a, lo)
    return _dot(hi, b) + _dot(lo, b)


def _mlstm_chain(q, k, v_aug, cum_col, cum_row, i_col, i_row, total, mask, ct_ref, m_ref, j):
    m_st = m_ref[j][0:1, 0:1]
    d_in = jnp.where(mask, cum_col - cum_row + i_row, NEG)
    m_inter = cum_col + m_st
    m_t = jnp.maximum(m_inter, d_in.max(axis=-1, keepdims=True))
    w = jnp.exp(d_in - m_t)
    a = jnp.exp(m_inter - m_t)
    qk = (_dot_nt(q, k) * w).astype(BF16)
    ct = ct_ref[j]
    hfull = a * _dot(q, ct.astype(BF16)) + _dot(qk, v_aug)
    num = hfull[:, 0:ML_HD]
    den = hfull[:, ML_HD:ML_HD + 1]
    h_out = num / jnp.maximum(jnp.abs(den), jnp.exp(-m_t))
    g = total - cum_col + i_col
    m_new = jnp.maximum(total + m_st, g.max(axis=0, keepdims=True))
    decay = jnp.exp(total + m_st - m_new)
    wk = jnp.exp(g - m_new)
    kw = (k.astype(F32) * wk).T.astype(BF16)
    ct_ref[j] = decay * ct + _dot(kw, v_aug)
    m_ref[j] = jnp.broadcast_to(m_new, m_ref.shape[1:])
    return h_out


def _mlstm_kernel(qkf_ref, vf_ref, gf_ref, gtf_ref, qkb_ref, vb_ref, gb_ref, gtb_ref,
                  hf_ref, hb_ref, ct_ref, m_ref):
    L = ML_CHUNK

    @pl.when(pl.program_id(1) == 0)
    def _():
        ct_ref[...] = jnp.zeros_like(ct_ref)
        m_ref[...] = jnp.zeros_like(m_ref)

    ri = lax.broadcasted_iota(jnp.int32, (L, L), 0)
    ci = lax.broadcasted_iota(jnp.int32, (L, L), 1)
    tril = jnp.where(ri >= ci, 1.0, 0.0).astype(BF16)
    triu = jnp.where(ri <= ci, 1.0, 0.0).astype(BF16)
    lane = lax.broadcasted_iota(jnp.int32, (L, LANES), 1)
    ones_col = jnp.where(lane == 0, 1.0, 0.0).astype(BF16)

    dirs = (
        (qkf_ref, vf_ref, gf_ref, gtf_ref, hf_ref, tril, triu, ri >= ci, L - 1, 0),
        (qkb_ref, vb_ref, gb_ref, gtb_ref, hb_ref, triu, tril, ri <= ci, 0, 2 * ML_HEADS),
    )
    for dnum, (qk_ref, v_ref, g_ref, gt_ref, h_ref, tcol, trow, mask, tot_row, goff) in enumerate(dirs):
        g = g_ref[0]
        gt = gt_ref[0]
        cum_col_all = _split_dot(tcol, _log_sigmoid(g), True)
        cum_row_all = _split_dot(_log_sigmoid(gt), trow, False)
        for hd in range(ML_HEADS):
            ic = goff + hd
            fc = goff + ML_HEADS + hd
            cs = slice(hd * ML_HD, (hd + 1) * ML_HD)
            q = qk_ref[0, :, cs]
            k = qk_ref[0, :, ML_WIDTH + hd * ML_HD:ML_WIDTH + (hd + 1) * ML_HD]
            v_aug = jnp.concatenate([v_ref[0, :, cs], ones_col], axis=-1)
            cum_col = cum_col_all[:, fc:fc + 1]
            h = _mlstm_chain(
                q, k, v_aug, cum_col, cum_row_all[fc:fc + 1, :],
                g[:, ic:ic + 1], gt[ic:ic + 1, :], cum_col[tot_row:tot_row + 1, :],
                mask, ct_ref, m_ref, dnum * ML_HEADS + hd)
            h_ref[0, :, cs] = h


def _mlstm_call(qk, v, gates, gates_t, nctx_c):
    b, ttot, _ = qk.shape
    nc = ttot // ML_CHUNK
    L = ML_CHUNK

    def fwd(bb, c):
        return (bb, c, 0)

    def bwd(bb, c):
        return (bb, jnp.where(c < nctx_c, nctx_c - 1 - c, nc - 1 + nctx_c - c), 0)

    def fwd_t(bb, c):
        return (bb, 0, c)

    def bwd_t(bb, c):
        return (bb, 0, jnp.where(c < nctx_c, nctx_c - 1 - c, nc - 1 + nctx_c - c))

    ins = []
    for row_map, col_map in ((fwd, fwd_t), (bwd, bwd_t)):
        ins += [pl.BlockSpec((1, L, 2 * ML_WIDTH), row_map),
                pl.BlockSpec((1, L, ML_WIDTH), row_map),
                pl.BlockSpec((1, L, LANES), row_map),
                pl.BlockSpec((1, ML_GATES, L), col_map)]
    out_sds = jax.ShapeDtypeStruct((b, ttot, ML_WIDTH), F32)
    return pl.pallas_call(
        _mlstm_kernel,
        out_shape=(out_sds, out_sds),
        grid=(b, nc),
        in_specs=ins,
        out_specs=(pl.BlockSpec((1, L, ML_WIDTH), fwd), pl.BlockSpec((1, L, ML_WIDTH), bwd)),
        scratch_shapes=[pltpu.VMEM((2 * ML_HEADS, ML_HD, 2 * ML_HD), F32),
                        pltpu.VMEM((2 * ML_HEADS, 8, LANES), F32)],
        compiler_params=_cparams(("parallel", "arbitrary")),
        name="mlstm_scan",
    )(qk, v, gates, gates_t, qk, v, gates, gates_t)


def _out_tail(x_ref, mod_ref, ng_ref, y, o_ref, d):
    m = mod_ref[0, 0]
    o_ref[0] = x_ref[0] + m[:, 2 * d:3 * d] * _rms(y, ng_ref[...])


def _even_out_kernel(x_ref, mod_ref, ng_ref, na_ref, hf_ref, hb_ref, op_ref, mlg_ref, w_ref,
                     o_ref, *, d):
    hs = jax.nn.sigmoid(op_ref[0]) * (hf_ref[0] + hb_ref[0])
    parts = []
    for hd in range(ML_HEADS):
        seg = hs[:, hd * ML_HD:(hd + 1) * ML_HD]
        mu = jnp.mean(seg, axis=-1, keepdims=True)
        cen = seg - mu
        var = jnp.mean(cen * cen, axis=-1, keepdims=True)
        parts.append(cen * lax.rsqrt(var + EPS))
    ml = (jnp.concatenate(parts, axis=-1) * mlg_ref[...]).astype(BF16)
    y = _dot(na_ref[0], w_ref[0:NA_WIDTH, :]) + _dot(ml, w_ref[NA_WIDTH:NA_WIDTH + ML_WIDTH, :])
    _out_tail(x_ref, mod_ref, ng_ref, y, o_ref, d)


def _even_out_call(x_all, modsel, ng, na, hf, hb, mlo, mlg, w_out, nctx_t):
    b, ttot, d = x_all.shape
    row = lambda c: pl.BlockSpec((1, TM, c), lambda bb, i: (bb, i, 0))
    return pl.pallas_call(
        functools.partial(_even_out_kernel, d=d),
        out_shape=jax.ShapeDtypeStruct((b, ttot, d), F32),
        grid=(b, ttot // TM),
        in_specs=[_x_spec(d), _mod_spec(6 * d, nctx_t), _resident((1, d)),
                  row(NA_WIDTH), row(ML_WIDTH), row(ML_WIDTH), row(ML_WIDTH),
                  _resident((1, ML_WIDTH)), _resident(w_out.shape)],
        out_specs=_x_spec(d),
        compiler_params=_cparams(("parallel", "parallel")),
        name="even_out_proj",
    )(x_all, modsel, ng, na, hf, hb, mlo, mlg, w_out)


def _odd_out_kernel(x_ref, mod_ref, ng_ref, a_ref, w_ref, o_ref, *, d):
    _out_tail(x_ref, mod_ref, ng_ref, _dot(a_ref[0], w_ref[...]), o_ref, d)


def _odd_out_call(x_all, modsel, ng, attn, w_o, nctx_t):
    b, ttot, d = x_all.shape
    return pl.pallas_call(
        functools.partial(_odd_out_kernel, d=d),
        out_shape=jax.ShapeDtypeStruct((b, ttot, d), F32),
        grid=(b, ttot // TM),
        in_specs=[_x_spec(d), _mod_spec(6 * d, nctx_t), _resident((1, d)),
                  _x_spec(attn.shape[-1]), _resident(w_o.shape)],
        out_specs=_x_spec(d),
        compiler_params=_cparams(("parallel", "parallel")),
        name="odd_out_proj",
    )(x_all, modsel, ng, attn, w_o)


FFN_CK = 256


def _ffn_kernel(x_ref, xp_ref, xn_ref, mod_ref, ng_in_ref, ng_out_ref, wup_ref, cw_ref, cb_ref,
                wdn_ref, o_ref, h_ref, u_ref, act_ref, *, d, nctx_t, nt):
    pv, nv = _halo_valid(nctx_t, nt)
    m = mod_ref[0, 0]
    g_in = ng_in_ref[...]
    hh = SUBLANES_BF16
    h_ref[0:hh] = (_modulated(xp_ref[0], m, g_in, d, 1) * pv).astype(BF16)
    h_ref[hh:hh + TM] = _modulated(x_ref[0], m, g_in, d, 1).astype(BF16)
    h_ref[hh + TM:hh + TM + hh] = (_modulated(xn_ref[0], m, g_in, d, 1) * nv).astype(BF16)
    hcat = h_ref[...]
    ck = FFN_CK
    for c in range(FFN_HIDDEN // ck):
        lo = c * ck
        glo = FFN_HIDDEN + lo
        u_ref[:, 0:ck] = _dot(hcat, wup_ref[:, lo:lo + ck])
        u_ref[:, ck:2 * ck] = _dot(hcat, wup_ref[:, glo:glo + ck])
        a = _conv3(u_ref, hh, cw_ref[:, lo:lo + ck], cb_ref[:, lo:lo + ck], 0, ck)
        g = _conv3(u_ref, hh, cw_ref[:, glo:glo + ck], cb_ref[:, glo:glo + ck], ck, 2 * ck)
        act_ref[:, lo:lo + ck] = (a * (g * jax.nn.sigmoid(g))).astype(BF16)
    y = _dot(act_ref[...], wdn_ref[...])
    o_ref[0] = x_ref[0] + m[:, 5 * d:6 * d] * _rms(y, ng_out_ref[...])


def _ffn_call(x_all, modsel, ng_in, ng_out, w_up, conv_w, conv_b, w_down, nctx_t):
    b, ttot, d = x_all.shape
    nt = ttot // TM
    hh = SUBLANES_BF16
    prev, nxt = _halo_specs(d, hh, ttot)
    return pl.pallas_call(
        functools.partial(_ffn_kernel, d=d, nctx_t=nctx_t, nt=nt),
        out_shape=jax.ShapeDtypeStruct((b, ttot, d), F32),
        grid=(b, nt),
        in_specs=[_x_spec(d), prev, nxt, _mod_spec(6 * d, nctx_t), _resident((1, d)),
                  _resident((1, d)), _resident(w_up.shape), _resident(conv_w.shape),
                  _resident(conv_b.shape), _resident(w_down.shape)],
        out_specs=_x_spec(d),
        scratch_shapes=[pltpu.VMEM((TM + 2 * hh, d), BF16),
                        pltpu.VMEM((TM + 2 * hh, 2 * FFN_CK), F32),
                        pltpu.VMEM((TM, FFN_HIDDEN), BF16)],
        compiler_params=_cparams(("parallel", "parallel")),
        name="conv_ffn",
    )(x_all, x_all, x_all, modsel, ng_in, ng_out, w_up, conv_w, conv_b, w_down)


MLA_QK_PAD = LANES


def _mla_in_kernel(x_ref, mod_ref, ng_ref, wdq_ref, qg_ref, wuqt_ref, wdkv_ref, kvg_ref,
                   wuk_ref, wuvt_ref, ropeq_ref, ropek_ref, qt_ref, k_ref, vt_ref, *, d):
    m = mod_ref[0, 0]
    hb = _modulated(x_ref[0], m, ng_ref[...], d, 0).astype(BF16)
    cq = _rms(_dot(hb, wdq_ref[...]), qg_ref[...]).astype(BF16)
    qt_all = _dot_nt(wuqt_ref[...], cq)
    rq = ropeq_ref[...]
    cos_r, sin_r, cos_c, sin_c = rq[0:8], rq[8:16], rq[16:24], rq[24:32]
    scale = MLA_QK ** -0.5
    for hd in range(MLA_HEADS):
        base = hd * MLA_QK_PAD
        nope = qt_all[base:base + MLA_NOPE]
        x1r = qt_all[base + 64:base + 72]
        x2r = qt_all[base + 72:base + 80]
        x1c = qt_all[base + 80:base + 88]
        x2c = qt_all[base + 88:base + 96]
        pad = qt_all[base + 96:base + 128]
        roped = jnp.concatenate([
            nope,
            x1r * cos_r - x2r * sin_r, x1r * sin_r + x2r * cos_r,
            x1c * cos_c - x2c * sin_c, x1c * sin_c + x2c * cos_c,
            pad], axis=0)
        qt_ref[0, hd] = (roped * scale).astype(BF16)
    ckv = _dot(hb, wdkv_ref[...])
    cn = _rms(ckv[:, 0:MLA_KV_RANK], kvg_ref[...]).astype(BF16)
    rk = ropek_ref[...]
    kpe = ckv[:, LANES:2 * LANES] * rk[:, 0:LANES] + ckv[:, 2 * LANES:3 * LANES] * rk[:, LANES:2 * LANES]
    k_all = _dot(cn, wuk_ref[...])
    vt_all = _dot_nt(wuvt_ref[...], cn)
    for hd in range(MLA_HEADS):
        k_ref[0, hd] = (k_all[:, hd * MLA_QK_PAD:(hd + 1) * MLA_QK_PAD] + kpe).astype(BF16)
        vt_ref[0, hd] = vt_all[hd * MLA_V:(hd + 1) * MLA_V].astype(BF16)


def _mla_in_call(x_all, modsel, ng, wdq, qg, wuqt, wdkv, kvg, wuk, wuvt, rope_q, rope_k, nctx_t):
    b, ttot, d = x_all.shape
    nt = ttot // TM
    hn = MLA_HEADS
    return pl.pallas_call(
        functools.partial(_mla_in_kernel, d=d),
        out_shape=(jax.ShapeDtypeStruct((b, hn, MLA_QK_PAD, ttot), BF16),
                   jax.ShapeDtypeStruct((b, hn, ttot, MLA_QK_PAD), BF16),
                   jax.ShapeDtypeStruct((b, hn, MLA_V, ttot), BF16)),
        grid=(b, nt),
        in_specs=[_x_spec(d), _mod_spec(6 * d, nctx_t), _resident((1, d)),
                  _resident(wdq.shape), _resident(qg.shape), _resident(wuqt.shape),
                  _resident(wdkv.shape), _resident(kvg.shape), _resident(wuk.shape),
                  _resident(wuvt.shape),
                  pl.BlockSpec((32, TM), lambda bb, i: (0, i)),
                  pl.BlockSpec((TM, 2 * LANES), lambda bb, i: (i, 0))],
        out_specs=(pl.BlockSpec((1, hn, MLA_QK_PAD, TM), lambda bb, i: (bb, 0, 0, i)),
                   pl.BlockSpec((1, hn, TM, MLA_QK_PAD), lambda bb, i: (bb, 0, i, 0)),
                   pl.BlockSpec((1, hn, MLA_V, TM), lambda bb, i: (bb, 0, 0, i))),
        compiler_params=_cparams(("parallel", "parallel")),
        name="mla_in_proj",
    )(x_all, modsel, ng, wdq, qg, wuqt, wdkv, kvg, wuk, wuvt, rope_q, rope_k)


MLA_KC = 256


def _mla_attn_kernel(qt_ref, k_ref, vt_ref, o_ref, *, nkc):
    outs = []
    for hh in range(2):
        qt = qt_ref[0, hh]

        def body(c, carry):
            m, l, acc = carry
            ks = pl.multiple_of(c * MLA_KC, MLA_KC)
            s = _dot(k_ref[0, hh, pl.ds(ks, MLA_KC), :], qt)
            m_new = jnp.maximum(m, s.max(axis=0, keepdims=True))
            alpha = jnp.exp(m - m_new)
            p = jnp.exp(s - m_new)
            l = alpha * l + p.sum(axis=0, keepdims=True)
            acc = alpha * acc + _dot(vt_ref[0, hh, :, pl.ds(ks, MLA_KC)], p.astype(BF16))
            return m_new, l, acc

        init = (jnp.full((1, TM), NEG, F32), jnp.zeros((1, TM), F32), jnp.zeros((MLA_V, TM), F32))
        m, l, acc = lax.fori_loop(0, nkc, body, init)
        outs.append(acc / l)
    o_ref[0] = jnp.concatenate(outs, axis=0).T.astype(BF16)


def _mla_attn_call(qt, k, vt, q_tile0, n_qtiles, kv_len):
    b, hn, _, ttot = qt.shape
    return pl.pallas_call(
        functools.partial(_mla_attn_kernel, nkc=kv_len // MLA_KC),
        out_shape=jax.ShapeDtypeStruct((b, n_qtiles * TM, hn * MLA_V), BF16),
        grid=(b, hn // 2, n_qtiles),
        in_specs=[
            pl.BlockSpec((1, 2, MLA_QK_PAD, TM), lambda bb, hp, qi: (bb, hp, 0, q_tile0 + qi)),
            pl.BlockSpec((1, 2, kv_len, MLA_QK_PAD), lambda bb, hp, qi: (bb, hp, 0, 0)),
            pl.BlockSpec((1, 2, MLA_V, kv_len), lambda bb, hp, qi: (bb, hp, 0, 0)),
        ],
        out_specs=pl.BlockSpec((1, TM, 2 * MLA_V), lambda bb, hp, qi: (bb, qi, hp)),
        compiler_params=_cparams(("parallel", "parallel", "arbitrary")),
        name="mla_attention",
    )(qt, k, vt)


def _rope_tables(ctx, t):
    pos = np.arange(t)
    half = MLA_ROPE // 4
    inv = ROPE_BASE ** (-jnp.arange(half, dtype=F32) / half)
    tabs = []
    for p in (pos // GRID_W, pos % GRID_W):
        ang = jnp.asarray(p, F32)[:, None] * inv
        cos = jnp.concatenate([jnp.ones((ctx, half), F32), jnp.cos(ang)], axis=0)
        sin = jnp.concatenate([jnp.zeros((ctx, half), F32), jnp.sin(ang)], axis=0)
        tabs.append((cos, sin))
    (cr, sr), (cc, sc) = tabs
    rope_q = jnp.concatenate([cr, sr, cc, sc], axis=1).T
    ttot = ctx + t
    zeros = jnp.zeros((ttot, MLA_NOPE), F32)
    tail = jnp.zeros((ttot, LANES - MLA_QK), F32)
    cos_k = jnp.concatenate([zeros, cr, cr, cc, cc, tail], axis=1)
    sin_k = jnp.concatenate([zeros, -sr, sr, -sc, sc, tail], axis=1)
    rope_k = jnp.concatenate([cos_k, sin_k], axis=1)
    return rope_q, rope_k


def _mla_weights(w_uq, w_dkv, w_ukv):
    hn = MLA_HEADS
    q_rank = w_uq.shape[0]
    wq = w_uq.reshape(q_rank, hn, MLA_QK)
    wq = jnp.pad(wq, ((0, 0), (0, 0), (0, MLA_QK_PAD - MLA_QK)))
    wuqt = wq.reshape(q_rank, hn * MLA_QK_PAD).T.astype(BF16)
    d = w_dkv.shape[0]
    kpe = w_dkv[:, MLA_KV_RANK:]
    h8 = MLA_ROPE // 4
    swap = jnp.concatenate([kpe[:, h8:2 * h8], kpe[:, 0:h8], kpe[:, 3 * h8:4 * h8],
                            kpe[:, 2 * h8:3 * h8]], axis=1)

    def place(cols):
        return jnp.concatenate([jnp.zeros((d, MLA_NOPE), F32), cols,
                                jnp.zeros((d, LANES - MLA_QK), F32)], axis=1)

    wdkv = jnp.concatenate([w_dkv[:, :MLA_KV_RANK], place(kpe), place(swap)], axis=1).astype(BF16)
    wkv = w_ukv.reshape(MLA_KV_RANK, hn, MLA_NOPE + MLA_V)
    wuk = jnp.pad(wkv[:, :, :MLA_NOPE], ((0, 0), (0, 0), (0, MLA_QK_PAD - MLA_NOPE)))
    wuk = wuk.reshape(MLA_KV_RANK, hn * MLA_QK_PAD).astype(BF16)
    wuvt = wkv[:, :, MLA_NOPE:].reshape(MLA_KV_RANK, hn * MLA_V).T.astype(BF16)
    return wuqt, wdkv, wuk, wuvt


def kernel(x, c, ctx, c_ctx, ada_w, ada_b, norm_g, ffn_w_up, ffn_conv_w, ffn_conv_b, ffn_w_down,
           ev_w_in, ev_gate_b, ev_conv_w, ev_conv_b, ev_rpb, ev_ml_norm_g, ev_w_out,
           od_w_dq, od_q_norm_g, od_w_uq, od_w_dkv, od_kv_norm_g, od_w_ukv, od_w_o):
    b, t, d = x.shape
    nctx = ctx.shape[1]
    depth = ada_w.shape[0]
    ttot = nctx + t
    assert t % TM == 0 and nctx % TM == 0 and t % GRID_W == 0
    nctx_t = nctx // TM

    x_all = jnp.concatenate([ctx, x], axis=1)

    rows = -(-(b + 1) // 8) * 8
    cvec = jnp.zeros((rows, d), F32).at[:b].set(c).at[b].set(c_ctx)
    mod = _ada_call(cvec, ada_w, ada_b)

    rope_q, rope_k = _rope_tables(nctx, t)
    ml_scale = jnp.concatenate([jnp.full((1, ML_WIDTH), ML_HD ** -0.5, F32),
                                jnp.ones((1, ML_WIDTH), F32)], axis=1)

    for l in range(depth):
        ctx_out = l < depth - 1
        modsel = jnp.stack([jnp.broadcast_to(mod[l, b], (b, 6 * d)), mod[l, :b]], axis=1)
        modsel = modsel.reshape(b, 2, 1, 6 * d)
        ng = norm_g[l].reshape(4, 1, d)
        if l % 2 == 0:
            e = l // 2
            w_in = ev_w_in[e]
            n_main = 3 * NA_WIDTH + 4 * ML_WIDTH
            w_gate = jnp.pad(w_in[:, n_main:], ((0, 0), (0, LANES - ML_GATES))).astype(BF16)
            gate_b = jnp.pad(ev_gate_b[e], (0, LANES - ML_GATES)).reshape(1, LANES)
            naq, nak, nav, mlqk, mlv, mlo, gates = _even_in_call(
                x_all, modsel, ng[0], w_in[:, :n_main].astype(BF16), w_gate, gate_b, nctx_t)
            na_x = _na_call(naq, nak, nav, _na_bias_table(ev_rpb[e]), nctx, t)
            na_c = _na_ctx_call(naq, nak, nav, nctx)
            na = jnp.concatenate([na_c, na_x], axis=1)
            qk = _ml_conv_call(mlqk, ev_conv_w[e], ev_conv_b[e].reshape(1, -1), ml_scale, nctx_t)
            gates_t = jnp.swapaxes(gates[:, :, :ML_GATES], 1, 2)
            hf, hb = _mlstm_call(qk, mlv, gates, gates_t, nctx // ML_CHUNK)
            x_all = _even_out_call(x_all, modsel, ng[1], na, hf, hb, mlo,
                                   ev_ml_norm_g[e].reshape(1, -1), ev_w_out[e].astype(BF16), nctx_t)
        else:
            o = l // 2
            wuqt, wdkv, wuk, wuvt = _mla_weights(od_w_uq[o], od_w_dkv[o], od_w_ukv[o])
            qt, kk, vt = _mla_in_call(
                x_all, modsel, ng[0], od_w_dq[o].astype(BF16), od_q_norm_g[o].reshape(1, -1),
                wuqt, wdkv, od_kv_norm_g[o].reshape(1, -1), wuk, wuvt, rope_q, rope_k, nctx_t)
            att_x = _mla_attn_call(qt, kk, vt, nctx_t, t // TM, ttot)
            if ctx_out:
                att_c = _mla_attn_call(qt, kk, vt, 0, nctx_t, nctx)
            else:
                att_c = jnp.zeros((b, nctx, att_x.shape[-1]), BF16)
            att = jnp.concatenate([att_c, att_x], axis=1)
            x_all = _odd_out_call(x_all, modsel, ng[1], att, od_w_o[o].astype(BF16), nctx_t)
        x_all = _ffn_call(x_all, modsel, ng[2], ng[3], ffn_w_up[l].astype(BF16), ffn_conv_w[l],
                          ffn_conv_b[l].reshape(1, -1), ffn_w_down[l].astype(BF16), nctx_t)
    return x_all[:, nctx:]
```

```python
import functools
import math

import jax
import jax.numpy as jnp
import numpy as np
from jax import lax
from jax.experimental import pallas as pl
from jax.experimental.pallas import tpu as pltpu

F32 = jnp.float32
BF16 = jnp.bfloat16

EPS = 1e-6
NEG = -1e30

GRID_W = 64
NA_HEADS = 8
NA_HD = 64
NA_WIN_ROWS = 8
NA_WIN_COLS = 16
NA_WIDTH = NA_HEADS * NA_HD

ML_HEADS = 4
ML_HD = 128
ML_WIDTH = ML_HEADS * ML_HD
ML_CHUNK = 128
ML_GATES = 4 * ML_HEADS

MLA_HEADS = 16
MLA_NOPE = 64
MLA_ROPE = 32
MLA_V = 64
MLA_QK = MLA_NOPE + MLA_ROPE
MLA_Q_RANK = 256
MLA_KV_RANK = 128
ROPE_BASE = 10000.0

FFN_HIDDEN = 2816

LANES = 128
SUBLANES_BF16 = 16
TM = 256
VMEM_LIMIT = 48 * 1024 * 1024


def _cparams(sem):
    return pltpu.CompilerParams(dimension_semantics=sem, vmem_limit_bytes=VMEM_LIMIT)


def _resident(shape):
    nd = len(shape)
    return pl.BlockSpec(shape, lambda *_: (0,) * nd, pipeline_mode=pl.Buffered(1))


def _rms(xf, g):
    ms = jnp.mean(xf * xf, axis=-1, keepdims=True)
    return xf * lax.rsqrt(ms + EPS) * g


def _dot(a, b):
    return jnp.dot(a, b, preferred_element_type=F32)


def _dot_nt(a, b):
    return lax.dot_general(a, b, (((1,), (1,)), ((), ())), preferred_element_type=F32)


def _ada_kernel(c_ref, w_ref, b_ref, o_ref):
    c = c_ref[...]
    s = c * jax.nn.sigmoid(c)
    o_ref[0] = _dot(s.astype(BF16), w_ref[0].astype(BF16)) + b_ref[0]


def _ada_call(cvec, ada_w, ada_b):
    depth, d, n = ada_w.shape
    rows = cvec.shape[0]
    tn = 1024
    return pl.pallas_call(
        _ada_kernel,
        out_shape=jax.ShapeDtypeStruct((depth, rows, n), F32),
        grid=(depth, n // tn),
        in_specs=[
            pl.BlockSpec((rows, d), lambda l, j: (0, 0)),
            pl.BlockSpec((1, d, tn), lambda l, j: (l, 0, j)),
            pl.BlockSpec((1, 1, tn), lambda l, j: (l, 0, j)),
        ],
        out_specs=pl.BlockSpec((1, rows, tn), lambda l, j: (l, 0, j)),
        compiler_params=_cparams(("parallel", "parallel")),
        name="ada_mod",
    )(cvec, ada_w, ada_b.reshape(depth, 1, n))


def _x_spec(d):
    return pl.BlockSpec((1, TM, d), lambda b, i: (b, i, 0))


def _mod_spec(d6, nctx_t):
    return pl.BlockSpec((1, 1, 1, d6), lambda b, i: (b, jnp.where(i >= nctx_t, 1, 0), 0, 0))


def _modulated(x, m, g, d, which):
    o = 3 * d * which
    return _rms(x, g) * (1.0 + m[:, o + d:o + 2 * d]) + m[:, o:o + d]


def _even_in_kernel(x_ref, mod_ref, ng_ref, w_ref, wg_ref, gb_ref,
                    naq_ref, nak_ref, nav_ref, mlqk_ref, mlv_ref, mlo_ref, g_ref, *, d):
    m = mod_ref[0, 0]
    hb = _modulated(x_ref[0], m, ng_ref[...], d, 0).astype(BF16)

    def seg(lo, hi):
        return _dot(hb, w_ref[:, lo:hi])

    w = NA_WIDTH
    naq_ref[0] = (seg(0, w) * (NA_HD ** -0.5)).astype(BF16)
    nak_ref[0] = seg(w, 2 * w).astype(BF16)
    nav_ref[0] = seg(2 * w, 3 * w).astype(BF16)
    o = 3 * w
    mlqk_ref[0, :, 0:ML_WIDTH] = seg(o, o + ML_WIDTH)
    mlqk_ref[0, :, ML_WIDTH:2 * ML_WIDTH] = seg(o + ML_WIDTH, o + 2 * ML_WIDTH)
    mlv_ref[0] = seg(o + 2 * ML_WIDTH, o + 3 * ML_WIDTH).astype(BF16)
    mlo_ref[0] = seg(o + 3 * ML_WIDTH, o + 4 * ML_WIDTH)
    g_ref[0] = _dot(hb, wg_ref[...]) + gb_ref[...]


def _even_in_call(x_all, modsel, ng, w_main, w_gate, gate_b, nctx_t):
    b, ttot, d = x_all.shape
    nt = ttot // TM
    n_main = w_main.shape[1]
    row = lambda c: pl.BlockSpec((1, TM, c), lambda bb, i: (bb, i, 0))
    sds = lambda c, dt: jax.ShapeDtypeStruct((b, ttot, c), dt)
    return pl.pallas_call(
        functools.partial(_even_in_kernel, d=d),
        out_shape=(sds(NA_WIDTH, BF16), sds(NA_WIDTH, BF16), sds(NA_WIDTH, BF16),
                   sds(2 * ML_WIDTH, F32), sds(ML_WIDTH, BF16), sds(ML_WIDTH, F32),
                   sds(LANES, F32)),
        grid=(b, nt),
        in_specs=[_x_spec(d), _mod_spec(6 * d, nctx_t), _resident((1, d)),
                  _resident((d, n_main)), _resident((d, LANES)), _resident((1, LANES))],
        out_specs=(row(NA_WIDTH), row(NA_WIDTH), row(NA_WIDTH), row(2 * ML_WIDTH),
                   row(ML_WIDTH), row(ML_WIDTH), row(LANES)),
        compiler_params=_cparams(("parallel", "parallel")),
        name="even_in_proj",
    )(x_all, modsel, ng, w_main, w_gate, gate_b)


def _pair_attention(qp, k_parts, v_parts, bias_parts):
    mq = qp.shape[0]
    lane = lax.broadcasted_iota(jnp.int32, (mq, LANES), 1)
    outs = []
    for hh in range(2):
        keep = (lane >= NA_HD) if hh else (lane < NA_HD)
        qh = jnp.where(keep, qp, jnp.zeros_like(qp))
        s = []
        for kk, bias in zip(k_parts, bias_parts):
            sp = _dot_nt(qh, kk)
            if bias is not None:
                sp = sp + bias[hh]
            s.append(sp)
        m = s[0].max(axis=-1, keepdims=True)
        for sp in s[1:]:
            m = jnp.maximum(m, sp.max(axis=-1, keepdims=True))
        l = jnp.zeros_like(m)
        o = jnp.zeros((mq, LANES), F32)
        for sp, vv in zip(s, v_parts):
            p = jnp.exp(sp - m)
            l = l + p.sum(axis=-1, keepdims=True)
            o = o + _dot(p.astype(BF16), vv)
        outs.append(o / l)
    return jnp.where(lane < NA_HD, outs[0], outs[1])


def _na_kernel(q_ref, k_ref, v_ref, bias_ref, o_ref, *, ctx, rows):
    r = pl.program_id(1)
    rs = jnp.clip(r - NA_WIN_ROWS // 2, 0, rows - NA_WIN_ROWS)
    start = pl.multiple_of(ctx + rs * GRID_W, GRID_W)
    band = NA_WIN_ROWS * GRID_W
    for p in range(NA_HEADS // 2):
        cs = slice(p * LANES, (p + 1) * LANES)
        kb = k_ref[0, pl.ds(start, band), cs]
        vb = v_ref[0, pl.ds(start, band), cs]
        kc = k_ref[0, 0:ctx, cs]
        vc = v_ref[0, 0:ctx, cs]
        bias = (bias_ref[0, 2 * p], bias_ref[0, 2 * p + 1])
        o = _pair_attention(q_ref[0, :, cs], (kb, kc), (vb, vc), (bias, None))
        o_ref[0, :, cs] = o.astype(BF16)


def _na_call(naq, nak, nav, bias_tab, ctx, t):
    b, ttot, w = naq.shape
    rows = t // GRID_W
    assert rows >= NA_WIN_ROWS and ctx % GRID_W == 0
    q_off = ctx // GRID_W

    def delta(bb, r):
        rs = jnp.clip(r - NA_WIN_ROWS // 2, 0, rows - NA_WIN_ROWS)
        return (rs - r + NA_WIN_ROWS - 1, 0, 0, 0)

    return pl.pallas_call(
        functools.partial(_na_kernel, ctx=ctx, rows=rows),
        out_shape=jax.ShapeDtypeStruct((b, t, w), BF16),
        grid=(b, rows),
        in_specs=[
            pl.BlockSpec((1, GRID_W, w), lambda bb, r: (bb, q_off + r, 0)),
            pl.BlockSpec((1, ttot, w), lambda bb, r: (bb, 0, 0)),
            pl.BlockSpec((1, ttot, w), lambda bb, r: (bb, 0, 0)),
            pl.BlockSpec((1, NA_HEADS, GRID_W, NA_WIN_ROWS * GRID_W), delta),
        ],
        out_specs=pl.BlockSpec((1, GRID_W, w), lambda bb, r: (bb, r, 0)),
        compiler_params=_cparams(("parallel", "arbitrary")),
        name="na_attention",
    )(naq, nak, nav, bias_tab)


def _na_ctx_kernel(q_ref, k_ref, v_ref, o_ref):
    for p in range(NA_HEADS // 2):
        cs = slice(p * LANES, (p + 1) * LANES)
        o = _pair_attention(q_ref[0, :, cs], (k_ref[0, :, cs],), (v_ref[0, :, cs],), (None,))
        o_ref[0, :, cs] = o.astype(BF16)


def _na_ctx_call(naq, nak, nav, ctx):
    b, _, w = naq.shape
    spec = pl.BlockSpec((1, ctx, w), lambda bb: (bb, 0, 0))
    return pl.pallas_call(
        _na_ctx_kernel,
        out_shape=jax.ShapeDtypeStruct((b, ctx, w), BF16),
        grid=(b,),
        in_specs=[spec, spec, spec],
        out_specs=spec,
        compiler_params=_cparams(("parallel",)),
        name="na_ctx_attention",
    )(naq, nak, nav)


def _na_bias_table(rpb):
    kw = NA_WIN_COLS
    col = jnp.arange(GRID_W)
    cs = jnp.clip(col - kw // 2, 0, GRID_W - kw)
    in_win = (col[None, :] >= cs[:, None]) & (col[None, :] < cs[:, None] + kw)
    dc_idx = jnp.clip(col[None, :] - col[:, None], -(kw - 1), kw - 1) + (NA_WIN_COLS - 1)
    rpb_cols = rpb.astype(F32)[:, :, dc_idx]
    rpb_cols = jnp.where(in_win[None, None], rpb_cols, NEG)
    dr = jnp.arange(NA_WIN_ROWS)[:, None] + jnp.arange(NA_WIN_ROWS)[None, :]
    tab = rpb_cols[:, dr]
    tab = tab.transpose(1, 0, 3, 2, 4)
    return tab.reshape(NA_WIN_ROWS, NA_HEADS, GRID_W, NA_WIN_ROWS * GRID_W)


def _halo_specs(c, rows_h, ttot):
    per = TM // rows_h
    last = ttot // rows_h - 1
    prev = pl.BlockSpec((1, rows_h, c), lambda b, i: (b, jnp.maximum(i * per - 1, 0), 0))
    nxt = pl.BlockSpec((1, rows_h, c), lambda b, i: (b, jnp.minimum((i + 1) * per, last), 0))
    return prev, nxt


def _halo_valid(nctx_t, nt):
    i = pl.program_id(1)
    pv = jnp.where((i == 0) | (i == nctx_t), 0.0, 1.0).astype(F32)
    nv = jnp.where((i == nctx_t - 1) | (i == nt - 1), 0.0, 1.0).astype(F32)
    return pv, nv


def _conv3(u_ref, h, cw, cb, lo, hi):
    return (cb
            + u_ref[h - 1:h - 1 + TM, lo:hi] * cw[0:1]
            + u_ref[h:h + TM, lo:hi] * cw[1:2]
            + u_ref[h + 1:h + 1 + TM, lo:hi] * cw[2:3])


def _ml_conv_kernel(x_ref, xp_ref, xn_ref, cw_ref, cb_ref, sc_ref, o_ref, u_ref, *, nctx_t, nt):
    pv, nv = _halo_valid(nctx_t, nt)
    h = 8
    u_ref[0:h] = xp_ref[0] * pv
    u_ref[h:h + TM] = x_ref[0]
    u_ref[h + TM:h + TM + h] = xn_ref[0] * nv
    c = x_ref.shape[-1]
    y = _conv3(u_ref, h, cw_ref[...], cb_ref[...], 0, c)
    y = y * jax.nn.sigmoid(y) * sc_ref[...]
    o_ref[0] = y.astype(BF16)


def _ml_conv_call(mlqk, conv_w, conv_b, scale, nctx_t):
    b, ttot, c = mlqk.shape
    nt = ttot // TM
    prev, nxt = _halo_specs(c, 8, ttot)
    return pl.pallas_call(
        functools.partial(_ml_conv_kernel, nctx_t=nctx_t, nt=nt),
        out_shape=jax.ShapeDtypeStruct((b, ttot, c), BF16),
        grid=(b, nt),
        in_specs=[_x_spec(c), prev, nxt, _resident((3, c)), _resident((1, c)), _resident((1, c))],
        out_specs=_x_spec(c),
        scratch_shapes=[pltpu.VMEM((TM + 16, c), F32)],
        compiler_params=_cparams(("parallel", "parallel")),
        name="mlstm_conv",
    )(mlqk, mlqk, mlqk, conv_w, conv_b, scale)


def _log_sigmoid(x):
    return jnp.minimum(x, 0.0) - jnp.log(1.0 + jnp.exp(-jnp.abs(x)))


def _split_dot(a, b, data_is_rhs):
    data = b if data_is_rhs else a
    hi = data.astype(BF16)
    lo = (data - hi.astype(F32)).astype(BF16)
    if data_is_rhs:
        return _dot(a, hi) + _dot(a, lo)
    return _dot(hi, b) + _dot(lo, b)


def _mlstm_chain(q, k, v_aug, cum_col, cum_row, i_col, i_row, total, mask, ct_ref, m_ref, j):
    m_st = m_ref[j][0:1, 0:1]
    d_in = jnp.where(mask, cum_col - cum_row + i_row, NEG)
    m_inter = cum_col + m_st
    m_t = jnp.maximum(m_inter, d_in.max(axis=-1, keepdims=True))
    w = jnp.exp(d_in - m_t)
    a = jnp.exp(m_inter - m_t)
    qk = (_dot_nt(q, k) * w).astype(BF16)
    ct = ct_ref[j]
    hfull = a * _dot(q, ct.astype(BF16)) + _dot(qk, v_aug)
    num = hfull[:, 0:ML_HD]
    den = hfull[:, ML_HD:ML_HD + 1]
    h_out = num / jnp.maximum(jnp.abs(den), jnp.exp(-m_t))
    g = total - cum_col + i_col
    m_new = jnp.maximum(total + m_st, g.max(axis=0, keepdims=True))
    decay = jnp.exp(total + m_st - m_new)
    wk = jnp.exp(g - m_new)
    kw = (k.astype(F32) * wk).T.astype(BF16)
    ct_ref[j] = decay * ct + _dot(kw, v_aug)
    m_ref[j] = jnp.broadcast_to(m_new, m_ref.shape[1:])
    return h_out


def _mlstm_kernel(qkf_ref, vf_ref, gf_ref, gtf_ref, qkb_ref, vb_ref, gb_ref, gtb_ref,
                  hf_ref, hb_ref, ct_ref, m_ref):
    L = ML_CHUNK

    @pl.when(pl.program_id(1) == 0)
    def _():
        ct_ref[...] = jnp.zeros_like(ct_ref)
        m_ref[...] = jnp.zeros_like(m_ref)

    ri = lax.broadcasted_iota(jnp.int32, (L, L), 0)
    ci = lax.broadcasted_iota(jnp.int32, (L, L), 1)
    tril = jnp.where(ri >= ci, 1.0, 0.0).astype(BF16)
    triu = jnp.where(ri <= ci, 1.0, 0.0).astype(BF16)
    lane = lax.broadcasted_iota(jnp.int32, (L, LANES), 1)
    ones_col = jnp.where(lane == 0, 1.0, 0.0).astype(BF16)

    dirs = (
        (qkf_ref, vf_ref, gf_ref, gtf_ref, hf_ref, tril, triu, ri >= ci, L - 1, 0),
        (qkb_ref, vb_ref, gb_ref, gtb_ref, hb_ref, triu, tril, ri <= ci, 0, 2 * ML_HEADS),
    )
    for dnum, (qk_ref, v_ref, g_ref, gt_ref, h_ref, tcol, trow, mask, tot_row, goff) in enumerate(dirs):
        g = g_ref[0]
        gt = gt_ref[0]
        cum_col_all = _split_dot(tcol, _log_sigmoid(g), True)
        cum_row_all = _split_dot(_log_sigmoid(gt), trow, False)
        for hd in range(ML_HEADS):
            ic = goff + hd
            fc = goff + ML_HEADS + hd
            cs = slice(hd * ML_HD, (hd + 1) * ML_HD)
            q = qk_ref[0, :, cs]
            k = qk_ref[0, :, ML_WIDTH + hd * ML_HD:ML_WIDTH + (hd + 1) * ML_HD]
            v_aug = jnp.concatenate([v_ref[0, :, cs], ones_col], axis=-1)
            cum_col = cum_col_all[:, fc:fc + 1]
            h = _mlstm_chain(
                q, k, v_aug, cum_col, cum_row_all[fc:fc + 1, :],
                g[:, ic:ic + 1], gt[ic:ic + 1, :], cum_col[tot_row:tot_row + 1, :],
                mask, ct_ref, m_ref, dnum * ML_HEADS + hd)
            h_ref[0, :, cs] = h


def _mlstm_call(qk, v, gates, gates_t, nctx_c):
    b, ttot, _ = qk.shape
    nc = ttot // ML_CHUNK
    L = ML_CHUNK

    def fwd(bb, c):
        return (bb, c, 0)

    def bwd(bb, c):
        return (bb, jnp.where(c < nctx_c, nctx_c - 1 - c, nc - 1 + nctx_c - c), 0)

    def fwd_t(bb, c):
        return (bb, 0, c)

    def bwd_t(bb, c):
        return (bb, 0, jnp.where(c < nctx_c, nctx_c - 1 - c, nc - 1 + nctx_c - c))

    ins = []
    for row_map, col_map in ((fwd, fwd_t), (bwd, bwd_t)):
        ins += [pl.BlockSpec((1, L, 2 * ML_WIDTH), row_map),
                pl.BlockSpec((1, L, ML_WIDTH), row_map),
                pl.BlockSpec((1, L, LANES), row_map),
                pl.BlockSpec((1, ML_GATES, L), col_map)]
    out_sds = jax.ShapeDtypeStruct((b, ttot, ML_WIDTH), F32)
    return pl.pallas_call(
        _mlstm_kernel,
        out_shape=(out_sds, out_sds),
        grid=(b, nc),
        in_specs=ins,
        out_specs=(pl.BlockSpec((1, L, ML_WIDTH), fwd), pl.BlockSpec((1, L, ML_WIDTH), bwd)),
        scratch_shapes=[pltpu.VMEM((2 * ML_HEADS, ML_HD, 2 * ML_HD), F32),
                        pltpu.VMEM((2 * ML_HEADS, 8, LANES), F32)],
        compiler_params=_cparams(("parallel", "arbitrary")),
        name="mlstm_scan",
    )(qk, v, gates, gates_t, qk, v, gates, gates_t)


def _out_tail(x_ref, mod_ref, ng_ref, y, o_ref, d):
    m = mod_ref[0, 0]
    o_ref[0] = x_ref[0] + m[:, 2 * d:3 * d] * _rms(y, ng_ref[...])


def _even_out_kernel(x_ref, mod_ref, ng_ref, na_ref, hf_ref, hb_ref, op_ref, mlg_ref, w_ref,
                     o_ref, *, d):
    hs = jax.nn.sigmoid(op_ref[0]) * (hf_ref[0] + hb_ref[0])
    parts = []
    for hd in range(ML_HEADS):
        seg = hs[:, hd * ML_HD:(hd + 1) * ML_HD]
        mu = jnp.mean(seg, axis=-1, keepdims=True)
        cen = seg - mu
        var = jnp.mean(cen * cen, axis=-1, keepdims=True)
        parts.append(cen * lax.rsqrt(var + EPS))
    ml = (jnp.concatenate(parts, axis=-1) * mlg_ref[...]).astype(BF16)
    y = _dot(na_ref[0], w_ref[0:NA_WIDTH, :]) + _dot(ml, w_ref[NA_WIDTH:NA_WIDTH + ML_WIDTH, :])
    _out_tail(x_ref, mod_ref, ng_ref, y, o_ref, d)


def _even_out_call(x_all, modsel, ng, na, hf, hb, mlo, mlg, w_out, nctx_t):
    b, ttot, d = x_all.shape
    row = lambda c: pl.BlockSpec((1, TM, c), lambda bb, i: (bb, i, 0))
    return pl.pallas_call(
        functools.partial(_even_out_kernel, d=d),
        out_shape=jax.ShapeDtypeStruct((b, ttot, d), F32),
        grid=(b, ttot // TM),
        in_specs=[_x_spec(d), _mod_spec(6 * d, nctx_t), _resident((1, d)),
                  row(NA_WIDTH), row(ML_WIDTH), row(ML_WIDTH), row(ML_WIDTH),
                  _resident((1, ML_WIDTH)), _resident(w_out.shape)],
        out_specs=_x_spec(d),
        compiler_params=_cparams(("parallel", "parallel")),
        name="even_out_proj",
    )(x_all, modsel, ng, na, hf, hb, mlo, mlg, w_out)


def _odd_out_kernel(x_ref, mod_ref, ng_ref, a_ref, w_ref, o_ref, *, d):
    _out_tail(x_ref, mod_ref, ng_ref, _dot(a_ref[0], w_ref[...]), o_ref, d)


def _odd_out_call(x_all, modsel, ng, attn, w_o, nctx_t):
    b, ttot, d = x_all.shape
    return pl.pallas_call(
        functools.partial(_odd_out_kernel, d=d),
        out_shape=jax.ShapeDtypeStruct((b, ttot, d), F32),
        grid=(b, ttot // TM),
        in_specs=[_x_spec(d), _mod_spec(6 * d, nctx_t), _resident((1, d)),
                  _x_spec(attn.shape[-1]), _resident(w_o.shape)],
        out_specs=_x_spec(d),
        compiler_params=_cparams(("parallel", "parallel")),
        name="odd_out_proj",
    )(x_all, modsel, ng, attn, w_o)


FFN_CK = 256


def _ffn_kernel(x_ref, xp_ref, xn_ref, mod_ref, ng_in_ref, ng_out_ref, wup_ref, cw_ref, cb_ref,
                wdn_ref, o_ref, h_ref, u_ref, act_ref, *, d, nctx_t, nt):
    pv, nv = _halo_valid(nctx_t, nt)
    m = mod_ref[0, 0]
    g_in = ng_in_ref[...]
    hh = SUBLANES_BF16
    h_ref[0:hh] = (_modulated(xp_ref[0], m, g_in, d, 1) * pv).astype(BF16)
    h_ref[hh:hh + TM] = _modulated(x_ref[0], m, g_in, d, 1).astype(BF16)
    h_ref[hh + TM:hh + TM + hh] = (_modulated(xn_ref[0], m, g_in, d, 1) * nv).astype(BF16)
    hcat = h_ref[...]
    ck = FFN_CK
    for c in range(FFN_HIDDEN // ck):
        lo = c * ck
        glo = FFN_HIDDEN + lo
        u_ref[:, 0:ck] = _dot(hcat, wup_ref[:, lo:lo + ck])
        u_ref[:, ck:2 * ck] = _dot(hcat, wup_ref[:, glo:glo + ck])
        a = _conv3(u_ref, hh, cw_ref[:, lo:lo + ck], cb_ref[:, lo:lo + ck], 0, ck)
        g = _conv3(u_ref, hh, cw_ref[:, glo:glo + ck], cb_ref[:, glo:glo + ck], ck, 2 * ck)
        act_ref[:, lo:lo + ck] = (a * (g * jax.nn.sigmoid(g))).astype(BF16)
    y = _dot(act_ref[...], wdn_ref[...])
    o_ref[0] = x_ref[0] + m[:, 5 * d:6 * d] * _rms(y, ng_out_ref[...])


def _ffn_call(x_all, modsel, ng_in, ng_out, w_up, conv_w, conv_b, w_down, nctx_t):
    b, ttot, d = x_all.shape
    nt = ttot // TM
    hh = SUBLANES_BF16
    prev, nxt = _halo_specs(d, hh, ttot)
    return pl.pallas_call(
        functools.partial(_ffn_kernel, d=d, nctx_t=nctx_t, nt=nt),
        out_shape=jax.ShapeDtypeStruct((b, ttot, d), F32),
        grid=(b, nt),
        in_specs=[_x_spec(d), prev, nxt, _mod_spec(6 * d, nctx_t), _resident((1, d)),
                  _resident((1, d)), _resident(w_up.shape), _resident(conv_w.shape),
                  _resident(conv_b.shape), _resident(w_down.shape)],
        out_specs=_x_spec(d),
        scratch_shapes=[pltpu.VMEM((TM + 2 * hh, d), BF16),
                        pltpu.VMEM((TM + 2 * hh, 2 * FFN_CK), F32),
                        pltpu.VMEM((TM, FFN_HIDDEN), BF16)],
        compiler_params=_cparams(("parallel", "parallel")),
        name="conv_ffn",
    )(x_all, x_all, x_all, modsel, ng_in, ng_out, w_up, conv_w, conv_b, w_down)


MLA_QK_PAD = LANES


def _mla_in_kernel(x_ref, mod_ref, ng_ref, wdq_ref, qg_ref, wuqt_ref, wdkv_ref, kvg_ref,
                   wuk_ref, wuvt_ref, ropeq_ref, ropek_ref, qt_ref, k_ref, vt_ref, *, d):
    m = mod_ref[0, 0]
    hb = _modulated(x_ref[0], m, ng_ref[...], d, 0).astype(BF16)
    cq = _rms(_dot(hb, wdq_ref[...]), qg_ref[...]).astype(BF16)
    qt_all = _dot_nt(wuqt_ref[...], cq)
    rq = ropeq_ref[...]
    cos_r, sin_r, cos_c, sin_c = rq[0:8], rq[8:16], rq[16:24], rq[24:32]
    scale = MLA_QK ** -0.5
    for hd in range(MLA_HEADS):
        base = hd * MLA_QK_PAD
        nope = qt_all[base:base + MLA_NOPE]
        x1r = qt_all[base + 64:base + 72]
        x2r = qt_all[base + 72:base + 80]
        x1c = qt_all[base + 80:base + 88]
        x2c = qt_all[base + 88:base + 96]
        pad = qt_all[base + 96:base + 128]
        roped = jnp.concatenate([
            nope,
            x1r * cos_r - x2r * sin_r, x1r * sin_r + x2r * cos_r,
            x1c * cos_c - x2c * sin_c, x1c * sin_c + x2c * cos_c,
            pad], axis=0)
        qt_ref[0, hd] = (roped * scale).astype(BF16)
    ckv = _dot(hb, wdkv_ref[...])
    cn = _rms(ckv[:, 0:MLA_KV_RANK], kvg_ref[...]).astype(BF16)
    rk = ropek_ref[...]
    kpe = ckv[:, LANES:2 * LANES] * rk[:, 0:LANES] + ckv[:, 2 * LANES:3 * LANES] * rk[:, LANES:2 * LANES]
    k_all = _dot(cn, wuk_ref[...])
    vt_all = _dot_nt(wuvt_ref[...], cn)
    for hd in range(MLA_HEADS):
        k_ref[0, hd] = (k_all[:, hd * MLA_QK_PAD:(hd + 1) * MLA_QK_PAD] + kpe).astype(BF16)
        vt_ref[0, hd] = vt_all[hd * MLA_V:(hd + 1) * MLA_V].astype(BF16)


def _mla_in_call(x_all, modsel, ng, wdq, qg, wuqt, wdkv, kvg, wuk, wuvt, rope_q, rope_k, nctx_t):
    b, ttot, d = x_all.shape
    nt = ttot // TM
    hn = MLA_HEADS
    return pl.pallas_call(
        functools.partial(_mla_in_kernel, d=d),
        out_shape=(jax.ShapeDtypeStruct((b, hn, MLA_QK_PAD, ttot), BF16),
                   jax.ShapeDtypeStruct((b, hn, ttot, MLA_QK_PAD), BF16),
                   jax.ShapeDtypeStruct((b, hn, MLA_V, ttot), BF16)),
        grid=(b, nt),
        in_specs=[_x_spec(d), _mod_spec(6 * d, nctx_t), _resident((1, d)),
                  _resident(wdq.shape), _resident(qg.shape), _resident(wuqt.shape),
                  _resident(wdkv.shape), _resident(kvg.shape), _resident(wuk.shape),
                  _resident(wuvt.shape),
                  pl.BlockSpec((32, TM), lambda bb, i: (0, i)),
                  pl.BlockSpec((TM, 2 * LANES), lambda bb, i: (i, 0))],
        out_specs=(pl.BlockSpec((1, hn, MLA_QK_PAD, TM), lambda bb, i: (bb, 0, 0, i)),
                   pl.BlockSpec((1, hn, TM, MLA_QK_PAD), lambda bb, i: (bb, 0, i, 0)),
                   pl.BlockSpec((1, hn, MLA_V, TM), lambda bb, i: (bb, 0, 0, i))),
        compiler_params=_cparams(("parallel", "parallel")),
        name="mla_in_proj",
    )(x_all, modsel, ng, wdq, qg, wuqt, wdkv, kvg, wuk, wuvt, rope_q, rope_k)


MLA_KC = 256
MLA_LOOKAHEAD = 4


def _mla_attn_kernel(qt_ref, k_ref, vt_ref, o_ref, *, nkc):
    outs = []
    for hh in range(2):
        qt = qt_ref[0, hh]
        ms, ls, accs, scores = [], [], [], []

        def chunk(c):
            return slice(c * MLA_KC, (c + 1) * MLA_KC)

        for c in range(nkc + MLA_LOOKAHEAD):
            if c < nkc:
                scores.append(_dot(k_ref[0, hh, chunk(c), :], qt))
            cc = c - MLA_LOOKAHEAD
            if cc >= 0:
                s = scores[cc]
                m_c = s.max(axis=0, keepdims=True)
                p = jnp.exp(s - m_c)
                ms.append(m_c)
                ls.append(p.sum(axis=0, keepdims=True))
                accs.append(_dot(vt_ref[0, hh, :, chunk(cc)], p.astype(BF16)))
        m = functools.reduce(jnp.maximum, ms)
        l = jnp.zeros((1, TM), F32)
        acc = jnp.zeros((MLA_V, TM), F32)
        for m_c, l_c, a_c in zip(ms, ls, accs):
            w = jnp.exp(m_c - m)
            l = l + w * l_c
            acc = acc + w * a_c
        outs.append(acc / l)
    o_ref[0] = jnp.concatenate(outs, axis=0).T.astype(BF16)


def _mla_attn_call(qt, k, vt, q_tile0, n_qtiles, kv_len):
    b, hn, _, ttot = qt.shape
    return pl.pallas_call(
        functools.partial(_mla_attn_kernel, nkc=kv_len // MLA_KC),
        out_shape=jax.ShapeDtypeStruct((b, n_qtiles * TM, hn * MLA_V), BF16),
        grid=(b, hn // 2, n_qtiles),
        in_specs=[
            pl.BlockSpec((1, 2, MLA_QK_PAD, TM), lambda bb, hp, qi: (bb, hp, 0, q_tile0 + qi)),
            pl.BlockSpec((1, 2, kv_len, MLA_QK_PAD), lambda bb, hp, qi: (bb, hp, 0, 0)),
            pl.BlockSpec((1, 2, MLA_V, kv_len), lambda bb, hp, qi: (bb, hp, 0, 0)),
        ],
        out_specs=pl.BlockSpec((1, TM, 2 * MLA_V), lambda bb, hp, qi: (bb, qi, hp)),
        compiler_params=_cparams(("parallel", "parallel", "arbitrary")),
        name="mla_attention",
    )(qt, k, vt)


def _rope_tables(ctx, t):
    pos = np.arange(t)
    half = MLA_ROPE // 4
    inv = ROPE_BASE ** (-jnp.arange(half, dtype=F32) / half)
    tabs = []
    for p in (pos // GRID_W, pos % GRID_W):
        ang = jnp.asarray(p, F32)[:, None] * inv
        cos = jnp.concatenate([jnp.ones((ctx, half), F32), jnp.cos(ang)], axis=0)
        sin = jnp.concatenate([jnp.zeros((ctx, half), F32), jnp.sin(ang)], axis=0)
        tabs.append((cos, sin))
    (cr, sr), (cc, sc) = tabs
    rope_q = jnp.concatenate([cr, sr, cc, sc], axis=1).T
    ttot = ctx + t
    zeros = jnp.zeros((ttot, MLA_NOPE), F32)
    tail = jnp.zeros((ttot, LANES - MLA_QK), F32)
    cos_k = jnp.concatenate([zeros, cr, cr, cc, cc, tail], axis=1)
    sin_k = jnp.concatenate([zeros, -sr, sr, -sc, sc, tail], axis=1)
    rope_k = jnp.concatenate([cos_k, sin_k], axis=1)
    return rope_q, rope_k


def _mla_weights(w_uq, w_dkv, w_ukv):
    hn = MLA_HEADS
    q_rank = w_uq.shape[0]
    wq = w_uq.reshape(q_rank, hn, MLA_QK)
    wq = jnp.pad(wq, ((0, 0), (0, 0), (0, MLA_QK_PAD - MLA_QK)))
    wuqt = wq.reshape(q_rank, hn * MLA_QK_PAD).T.astype(BF16)
    d = w_dkv.shape[0]
    kpe = w_dkv[:, MLA_KV_RANK:]
    h8 = MLA_ROPE // 4
    swap = jnp.concatenate([kpe[:, h8:2 * h8], kpe[:, 0:h8], kpe[:, 3 * h8:4 * h8],
                            kpe[:, 2 * h8:3 * h8]], axis=1)

    def place(cols):
        return jnp.concatenate([jnp.zeros((d, MLA_NOPE), F32), cols,
                                jnp.zeros((d, LANES - MLA_QK), F32)], axis=1)

    wdkv = jnp.concatenate([w_dkv[:, :MLA_KV_RANK], place(kpe), place(swap)], axis=1).astype(BF16)
    wkv = w_ukv.reshape(MLA_KV_RANK, hn, MLA_NOPE + MLA_V)
    wuk = jnp.pad(wkv[:, :, :MLA_NOPE], ((0, 0), (0, 0), (0, MLA_QK_PAD - MLA_NOPE)))
    wuk = wuk.reshape(MLA_KV_RANK, hn * MLA_QK_PAD).astype(BF16)
    wuvt = wkv[:, :, MLA_NOPE:].reshape(MLA_KV_RANK, hn * MLA_V).T.astype(BF16)
    return wuqt, wdkv, wuk, wuvt


def kernel(x, c, ctx, c_ctx, ada_w, ada_b, norm_g, ffn_w_up, ffn_conv_w, ffn_conv_b, ffn_w_down,
           ev_w_in, ev_gate_b, ev_conv_w, ev_conv_b, ev_rpb, ev_ml_norm_g, ev_w_out,
           od_w_dq, od_q_norm_g, od_w_uq, od_w_dkv, od_kv_norm_g, od_w_ukv, od_w_o):
    b, t, d = x.shape
    nctx = ctx.shape[1]
    depth = ada_w.shape[0]
    ttot = nctx + t
    assert t % TM == 0 and nctx % TM == 0 and t % GRID_W == 0
    nctx_t = nctx // TM

    x_all = jnp.concatenate([ctx, x], axis=1)

    rows = -(-(b + 1) // 8) * 8
    cvec = jnp.zeros((rows, d), F32).at[:b].set(c).at[b].set(c_ctx)
    mod = _ada_call(cvec, ada_w, ada_b)

    rope_q, rope_k = _rope_tables(nctx, t)
    ml_scale = jnp.concatenate([jnp.full((1, ML_WIDTH), ML_HD ** -0.5, F32),
                                jnp.ones((1, ML_WIDTH), F32)], axis=1)

    for l in range(depth):
        ctx_out = l < depth - 1
        modsel = jnp.stack([jnp.broadcast_to(mod[l, b], (b, 6 * d)), mod[l, :b]], axis=1)
        modsel = modsel.reshape(b, 2, 1, 6 * d)
        ng = norm_g[l].reshape(4, 1, d)
        if l % 2 == 0:
            e = l // 2
            w_in = ev_w_in[e]
            n_main = 3 * NA_WIDTH + 4 * ML_WIDTH
            w_gate = jnp.pad(w_in[:, n_main:], ((0, 0), (0, LANES - ML_GATES))).astype(BF16)
            gate_b = jnp.pad(ev_gate_b[e], (0, LANES - ML_GATES)).reshape(1, LANES)
            naq, nak, nav, mlqk, mlv, mlo, gates = _even_in_call(
                x_all, modsel, ng[0], w_in[:, :n_main].astype(BF16), w_gate, gate_b, nctx_t)
            na_x = _na_call(naq, nak, nav, _na_bias_table(ev_rpb[e]), nctx, t)
            na_c = _na_ctx_call(naq, nak, nav, nctx)
            na = jnp.concatenate([na_c, na_x], axis=1)
            qk = _ml_conv_call(mlqk, ev_conv_w[e], ev_conv_b[e].reshape(1, -1), ml_scale, nctx_t)
            gates_t = jnp.swapaxes(gates[:, :, :ML_GATES], 1, 2)
            hf, hb = _mlstm_call(qk, mlv, gates, gates_t, nctx // ML_CHUNK)
            x_all = _even_out_call(x_all, modsel, ng[1], na, hf, hb, mlo,
                                   ev_ml_norm_g[e].reshape(1, -1), ev_w_out[e].astype(BF16), nctx_t)
        else:
            o = l // 2
            wuqt, wdkv, wuk, wuvt = _mla_weights(od_w_uq[o], od_w_dkv[o], od_w_ukv[o])
            qt, kk, vt = _mla_in_call(
                x_all, modsel, ng[0], od_w_dq[o].astype(BF16), od_q_norm_g[o].reshape(1, -1),
                wuqt, wdkv, od_kv_norm_g[o].reshape(1, -1), wuk, wuvt, rope_q, rope_k, nctx_t)
            att_x = _mla_attn_call(qt, kk, vt, nctx_t, t // TM, ttot)
            if ctx_out:
                att_c = _mla_attn_call(qt, kk, vt, 0, nctx_t, nctx)
            else:
                att_c = jnp.zeros((b, nctx, att_x.shape[-1]), BF16)
            att = jnp.concatenate([att_c, att_x], axis=1)
            x_all = _odd_out_call(x_all, modsel, ng[1], att, od_w_o[o].astype(BF16), nctx_t)
        x_all = _ffn_call(x_all, modsel, ng[2], ng[3], ffn_w_up[l].astype(BF16), ffn_conv_w[l],
                          ffn_conv_b[l].reshape(1, -1), ffn_w_down[l].astype(BF16), nctx_t)
    return x_all[:, nctx:]
```

```python
import functools
import math

import jax
import jax.numpy as jnp
import numpy as np
from jax import lax
from jax.experimental import pallas as pl
from jax.experimental.pallas import tpu as pltpu

F32 = jnp.float32
BF16 = jnp.bfloat16

EPS = 1e-6
NEG = -1e30
LOG2E = math.log2(math.e)

GRID_W = 64
NA_HEADS = 8
NA_HD = 64
NA_WIN_ROWS = 8
NA_WIN_COLS = 16
NA_WIDTH = NA_HEADS * NA_HD

ML_HEADS = 4
ML_HD = 128
ML_WIDTH = ML_HEADS * ML_HD
ML_CHUNK = 128
ML_GATES = 4 * ML_HEADS

MLA_HEADS = 16
MLA_NOPE = 64
MLA_ROPE = 32
MLA_V = 64
MLA_QK = MLA_NOPE + MLA_ROPE
MLA_Q_RANK = 256
MLA_KV_RANK = 128
ROPE_BASE = 10000.0

FFN_HIDDEN = 2816

LANES = 128
SUBLANES_BF16 = 16
TM = 256
VMEM_LIMIT = 48 * 1024 * 1024


def _cparams(sem):
    return pltpu.CompilerParams(dimension_semantics=sem, vmem_limit_bytes=VMEM_LIMIT)


def _resident(shape):
    nd = len(shape)
    return pl.BlockSpec(shape, lambda *_: (0,) * nd, pipeline_mode=pl.Buffered(1))


def _rms(xf, g):
    ms = jnp.mean(xf * xf, axis=-1, keepdims=True)
    return xf * lax.rsqrt(ms + EPS) * g


def _dot(a, b):
    return jnp.dot(a, b, preferred_element_type=F32)


def _dot_nt(a, b):
    return lax.dot_general(a, b, (((1,), (1,)), ((), ())), preferred_element_type=F32)


def _ada_kernel(c_ref, w_ref, b_ref, o_ref):
    c = c_ref[...]
    s = c * jax.nn.sigmoid(c)
    o_ref[0] = _dot(s.astype(BF16), w_ref[0].astype(BF16)) + b_ref[0]


def _ada_call(cvec, ada_w, ada_b):
    depth, d, n = ada_w.shape
    rows = cvec.shape[0]
    tn = 1024
    return pl.pallas_call(
        _ada_kernel,
        out_shape=jax.ShapeDtypeStruct((depth, rows, n), F32),
        grid=(depth, n // tn),
        in_specs=[
            pl.BlockSpec((rows, d), lambda l, j: (0, 0)),
            pl.BlockSpec((1, d, tn), lambda l, j: (l, 0, j)),
            pl.BlockSpec((1, 1, tn), lambda l, j: (l, 0, j)),
        ],
        out_specs=pl.BlockSpec((1, rows, tn), lambda l, j: (l, 0, j)),
        compiler_params=_cparams(("parallel", "parallel")),
        name="ada_mod",
    )(cvec, ada_w, ada_b.reshape(depth, 1, n))


def _x_spec(d):
    return pl.BlockSpec((1, TM, d), lambda b, i: (b, i, 0))


def _mod_spec(d6, nctx_t):
    return pl.BlockSpec((1, 1, 1, d6), lambda b, i: (b, jnp.where(i >= nctx_t, 1, 0), 0, 0))


def _modulated(x, m, g, d, which):
    o = 3 * d * which
    return _rms(x, g) * (1.0 + m[:, o + d:o + 2 * d]) + m[:, o:o + d]


def _even_in_kernel(x_ref, mod_ref, ng_ref, w_ref, wg_ref, gb_ref,
                    naq_ref, nak_ref, nav_ref, mlqk_ref, mlv_ref, mlo_ref, g_ref, *, d):
    m = mod_ref[0, 0]
    hb = _modulated(x_ref[0], m, ng_ref[...], d, 0).astype(BF16)

    def seg(lo, hi):
        return _dot(hb, w_ref[:, lo:hi])

    w = NA_WIDTH
    naq_ref[0] = (seg(0, w) * (NA_HD ** -0.5 * LOG2E)).astype(BF16)
    nak_ref[0] = seg(w, 2 * w).astype(BF16)
    nav_ref[0] = seg(2 * w, 3 * w).astype(BF16)
    o = 3 * w
    mlqk_ref[0, :, 0:ML_WIDTH] = seg(o, o + ML_WIDTH)
    mlqk_ref[0, :, ML_WIDTH:2 * ML_WIDTH] = seg(o + ML_WIDTH, o + 2 * ML_WIDTH)
    mlv_ref[0] = seg(o + 2 * ML_WIDTH, o + 3 * ML_WIDTH).astype(BF16)
    mlo_ref[0] = seg(o + 3 * ML_WIDTH, o + 4 * ML_WIDTH)
    g_ref[0] = _dot(hb, wg_ref[...]) + gb_ref[...]


def _even_in_call(x_all, modsel, ng, w_main, w_gate, gate_b, nctx_t):
    b, ttot, d = x_all.shape
    nt = ttot // TM
    n_main = w_main.shape[1]
    row = lambda c: pl.BlockSpec((1, TM, c), lambda bb, i: (bb, i, 0))
    sds = lambda c, dt: jax.ShapeDtypeStruct((b, ttot, c), dt)
    return pl.pallas_call(
        functools.partial(_even_in_kernel, d=d),
        out_shape=(sds(NA_WIDTH, BF16), sds(NA_WIDTH, BF16), sds(NA_WIDTH, BF16),
                   sds(2 * ML_WIDTH, F32), sds(ML_WIDTH, BF16), sds(ML_WIDTH, F32),
                   sds(LANES, F32)),
        grid=(b, nt),
        in_specs=[_x_spec(d), _mod_spec(6 * d, nctx_t), _resident((1, d)),
                  _resident((d, n_main)), _resident((d, LANES)), _resident((1, LANES))],
        out_specs=(row(NA_WIDTH), row(NA_WIDTH), row(NA_WIDTH), row(2 * ML_WIDTH),
                   row(ML_WIDTH), row(ML_WIDTH), row(LANES)),
        compiler_params=_cparams(("parallel", "parallel")),
        name="even_in_proj",
    )(x_all, modsel, ng, w_main, w_gate, gate_b)


def _pair_scores(qp, k_parts, bias_parts):
    lane = lax.broadcasted_iota(jnp.int32, qp.shape, 1)
    scores = []
    for hh in range(2):
        keep = (lane >= NA_HD) if hh else (lane < NA_HD)
        qh = jnp.where(keep, qp, jnp.zeros_like(qp))
        s = []
        for kk, bias in zip(k_parts, bias_parts):
            sp = _dot_nt(qh, kk)
            if bias is not None:
                sp = sp + bias[hh]
            s.append(sp)
        scores.append(s)
    return scores


def _pair_finish(scores, v_parts):
    mq = scores[0][0].shape[0]
    lane = lax.broadcasted_iota(jnp.int32, (mq, LANES), 1)
    v_aug = []
    for vv in v_parts:
        ones_col = jnp.where(lax.broadcasted_iota(jnp.int32, vv.shape, 1) == 0, 1.0, 0.0)
        v_aug.append(jnp.concatenate([vv, ones_col.astype(BF16)], axis=-1))
    outs = []
    for s in scores:
        m = functools.reduce(jnp.maximum, [sp.max(axis=-1, keepdims=True) for sp in s])
        o = jnp.zeros((mq, 2 * LANES), F32)
        for sp, vv in zip(s, v_aug):
            o = o + _dot(jnp.exp2(sp - m).astype(BF16), vv)
        outs.append(o[:, 0:LANES] / o[:, LANES:LANES + 1])
    return jnp.where(lane < NA_HD, outs[0], outs[1])


def _na_kernel(q_ref, k_ref, v_ref, bias_ref, o_ref, *, ctx, rows):
    r = pl.program_id(1)
    rs = jnp.clip(r - NA_WIN_ROWS // 2, 0, rows - NA_WIN_ROWS)
    start = pl.multiple_of(ctx + rs * GRID_W, GRID_W)
    band = NA_WIN_ROWS * GRID_W
    pairs = [slice(p * LANES, (p + 1) * LANES) for p in range(NA_HEADS // 2)]
    scores = []
    for p, cs in enumerate(pairs):
        kb = k_ref[0, pl.ds(start, band), cs]
        kc = k_ref[0, 0:ctx, cs]
        bias = (bias_ref[0, 2 * p], bias_ref[0, 2 * p + 1])
        scores.append(_pair_scores(q_ref[0, :, cs], (kb, kc), (bias, None)))
    for sc, cs in zip(scores, pairs):
        vb = v_ref[0, pl.ds(start, band), cs]
        vc = v_ref[0, 0:ctx, cs]
        o_ref[0, :, cs] = _pair_finish(sc, (vb, vc)).astype(BF16)


def _na_call(naq, nak, nav, bias_tab, ctx, t):
    b, ttot, w = naq.shape
    rows = t // GRID_W
    assert rows >= NA_WIN_ROWS and ctx % GRID_W == 0
    q_off = ctx // GRID_W

    def delta(bb, r):
        rs = jnp.clip(r - NA_WIN_ROWS // 2, 0, rows - NA_WIN_ROWS)
        return (rs - r + NA_WIN_ROWS - 1, 0, 0, 0)

    return pl.pallas_call(
        functools.partial(_na_kernel, ctx=ctx, rows=rows),
        out_shape=jax.ShapeDtypeStruct((b, t, w), BF16),
        grid=(b, rows),
        in_specs=[
            pl.BlockSpec((1, GRID_W, w), lambda bb, r: (bb, q_off + r, 0)),
            pl.BlockSpec((1, ttot, w), lambda bb, r: (bb, 0, 0)),
            pl.BlockSpec((1, ttot, w), lambda bb, r: (bb, 0, 0)),
            pl.BlockSpec((1, NA_HEADS, GRID_W, NA_WIN_ROWS * GRID_W), delta),
        ],
        out_specs=pl.BlockSpec((1, GRID_W, w), lambda bb, r: (bb, r, 0)),
        compiler_params=_cparams(("parallel", "arbitrary")),
        name="na_attention",
    )(naq, nak, nav, bias_tab)


def _na_ctx_kernel(q_ref, k_ref, v_ref, o_ref):
    pairs = [slice(p * LANES, (p + 1) * LANES) for p in range(NA_HEADS // 2)]
    scores = [_pair_scores(q_ref[0, :, cs], (k_ref[0, :, cs],), (None,)) for cs in pairs]
    for sc, cs in zip(scores, pairs):
        o_ref[0, :, cs] = _pair_finish(sc, (v_ref[0, :, cs],)).astype(BF16)


def _na_ctx_call(naq, nak, nav, ctx):
    b, _, w = naq.shape
    spec = pl.BlockSpec((1, ctx, w), lambda bb: (bb, 0, 0))
    return pl.pallas_call(
        _na_ctx_kernel,
        out_shape=jax.ShapeDtypeStruct((b, ctx, w), BF16),
        grid=(b,),
        in_specs=[spec, spec, spec],
        out_specs=spec,
        compiler_params=_cparams(("parallel",)),
        name="na_ctx_attention",
    )(naq, nak, nav)


def _na_bias_table(rpb):
    kw = NA_WIN_COLS
    col = jnp.arange(GRID_W)
    cs = jnp.clip(col - kw // 2, 0, GRID_W - kw)
    in_win = (col[None, :] >= cs[:, None]) & (col[None, :] < cs[:, None] + kw)
    dc_idx = jnp.clip(col[None, :] - col[:, None], -(kw - 1), kw - 1) + (NA_WIN_COLS - 1)
    rpb_cols = rpb.astype(F32)[:, :, dc_idx] * LOG2E
    rpb_cols = jnp.where(in_win[None, None], rpb_cols, NEG)
    dr = jnp.arange(NA_WIN_ROWS)[:, None] + jnp.arange(NA_WIN_ROWS)[None, :]
    tab = rpb_cols[:, dr]
    tab = tab.transpose(1, 0, 3, 2, 4)
    return tab.reshape(NA_WIN_ROWS, NA_HEADS, GRID_W, NA_WIN_ROWS * GRID_W)


def _halo_specs(c, rows_h, ttot):
    per = TM // rows_h
    last = ttot // rows_h - 1
    prev = pl.BlockSpec((1, rows_h, c), lambda b, i: (b, jnp.maximum(i * per - 1, 0), 0))
    nxt = pl.BlockSpec((1, rows_h, c), lambda b, i: (b, jnp.minimum((i + 1) * per, last), 0))
    return prev, nxt


def _halo_valid(nctx_t, nt):
    i = pl.program_id(1)
    pv = jnp.where((i == 0) | (i == nctx_t), 0.0, 1.0).astype(F32)
    nv = jnp.where((i == nctx_t - 1) | (i == nt - 1), 0.0, 1.0).astype(F32)
    return pv, nv


def _conv3(u_ref, h, cw, cb, lo, hi):
    return (cb
            + u_ref[h - 1:h - 1 + TM, lo:hi] * cw[0:1]
            + u_ref[h:h + TM, lo:hi] * cw[1:2]
            + u_ref[h + 1:h + 1 + TM, lo:hi] * cw[2:3])


def _ml_conv_kernel(x_ref, xp_ref, xn_ref, cw_ref, cb_ref, sc_ref, o_ref, u_ref, *, nctx_t, nt):
    pv, nv = _halo_valid(nctx_t, nt)
    h = 8
    u_ref[0:h] = xp_ref[0] * pv
    u_ref[h:h + TM] = x_ref[0]
    u_ref[h + TM:h + TM + h] = xn_ref[0] * nv
    c = x_ref.shape[-1]
    y = _conv3(u_ref, h, cw_ref[...], cb_ref[...], 0, c)
    y = y * jax.nn.sigmoid(y) * sc_ref[...]
    o_ref[0] = y.astype(BF16)


def _ml_conv_call(mlqk, conv_w, conv_b, scale, nctx_t):
    b, ttot, c = mlqk.shape
    nt = ttot // TM
    prev, nxt = _halo_specs(c, 8, ttot)
    return pl.pallas_call(
        functools.partial(_ml_conv_kernel, nctx_t=nctx_t, nt=nt),
        out_shape=jax.ShapeDtypeStruct((b, ttot, c), BF16),
        grid=(b, nt),
        in_specs=[_x_spec(c), prev, nxt, _resident((3, c)), _resident((1, c)), _resident((1, c))],
        out_specs=_x_spec(c),
        scratch_shapes=[pltpu.VMEM((TM + 16, c), F32)],
        compiler_params=_cparams(("parallel", "parallel")),
        name="mlstm_conv",
    )(mlqk, mlqk, mlqk, conv_w, conv_b, scale)


def _log_sigmoid(x):
    return jnp.minimum(x, 0.0) - jnp.log(1.0 + jnp.exp(-jnp.abs(x)))


def _split_dot(a, b, data_is_rhs):
    data = b if data_is_rhs else a
    hi = data.astype(BF16)
    lo = (data - hi.astype(F32)).astype(BF16)
    if data_is_rhs:
        return _dot(a, hi) + _dot(a, lo)
    return _dot(hi, b) + _dot(lo, b)


def _mlstm_gate(ch, m_st):
    cum_col, total = ch["cum_col"], ch["total"]
    d_in = jnp.where(ch["mask"], cum_col - ch["cum_row"] + ch["i_row"], NEG)
    m_inter = cum_col + m_st
    m_t = jnp.maximum(m_inter, d_in.max(axis=-1, keepdims=True))
    w = jnp.exp(d_in - m_t)
    a = jnp.exp(m_inter - m_t)
    qk = (ch["s"] * w).astype(BF16)
    hfull = a * ch["hq"] + _dot(qk, ch["v_aug"])
    num = hfull[:, 0:ML_HD]
    den = hfull[:, ML_HD:ML_HD + 1]
    h_out = num / jnp.maximum(jnp.abs(den), jnp.exp(-m_t))
    g = total - cum_col + ch["i_col"]
    m_new = jnp.maximum(total + m_st, g.max(axis=0, keepdims=True))
    decay = jnp.exp(total + m_st - m_new)
    kw = (ch["k"].astype(F32) * jnp.exp(g - m_new)).T.astype(BF16)
    return h_out, kw, decay, m_new


def _mlstm_kernel(qkf_ref, vf_ref, gf_ref, gtf_ref, qkb_ref, vb_ref, gb_ref, gtb_ref,
                  hf_ref, hb_ref, ct_ref, m_ref):
    L = ML_CHUNK

    @pl.when(pl.program_id(1) == 0)
    def _():
        ct_ref[...] = jnp.zeros_like(ct_ref)
        m_ref[...] = jnp.zeros_like(m_ref)

    ri = lax.broadcasted_iota(jnp.int32, (L, L), 0)
    ci = lax.broadcasted_iota(jnp.int32, (L, L), 1)
    tril = jnp.where(ri >= ci, 1.0, 0.0).astype(BF16)
    triu = jnp.where(ri <= ci, 1.0, 0.0).astype(BF16)
    lane = lax.broadcasted_iota(jnp.int32, (L, LANES), 1)
    ones_col = jnp.where(lane == 0, 1.0, 0.0).astype(BF16)

    dirs = (
        (qkf_ref, vf_ref, gf_ref, gtf_ref, hf_ref, tril, triu, ri >= ci, L - 1, 0),
        (qkb_ref, vb_ref, gb_ref, gtb_ref, hb_ref, triu, tril, ri <= ci, 0, 2 * ML_HEADS),
    )
    chains = []
    for dnum, (qk_ref, v_ref, g_ref, gt_ref, h_ref, tcol, trow, mask, tot_row, goff) in enumerate(dirs):
        g = g_ref[0]
        gt = gt_ref[0]
        cum_col_all = _split_dot(tcol, _log_sigmoid(g), True)
        cum_row_all = _split_dot(_log_sigmoid(gt), trow, False)
        for hd in range(ML_HEADS):
            ic = goff + hd
            fc = goff + ML_HEADS + hd
            cs = slice(hd * ML_HD, (hd + 1) * ML_HD)
            j = dnum * ML_HEADS + hd
            q = qk_ref[0, :, cs]
            k = qk_ref[0, :, ML_WIDTH + hd * ML_HD:ML_WIDTH + (hd + 1) * ML_HD]
            ct = ct_ref[j]
            cum_col = cum_col_all[:, fc:fc + 1]
            chains.append(dict(
                j=j, h_ref=h_ref, cs=cs, k=k, ct=ct, mask=mask,
                v_aug=jnp.concatenate([v_ref[0, :, cs], ones_col], axis=-1),
                s=_dot_nt(q, k), hq=_dot(q, ct.astype(BF16)),
                cum_col=cum_col, cum_row=cum_row_all[fc:fc + 1, :],
                i_col=g[:, ic:ic + 1], i_row=gt[ic:ic + 1, :],
                total=cum_col[tot_row:tot_row + 1, :]))
    updates = []
    for ch in chains:
        h_out, kw, decay, m_new = _mlstm_gate(ch, m_ref[ch["j"]][0:1, 0:1])
        ch["h_ref"][0, :, ch["cs"]] = h_out
        updates.append((ch, kw, decay, m_new))
    for ch, kw, decay, m_new in updates:
        ct_ref[ch["j"]] = decay * ch["ct"] + _dot(kw, ch["v_aug"])
        m_ref[ch["j"]] = jnp.broadcast_to(m_new, m_ref.shape[1:])


def _mlstm_call(qk, v, gates, gates_t, nctx_c):
    b, ttot, _ = qk.shape
    nc = ttot // ML_CHUNK
    L = ML_CHUNK

    def fwd(bb, c):
        return (bb, c, 0)

    def bwd(bb, c):
        return (bb, jnp.where(c < nctx_c, nctx_c - 1 - c, nc - 1 + nctx_c - c), 0)

    def fwd_t(bb, c):
        return (bb, 0, c)

    def bwd_t(bb, c):
        return (bb, 0, jnp.where(c < nctx_c, nctx_c - 1 - c, nc - 1 + nctx_c - c))

    ins = []
    for row_map, col_map in ((fwd, fwd_t), (bwd, bwd_t)):
        ins += [pl.BlockSpec((1, L, 2 * ML_WIDTH), row_map),
                pl.BlockSpec((1, L, ML_WIDTH), row_map),
                pl.BlockSpec((1, L, LANES), row_map),
                pl.BlockSpec((1, ML_GATES, L), col_map)]
    out_sds = jax.ShapeDtypeStruct((b, ttot, ML_WIDTH), F32)
    return pl.pallas_call(
        _mlstm_kernel,
        out_shape=(out_sds, out_sds),
        grid=(b, nc),
        in_specs=ins,
        out_specs=(pl.BlockSpec((1, L, ML_WIDTH), fwd), pl.BlockSpec((1, L, ML_WIDTH), bwd)),
        scratch_shapes=[pltpu.VMEM((2 * ML_HEADS, ML_HD, 2 * ML_HD), F32),
                        pltpu.VMEM((2 * ML_HEADS, 8, LANES), F32)],
        compiler_params=_cparams(("parallel", "arbitrary")),
        name="mlstm_scan",
    )(qk, v, gates, gates_t, qk, v, gates, gates_t)


def _out_tail(x_ref, mod_ref, ng_ref, y, o_ref, d):
    m = mod_ref[0, 0]
    o_ref[0] = x_ref[0] + m[:, 2 * d:3 * d] * _rms(y, ng_ref[...])


def _even_out_kernel(x_ref, mod_ref, ng_ref, na_ref, hf_ref, hb_ref, op_ref, mlg_ref, w_ref,
                     o_ref, *, d):
    hs = jax.nn.sigmoid(op_ref[0]) * (hf_ref[0] + hb_ref[0])
    parts = []
    for hd in range(ML_HEADS):
        seg = hs[:, hd * ML_HD:(hd + 1) * ML_HD]
        mu = jnp.mean(seg, axis=-1, keepdims=True)
        cen = seg - mu
        var = jnp.mean(cen * cen, axis=-1, keepdims=True)
        parts.append(cen * lax.rsqrt(var + EPS))
    ml = (jnp.concatenate(parts, axis=-1) * mlg_ref[...]).astype(BF16)
    y = _dot(na_ref[0], w_ref[0:NA_WIDTH, :]) + _dot(ml, w_ref[NA_WIDTH:NA_WIDTH + ML_WIDTH, :])
    _out_tail(x_ref, mod_ref, ng_ref, y, o_ref, d)


def _even_out_call(x_all, modsel, ng, na, hf, hb, mlo, mlg, w_out, nctx_t):
    b, ttot, d = x_all.shape
    row = lambda c: pl.BlockSpec((1, TM, c), lambda bb, i: (bb, i, 0))
    return pl.pallas_call(
        functools.partial(_even_out_kernel, d=d),
        out_shape=jax.ShapeDtypeStruct((b, ttot, d), F32),
        grid=(b, ttot // TM),
        in_specs=[_x_spec(d), _mod_spec(6 * d, nctx_t), _resident((1, d)),
                  row(NA_WIDTH), row(ML_WIDTH), row(ML_WIDTH), row(ML_WIDTH),
                  _resident((1, ML_WIDTH)), _resident(w_out.shape)],
        out_specs=_x_spec(d),
        compiler_params=_cparams(("parallel", "parallel")),
        name="even_out_proj",
    )(x_all, modsel, ng, na, hf, hb, mlo, mlg, w_out)


def _odd_out_kernel(x_ref, mod_ref, ng_ref, a_ref, w_ref, o_ref, *, d):
    _out_tail(x_ref, mod_ref, ng_ref, _dot(a_ref[0], w_ref[...]), o_ref, d)


def _odd_out_call(x_all, modsel, ng, attn, w_o, nctx_t):
    b, ttot, d = x_all.shape
    return pl.pallas_call(
        functools.partial(_odd_out_kernel, d=d),
        out_shape=jax.ShapeDtypeStruct((b, ttot, d), F32),
        grid=(b, ttot // TM),
        in_specs=[_x_spec(d), _mod_spec(6 * d, nctx_t), _resident((1, d)),
                  _x_spec(attn.shape[-1]), _resident(w_o.shape)],
        out_specs=_x_spec(d),
        compiler_params=_cparams(("parallel", "parallel")),
        name="odd_out_proj",
    )(x_all, modsel, ng, attn, w_o)


FFN_CK = 256


def _ffn_kernel(x_ref, xp_ref, xn_ref, mod_ref, ng_in_ref, ng_out_ref, wup_ref, cw_ref, cb_ref,
                wdn_ref, o_ref, h_ref, u_ref, act_ref, *, d, nctx_t, nt):
    pv, nv = _halo_valid(nctx_t, nt)
    m = mod_ref[0, 0]
    g_in = ng_in_ref[...]
    hh = SUBLANES_BF16
    h_ref[0:hh] = (_modulated(xp_ref[0], m, g_in, d, 1) * pv).astype(BF16)
    h_ref[hh:hh + TM] = _modulated(x_ref[0], m, g_in, d, 1).astype(BF16)
    h_ref[hh + TM:hh + TM + hh] = (_modulated(xn_ref[0], m, g_in, d, 1) * nv).astype(BF16)
    hcat = h_ref[...]
    ck = FFN_CK
    for c in range(FFN_HIDDEN // ck):
        lo = c * ck
        glo = FFN_HIDDEN + lo
        u_ref[:, 0:ck] = _dot(hcat, wup_ref[:, lo:lo + ck])
        u_ref[:, ck:2 * ck] = _dot(hcat, wup_ref[:, glo:glo + ck])
        a = _conv3(u_ref, hh, cw_ref[:, lo:lo + ck], cb_ref[:, lo:lo + ck], 0, ck)
        g = _conv3(u_ref, hh, cw_ref[:, glo:glo + ck], cb_ref[:, glo:glo + ck], ck, 2 * ck)
        act_ref[:, lo:lo + ck] = (a * (g * jax.nn.sigmoid(g))).astype(BF16)
    y = _dot(act_ref[...], wdn_ref[...])
    o_ref[0] = x_ref[0] + m[:, 5 * d:6 * d] * _rms(y, ng_out_ref[...])


def _ffn_call(x_all, modsel, ng_in, ng_out, w_up, conv_w, conv_b, w_down, nctx_t):
    b, ttot, d = x_all.shape
    nt = ttot // TM
    hh = SUBLANES_BF16
    prev, nxt = _halo_specs(d, hh, ttot)
    return pl.pallas_call(
        functools.partial(_ffn_kernel, d=d, nctx_t=nctx_t, nt=nt),
        out_shape=jax.ShapeDtypeStruct((b, ttot, d), F32),
        grid=(b, nt),
        in_specs=[_x_spec(d), prev, nxt, _mod_spec(6 * d, nctx_t), _resident((1, d)),
                  _resident((1, d)), _resident(w_up.shape), _resident(conv_w.shape),
                  _resident(conv_b.shape), _resident(w_down.shape)],
        out_specs=_x_spec(d),
        scratch_shapes=[pltpu.VMEM((TM + 2 * hh, d), BF16),
                        pltpu.VMEM((TM + 2 * hh, 2 * FFN_CK), F32),
                        pltpu.VMEM((TM, FFN_HIDDEN), BF16)],
        compiler_params=_cparams(("parallel", "parallel")),
        name="conv_ffn",
    )(x_all, x_all, x_all, modsel, ng_in, ng_out, w_up, conv_w, conv_b, w_down)


MLA_QK_PAD = LANES
MLA_VA = MLA_V + SUBLANES_BF16


def _mla_in_kernel(x_ref, mod_ref, ng_ref, wdq_ref, qg_ref, wuqt_ref, wdkv_ref, kvg_ref,
                   wuk_ref, wuvt_ref, ropeq_ref, ropek_ref, qt_ref, k_ref, vt_ref, *, d):
    m = mod_ref[0, 0]
    hb = _modulated(x_ref[0], m, ng_ref[...], d, 0).astype(BF16)
    cq = _rms(_dot(hb, wdq_ref[...]), qg_ref[...]).astype(BF16)
    qt_all = _dot_nt(wuqt_ref[...], cq)
    rq = ropeq_ref[...]
    cos_r, sin_r, cos_c, sin_c = rq[0:8], rq[8:16], rq[16:24], rq[24:32]
    scale = MLA_QK ** -0.5 * LOG2E
    for hd in range(MLA_HEADS):
        base = hd * MLA_QK_PAD
        nope = qt_all[base:base + MLA_NOPE]
        x1r = qt_all[base + 64:base + 72]
        x2r = qt_all[base + 72:base + 80]
        x1c = qt_all[base + 80:base + 88]
        x2c = qt_all[base + 88:base + 96]
        pad = qt_all[base + 96:base + 128]
        roped = jnp.concatenate([
            nope,
            x1r * cos_r - x2r * sin_r, x1r * sin_r + x2r * cos_r,
            x1c * cos_c - x2c * sin_c, x1c * sin_c + x2c * cos_c,
            pad], axis=0)
        qt_ref[0, hd] = (roped * scale).astype(BF16)
    ckv = _dot(hb, wdkv_ref[...])
    cn = _rms(ckv[:, 0:MLA_KV_RANK], kvg_ref[...]).astype(BF16)
    rk = ropek_ref[...]
    kpe = ckv[:, LANES:2 * LANES] * rk[:, 0:LANES] + ckv[:, 2 * LANES:3 * LANES] * rk[:, LANES:2 * LANES]
    k_all = _dot(cn, wuk_ref[...])
    vt_all = _dot_nt(wuvt_ref[...], cn)
    ones_rows = jnp.where(lax.broadcasted_iota(jnp.int32, (MLA_VA - MLA_V, TM), 0) == 0, 1.0, 0.0)
    for hd in range(MLA_HEADS):
        k_ref[0, hd] = (k_all[:, hd * MLA_QK_PAD:(hd + 1) * MLA_QK_PAD] + kpe).astype(BF16)
        vt_ref[0, hd] = jnp.concatenate(
            [vt_all[hd * MLA_V:(hd + 1) * MLA_V], ones_rows], axis=0).astype(BF16)


def _mla_in_call(x_all, modsel, ng, wdq, qg, wuqt, wdkv, kvg, wuk, wuvt, rope_q, rope_k, nctx_t):
    b, ttot, d = x_all.shape
    nt = ttot // TM
    hn = MLA_HEADS
    return pl.pallas_call(
        functools.partial(_mla_in_kernel, d=d),
        out_shape=(jax.ShapeDtypeStruct((b, hn, MLA_QK_PAD, ttot), BF16),
                   jax.ShapeDtypeStruct((b, hn, ttot, MLA_QK_PAD), BF16),
                   jax.ShapeDtypeStruct((b, hn, MLA_VA, ttot), BF16)),
        grid=(b, nt),
        in_specs=[_x_spec(d), _mod_spec(6 * d, nctx_t), _resident((1, d)),
                  _resident(wdq.shape), _resident(qg.shape), _resident(wuqt.shape),
                  _resident(wdkv.shape), _resident(kvg.shape), _resident(wuk.shape),
                  _resident(wuvt.shape),
                  pl.BlockSpec((32, TM), lambda bb, i: (0, i)),
                  pl.BlockSpec((TM, 2 * LANES), lambda bb, i: (i, 0))],
        out_specs=(pl.BlockSpec((1, hn, MLA_QK_PAD, TM), lambda bb, i: (bb, 0, 0, i)),
                   pl.BlockSpec((1, hn, TM, MLA_QK_PAD), lambda bb, i: (bb, 0, i, 0)),
                   pl.BlockSpec((1, hn, MLA_VA, TM), lambda bb, i: (bb, 0, 0, i))),
        compiler_params=_cparams(("parallel", "parallel")),
        name="mla_in_proj",
    )(x_all, modsel, ng, wdq, qg, wuqt, wdkv, kvg, wuk, wuvt, rope_q, rope_k)


MLA_KC = 256
MLA_LOOKAHEAD = 4


def _mla_attn_kernel(qt_ref, k_ref, vt_ref, o_ref, *, nkc):
    outs = []
    for hh in range(2):
        qt = qt_ref[0, hh]
        ms, accs, scores = [], [], []

        def chunk(c):
            return slice(c * MLA_KC, (c + 1) * MLA_KC)

        for c in range(nkc + MLA_LOOKAHEAD):
            if c < nkc:
                scores.append(_dot(k_ref[0, hh, chunk(c), :], qt))
            cc = c - MLA_LOOKAHEAD
            if cc >= 0:
                s = scores[cc]
                m_c = s.max(axis=0, keepdims=True)
                ms.append(m_c)
                accs.append(_dot(vt_ref[0, hh, :, chunk(cc)], jnp.exp2(s - m_c).astype(BF16)))
        m = functools.reduce(jnp.maximum, ms)
        acc = jnp.zeros((MLA_VA, TM), F32)
        for m_c, a_c in zip(ms, accs):
            acc = acc + jnp.exp2(m_c - m) * a_c
        outs.append(acc[0:MLA_V] / acc[MLA_V:MLA_V + 1])
    o_ref[0] = jnp.concatenate(outs, axis=0).T.astype(BF16)


def _mla_attn_call(qt, k, vt, q_tile0, n_qtiles, kv_len):
    b, hn, _, ttot = qt.shape
    return pl.pallas_call(
        functools.partial(_mla_attn_kernel, nkc=kv_len // MLA_KC),
        out_shape=jax.ShapeDtypeStruct((b, n_qtiles * TM, hn * MLA_V), BF16),
        grid=(b, hn // 2, n_qtiles),
        in_specs=[
            pl.BlockSpec((1, 2, MLA_QK_PAD, TM), lambda bb, hp, qi: (bb, hp, 0, q_tile0 + qi)),
            pl.BlockSpec((1, 2, kv_len, MLA_QK_PAD), lambda bb, hp, qi: (bb, hp, 0, 0)),
            pl.BlockSpec((1, 2, MLA_VA, kv_len), lambda bb, hp, qi: (bb, hp, 0, 0)),
        ],
        out_specs=pl.BlockSpec((1, TM, 2 * MLA_V), lambda bb, hp, qi: (bb, qi, hp)),
        compiler_params=_cparams(("parallel", "parallel", "arbitrary")),
        name="mla_attention",
    )(qt, k, vt)


def _rope_tables(ctx, t):
    pos = np.arange(t)
    half = MLA_ROPE // 4
    inv = ROPE_BASE ** (-jnp.arange(half, dtype=F32) / half)
    tabs = []
    for p in (pos // GRID_W, pos % GRID_W):
        ang = jnp.asarray(p, F32)[:, None] * inv
        cos = jnp.concatenate([jnp.ones((ctx, half), F32), jnp.cos(ang)], axis=0)
        sin = jnp.concatenate([jnp.zeros((ctx, half), F32), jnp.sin(ang)], axis=0)
        tabs.append((cos, sin))
    (cr, sr), (cc, sc) = tabs
    rope_q = jnp.concatenate([cr, sr, cc, sc], axis=1).T
    ttot = ctx + t
    zeros = jnp.zeros((ttot, MLA_NOPE), F32)
    tail = jnp.zeros((ttot, LANES - MLA_QK), F32)
    cos_k = jnp.concatenate([zeros, cr, cr, cc, cc, tail], axis=1)
    sin_k = jnp.concatenate([zeros, -sr, sr, -sc, sc, tail], axis=1)
    rope_k = jnp.concatenate([cos_k, sin_k], axis=1)
    return rope_q, rope_k


def _mla_weights(w_uq, w_dkv, w_ukv):
    hn = MLA_HEADS
    q_rank = w_uq.shape[0]
    wq = w_uq.reshape(q_rank, hn, MLA_QK)
    wq = jnp.pad(wq, ((0, 0), (0, 0), (0, MLA_QK_PAD - MLA_QK)))
    wuqt = wq.reshape(q_rank, hn * MLA_QK_PAD).T.astype(BF16)
    d = w_dkv.shape[0]
    kpe = w_dkv[:, MLA_KV_RANK:]
    h8 = MLA_ROPE // 4
    swap = jnp.concatenate([kpe[:, h8:2 * h8], kpe[:, 0:h8], kpe[:, 3 * h8:4 * h8],
                            kpe[:, 2 * h8:3 * h8]], axis=1)

    def place(cols):
        return jnp.concatenate([jnp.zeros((d, MLA_NOPE), F32), cols,
                                jnp.zeros((d, LANES - MLA_QK), F32)], axis=1)

    wdkv = jnp.concatenate([w_dkv[:, :MLA_KV_RANK], place(kpe), place(swap)], axis=1).astype(BF16)
    wkv = w_ukv.reshape(MLA_KV_RANK, hn, MLA_NOPE + MLA_V)
    wuk = jnp.pad(wkv[:, :, :MLA_NOPE], ((0, 0), (0, 0), (0, MLA_QK_PAD - MLA_NOPE)))
    wuk = wuk.reshape(MLA_KV_RANK, hn * MLA_QK_PAD).astype(BF16)
    wuvt = wkv[:, :, MLA_NOPE:].reshape(MLA_KV_RANK, hn * MLA_V).T.astype(BF16)
    return wuqt, wdkv, wuk, wuvt


def kernel(x, c, ctx, c_ctx, ada_w, ada_b, norm_g, ffn_w_up, ffn_conv_w, ffn_conv_b, ffn_w_down,
           ev_w_in, ev_gate_b, ev_conv_w, ev_conv_b, ev_rpb, ev_ml_norm_g, ev_w_out,
           od_w_dq, od_q_norm_g, od_w_uq, od_w_dkv, od_kv_norm_g, od_w_ukv, od_w_o):
    b, t, d = x.shape
    nctx = ctx.shape[1]
    depth = ada_w.shape[0]
    ttot = nctx + t
    assert t % TM == 0 and nctx % TM == 0 and t % GRID_W == 0
    nctx_t = nctx // TM

    x_all = jnp.concatenate([ctx, x], axis=1)

    rows = -(-(b + 1) // 8) * 8
    cvec = jnp.zeros((rows, d), F32).at[:b].set(c).at[b].set(c_ctx)
    mod = _ada_call(cvec, ada_w, ada_b)

    rope_q, rope_k = _rope_tables(nctx, t)
    ml_scale = jnp.concatenate([jnp.full((1, ML_WIDTH), ML_HD ** -0.5, F32),
                                jnp.ones((1, ML_WIDTH), F32)], axis=1)

    for l in range(depth):
        ctx_out = l < depth - 1
        modsel = jnp.stack([jnp.broadcast_to(mod[l, b], (b, 6 * d)), mod[l, :b]], axis=1)
        modsel = modsel.reshape(b, 2, 1, 6 * d)
        ng = norm_g[l].reshape(4, 1, d)
        if l % 2 == 0:
            e = l // 2
            w_in = ev_w_in[e]
            n_main = 3 * NA_WIDTH + 4 * ML_WIDTH
            w_gate = jnp.pad(w_in[:, n_main:], ((0, 0), (0, LANES - ML_GATES))).astype(BF16)
            gate_b = jnp.pad(ev_gate_b[e], (0, LANES - ML_GATES)).reshape(1, LANES)
            naq, nak, nav, mlqk, mlv, mlo, gates = _even_in_call(
                x_all, modsel, ng[0], w_in[:, :n_main].astype(BF16), w_gate, gate_b, nctx_t)
            na_x = _na_call(naq, nak, nav, _na_bias_table(ev_rpb[e]), nctx, t)
            na_c = _na_ctx_call(naq, nak, nav, nctx)
            na = jnp.concatenate([na_c, na_x], axis=1)
            qk = _ml_conv_call(mlqk, ev_conv_w[e], ev_conv_b[e].reshape(1, -1), ml_scale, nctx_t)
            gates_t = jnp.swapaxes(gates[:, :, :ML_GATES], 1, 2)
            hf, hb = _mlstm_call(qk, mlv, gates, gates_t, nctx // ML_CHUNK)
            x_all = _even_out_call(x_all, modsel, ng[1], na, hf, hb, mlo,
                                   ev_ml_norm_g[e].reshape(1, -1), ev_w_out[e].astype(BF16), nctx_t)
        else:
            o = l // 2
            wuqt, wdkv, wuk, wuvt = _mla_weights(od_w_uq[o], od_w_dkv[o], od_w_ukv[o])
            qt, kk, vt = _mla_in_call(
                x_all, modsel, ng[0], od_w_dq[o].astype(BF16), od_q_norm_g[o].reshape(1, -1),
                wuqt, wdkv, od_kv_norm_g[o].reshape(1, -1), wuk, wuvt, rope_q, rope_k, nctx_t)
            att_x = _mla_attn_call(qt, kk, vt, nctx_t, t // TM, ttot)
            if ctx_out:
                att_c = _mla_attn_call(qt, kk, vt, 0, nctx_t, nctx)
            else:
                att_c = jnp.zeros((b, nctx, att_x.shape[-1]), BF16)
            att = jnp.concatenate([att_c, att_x], axis=1)
            x_all = _odd_out_call(x_all, modsel, ng[1], att, od_w_o[o].astype(BF16), nctx_t)
        x_all = _ffn_call(x_all, modsel, ng[2], ng[3], ffn_w_up[l].astype(BF16), ffn_conv_w[l],
                          ffn_conv_b[l].reshape(1, -1), ffn_w_down[l].astype(BF16), nctx_t)
    return x_all[:, nctx:]
```

```python
import functools
import math

import jax
import jax.numpy as jnp
import numpy as np
from jax import lax
from jax.experimental import pallas as pl
from jax.experimental.pallas import tpu as pltpu

F32 = jnp.float32
BF16 = jnp.bfloat16

EPS = 1e-6
NEG = -1e30
LOG2E = math.log2(math.e)

GRID_W = 64
NA_HEADS = 8
NA_HD = 64
NA_WIN_ROWS = 8
NA_WIN_COLS = 16
NA_WIDTH = NA_HEADS * NA_HD

ML_HEADS = 4
ML_HD = 128
ML_WIDTH = ML_HEADS * ML_HD
ML_CHUNK = 128
ML_GATES = 4 * ML_HEADS

MLA_HEADS = 16
MLA_NOPE = 64
MLA_ROPE = 32
MLA_V = 64
MLA_QK = MLA_NOPE + MLA_ROPE
MLA_Q_RANK = 256
MLA_KV_RANK = 128
ROPE_BASE = 10000.0

FFN_HIDDEN = 2816

LANES = 128
SUBLANES_BF16 = 16
TM = 256
VMEM_LIMIT = 48 * 1024 * 1024


def _cparams(sem):
    return pltpu.CompilerParams(dimension_semantics=sem, vmem_limit_bytes=VMEM_LIMIT)


def _resident(shape):
    nd = len(shape)
    return pl.BlockSpec(shape, lambda *_: (0,) * nd, pipeline_mode=pl.Buffered(1))


def _rms(xf, g):
    ms = jnp.mean(xf * xf, axis=-1, keepdims=True)
    return xf * lax.rsqrt(ms + EPS) * g


def _dot(a, b):
    return jnp.dot(a, b, preferred_element_type=F32)


def _dot_nt(a, b):
    return lax.dot_general(a, b, (((1,), (1,)), ((), ())), preferred_element_type=F32)


def _ada_kernel(c_ref, w_ref, b_ref, o_ref):
    c = c_ref[...]
    s = c * jax.nn.sigmoid(c)
    o_ref[0] = _dot(s.astype(BF16), w_ref[0].astype(BF16)) + b_ref[0]


def _ada_call(cvec, ada_w, ada_b):
    depth, d, n = ada_w.shape
    rows = cvec.shape[0]
    tn = 1024
    return pl.pallas_call(
        _ada_kernel,
        out_shape=jax.ShapeDtypeStruct((depth, rows, n), F32),
        grid=(depth, n // tn),
        in_specs=[
            pl.BlockSpec((rows, d), lambda l, j: (0, 0)),
            pl.BlockSpec((1, d, tn), lambda l, j: (l, 0, j)),
            pl.BlockSpec((1, 1, tn), lambda l, j: (l, 0, j)),
        ],
        out_specs=pl.BlockSpec((1, rows, tn), lambda l, j: (l, 0, j)),
        compiler_params=_cparams(("parallel", "parallel")),
        name="ada_mod",
    )(cvec, ada_w, ada_b.reshape(depth, 1, n))


def _x_spec(d):
    return pl.BlockSpec((1, TM, d), lambda b, i: (b, i, 0))


def _mod_spec(d6, nctx_t):
    return pl.BlockSpec((1, 1, 1, d6), lambda b, i: (b, jnp.where(i >= nctx_t, 1, 0), 0, 0))


def _modulated(x, m, g, d, which):
    o = 3 * d * which
    return _rms(x, g) * (1.0 + m[:, o + d:o + 2 * d]) + m[:, o:o + d]


def _even_in_kernel(x_ref, mod_ref, ng_ref, w_ref, wg_ref, gb_ref,
                    naq_ref, nak_ref, nav_ref, mlqk_ref, mlv_ref, mlo_ref, g_ref, *, d):
    m = mod_ref[0, 0]
    hb = _modulated(x_ref[0], m, ng_ref[...], d, 0).astype(BF16)

    def seg(lo, hi):
        return _dot(hb, w_ref[:, lo:hi])

    w = NA_WIDTH
    naq_ref[0] = (seg(0, w) * (NA_HD ** -0.5 * LOG2E)).astype(BF16)
    nak_ref[0] = seg(w, 2 * w).astype(BF16)
    nav_ref[0] = seg(2 * w, 3 * w).astype(BF16)
    o = 3 * w
    mlqk_ref[0, :, 0:ML_WIDTH] = seg(o, o + ML_WIDTH)
    mlqk_ref[0, :, ML_WIDTH:2 * ML_WIDTH] = seg(o + ML_WIDTH, o + 2 * ML_WIDTH)
    mlv_ref[0] = seg(o + 2 * ML_WIDTH, o + 3 * ML_WIDTH).astype(BF16)
    mlo_ref[0] = seg(o + 3 * ML_WIDTH, o + 4 * ML_WIDTH)
    g_ref[0] = _dot(hb, wg_ref[...]) + gb_ref[...]


def _even_in_call(x_all, modsel, ng, w_main, w_gate, gate_b, nctx_t):
    b, ttot, d = x_all.shape
    nt = ttot // TM
    n_main = w_main.shape[1]
    row = lambda c: pl.BlockSpec((1, TM, c), lambda bb, i: (bb, i, 0))
    sds = lambda c, dt: jax.ShapeDtypeStruct((b, ttot, c), dt)
    return pl.pallas_call(
        functools.partial(_even_in_kernel, d=d),
        out_shape=(sds(NA_WIDTH, BF16), sds(NA_WIDTH, BF16), sds(NA_WIDTH, BF16),
                   sds(2 * ML_WIDTH, F32), sds(ML_WIDTH, BF16), sds(ML_WIDTH, F32),
                   sds(LANES, F32)),
        grid=(b, nt),
        in_specs=[_x_spec(d), _mod_spec(6 * d, nctx_t), _resident((1, d)),
                  _resident((d, n_main)), _resident((d, LANES)), _resident((1, LANES))],
        out_specs=(row(NA_WIDTH), row(NA_WIDTH), row(NA_WIDTH), row(2 * ML_WIDTH),
                   row(ML_WIDTH), row(ML_WIDTH), row(LANES)),
        compiler_params=_cparams(("parallel", "parallel")),
        name="even_in_proj",
    )(x_all, modsel, ng, w_main, w_gate, gate_b)


def _pair_scores(qp, k_parts, bias_parts):
    lane = lax.broadcasted_iota(jnp.int32, qp.shape, 1)
    scores = []
    for hh in range(2):
        keep = (lane >= NA_HD) if hh else (lane < NA_HD)
        qh = jnp.where(keep, qp, jnp.zeros_like(qp))
        s = []
        for kk, bias in zip(k_parts, bias_parts):
            sp = _dot_nt(qh, kk)
            if bias is not None:
                sp = sp + bias[hh]
            s.append(sp)
        scores.append(s)
    return scores


def _pair_finish(scores, v_parts):
    mq = scores[0][0].shape[0]
    lane = lax.broadcasted_iota(jnp.int32, (mq, LANES), 1)
    v_aug = []
    for vv in v_parts:
        ones_col = jnp.where(lax.broadcasted_iota(jnp.int32, vv.shape, 1) == 0, 1.0, 0.0)
        v_aug.append(jnp.concatenate([vv, ones_col.astype(BF16)], axis=-1))
    outs = []
    for s in scores:
        m = functools.reduce(jnp.maximum, [sp.max(axis=-1, keepdims=True) for sp in s])
        o = jnp.zeros((mq, 2 * LANES), F32)
        for sp, vv in zip(s, v_aug):
            o = o + _dot(jnp.exp2(sp - m).astype(BF16), vv)
        outs.append(o[:, 0:LANES] / o[:, LANES:LANES + 1])
    return jnp.where(lane < NA_HD, outs[0], outs[1])


def _na_kernel(q_ref, k_ref, v_ref, bias_ref, o_ref, *, ctx, rows):
    r = pl.program_id(1)
    rs = jnp.clip(r - NA_WIN_ROWS // 2, 0, rows - NA_WIN_ROWS)
    start = pl.multiple_of(ctx + rs * GRID_W, GRID_W)
    band = NA_WIN_ROWS * GRID_W
    pairs = [slice(p * LANES, (p + 1) * LANES) for p in range(NA_HEADS // 2)]
    scores = []
    for p, cs in enumerate(pairs):
        kb = k_ref[0, pl.ds(start, band), cs]
        kc = k_ref[0, 0:ctx, cs]
        bias = (bias_ref[0, 2 * p], bias_ref[0, 2 * p + 1])
        scores.append(_pair_scores(q_ref[0, :, cs], (kb, kc), (bias, None)))
    for sc, cs in zip(scores, pairs):
        vb = v_ref[0, pl.ds(start, band), cs]
        vc = v_ref[0, 0:ctx, cs]
        o_ref[0, :, cs] = _pair_finish(sc, (vb, vc)).astype(BF16)


def _na_call(naq, nak, nav, bias_tab, ctx, t):
    b, ttot, w = naq.shape
    rows = t // GRID_W
    assert rows >= NA_WIN_ROWS and ctx % GRID_W == 0
    q_off = ctx // GRID_W

    def delta(bb, r):
        rs = jnp.clip(r - NA_WIN_ROWS // 2, 0, rows - NA_WIN_ROWS)
        return (rs - r + NA_WIN_ROWS - 1, 0, 0, 0)

    return pl.pallas_call(
        functools.partial(_na_kernel, ctx=ctx, rows=rows),
        out_shape=jax.ShapeDtypeStruct((b, t, w), BF16),
        grid=(b, rows),
        in_specs=[
            pl.BlockSpec((1, GRID_W, w), lambda bb, r: (bb, q_off + r, 0)),
            pl.BlockSpec((1, ttot, w), lambda bb, r: (bb, 0, 0)),
            pl.BlockSpec((1, ttot, w), lambda bb, r: (bb, 0, 0)),
            pl.BlockSpec((1, NA_HEADS, GRID_W, NA_WIN_ROWS * GRID_W), delta),
        ],
        out_specs=pl.BlockSpec((1, GRID_W, w), lambda bb, r: (bb, r, 0)),
        compiler_params=_cparams(("parallel", "arbitrary")),
        name="na_attention",
    )(naq, nak, nav, bias_tab)


def _na_ctx_kernel(q_ref, k_ref, v_ref, o_ref):
    pairs = [slice(p * LANES, (p + 1) * LANES) for p in range(NA_HEADS // 2)]
    scores = [_pair_scores(q_ref[0, :, cs], (k_ref[0, :, cs],), (None,)) for cs in pairs]
    for sc, cs in zip(scores, pairs):
        o_ref[0, :, cs] = _pair_finish(sc, (v_ref[0, :, cs],)).astype(BF16)


def _na_ctx_call(naq, nak, nav, ctx):
    b, _, w = naq.shape
    spec = pl.BlockSpec((1, ctx, w), lambda bb: (bb, 0, 0))
    return pl.pallas_call(
        _na_ctx_kernel,
        out_shape=jax.ShapeDtypeStruct((b, ctx, w), BF16),
        grid=(b,),
        in_specs=[spec, spec, spec],
        out_specs=spec,
        compiler_params=_cparams(("parallel",)),
        name="na_ctx_attention",
    )(naq, nak, nav)


def _na_bias_table(rpb):
    kw = NA_WIN_COLS
    col = jnp.arange(GRID_W)
    cs = jnp.clip(col - kw // 2, 0, GRID_W - kw)
    in_win = (col[None, :] >= cs[:, None]) & (col[None, :] < cs[:, None] + kw)
    dc_idx = jnp.clip(col[None, :] - col[:, None], -(kw - 1), kw - 1) + (NA_WIN_COLS - 1)
    rpb_cols = rpb.astype(F32)[:, :, dc_idx] * LOG2E
    rpb_cols = jnp.where(in_win[None, None], rpb_cols, NEG)
    dr = jnp.arange(NA_WIN_ROWS)[:, None] + jnp.arange(NA_WIN_ROWS)[None, :]
    tab = rpb_cols[:, dr]
    tab = tab.transpose(1, 0, 3, 2, 4)
    return tab.reshape(NA_WIN_ROWS, NA_HEADS, GRID_W, NA_WIN_ROWS * GRID_W)


def _halo_specs(c, rows_h, ttot):
    per = TM // rows_h
    last = ttot // rows_h - 1
    prev = pl.BlockSpec((1, rows_h, c), lambda b, i: (b, jnp.maximum(i * per - 1, 0), 0))
    nxt = pl.BlockSpec((1, rows_h, c), lambda b, i: (b, jnp.minimum((i + 1) * per, last), 0))
    return prev, nxt


def _halo_valid(nctx_t, nt):
    i = pl.program_id(1)
    pv = jnp.where((i == 0) | (i == nctx_t), 0.0, 1.0).astype(F32)
    nv = jnp.where((i == nctx_t - 1) | (i == nt - 1), 0.0, 1.0).astype(F32)
    return pv, nv


def _conv3(u_ref, h, cw, cb, lo, hi):
    return (cb
            + u_ref[h - 1:h - 1 + TM, lo:hi] * cw[0:1]
            + u_ref[h:h + TM, lo:hi] * cw[1:2]
            + u_ref[h + 1:h + 1 + TM, lo:hi] * cw[2:3])


def _ml_conv_kernel(x_ref, xp_ref, xn_ref, cw_ref, cb_ref, sc_ref, o_ref, kt_ref, u_ref, *, nctx_t, nt):
    pv, nv = _halo_valid(nctx_t, nt)
    h = 8
    u_ref[0:h] = xp_ref[0] * pv
    u_ref[h:h + TM] = x_ref[0]
    u_ref[h + TM:h + TM + h] = xn_ref[0] * nv
    c = x_ref.shape[-1]
    y = _conv3(u_ref, h, cw_ref[...], cb_ref[...], 0, c)
    y = y * jax.nn.sigmoid(y) * sc_ref[...]
    o_ref[0] = y.astype(BF16)
    kt_ref[0] = y[:, ML_WIDTH:2 * ML_WIDTH].T.astype(BF16)


def _ml_conv_call(mlqk, conv_w, conv_b, scale, nctx_t):
    b, ttot, c = mlqk.shape
    nt = ttot // TM
    prev, nxt = _halo_specs(c, 8, ttot)
    return pl.pallas_call(
        functools.partial(_ml_conv_kernel, nctx_t=nctx_t, nt=nt),
        out_shape=(jax.ShapeDtypeStruct((b, ttot, c), BF16),
                   jax.ShapeDtypeStruct((b, ML_WIDTH, ttot), BF16)),
        grid=(b, nt),
        in_specs=[_x_spec(c), prev, nxt, _resident((3, c)), _resident((1, c)), _resident((1, c))],
        out_specs=(_x_spec(c), pl.BlockSpec((1, ML_WIDTH, TM), lambda bb, i: (bb, 0, i))),
        scratch_shapes=[pltpu.VMEM((TM + 16, c), F32)],
        compiler_params=_cparams(("parallel", "parallel")),
        name="mlstm_conv",
    )(mlqk, mlqk, mlqk, conv_w, conv_b, scale)


def _log_sigmoid(x):
    return jnp.minimum(x, 0.0) - jnp.log(1.0 + jnp.exp(-jnp.abs(x)))


def _split_dot(a, b, data_is_rhs):
    data = b if data_is_rhs else a
    hi = data.astype(BF16)
    lo = (data - hi.astype(F32)).astype(BF16)
    if data_is_rhs:
        return _dot(a, hi) + _dot(a, lo)
    return _dot(hi, b) + _dot(lo, b)


def _mlstm_kernel(qkf_ref, ktf_ref, vf_ref, gf_ref, gtf_ref, qkb_ref, ktb_ref, vb_ref, gb_ref, gtb_ref,
                  hf_ref, hb_ref, ct_ref, m_ref):
    L = ML_CHUNK

    @pl.when(pl.program_id(1) == 0)
    def _():
        ct_ref[...] = jnp.zeros_like(ct_ref)
        m_ref[...] = jnp.zeros_like(m_ref)

    ri = lax.broadcasted_iota(jnp.int32, (L, L), 0)
    ci = lax.broadcasted_iota(jnp.int32, (L, L), 1)
    tril = jnp.where(ri >= ci, 1.0, 0.0).astype(BF16)
    triu = jnp.where(ri <= ci, 1.0, 0.0).astype(BF16)
    lane = lax.broadcasted_iota(jnp.int32, (L, LANES), 1)
    ones_col = jnp.where(lane == 0, 1.0, 0.0).astype(BF16)

    dirs = (
        (qkf_ref, ktf_ref, vf_ref, gf_ref, gtf_ref, hf_ref, tril, triu, ri >= ci, L - 1, 0),
        (qkb_ref, ktb_ref, vb_ref, gb_ref, gtb_ref, hb_ref, triu, tril, ri <= ci, 0, 2 * ML_HEADS),
    )
    chains = []
    for dnum, (qk_ref, kt_ref, v_ref, g_ref, gt_ref, h_ref, tcol, trow, mask, tot_row, goff) in enumerate(dirs):
        gt = gt_ref[0]
        cum_col_all = _split_dot(tcol, _log_sigmoid(g_ref[0]), True)
        cum_row_all = _split_dot(_log_sigmoid(gt), trow, False)
        for hd in range(ML_HEADS):
            ic = goff + hd
            fc = goff + ML_HEADS + hd
            cs = slice(hd * ML_HD, (hd + 1) * ML_HD)
            j = dnum * ML_HEADS + hd
            q = qk_ref[0, :, cs]
            k = qk_ref[0, :, ML_WIDTH + hd * ML_HD:ML_WIDTH + (hd + 1) * ML_HD]
            ct = ct_ref[j]
            cum_col = cum_col_all[:, fc:fc + 1]
            chains.append(dict(
                j=j, h_ref=h_ref, cs=cs, kt=kt_ref[0, cs, :], ct=ct, mask=mask,
                v_aug=jnp.concatenate([v_ref[0, :, cs], ones_col], axis=-1),
                s=_dot_nt(q, k), hq=_dot(q, ct.astype(BF16)),
                cum_col=cum_col, cum_row=cum_row_all[fc:fc + 1, :],
                i_row=gt[ic:ic + 1, :], total=cum_col[tot_row:tot_row + 1, :]))
    for ch in chains:
        m_st = m_ref[ch["j"]][0:1, 0:1]
        cum_col, total = ch["cum_col"], ch["total"]
        e_row = ch["i_row"] - ch["cum_row"]
        base = jnp.where(ch["mask"], e_row, NEG)
        m_inter = cum_col + m_st
        m_t = jnp.maximum(m_inter, cum_col + base.max(axis=-1, keepdims=True))
        g_row = total + e_row
        m_new = jnp.maximum(total + m_st, g_row.max(axis=-1, keepdims=True))
        ch.update(base=base, m_inter=m_inter, m_t=m_t, g_row=g_row, m_new=m_new,
                  decay=jnp.exp(total + m_st - m_new))
    for ch in chains:
        ch["qk"] = (ch["s"] * jnp.exp(ch["base"] + (ch["cum_col"] - ch["m_t"]))).astype(BF16)
        ch["a"] = jnp.exp(ch["m_inter"] - ch["m_t"])
        ch["floor"] = jnp.exp(-ch["m_t"])
        ch["kw"] = (ch["kt"].astype(F32) * jnp.exp(ch["g_row"] - ch["m_new"])).astype(BF16)
    for ch in chains:
        ch["p"] = _dot(ch["qk"], ch["v_aug"])
    for ch in chains:
        ch["u"] = _dot(ch["kw"], ch["v_aug"])
    for ch in chains:
        hfull = ch["a"] * ch["hq"] + ch["p"]
        den = hfull[:, ML_HD:ML_HD + 1]
        ch["h_ref"][0, :, ch["cs"]] = hfull[:, 0:ML_HD] / jnp.maximum(jnp.abs(den), ch["floor"])
    for ch in chains:
        ct_ref[ch["j"]] = ch["decay"] * ch["ct"] + ch["u"]
        m_ref[ch["j"]] = jnp.broadcast_to(ch["m_new"], m_ref.shape[1:])


def _mlstm_call(qk, kt, v, gates, gates_t, nctx_c):
    b, ttot, _ = qk.shape
    nc = ttot // ML_CHUNK
    L = ML_CHUNK

    def fwd(bb, c):
        return (bb, c, 0)

    def bwd(bb, c):
        return (bb, jnp.where(c < nctx_c, nctx_c - 1 - c, nc - 1 + nctx_c - c), 0)

    def fwd_t(bb, c):
        return (bb, 0, c)

    def bwd_t(bb, c):
        return (bb, 0, jnp.where(c < nctx_c, nctx_c - 1 - c, nc - 1 + nctx_c - c))

    ins = []
    for row_map, col_map in ((fwd, fwd_t), (bwd, bwd_t)):
        ins += [pl.BlockSpec((1, L, 2 * ML_WIDTH), row_map),
                pl.BlockSpec((1, ML_WIDTH, L), col_map),
                pl.BlockSpec((1, L, ML_WIDTH), row_map),
                pl.BlockSpec((1, L, LANES), row_map),
                pl.BlockSpec((1, ML_GATES, L), col_map)]
    out_sds = jax.ShapeDtypeStruct((b, ttot, ML_WIDTH), F32)
    return pl.pallas_call(
        _mlstm_kernel,
        out_shape=(out_sds, out_sds),
        grid=(b, nc),
        in_specs=ins,
        out_specs=(pl.BlockSpec((1, L, ML_WIDTH), fwd), pl.BlockSpec((1, L, ML_WIDTH), bwd)),
        scratch_shapes=[pltpu.VMEM((2 * ML_HEADS, ML_HD, 2 * ML_HD), F32),
                        pltpu.VMEM((2 * ML_HEADS, 8, LANES), F32)],
        compiler_params=_cparams(("parallel", "arbitrary")),
        name="mlstm_scan",
    )(qk, kt, v, gates, gates_t, qk, kt, v, gates, gates_t)


def _out_tail(x_ref, mod_ref, ng_ref, y, o_ref, d):
    m = mod_ref[0, 0]
    o_ref[0] = x_ref[0] + m[:, 2 * d:3 * d] * _rms(y, ng_ref[...])


def _even_out_kernel(x_ref, mod_ref, ng_ref, na_ref, hf_ref, hb_ref, op_ref, mlg_ref, w_ref,
                     o_ref, *, d):
    hs = jax.nn.sigmoid(op_ref[0]) * (hf_ref[0] + hb_ref[0])
    parts = []
    for hd in range(ML_HEADS):
        seg = hs[:, hd * ML_HD:(hd + 1) * ML_HD]
        mu = jnp.mean(seg, axis=-1, keepdims=True)
        cen = seg - mu
        var = jnp.mean(cen * cen, axis=-1, keepdims=True)
        parts.append(cen * lax.rsqrt(var + EPS))
    ml = (jnp.concatenate(parts, axis=-1) * mlg_ref[...]).astype(BF16)
    y = _dot(na_ref[0], w_ref[0:NA_WIDTH, :]) + _dot(ml, w_ref[NA_WIDTH:NA_WIDTH + ML_WIDTH, :])
    _out_tail(x_ref, mod_ref, ng_ref, y, o_ref, d)


def _even_out_call(x_all, modsel, ng, na, hf, hb, mlo, mlg, w_out, nctx_t):
    b, ttot, d = x_all.shape
    row = lambda c: pl.BlockSpec((1, TM, c), lambda bb, i: (bb, i, 0))
    return pl.pallas_call(
        functools.partial(_even_out_kernel, d=d),
        out_shape=jax.ShapeDtypeStruct((b, ttot, d), F32),
        grid=(b, ttot // TM),
        in_specs=[_x_spec(d), _mod_spec(6 * d, nctx_t), _resident((1, d)),
                  row(NA_WIDTH), row(ML_WIDTH), row(ML_WIDTH), row(ML_WIDTH),
                  _resident((1, ML_WIDTH)), _resident(w_out.shape)],
        out_specs=_x_spec(d),
        compiler_params=_cparams(("parallel", "parallel")),
        name="even_out_proj",
    )(x_all, modsel, ng, na, hf, hb, mlo, mlg, w_out)


def _odd_out_kernel(x_ref, mod_ref, ng_ref, a_ref, w_ref, o_ref, *, d):
    _out_tail(x_ref, mod_ref, ng_ref, _dot(a_ref[0], w_ref[...]), o_ref, d)


def _odd_out_call(x_all, modsel, ng, attn, w_o, nctx_t):
    b, ttot, d = x_all.shape
    return pl.pallas_call(
        functools.partial(_odd_out_kernel, d=d),
        out_shape=jax.ShapeDtypeStruct((b, ttot, d), F32),
        grid=(b, ttot // TM),
        in_specs=[_x_spec(d), _mod_spec(6 * d, nctx_t), _resident((1, d)),
                  _x_spec(attn.shape[-1]), _resident(w_o.shape)],
        out_specs=_x_spec(d),
        compiler_params=_cparams(("parallel", "parallel")),
        name="odd_out_proj",
    )(x_all, modsel, ng, attn, w_o)


FFN_CK = 256


def _ffn_kernel(x_ref, xp_ref, xn_ref, mod_ref, ng_in_ref, ng_out_ref, wup_ref, cw_ref, cb_ref,
                wdn_ref, o_ref, h_ref, u_ref, act_ref, *, d, nctx_t, nt):
    pv, nv = _halo_valid(nctx_t, nt)
    m = mod_ref[0, 0]
    g_in = ng_in_ref[...]
    hh = SUBLANES_BF16
    h_ref[0:hh] = (_modulated(xp_ref[0], m, g_in, d, 1) * pv).astype(BF16)
    h_ref[hh:hh + TM] = _modulated(x_ref[0], m, g_in, d, 1).astype(BF16)
    h_ref[hh + TM:hh + TM + hh] = (_modulated(xn_ref[0], m, g_in, d, 1) * nv).astype(BF16)
    hcat = h_ref[...]
    ck = FFN_CK
    nslab = ck // LANES

    def conv_slab(s, col):
        cw = cw_ref[:, col:col + LANES]
        return (cb_ref[:, col:col + LANES]
                + u_ref[s, hh - 1:hh - 1 + TM, :] * cw[0:1]
                + u_ref[s, hh:hh + TM, :] * cw[1:2]
                + u_ref[s, hh + 1:hh + 1 + TM, :] * cw[2:3])

    for c in range(FFN_HIDDEN // ck):
        lo = c * ck
        glo = FFN_HIDDEN + lo
        ua = _dot(hcat, wup_ref[:, lo:lo + ck])
        ug = _dot(hcat, wup_ref[:, glo:glo + ck])
        for s in range(nslab):
            u_ref[s] = ua[:, s * LANES:(s + 1) * LANES]
            u_ref[nslab + s] = ug[:, s * LANES:(s + 1) * LANES]
        for s in range(nslab):
            a = conv_slab(s, lo + s * LANES)
            g = conv_slab(nslab + s, glo + s * LANES)
            act_ref[:, lo + s * LANES:lo + (s + 1) * LANES] = (a * (g * jax.nn.sigmoid(g))).astype(BF16)
    y = _dot(act_ref[...], wdn_ref[...])
    o_ref[0] = x_ref[0] + m[:, 5 * d:6 * d] * _rms(y, ng_out_ref[...])


def _ffn_call(x_all, modsel, ng_in, ng_out, w_up, conv_w, conv_b, w_down, nctx_t):
    b, ttot, d = x_all.shape
    nt = ttot // TM
    hh = SUBLANES_BF16
    prev, nxt = _halo_specs(d, hh, ttot)
    return pl.pallas_call(
        functools.partial(_ffn_kernel, d=d, nctx_t=nctx_t, nt=nt),
        out_shape=jax.ShapeDtypeStruct((b, ttot, d), F32),
        grid=(b, nt),
        in_specs=[_x_spec(d), prev, nxt, _mod_spec(6 * d, nctx_t), _resident((1, d)),
                  _resident((1, d)), _resident(w_up.shape), _resident(conv_w.shape),
                  _resident(conv_b.shape), _resident(w_down.shape)],
        out_specs=_x_spec(d),
        scratch_shapes=[pltpu.VMEM((TM + 2 * hh, d), BF16),
                        pltpu.VMEM((2 * FFN_CK // LANES, TM + 2 * hh, LANES), F32),
                        pltpu.VMEM((TM, FFN_HIDDEN), BF16)],
        compiler_params=_cparams(("parallel", "parallel")),
        name="conv_ffn",
    )(x_all, x_all, x_all, modsel, ng_in, ng_out, w_up, conv_w, conv_b, w_down)


MLA_QK_PAD = LANES
MLA_VA = MLA_V + SUBLANES_BF16


def _mla_in_kernel(x_ref, mod_ref, ng_ref, wdq_ref, qg_ref, wuqt_ref, wdkv_ref, kvg_ref,
                   wuk_ref, wuvt_ref, ropeq_ref, ropek_ref, qt_ref, k_ref, vt_ref, *, d):
    m = mod_ref[0, 0]
    hb = _modulated(x_ref[0], m, ng_ref[...], d, 0).astype(BF16)
    cq = _rms(_dot(hb, wdq_ref[...]), qg_ref[...]).astype(BF16)
    qt_all = _dot_nt(wuqt_ref[...], cq)
    rq = ropeq_ref[...]
    cos_r, sin_r, cos_c, sin_c = rq[0:8], rq[8:16], rq[16:24], rq[24:32]
    scale = MLA_QK ** -0.5 * LOG2E
    for hd in range(MLA_HEADS):
        base = hd * MLA_QK_PAD
        nope = qt_all[base:base + MLA_NOPE]
        x1r = qt_all[base + 64:base + 72]
        x2r = qt_all[base + 72:base + 80]
        x1c = qt_all[base + 80:base + 88]
        x2c = qt_all[base + 88:base + 96]
        pad = qt_all[base + 96:base + 128]
        roped = jnp.concatenate([
            nope,
            x1r * cos_r - x2r * sin_r, x1r * sin_r + x2r * cos_r,
            x1c * cos_c - x2c * sin_c, x1c * sin_c + x2c * cos_c,
            pad], axis=0)
        qt_ref[0, hd] = (roped * scale).astype(BF16)
    ckv = _dot(hb, wdkv_ref[...])
    cn = _rms(ckv[:, 0:MLA_KV_RANK], kvg_ref[...]).astype(BF16)
    rk = ropek_ref[...]
    kpe = ckv[:, LANES:2 * LANES] * rk[:, 0:LANES] + ckv[:, 2 * LANES:3 * LANES] * rk[:, LANES:2 * LANES]
    k_all = _dot(cn, wuk_ref[...])
    vt_all = _dot_nt(wuvt_ref[...], cn)
    ones_rows = jnp.where(lax.broadcasted_iota(jnp.int32, (MLA_VA - MLA_V, TM), 0) == 0, 1.0, 0.0)
    for hd in range(MLA_HEADS):
        k_ref[0, hd] = (k_all[:, hd * MLA_QK_PAD:(hd + 1) * MLA_QK_PAD] + kpe).astype(BF16)
        vt_ref[0, hd] = jnp.concatenate(
            [vt_all[hd * MLA_V:(hd + 1) * MLA_V], ones_rows], axis=0).astype(BF16)


def _mla_in_call(x_all, modsel, ng, wdq, qg, wuqt, wdkv, kvg, wuk, wuvt, rope_q, rope_k, nctx_t):
    b, ttot, d = x_all.shape
    nt = ttot // TM
    hn = MLA_HEADS
    return pl.pallas_call(
        functools.partial(_mla_in_kernel, d=d),
        out_shape=(jax.ShapeDtypeStruct((b, hn, MLA_QK_PAD, ttot), BF16),
                   jax.ShapeDtypeStruct((b, hn, ttot, MLA_QK_PAD), BF16),
                   jax.ShapeDtypeStruct((b, hn, MLA_VA, ttot), BF16)),
        grid=(b, nt),
        in_specs=[_x_spec(d), _mod_spec(6 * d, nctx_t), _resident((1, d)),
                  _resident(wdq.shape), _resident(qg.shape), _resident(wuqt.shape),
                  _resident(wdkv.shape), _resident(kvg.shape), _resident(wuk.shape),
                  _resident(wuvt.shape),
                  pl.BlockSpec((32, TM), lambda bb, i: (0, i)),
                  pl.BlockSpec((TM, 2 * LANES), lambda bb, i: (i, 0))],
        out_specs=(pl.BlockSpec((1, hn, MLA_QK_PAD, TM), lambda bb, i: (bb, 0, 0, i)),
                   pl.BlockSpec((1, hn, TM, MLA_QK_PAD), lambda bb, i: (bb, 0, i, 0)),
                   pl.BlockSpec((1, hn, MLA_VA, TM), lambda bb, i: (bb, 0, 0, i))),
        compiler_params=_cparams(("parallel", "parallel")),
        name="mla_in_proj",
    )(x_all, modsel, ng, wdq, qg, wuqt, wdkv, kvg, wuk, wuvt, rope_q, rope_k)


MLA_KC = 256
MLA_LOOKAHEAD = 4


def _mla_attn_kernel(qt_ref, k_ref, vt_ref, o_ref, *, nkc):
    outs = []
    for hh in range(2):
        qt = qt_ref[0, hh]
        ms, accs, scores = [], [], []

        def chunk(c):
            return slice(c * MLA_KC, (c + 1) * MLA_KC)

        for c in range(nkc + MLA_LOOKAHEAD):
            if c < nkc:
                scores.append(_dot(k_ref[0, hh, chunk(c), :], qt))
            cc = c - MLA_LOOKAHEAD
            if cc >= 0:
                s = scores[cc]
                m_c = s.max(axis=0, keepdims=True)
                ms.append(m_c)
                accs.append(_dot(vt_ref[0, hh, :, chunk(cc)], jnp.exp2(s - m_c).astype(BF16)))
        m = functools.reduce(jnp.maximum, ms)
        acc = jnp.zeros((MLA_VA, TM), F32)
        for m_c, a_c in zip(ms, accs):
            acc = acc + jnp.exp2(m_c - m) * a_c
        outs.append(acc[0:MLA_V] / acc[MLA_V:MLA_V + 1])
    o_ref[0] = jnp.concatenate(outs, axis=0).T.astype(BF16)


def _mla_attn_call(qt, k, vt, q_tile0, n_qtiles, kv_len):
    b, hn, _, ttot = qt.shape
    return pl.pallas_call(
        functools.partial(_mla_attn_kernel, nkc=kv_len // MLA_KC),
        out_shape=jax.ShapeDtypeStruct((b, n_qtiles * TM, hn * MLA_V), BF16),
        grid=(b, hn // 2, n_qtiles),
        in_specs=[
            pl.BlockSpec((1, 2, MLA_QK_PAD, TM), lambda bb, hp, qi: (bb, hp, 0, q_tile0 + qi)),
            pl.BlockSpec((1, 2, kv_len, MLA_QK_PAD), lambda bb, hp, qi: (bb, hp, 0, 0)),
            pl.BlockSpec((1, 2, MLA_VA, kv_len), lambda bb, hp, qi: (bb, hp, 0, 0)),
        ],
        out_specs=pl.BlockSpec((1, TM, 2 * MLA_V), lambda bb, hp, qi: (bb, qi, hp)),
        compiler_params=_cparams(("parallel", "parallel", "arbitrary")),
        name="mla_attention",
    )(qt, k, vt)


def _rope_tables(ctx, t):
    pos = np.arange(t)
    half = MLA_ROPE // 4
    inv = ROPE_BASE ** (-jnp.arange(half, dtype=F32) / half)
    tabs = []
    for p in (pos // GRID_W, pos % GRID_W):
        ang = jnp.asarray(p, F32)[:, None] * inv
        cos = jnp.concatenate([jnp.ones((ctx, half), F32), jnp.cos(ang)], axis=0)
        sin = jnp.concatenate([jnp.zeros((ctx, half), F32), jnp.sin(ang)], axis=0)
        tabs.append((cos, sin))
    (cr, sr), (cc, sc) = tabs
    rope_q = jnp.concatenate([cr, sr, cc, sc], axis=1).T
    ttot = ctx + t
    zeros = jnp.zeros((ttot, MLA_NOPE), F32)
    tail = jnp.zeros((ttot, LANES - MLA_QK), F32)
    cos_k = jnp.concatenate([zeros, cr, cr, cc, cc, tail], axis=1)
    sin_k = jnp.concatenate([zeros, -sr, sr, -sc, sc, tail], axis=1)
    rope_k = jnp.concatenate([cos_k, sin_k], axis=1)
    return rope_q, rope_k


def _mla_weights(w_uq, w_dkv, w_ukv):
    hn = MLA_HEADS
    q_rank = w_uq.shape[0]
    wq = w_uq.reshape(q_rank, hn, MLA_QK)
    wq = jnp.pad(wq, ((0, 0), (0, 0), (0, MLA_QK_PAD - MLA_QK)))
    wuqt = wq.reshape(q_rank, hn * MLA_QK_PAD).T.astype(BF16)
    d = w_dkv.shape[0]
    kpe = w_dkv[:, MLA_KV_RANK:]
    h8 = MLA_ROPE // 4
    swap = jnp.concatenate([kpe[:, h8:2 * h8], kpe[:, 0:h8], kpe[:, 3 * h8:4 * h8],
                            kpe[:, 2 * h8:3 * h8]], axis=1)

    def place(cols):
        return jnp.concatenate([jnp.zeros((d, MLA_NOPE), F32), cols,
                                jnp.zeros((d, LANES - MLA_QK), F32)], axis=1)

    wdkv = jnp.concatenate([w_dkv[:, :MLA_KV_RANK], place(kpe), place(swap)], axis=1).astype(BF16)
    wkv = w_ukv.reshape(MLA_KV_RANK, hn, MLA_NOPE + MLA_V)
    wuk = jnp.pad(wkv[:, :, :MLA_NOPE], ((0, 0), (0, 0), (0, MLA_QK_PAD - MLA_NOPE)))
    wuk = wuk.reshape(MLA_KV_RANK, hn * MLA_QK_PAD).astype(BF16)
    wuvt = wkv[:, :, MLA_NOPE:].reshape(MLA_KV_RANK, hn * MLA_V).T.astype(BF16)
    return wuqt, wdkv, wuk, wuvt


def kernel(x, c, ctx, c_ctx, ada_w, ada_b, norm_g, ffn_w_up, ffn_conv_w, ffn_conv_b, ffn_w_down,
           ev_w_in, ev_gate_b, ev_conv_w, ev_conv_b, ev_rpb, ev_ml_norm_g, ev_w_out,
           od_w_dq, od_q_norm_g, od_w_uq, od_w_dkv, od_kv_norm_g, od_w_ukv, od_w_o):
    b, t, d = x.shape
    nctx = ctx.shape[1]
    depth = ada_w.shape[0]
    ttot = nctx + t
    assert t % TM == 0 and nctx % TM == 0 and t % GRID_W == 0
    nctx_t = nctx // TM

    x_all = jnp.concatenate([ctx, x], axis=1)

    rows = -(-(b + 1) // 8) * 8
    cvec = jnp.zeros((rows, d), F32).at[:b].set(c).at[b].set(c_ctx)
    mod = _ada_call(cvec, ada_w, ada_b)

    rope_q, rope_k = _rope_tables(nctx, t)
    ml_scale = jnp.concatenate([jnp.full((1, ML_WIDTH), ML_HD ** -0.5, F32),
                                jnp.ones((1, ML_WIDTH), F32)], axis=1)

    for l in range(depth):
        ctx_out = l < depth - 1
        modsel = jnp.stack([jnp.broadcast_to(mod[l, b], (b, 6 * d)), mod[l, :b]], axis=1)
        modsel = modsel.reshape(b, 2, 1, 6 * d)
        ng = norm_g[l].reshape(4, 1, d)
        if l % 2 == 0:
            e = l // 2
            w_in = ev_w_in[e]
            n_main = 3 * NA_WIDTH + 4 * ML_WIDTH
            w_gate = jnp.pad(w_in[:, n_main:], ((0, 0), (0, LANES - ML_GATES))).astype(BF16)
            gate_b = jnp.pad(ev_gate_b[e], (0, LANES - ML_GATES)).reshape(1, LANES)
            naq, nak, nav, mlqk, mlv, mlo, gates = _even_in_call(
                x_all, modsel, ng[0], w_in[:, :n_main].astype(BF16), w_gate, gate_b, nctx_t)
            na_x = _na_call(naq, nak, nav, _na_bias_table(ev_rpb[e]), nctx, t)
            na_c = _na_ctx_call(naq, nak, nav, nctx)
            na = jnp.concatenate([na_c, na_x], axis=1)
            qk, kt = _ml_conv_call(mlqk, ev_conv_w[e], ev_conv_b[e].reshape(1, -1), ml_scale, nctx_t)
            gates_t = jnp.swapaxes(gates[:, :, :ML_GATES], 1, 2)
            hf, hb = _mlstm_call(qk, kt, mlv, gates, gates_t, nctx // ML_CHUNK)
            x_all = _even_out_call(x_all, modsel, ng[1], na, hf, hb, mlo,
                                   ev_ml_norm_g[e].reshape(1, -1), ev_w_out[e].astype(BF16), nctx_t)
        else:
            o = l // 2
            wuqt, wdkv, wuk, wuvt = _mla_weights(od_w_uq[o], od_w_dkv[o], od_w_ukv[o])
            qt, kk, vt = _mla_in_call(
                x_all, modsel, ng[0], od_w_dq[o].astype(BF16), od_q_norm_g[o].reshape(1, -1),
                wuqt, wdkv, od_kv_norm_g[o].reshape(1, -1), wuk, wuvt, rope_q, rope_k, nctx_t)
            att_x = _mla_attn_call(qt, kk, vt, nctx_t, t // TM, ttot)
            if ctx_out:
                att_c = _mla_attn_call(qt, kk, vt, 0, nctx_t, nctx)
            else:
                att_c = jnp.zeros((b, nctx, att_x.shape[-1]), BF16)
            att = jnp.concatenate([att_c, att_x], axis=1)
            x_all = _odd_out_call(x_all, modsel, ng[1], att, od_w_o[o].astype(BF16), nctx_t)
        x_all = _ffn_call(x_all, modsel, ng[2], ng[3], ffn_w_up[l].astype(BF16), ffn_conv_w[l],
                          ffn_conv_b[l].reshape(1, -1), ffn_w_down[l].astype(BF16), nctx_t)
    return x_all[:, nctx:]
```

```python
import functools
import math

import jax
import jax.numpy as jnp
import numpy as np
from jax import lax
from jax.experimental import pallas as pl
from jax.experimental.pallas import tpu as pltpu

F32 = jnp.float32
BF16 = jnp.bfloat16

EPS = 1e-6
NEG = -1e30
LOG2E = math.log2(math.e)

GRID_W = 64
NA_HEADS = 8
NA_HD = 64
NA_WIN_ROWS = 8
NA_WIN_COLS = 16
NA_WIDTH = NA_HEADS * NA_HD

ML_HEADS = 4
ML_HD = 128
ML_WIDTH = ML_HEADS * ML_HD
ML_CHUNK = 128
ML_GATES = 4 * ML_HEADS

MLA_HEADS = 16
MLA_NOPE = 64
MLA_ROPE = 32
MLA_V = 64
MLA_QK = MLA_NOPE + MLA_ROPE
MLA_Q_RANK = 256
MLA_KV_RANK = 128
ROPE_BASE = 10000.0

FFN_HIDDEN = 2816

LANES = 128
SUBLANES_BF16 = 16
TM = 256
VMEM_LIMIT = 48 * 1024 * 1024


def _cparams(sem):
    return pltpu.CompilerParams(dimension_semantics=sem, vmem_limit_bytes=VMEM_LIMIT)


def _resident(shape):
    nd = len(shape)
    return pl.BlockSpec(shape, lambda *_: (0,) * nd, pipeline_mode=pl.Buffered(1))


def _rms(xf, g):
    ms = jnp.mean(xf * xf, axis=-1, keepdims=True)
    return xf * lax.rsqrt(ms + EPS) * g


def _dot(a, b):
    return jnp.dot(a, b, preferred_element_type=F32)


def _dot_nt(a, b):
    return lax.dot_general(a, b, (((1,), (1,)), ((), ())), preferred_element_type=F32)


def _ada_kernel(c_ref, w_ref, b_ref, o_ref):
    c = c_ref[...]
    s = c * jax.nn.sigmoid(c)
    o_ref[0] = _dot(s.astype(BF16), w_ref[0].astype(BF16)) + b_ref[0]


def _ada_call(cvec, ada_w, ada_b):
    depth, d, n = ada_w.shape
    rows = cvec.shape[0]
    tn = 1024
    return pl.pallas_call(
        _ada_kernel,
        out_shape=jax.ShapeDtypeStruct((depth, rows, n), F32),
        grid=(depth, n // tn),
        in_specs=[
            pl.BlockSpec((rows, d), lambda l, j: (0, 0)),
            pl.BlockSpec((1, d, tn), lambda l, j: (l, 0, j)),
            pl.BlockSpec((1, 1, tn), lambda l, j: (l, 0, j)),
        ],
        out_specs=pl.BlockSpec((1, rows, tn), lambda l, j: (l, 0, j)),
        compiler_params=_cparams(("parallel", "parallel")),
        name="ada_mod",
    )(cvec, ada_w, ada_b.reshape(depth, 1, n))


def _x_spec(d):
    return pl.BlockSpec((1, TM, d), lambda b, i: (b, i, 0))


def _mod_spec(d6, nctx_t):
    return pl.BlockSpec((1, 1, 1, d6), lambda b, i: (b, jnp.where(i >= nctx_t, 1, 0), 0, 0))


def _modulated(x, m, g, d, which):
    o = 3 * d * which
    return _rms(x, g) * (1.0 + m[:, o + d:o + 2 * d]) + m[:, o:o + d]


def _even_in_kernel(x_ref, mod_ref, ng_ref, w_ref, wg_ref, gb_ref,
                    naq_ref, nak_ref, nav_ref, mlqk_ref, mlv_ref, mlo_ref, g_ref, *, d):
    m = mod_ref[0, 0]
    hb = _modulated(x_ref[0], m, ng_ref[...], d, 0).astype(BF16)

    def seg(lo, hi):
        return _dot(hb, w_ref[:, lo:hi])

    w = NA_WIDTH
    naq_ref[0] = (seg(0, w) * (NA_HD ** -0.5 * LOG2E)).astype(BF16)
    nak_ref[0] = seg(w, 2 * w).astype(BF16)
    nav_ref[0] = seg(2 * w, 3 * w).astype(BF16)
    o = 3 * w
    mlqk_ref[0, :, 0:ML_WIDTH] = seg(o, o + ML_WIDTH)
    mlqk_ref[0, :, ML_WIDTH:2 * ML_WIDTH] = seg(o + ML_WIDTH, o + 2 * ML_WIDTH)
    mlv_ref[0] = seg(o + 2 * ML_WIDTH, o + 3 * ML_WIDTH).astype(BF16)
    mlo_ref[0] = seg(o + 3 * ML_WIDTH, o + 4 * ML_WIDTH)
    g_ref[0] = _dot(hb, wg_ref[...]) + gb_ref[...]


def _even_in_call(x_all, modsel, ng, w_main, w_gate, gate_b, nctx_t):
    b, ttot, d = x_all.shape
    nt = ttot // TM
    n_main = w_main.shape[1]
    row = lambda c: pl.BlockSpec((1, TM, c), lambda bb, i: (bb, i, 0))
    sds = lambda c, dt: jax.ShapeDtypeStruct((b, ttot, c), dt)
    return pl.pallas_call(
        functools.partial(_even_in_kernel, d=d),
        out_shape=(sds(NA_WIDTH, BF16), sds(NA_WIDTH, BF16), sds(NA_WIDTH, BF16),
                   sds(2 * ML_WIDTH, F32), sds(ML_WIDTH, BF16), sds(ML_WIDTH, F32),
                   sds(LANES, F32)),
        grid=(b, nt),
        in_specs=[_x_spec(d), _mod_spec(6 * d, nctx_t), _resident((1, d)),
                  _resident((d, n_main)), _resident((d, LANES)), _resident((1, LANES))],
        out_specs=(row(NA_WIDTH), row(NA_WIDTH), row(NA_WIDTH), row(2 * ML_WIDTH),
                   row(ML_WIDTH), row(ML_WIDTH), row(LANES)),
        compiler_params=_cparams(("parallel", "parallel")),
        name="even_in_proj",
    )(x_all, modsel, ng, w_main, w_gate, gate_b)


def _pair_scores(qp, k_parts, bias_parts):
    lane = lax.broadcasted_iota(jnp.int32, qp.shape, 1)
    scores = []
    for hh in range(2):
        keep = (lane >= NA_HD) if hh else (lane < NA_HD)
        qh = jnp.where(keep, qp, jnp.zeros_like(qp))
        s = []
        for kk, bias in zip(k_parts, bias_parts):
            sp = _dot_nt(qh, kk)
            if bias is not None:
                sp = sp + bias[hh]
            s.append(sp)
        scores.append(s)
    return scores


def _pair_finish(scores, v_parts):
    mq = scores[0][0].shape[0]
    lane = lax.broadcasted_iota(jnp.int32, (mq, LANES), 1)
    v_aug = []
    for vv in v_parts:
        ones_col = jnp.where(lax.broadcasted_iota(jnp.int32, vv.shape, 1) == 0, 1.0, 0.0)
        v_aug.append(jnp.concatenate([vv, ones_col.astype(BF16)], axis=-1))
    outs = []
    for s in scores:
        m = functools.reduce(jnp.maximum, [sp.max(axis=-1, keepdims=True) for sp in s])
        o = jnp.zeros((mq, 2 * LANES), F32)
        for sp, vv in zip(s, v_aug):
            o = o + _dot(jnp.exp2(sp - m).astype(BF16), vv)
        outs.append(o[:, 0:LANES] / o[:, LANES:LANES + 1])
    return jnp.where(lane < NA_HD, outs[0], outs[1])


NA_RG = 4
NA_UB = NA_WIN_ROWS + NA_RG - 1


def _na_union_start(r0, rows):
    return jnp.clip(r0 - NA_WIN_ROWS // 2, 0, rows - NA_UB)


def _na_kernel(q_ref, k_ref, v_ref, bias_ref, o_ref, *, ctx, rows):
    us = _na_union_start(pl.program_id(1) * NA_RG, rows)
    start = pl.multiple_of(ctx + us * GRID_W, GRID_W)
    band = NA_UB * GRID_W
    pairs = [slice(p * LANES, (p + 1) * LANES) for p in range(NA_HEADS // 2)]

    def pair_scores(p):
        cs = pairs[p]
        kb = k_ref[0, pl.ds(start, band), cs]
        kc = k_ref[0, 0:ctx, cs]
        bias = (bias_ref[0, 2 * p], bias_ref[0, 2 * p + 1])
        return _pair_scores(q_ref[0, :, cs], (kb, kc), (bias, None))

    scores = [pair_scores(0)]
    for p, cs in enumerate(pairs):
        if p + 1 < len(pairs):
            scores.append(pair_scores(p + 1))
        vb = v_ref[0, pl.ds(start, band), cs]
        vc = v_ref[0, 0:ctx, cs]
        o_ref[0, :, cs] = _pair_finish(scores[p], (vb, vc)).astype(BF16)


def _na_call(naq, nak, nav, bias_tab, ctx, t):
    b, ttot, w = naq.shape
    rows = t // GRID_W
    groups = rows // NA_RG
    assert rows % NA_RG == 0 and groups >= 3 and ctx % (NA_RG * GRID_W) == 0
    q_off = ctx // (NA_RG * GRID_W)
    mq = NA_RG * GRID_W

    def bias_type(bb, g):
        return (jnp.where(g == 0, 0, jnp.where(g == groups - 1, 2, 1)), 0, 0, 0)

    whole = lambda bb, g: (bb, 0, 0)
    return pl.pallas_call(
        functools.partial(_na_kernel, ctx=ctx, rows=rows),
        out_shape=jax.ShapeDtypeStruct((b, t, w), BF16),
        grid=(b, groups),
        in_specs=[
            pl.BlockSpec((1, mq, w), lambda bb, g: (bb, q_off + g, 0)),
            pl.BlockSpec((1, ttot, w), whole, pipeline_mode=pl.Buffered(1)),
            pl.BlockSpec((1, ttot, w), whole, pipeline_mode=pl.Buffered(1)),
            pl.BlockSpec((1, NA_HEADS, mq, NA_UB * GRID_W), bias_type, pipeline_mode=pl.Buffered(1)),
        ],
        out_specs=pl.BlockSpec((1, mq, w), lambda bb, g: (bb, g, 0)),
        compiler_params=_cparams(("parallel", "arbitrary")),
        name="na_attention",
    )(naq, nak, nav, bias_tab)


def _na_ctx_kernel(q_ref, k_ref, v_ref, o_ref):
    pairs = [slice(p * LANES, (p + 1) * LANES) for p in range(NA_HEADS // 2)]
    scores = [_pair_scores(q_ref[0, :, cs], (k_ref[0, :, cs],), (None,)) for cs in pairs]
    for sc, cs in zip(scores, pairs):
        o_ref[0, :, cs] = _pair_finish(sc, (v_ref[0, :, cs],)).astype(BF16)


def _na_ctx_call(naq, nak, nav, ctx):
    b, _, w = naq.shape
    spec = pl.BlockSpec((1, ctx, w), lambda bb: (bb, 0, 0))
    return pl.pallas_call(
        _na_ctx_kernel,
        out_shape=jax.ShapeDtypeStruct((b, ctx, w), BF16),
        grid=(b,),
        in_specs=[spec, spec, spec],
        out_specs=spec,
        compiler_params=_cparams(("parallel",)),
        name="na_ctx_attention",
    )(naq, nak, nav)


def _na_group_geometry(g, rows):
    r = g * NA_RG + np.arange(NA_RG)[:, None]
    us = int(np.clip(g * NA_RG - NA_WIN_ROWS // 2, 0, rows - NA_UB))
    kr = us + np.arange(NA_UB)[None, :]
    rs = np.clip(r - NA_WIN_ROWS // 2, 0, rows - NA_WIN_ROWS)
    valid = (kr >= rs) & (kr < rs + NA_WIN_ROWS)
    dr_idx = np.clip(kr - r + NA_WIN_ROWS - 1, 0, 2 * NA_WIN_ROWS - 2)
    return dr_idx, valid


def _na_bias_table(rpb, rows):
    groups = rows // NA_RG
    geo = [_na_group_geometry(g, rows) for g in range(groups)]
    for dr_g, valid_g in geo[1:groups - 1]:
        assert (valid_g == geo[1][1]).all() and (dr_g[valid_g] == geo[1][0][valid_g]).all()
    dr_idx = np.stack([geo[g][0] for g in (0, 1, groups - 1)])
    valid = np.stack([geo[g][1] for g in (0, 1, groups - 1)])
    kw = NA_WIN_COLS
    col = jnp.arange(GRID_W)
    cs = jnp.clip(col - kw // 2, 0, GRID_W - kw)
    in_win = (col[None, :] >= cs[:, None]) & (col[None, :] < cs[:, None] + kw)
    dc_idx = jnp.clip(col[None, :] - col[:, None], -(kw - 1), kw - 1) + (NA_WIN_COLS - 1)
    rpb_cols = rpb.astype(F32)[:, :, dc_idx] * LOG2E
    rpb_cols = jnp.where(in_win[None, None], rpb_cols, NEG)
    tab = rpb_cols[:, dr_idx]
    tab = jnp.where(jnp.asarray(valid)[None, :, :, :, None, None], tab, NEG)
    tab = tab.transpose(1, 0, 2, 4, 3, 5)
    return tab.reshape(3, NA_HEADS, NA_RG * GRID_W, NA_UB * GRID_W)


def _halo_specs(c, rows_h, ttot, tile0=0):
    per = TM // rows_h
    last = ttot // rows_h - 1
    prev = pl.BlockSpec((1, rows_h, c), lambda b, i: (b, jnp.maximum((i + tile0) * per - 1, 0), 0))
    nxt = pl.BlockSpec((1, rows_h, c), lambda b, i: (b, jnp.minimum((i + tile0 + 1) * per, last), 0))
    return prev, nxt


def _halo_valid(nctx_t, nt, tile0=0):
    i = pl.program_id(1) + tile0
    pv = jnp.where((i == 0) | (i == nctx_t), 0.0, 1.0).astype(F32)
    nv = jnp.where((i == nctx_t - 1) | (i == nt - 1), 0.0, 1.0).astype(F32)
    return pv, nv


def _conv3(u_ref, h, cw, cb, lo, hi):
    return (cb
            + u_ref[h - 1:h - 1 + TM, lo:hi] * cw[0:1]
            + u_ref[h:h + TM, lo:hi] * cw[1:2]
            + u_ref[h + 1:h + 1 + TM, lo:hi] * cw[2:3])


def _ml_conv_kernel(x_ref, xp_ref, xn_ref, cw_ref, cb_ref, sc_ref, o_ref, kt_ref, u_ref, *, nctx_t, nt):
    pv, nv = _halo_valid(nctx_t, nt)
    h = 8
    u_ref[0:h] = xp_ref[0] * pv
    u_ref[h:h + TM] = x_ref[0]
    u_ref[h + TM:h + TM + h] = xn_ref[0] * nv
    c = x_ref.shape[-1]
    y = _conv3(u_ref, h, cw_ref[...], cb_ref[...], 0, c)
    y = y * jax.nn.sigmoid(y) * sc_ref[...]
    o_ref[0] = y.astype(BF16)
    kt_ref[0] = y[:, ML_WIDTH:2 * ML_WIDTH].T.astype(BF16)


def _ml_conv_call(mlqk, conv_w, conv_b, scale, nctx_t):
    b, ttot, c = mlqk.shape
    nt = ttot // TM
    prev, nxt = _halo_specs(c, 8, ttot)
    return pl.pallas_call(
        functools.partial(_ml_conv_kernel, nctx_t=nctx_t, nt=nt),
        out_shape=(jax.ShapeDtypeStruct((b, ttot, c), BF16),
                   jax.ShapeDtypeStruct((b, ML_WIDTH, ttot), BF16)),
        grid=(b, nt),
        in_specs=[_x_spec(c), prev, nxt, _resident((3, c)), _resident((1, c)), _resident((1, c))],
        out_specs=(_x_spec(c), pl.BlockSpec((1, ML_WIDTH, TM), lambda bb, i: (bb, 0, i))),
        scratch_shapes=[pltpu.VMEM((TM + 16, c), F32)],
        compiler_params=_cparams(("parallel", "parallel")),
        name="mlstm_conv",
    )(mlqk, mlqk, mlqk, conv_w, conv_b, scale)


def _log_sigmoid(x):
    return jnp.minimum(x, 0.0) - jnp.log(1.0 + jnp.exp(-jnp.abs(x)))


def _split_dot(a, b, data_is_rhs):
    data = b if data_is_rhs else a
    hi = data.astype(BF16)
    lo = (data - hi.astype(F32)).astype(BF16)
    if data_is_rhs:
        return _dot(a, hi) + _dot(a, lo)
    return _dot(hi, b) + _dot(lo, b)


def _mlstm_kernel(qkf_ref, ktf_ref, vf_ref, gf_ref, gtf_ref, qkb_ref, ktb_ref, vb_ref, gb_ref, gtb_ref,
                  hf_ref, hb_ref, ct_ref, m_ref):
    L = ML_CHUNK

    @pl.when(pl.program_id(1) == 0)
    def _():
        ct_ref[...] = jnp.zeros_like(ct_ref)
        m_ref[...] = jnp.zeros_like(m_ref)

    ri = lax.broadcasted_iota(jnp.int32, (L, L), 0)
    ci = lax.broadcasted_iota(jnp.int32, (L, L), 1)
    tril = jnp.where(ri >= ci, 1.0, 0.0).astype(BF16)
    triu = jnp.where(ri <= ci, 1.0, 0.0).astype(BF16)
    lane = lax.broadcasted_iota(jnp.int32, (L, LANES), 1)
    ones_col = jnp.where(lane == 0, 1.0, 0.0).astype(BF16)

    dirs = (
        (qkf_ref, ktf_ref, vf_ref, gf_ref, gtf_ref, hf_ref, tril, triu, ri >= ci, L - 1, 0),
        (qkb_ref, ktb_ref, vb_ref, gb_ref, gtb_ref, hb_ref, triu, tril, ri <= ci, 0, 2 * ML_HEADS),
    )
    chains = []
    for dnum, (qk_ref, kt_ref, v_ref, g_ref, gt_ref, h_ref, tcol, trow, mask, tot_row, goff) in enumerate(dirs):
        gt = gt_ref[0]
        cum_col_all = _split_dot(tcol, _log_sigmoid(g_ref[0]), True)
        cum_row_all = _split_dot(_log_sigmoid(gt), trow, False)
        for hd in range(ML_HEADS):
            ic = goff + hd
            fc = goff + ML_HEADS + hd
            cs = slice(hd * ML_HD, (hd + 1) * ML_HD)
            j = dnum * ML_HEADS + hd
            q = qk_ref[0, :, cs]
            k = qk_ref[0, :, ML_WIDTH + hd * ML_HD:ML_WIDTH + (hd + 1) * ML_HD]
            ct = ct_ref[j]
            cum_col = cum_col_all[:, fc:fc + 1]
            chains.append(dict(
                j=j, h_ref=h_ref, cs=cs, kt=kt_ref[0, cs, :], ct=ct, mask=mask,
                v_aug=jnp.concatenate([v_ref[0, :, cs], ones_col], axis=-1),
                s=_dot_nt(q, k), hq=_dot(q, ct.astype(BF16)),
                cum_col=cum_col, cum_row=cum_row_all[fc:fc + 1, :],
                i_row=gt[ic:ic + 1, :], total=cum_col[tot_row:tot_row + 1, :]))
    for ch in chains:
        m_st = m_ref[ch["j"]][0:1, 0:1]
        cum_col, total = ch["cum_col"], ch["total"]
        e_row = ch["i_row"] - ch["cum_row"]
        base = jnp.where(ch["mask"], e_row, NEG)
        m_inter = cum_col + m_st
        m_t = jnp.maximum(m_inter, cum_col + base.max(axis=-1, keepdims=True))
        g_row = total + e_row
        m_new = jnp.maximum(total + m_st, g_row.max(axis=-1, keepdims=True))
        ch.update(base=base, m_inter=m_inter, m_t=m_t, g_row=g_row, m_new=m_new,
                  decay=jnp.exp(total + m_st - m_new))
    for ch in chains:
        ch["qk"] = (ch["s"] * jnp.exp(ch["base"] + (ch["cum_col"] - ch["m_t"]))).astype(BF16)
        ch["a"] = jnp.exp(ch["m_inter"] - ch["m_t"])
        ch["floor"] = jnp.exp(-ch["m_t"])
        ch["kw"] = (ch["kt"].astype(F32) * jnp.exp(ch["g_row"] - ch["m_new"])).astype(BF16)
    for ch in chains:
        ch["p"] = _dot(ch["qk"], ch["v_aug"])
    for ch in chains:
        ch["u"] = _dot(ch["kw"], ch["v_aug"])
    for ch in chains:
        hfull = ch["a"] * ch["hq"] + ch["p"]
        den = hfull[:, ML_HD:ML_HD + 1]
        ch["h_ref"][0, :, ch["cs"]] = hfull[:, 0:ML_HD] / jnp.maximum(jnp.abs(den), ch["floor"])
    for ch in chains:
        ct_ref[ch["j"]] = ch["decay"] * ch["ct"] + ch["u"]
        m_ref[ch["j"]] = jnp.broadcast_to(ch["m_new"], m_ref.shape[1:])


def _mlstm_call(qk, kt, v, gates, gates_t, nctx_c):
    b, ttot, _ = qk.shape
    nc = ttot // ML_CHUNK
    L = ML_CHUNK

    def fwd(bb, c):
        return (bb, c, 0)

    def bwd(bb, c):
        return (bb, jnp.where(c < nctx_c, nctx_c - 1 - c, nc - 1 + nctx_c - c), 0)

    def fwd_t(bb, c):
        return (bb, 0, c)

    def bwd_t(bb, c):
        return (bb, 0, jnp.where(c < nctx_c, nctx_c - 1 - c, nc - 1 + nctx_c - c))

    ins = []
    for row_map, col_map in ((fwd, fwd_t), (bwd, bwd_t)):
        ins += [pl.BlockSpec((1, L, 2 * ML_WIDTH), row_map),
                pl.BlockSpec((1, ML_WIDTH, L), col_map),
                pl.BlockSpec((1, L, ML_WIDTH), row_map),
                pl.BlockSpec((1, L, LANES), row_map),
                pl.BlockSpec((1, ML_GATES, L), col_map)]
    out_sds = jax.ShapeDtypeStruct((b, ttot, ML_WIDTH), F32)
    return pl.pallas_call(
        _mlstm_kernel,
        out_shape=(out_sds, out_sds),
        grid=(b, nc),
        in_specs=ins,
        out_specs=(pl.BlockSpec((1, L, ML_WIDTH), fwd), pl.BlockSpec((1, L, ML_WIDTH), bwd)),
        scratch_shapes=[pltpu.VMEM((2 * ML_HEADS, ML_HD, 2 * ML_HD), F32),
                        pltpu.VMEM((2 * ML_HEADS, 8, LANES), F32)],
        compiler_params=_cparams(("parallel", "arbitrary")),
        name="mlstm_scan",
    )(qk, kt, v, gates, gates_t, qk, kt, v, gates, gates_t)


def _out_tail(x_ref, mod_ref, ng_ref, y, o_ref, d):
    m = mod_ref[0, 0]
    o_ref[0] = x_ref[0] + m[:, 2 * d:3 * d] * _rms(y, ng_ref[...])


def _even_out_kernel(x_ref, mod_ref, ng_ref, na_ref, hf_ref, hb_ref, op_ref, mlg_ref, w_ref,
                     o_ref, *, d):
    hs = jax.nn.sigmoid(op_ref[0]) * (hf_ref[0] + hb_ref[0])
    parts = []
    for hd in range(ML_HEADS):
        seg = hs[:, hd * ML_HD:(hd + 1) * ML_HD]
        mu = jnp.mean(seg, axis=-1, keepdims=True)
        cen = seg - mu
        var = jnp.mean(cen * cen, axis=-1, keepdims=True)
        parts.append(cen * lax.rsqrt(var + EPS))
    ml = (jnp.concatenate(parts, axis=-1) * mlg_ref[...]).astype(BF16)
    y = _dot(na_ref[0], w_ref[0:NA_WIDTH, :]) + _dot(ml, w_ref[NA_WIDTH:NA_WIDTH + ML_WIDTH, :])
    _out_tail(x_ref, mod_ref, ng_ref, y, o_ref, d)


def _even_out_call(x_all, modsel, ng, na, hf, hb, mlo, mlg, w_out, nctx_t):
    b, ttot, d = x_all.shape
    row = lambda c: pl.BlockSpec((1, TM, c), lambda bb, i: (bb, i, 0))
    return pl.pallas_call(
        functools.partial(_even_out_kernel, d=d),
        out_shape=jax.ShapeDtypeStruct((b, ttot, d), F32),
        grid=(b, ttot // TM),
        in_specs=[_x_spec(d), _mod_spec(6 * d, nctx_t), _resident((1, d)),
                  row(NA_WIDTH), row(ML_WIDTH), row(ML_WIDTH), row(ML_WIDTH),
                  _resident((1, ML_WIDTH)), _resident(w_out.shape)],
        out_specs=_x_spec(d),
        compiler_params=_cparams(("parallel", "parallel")),
        name="even_out_proj",
    )(x_all, modsel, ng, na, hf, hb, mlo, mlg, w_out)


def _odd_out_kernel(x_ref, mod_ref, ng_ref, a_ref, w_ref, o_ref, *, d):
    _out_tail(x_ref, mod_ref, ng_ref, _dot(a_ref[0], w_ref[...]), o_ref, d)


def _odd_out_call(x_all, modsel, ng, attn, w_o, nctx_t):
    b, ttot, d = x_all.shape
    return pl.pallas_call(
        functools.partial(_odd_out_kernel, d=d),
        out_shape=jax.ShapeDtypeStruct((b, ttot, d), F32),
        grid=(b, ttot // TM),
        in_specs=[_x_spec(d), _mod_spec(6 * d, nctx_t), _resident((1, d)),
                  _x_spec(attn.shape[-1]), _resident(w_o.shape)],
        out_specs=_x_spec(d),
        compiler_params=_cparams(("parallel", "parallel")),
        name="odd_out_proj",
    )(x_all, modsel, ng, attn, w_o)


FFN_CK = 256


def _ffn_kernel(x_ref, xp_ref, xn_ref, mod_ref, ng_in_ref, ng_out_ref, wup_ref, cw_ref, cb_ref,
                wdn_ref, o_ref, h_ref, u_ref, act_ref, *, d, nctx_t, nt, tile0):
    pv, nv = _halo_valid(nctx_t, nt, tile0)
    m = mod_ref[0, 0]
    g_in = ng_in_ref[...]
    hh = SUBLANES_BF16
    h_ref[0:hh] = (_modulated(xp_ref[0], m, g_in, d, 1) * pv).astype(BF16)
    h_ref[hh:hh + TM] = _modulated(x_ref[0], m, g_in, d, 1).astype(BF16)
    h_ref[hh + TM:hh + TM + hh] = (_modulated(xn_ref[0], m, g_in, d, 1) * nv).astype(BF16)
    hcat = h_ref[...]
    ck = FFN_CK
    nslab = ck // LANES

    def conv_slab(s, col):
        cw = cw_ref[:, col:col + LANES]
        return (cb_ref[:, col:col + LANES]
                + u_ref[s, hh - 1:hh - 1 + TM, :] * cw[0:1]
                + u_ref[s, hh:hh + TM, :] * cw[1:2]
                + u_ref[s, hh + 1:hh + 1 + TM, :] * cw[2:3])

    for c in range(FFN_HIDDEN // ck):
        lo = c * ck
        glo = FFN_HIDDEN + lo
        ua = _dot(hcat, wup_ref[:, lo:lo + ck])
        ug = _dot(hcat, wup_ref[:, glo:glo + ck])
        for s in range(nslab):
            u_ref[s] = ua[:, s * LANES:(s + 1) * LANES]
            u_ref[nslab + s] = ug[:, s * LANES:(s + 1) * LANES]
        for s in range(nslab):
            a = conv_slab(s, lo + s * LANES)
            g = conv_slab(nslab + s, glo + s * LANES)
            act_ref[:, lo + s * LANES:lo + (s + 1) * LANES] = (a * (g * jax.nn.sigmoid(g))).astype(BF16)
    y = _dot(act_ref[...], wdn_ref[...])
    o_ref[0] = x_ref[0] + m[:, 5 * d:6 * d] * _rms(y, ng_out_ref[...])


def _ffn_call(x_all, modsel, ng_in, ng_out, w_up, conv_w, conv_b, w_down, nctx_t, latents_only):
    b, ttot, d = x_all.shape
    nt = ttot // TM
    hh = SUBLANES_BF16
    tile0 = nctx_t if latents_only else 0
    prev, nxt = _halo_specs(d, hh, ttot, tile0)
    return pl.pallas_call(
        functools.partial(_ffn_kernel, d=d, nctx_t=nctx_t, nt=nt, tile0=tile0),
        out_shape=jax.ShapeDtypeStruct((b, ttot - tile0 * TM, d), F32),
        grid=(b, nt - tile0),
        in_specs=[pl.BlockSpec((1, TM, d), lambda bb, i: (bb, i + tile0, 0)), prev, nxt,
                  _mod_spec(6 * d, nctx_t - tile0), _resident((1, d)),
                  _resident((1, d)), _resident(w_up.shape), _resident(conv_w.shape),
                  _resident(conv_b.shape), _resident(w_down.shape)],
        out_specs=_x_spec(d),
        scratch_shapes=[pltpu.VMEM((TM + 2 * hh, d), BF16),
                        pltpu.VMEM((2 * FFN_CK // LANES, TM + 2 * hh, LANES), F32),
                        pltpu.VMEM((TM, FFN_HIDDEN), BF16)],
        compiler_params=_cparams(("parallel", "parallel")),
        name="conv_ffn",
    )(x_all, x_all, x_all, modsel, ng_in, ng_out, w_up, conv_w, conv_b, w_down)


MLA_QK_PAD = LANES
MLA_VA = MLA_V + SUBLANES_BF16


def _mla_in_kernel(x_ref, mod_ref, ng_ref, wdq_ref, qg_ref, wuqt_ref, wdkv_ref, kvg_ref,
                   wuk_ref, wuvt_ref, ropeq_ref, ropek_ref, qt_ref, k_ref, vt_ref, *, d):
    m = mod_ref[0, 0]
    hb = _modulated(x_ref[0], m, ng_ref[...], d, 0).astype(BF16)
    cq = _rms(_dot(hb, wdq_ref[...]), qg_ref[...]).astype(BF16)
    qt_all = _dot_nt(wuqt_ref[...], cq)
    rq = ropeq_ref[...]
    cos_r, sin_r, cos_c, sin_c = rq[0:8], rq[8:16], rq[16:24], rq[24:32]
    scale = MLA_QK ** -0.5 * LOG2E
    for hd in range(MLA_HEADS):
        base = hd * MLA_QK_PAD
        nope = qt_all[base:base + MLA_NOPE]
        x1r = qt_all[base + 64:base + 72]
        x2r = qt_all[base + 72:base + 80]
        x1c = qt_all[base + 80:base + 88]
        x2c = qt_all[base + 88:base + 96]
        pad = qt_all[base + 96:base + 128]
        roped = jnp.concatenate([
            nope,
            x1r * cos_r - x2r * sin_r, x1r * sin_r + x2r * cos_r,
            x1c * cos_c - x2c * sin_c, x1c * sin_c + x2c * cos_c,
            pad], axis=0)
        qt_ref[0, hd] = (roped * scale).astype(BF16)
    ckv = _dot(hb, wdkv_ref[...])
    cn = _rms(ckv[:, 0:MLA_KV_RANK], kvg_ref[...]).astype(BF16)
    rk = ropek_ref[...]
    kpe = ckv[:, LANES:2 * LANES] * rk[:, 0:LANES] + ckv[:, 2 * LANES:3 * LANES] * rk[:, LANES:2 * LANES]
    k_all = _dot(cn, wuk_ref[...])
    vt_all = _dot_nt(wuvt_ref[...], cn)
    ones_rows = jnp.where(lax.broadcasted_iota(jnp.int32, (MLA_VA - MLA_V, TM), 0) == 0, 1.0, 0.0)
    for hd in range(MLA_HEADS):
        k_ref[0, hd] = (k_all[:, hd * MLA_QK_PAD:(hd + 1) * MLA_QK_PAD] + kpe).astype(BF16)
        vt_ref[0, hd] = jnp.concatenate(
            [vt_all[hd * MLA_V:(hd + 1) * MLA_V], ones_rows], axis=0).astype(BF16)


def _mla_in_call(x_all, modsel, ng, wdq, qg, wuqt, wdkv, kvg, wuk, wuvt, rope_q, rope_k, nctx_t):
    b, ttot, d = x_all.shape
    nt = ttot // TM
    hn = MLA_HEADS
    return pl.pallas_call(
        functools.partial(_mla_in_kernel, d=d),
        out_shape=(jax.ShapeDtypeStruct((b, hn, MLA_QK_PAD, ttot), BF16),
                   jax.ShapeDtypeStruct((b, hn, ttot, MLA_QK_PAD), BF16),
                   jax.ShapeDtypeStruct((b, hn, MLA_VA, ttot), BF16)),
        grid=(b, nt),
        in_specs=[_x_spec(d), _mod_spec(6 * d, nctx_t), _resident((1, d)),
                  _resident(wdq.shape), _resident(qg.shape), _resident(wuqt.shape),
                  _resident(wdkv.shape), _resident(kvg.shape), _resident(wuk.shape),
                  _resident(wuvt.shape),
                  pl.BlockSpec((32, TM), lambda bb, i: (0, i)),
                  pl.BlockSpec((TM, 2 * LANES), lambda bb, i: (i, 0))],
        out_specs=(pl.BlockSpec((1, hn, MLA_QK_PAD, TM), lambda bb, i: (bb, 0, 0, i)),
                   pl.BlockSpec((1, hn, TM, MLA_QK_PAD), lambda bb, i: (bb, 0, i, 0)),
                   pl.BlockSpec((1, hn, MLA_VA, TM), lambda bb, i: (bb, 0, 0, i))),
        compiler_params=_cparams(("parallel", "parallel")),
        name="mla_in_proj",
    )(x_all, modsel, ng, wdq, qg, wuqt, wdkv, kvg, wuk, wuvt, rope_q, rope_k)


MLA_KC = 256
MLA_LOOKAHEAD = 6


def _mla_attn_kernel(qt_ref, k_ref, vt_ref, o_ref, *, nkc):
    outs = []
    for hh in range(2):
        qt = qt_ref[0, hh]
        ms, accs, scores = [], [], []

        def chunk(c):
            return slice(c * MLA_KC, (c + 1) * MLA_KC)

        for c in range(nkc + MLA_LOOKAHEAD):
            if c < nkc:
                scores.append(_dot(k_ref[0, hh, chunk(c), :], qt))
            cc = c - MLA_LOOKAHEAD
            if cc >= 0:
                s = scores[cc]
                m_c = s.max(axis=0, keepdims=True)
                ms.append(m_c)
                accs.append(_dot(vt_ref[0, hh, :, chunk(cc)], jnp.exp2(s - m_c).astype(BF16)))
        m = functools.reduce(jnp.maximum, ms)
        acc = jnp.zeros((MLA_VA, TM), F32)
        for m_c, a_c in zip(ms, accs):
            acc = acc + jnp.exp2(m_c - m) * a_c
        outs.append(acc[0:MLA_V] / acc[MLA_V:MLA_V + 1])
    o_ref[0] = jnp.concatenate(outs, axis=0).T.astype(BF16)


def _mla_attn_call(qt, k, vt, q_tile0, n_qtiles, kv_len):
    b, hn, _, ttot = qt.shape
    return pl.pallas_call(
        functools.partial(_mla_attn_kernel, nkc=kv_len // MLA_KC),
        out_shape=jax.ShapeDtypeStruct((b, n_qtiles * TM, hn * MLA_V), BF16),
        grid=(b, hn // 2, n_qtiles),
        in_specs=[
            pl.BlockSpec((1, 2, MLA_QK_PAD, TM), lambda bb, hp, qi: (bb, hp, 0, q_tile0 + qi)),
            pl.BlockSpec((1, 2, kv_len, MLA_QK_PAD), lambda bb, hp, qi: (bb, hp, 0, 0)),
            pl.BlockSpec((1, 2, MLA_VA, kv_len), lambda bb, hp, qi: (bb, hp, 0, 0)),
        ],
        out_specs=pl.BlockSpec((1, TM, 2 * MLA_V), lambda bb, hp, qi: (bb, qi, hp)),
        compiler_params=_cparams(("parallel", "parallel", "arbitrary")),
        name="mla_attention",
    )(qt, k, vt)


def _rope_tables(ctx, t):
    pos = np.arange(t)
    half = MLA_ROPE // 4
    inv = ROPE_BASE ** (-jnp.arange(half, dtype=F32) / half)
    tabs = []
    for p in (pos // GRID_W, pos % GRID_W):
        ang = jnp.asarray(p, F32)[:, None] * inv
        cos = jnp.concatenate([jnp.ones((ctx, half), F32), jnp.cos(ang)], axis=0)
        sin = jnp.concatenate([jnp.zeros((ctx, half), F32), jnp.sin(ang)], axis=0)
        tabs.append((cos, sin))
    (cr, sr), (cc, sc) = tabs
    rope_q = jnp.concatenate([cr, sr, cc, sc], axis=1).T
    ttot = ctx + t
    zeros = jnp.zeros((ttot, MLA_NOPE), F32)
    tail = jnp.zeros((ttot, LANES - MLA_QK), F32)
    cos_k = jnp.concatenate([zeros, cr, cr, cc, cc, tail], axis=1)
    sin_k = jnp.concatenate([zeros, -sr, sr, -sc, sc, tail], axis=1)
    rope_k = jnp.concatenate([cos_k, sin_k], axis=1)
    return rope_q, rope_k


def _mla_weights(w_uq, w_dkv, w_ukv):
    hn = MLA_HEADS
    q_rank = w_uq.shape[0]
    wq = w_uq.reshape(q_rank, hn, MLA_QK)
    wq = jnp.pad(wq, ((0, 0), (0, 0), (0, MLA_QK_PAD - MLA_QK)))
    wuqt = wq.reshape(q_rank, hn * MLA_QK_PAD).T.astype(BF16)
    d = w_dkv.shape[0]
    kpe = w_dkv[:, MLA_KV_RANK:]
    h8 = MLA_ROPE // 4
    swap = jnp.concatenate([kpe[:, h8:2 * h8], kpe[:, 0:h8], kpe[:, 3 * h8:4 * h8],
                            kpe[:, 2 * h8:3 * h8]], axis=1)

    def place(cols):
        return jnp.concatenate([jnp.zeros((d, MLA_NOPE), F32), cols,
                                jnp.zeros((d, LANES - MLA_QK), F32)], axis=1)

    wdkv = jnp.concatenate([w_dkv[:, :MLA_KV_RANK], place(kpe), place(swap)], axis=1).astype(BF16)
    wkv = w_ukv.reshape(MLA_KV_RANK, hn, MLA_NOPE + MLA_V)
    wuk = jnp.pad(wkv[:, :, :MLA_NOPE], ((0, 0), (0, 0), (0, MLA_QK_PAD - MLA_NOPE)))
    wuk = wuk.reshape(MLA_KV_RANK, hn * MLA_QK_PAD).astype(BF16)
    wuvt = wkv[:, :, MLA_NOPE:].reshape(MLA_KV_RANK, hn * MLA_V).T.astype(BF16)
    return wuqt, wdkv, wuk, wuvt


def kernel(x, c, ctx, c_ctx, ada_w, ada_b, norm_g, ffn_w_up, ffn_conv_w, ffn_conv_b, ffn_w_down,
           ev_w_in, ev_gate_b, ev_conv_w, ev_conv_b, ev_rpb, ev_ml_norm_g, ev_w_out,
           od_w_dq, od_q_norm_g, od_w_uq, od_w_dkv, od_kv_norm_g, od_w_ukv, od_w_o):
    b, t, d = x.shape
    nctx = ctx.shape[1]
    depth = ada_w.shape[0]
    ttot = nctx + t
    assert t % TM == 0 and nctx % TM == 0 and t % GRID_W == 0
    nctx_t = nctx // TM

    x_all = jnp.concatenate([ctx, x], axis=1)

    rows = -(-(b + 1) // 8) * 8
    cvec = jnp.zeros((rows, d), F32).at[:b].set(c).at[b].set(c_ctx)
    mod = _ada_call(cvec, ada_w, ada_b)

    rope_q, rope_k = _rope_tables(nctx, t)
    ml_scale = jnp.concatenate([jnp.full((1, ML_WIDTH), ML_HD ** -0.5, F32),
                                jnp.ones((1, ML_WIDTH), F32)], axis=1)

    for l in range(depth):
        ctx_out = l < depth - 1
        modsel = jnp.stack([jnp.broadcast_to(mod[l, b], (b, 6 * d)), mod[l, :b]], axis=1)
        modsel = modsel.reshape(b, 2, 1, 6 * d)
        ng = norm_g[l].reshape(4, 1, d)
        if l % 2 == 0:
            e = l // 2
            w_in = ev_w_in[e]
            n_main = 3 * NA_WIDTH + 4 * ML_WIDTH
            w_gate = jnp.pad(w_in[:, n_main:], ((0, 0), (0, LANES - ML_GATES))).astype(BF16)
            gate_b = jnp.pad(ev_gate_b[e], (0, LANES - ML_GATES)).reshape(1, LANES)
            naq, nak, nav, mlqk, mlv, mlo, gates = _even_in_call(
                x_all, modsel, ng[0], w_in[:, :n_main].astype(BF16), w_gate, gate_b, nctx_t)
            na_x = _na_call(naq, nak, nav, _na_bias_table(ev_rpb[e], t // GRID_W), nctx, t)
            na_c = _na_ctx_call(naq, nak, nav, nctx)
            na = jnp.concatenate([na_c, na_x], axis=1)
            qk, kt = _ml_conv_call(mlqk, ev_conv_w[e], ev_conv_b[e].reshape(1, -1), ml_scale, nctx_t)
            gates_t = jnp.swapaxes(gates[:, :, :ML_GATES], 1, 2)
            hf, hb = _mlstm_call(qk, kt, mlv, gates, gates_t, nctx // ML_CHUNK)
            x_all = _even_out_call(x_all, modsel, ng[1], na, hf, hb, mlo,
                                   ev_ml_norm_g[e].reshape(1, -1), ev_w_out[e].astype(BF16), nctx_t)
        else:
            o = l // 2
            wuqt, wdkv, wuk, wuvt = _mla_weights(od_w_uq[o], od_w_dkv[o], od_w_ukv[o])
            qt, kk, vt = _mla_in_call(
                x_all, modsel, ng[0], od_w_dq[o].astype(BF16), od_q_norm_g[o].reshape(1, -1),
                wuqt, wdkv, od_kv_norm_g[o].reshape(1, -1), wuk, wuvt, rope_q, rope_k, nctx_t)
            att_x = _mla_attn_call(qt, kk, vt, nctx_t, t // TM, ttot)
            if ctx_out:
                att_c = _mla_attn_call(qt, kk, vt, 0, nctx_t, nctx)
            else:
                att_c = jnp.zeros((b, nctx, att_x.shape[-1]), BF16)
            att = jnp.concatenate([att_c, att_x], axis=1)
            x_all = _odd_out_call(x_all, modsel, ng[1], att, od_w_o[o].astype(BF16), nctx_t)
        x_all = _ffn_call(x_all, modsel, ng[2], ng[3], ffn_w_up[l].astype(BF16), ffn_conv_w[l],
                          ffn_conv_b[l].reshape(1, -1), ffn_w_down[l].astype(BF16), nctx_t,
                          latents_only=not ctx_out)
    return x_all
```

```python
import functools
import math

import jax
import jax.numpy as jnp
import numpy as np
from jax import lax
from jax.experimental import pallas as pl
from jax.experimental.pallas import tpu as pltpu

F32 = jnp.float32
BF16 = jnp.bfloat16

EPS = 1e-6
NEG = -1e30
LOG2E = math.log2(math.e)

GRID_W = 64
NA_HEADS = 8
NA_HD = 64
NA_WIN_ROWS = 8
NA_WIN_COLS = 16
NA_WIDTH = NA_HEADS * NA_HD

ML_HEADS = 4
ML_HD = 128
ML_WIDTH = ML_HEADS * ML_HD
ML_CHUNK = 128
ML_GATES = 4 * ML_HEADS

MLA_HEADS = 16
MLA_NOPE = 64
MLA_ROPE = 32
MLA_V = 64
MLA_QK = MLA_NOPE + MLA_ROPE
MLA_Q_RANK = 256
MLA_KV_RANK = 128
ROPE_BASE = 10000.0

FFN_HIDDEN = 2816

LANES = 128
SUBLANES_BF16 = 16
TM = 256
VMEM_LIMIT = 48 * 1024 * 1024


def _cparams(sem):
    return pltpu.CompilerParams(dimension_semantics=sem, vmem_limit_bytes=VMEM_LIMIT)


def _resident(shape):
    nd = len(shape)
    return pl.BlockSpec(shape, lambda *_: (0,) * nd, pipeline_mode=pl.Buffered(1))


def _rms(xf, g):
    ms = jnp.mean(xf * xf, axis=-1, keepdims=True)
    return xf * lax.rsqrt(ms + EPS) * g


def _dot(a, b):
    return jnp.dot(a, b, preferred_element_type=F32)


def _dot_nt(a, b):
    return lax.dot_general(a, b, (((1,), (1,)), ((), ())), preferred_element_type=F32)


def _ada_kernel(c_ref, w_ref, b_ref, o_ref):
    c = c_ref[...]
    s = c * jax.nn.sigmoid(c)
    o_ref[0] = _dot(s.astype(BF16), w_ref[0].astype(BF16)) + b_ref[0]


def _ada_call(cvec, ada_w, ada_b):
    depth, d, n = ada_w.shape
    rows = cvec.shape[0]
    tn = 1024
    return pl.pallas_call(
        _ada_kernel,
        out_shape=jax.ShapeDtypeStruct((depth, rows, n), F32),
        grid=(depth, n // tn),
        in_specs=[
            pl.BlockSpec((rows, d), lambda l, j: (0, 0)),
            pl.BlockSpec((1, d, tn), lambda l, j: (l, 0, j)),
            pl.BlockSpec((1, 1, tn), lambda l, j: (l, 0, j)),
        ],
        out_specs=pl.BlockSpec((1, rows, tn), lambda l, j: (l, 0, j)),
        compiler_params=_cparams(("parallel", "parallel")),
        name="ada_mod",
    )(cvec, ada_w, ada_b.reshape(depth, 1, n))


def _x_spec(d):
    return pl.BlockSpec((1, TM, d), lambda b, i: (b, i, 0))


def _mod_spec(d6, nctx_t):
    return pl.BlockSpec((1, 1, 1, d6), lambda b, i: (b, jnp.where(i >= nctx_t, 1, 0), 0, 0))


def _modulated(x, m, g, d, which):
    o = 3 * d * which
    return _rms(x, g) * (1.0 + m[:, o + d:o + 2 * d]) + m[:, o:o + d]


def _even_in_kernel(x_ref, mod_ref, ng_ref, w_ref, wg_ref, gb_ref,
                    naq_ref, nak_ref, nav_ref, mlqk_ref, mlv_ref, mlo_ref, g_ref, *, d):
    m = mod_ref[0, 0]
    hb = _modulated(x_ref[0], m, ng_ref[...], d, 0).astype(BF16)

    def seg(lo, hi):
        return _dot(hb, w_ref[:, lo:hi])

    w = NA_WIDTH
    naq_ref[0] = (seg(0, w) * (NA_HD ** -0.5 * LOG2E)).astype(BF16)
    nak_ref[0] = seg(w, 2 * w).astype(BF16)
    nav_ref[0] = seg(2 * w, 3 * w).astype(BF16)
    o = 3 * w
    mlqk_ref[0, :, 0:ML_WIDTH] = seg(o, o + ML_WIDTH).astype(BF16)
    mlqk_ref[0, :, ML_WIDTH:2 * ML_WIDTH] = seg(o + ML_WIDTH, o + 2 * ML_WIDTH).astype(BF16)
    mlv_ref[0] = seg(o + 2 * ML_WIDTH, o + 3 * ML_WIDTH).astype(BF16)
    mlo_ref[0] = seg(o + 3 * ML_WIDTH, o + 4 * ML_WIDTH).astype(BF16)
    g_ref[0] = _dot(hb, wg_ref[...]) + gb_ref[...]


def _even_in_call(x_all, modsel, ng, w_main, w_gate, gate_b, nctx_t):
    b, ttot, d = x_all.shape
    nt = ttot // TM
    n_main = w_main.shape[1]
    row = lambda c: pl.BlockSpec((1, TM, c), lambda bb, i: (bb, i, 0))
    sds = lambda c, dt: jax.ShapeDtypeStruct((b, ttot, c), dt)
    return pl.pallas_call(
        functools.partial(_even_in_kernel, d=d),
        out_shape=(sds(NA_WIDTH, BF16), sds(NA_WIDTH, BF16), sds(NA_WIDTH, BF16),
                   sds(2 * ML_WIDTH, BF16), sds(ML_WIDTH, BF16), sds(ML_WIDTH, BF16),
                   sds(LANES, F32)),
        grid=(b, nt),
        in_specs=[_x_spec(d), _mod_spec(6 * d, nctx_t), _resident((1, d)),
                  _resident((d, n_main)), _resident((d, LANES)), _resident((1, LANES))],
        out_specs=(row(NA_WIDTH), row(NA_WIDTH), row(NA_WIDTH), row(2 * ML_WIDTH),
                   row(ML_WIDTH), row(ML_WIDTH), row(LANES)),
        compiler_params=_cparams(("parallel", "parallel")),
        name="even_in_proj",
    )(x_all, modsel, ng, w_main, w_gate, gate_b)


def _pair_scores(qp, k_parts, bias_parts):
    lane = lax.broadcasted_iota(jnp.int32, qp.shape, 1)
    scores = []
    for hh in range(2):
        keep = (lane >= NA_HD) if hh else (lane < NA_HD)
        qh = jnp.where(keep, qp, jnp.zeros_like(qp))
        s = []
        for kk, bias in zip(k_parts, bias_parts):
            sp = _dot_nt(qh, kk)
            if bias is not None:
                sp = sp + bias[hh]
            s.append(sp)
        scores.append(s)
    return scores


def _pair_finish(scores, v_parts):
    mq = scores[0][0].shape[0]
    lane = lax.broadcasted_iota(jnp.int32, (mq, LANES), 1)
    v_aug = []
    for vv in v_parts:
        ones_col = jnp.where(lax.broadcasted_iota(jnp.int32, vv.shape, 1) == 0, 1.0, 0.0)
        v_aug.append(jnp.concatenate([vv, ones_col.astype(BF16)], axis=-1))
    outs = []
    for s in scores:
        m = functools.reduce(jnp.maximum, [sp.max(axis=-1, keepdims=True) for sp in s])
        o = jnp.zeros((mq, 2 * LANES), F32)
        for sp, vv in zip(s, v_aug):
            o = o + _dot(jnp.exp2(sp - m).astype(BF16), vv)
        outs.append(o[:, 0:LANES] / o[:, LANES:LANES + 1])
    return jnp.where(lane < NA_HD, outs[0], outs[1])


NA_RG = 4
NA_UB = NA_WIN_ROWS + NA_RG - 1


def _na_union_start(r0, rows):
    return jnp.clip(r0 - NA_WIN_ROWS // 2, 0, rows - NA_UB)


def _na_kernel(q_ref, k_ref, v_ref, bias_ref, o_ref, *, ctx, rows):
    us = _na_union_start(pl.program_id(1) * NA_RG, rows)
    start = pl.multiple_of(ctx + us * GRID_W, GRID_W)
    band = NA_UB * GRID_W
    pairs = [slice(p * LANES, (p + 1) * LANES) for p in range(NA_HEADS // 2)]

    def pair_scores(p):
        cs = pairs[p]
        kb = k_ref[0, pl.ds(start, band), cs]
        kc = k_ref[0, 0:ctx, cs]
        bias = (bias_ref[0, 2 * p], bias_ref[0, 2 * p + 1])
        return _pair_scores(q_ref[0, :, cs], (kb, kc), (bias, None))

    scores = [pair_scores(0)]
    for p, cs in enumerate(pairs):
        if p + 1 < len(pairs):
            scores.append(pair_scores(p + 1))
        vb = v_ref[0, pl.ds(start, band), cs]
        vc = v_ref[0, 0:ctx, cs]
        o_ref[0, :, cs] = _pair_finish(scores[p], (vb, vc)).astype(BF16)


def _na_call(naq, nak, nav, bias_tab, ctx, t):
    b, ttot, w = naq.shape
    rows = t // GRID_W
    groups = rows // NA_RG
    assert rows % NA_RG == 0 and groups >= 3 and ctx % (NA_RG * GRID_W) == 0
    q_off = ctx // (NA_RG * GRID_W)
    mq = NA_RG * GRID_W

    def bias_type(bb, g):
        return (jnp.where(g == 0, 0, jnp.where(g == groups - 1, 2, 1)), 0, 0, 0)

    whole = lambda bb, g: (bb, 0, 0)
    return pl.pallas_call(
        functools.partial(_na_kernel, ctx=ctx, rows=rows),
        out_shape=jax.ShapeDtypeStruct((b, t, w), BF16),
        grid=(b, groups),
        in_specs=[
            pl.BlockSpec((1, mq, w), lambda bb, g: (bb, q_off + g, 0)),
            pl.BlockSpec((1, ttot, w), whole, pipeline_mode=pl.Buffered(1)),
            pl.BlockSpec((1, ttot, w), whole, pipeline_mode=pl.Buffered(1)),
            pl.BlockSpec((1, NA_HEADS, mq, NA_UB * GRID_W), bias_type, pipeline_mode=pl.Buffered(1)),
        ],
        out_specs=pl.BlockSpec((1, mq, w), lambda bb, g: (bb, g, 0)),
        compiler_params=_cparams(("parallel", "arbitrary")),
        name="na_attention",
    )(naq, nak, nav, bias_tab)


def _na_ctx_kernel(q_ref, k_ref, v_ref, o_ref):
    pairs = [slice(p * LANES, (p + 1) * LANES) for p in range(NA_HEADS // 2)]
    scores = [_pair_scores(q_ref[0, :, cs], (k_ref[0, :, cs],), (None,)) for cs in pairs]
    for sc, cs in zip(scores, pairs):
        o_ref[0, :, cs] = _pair_finish(sc, (v_ref[0, :, cs],)).astype(BF16)


def _na_ctx_call(naq, nak, nav, ctx):
    b, _, w = naq.shape
    spec = pl.BlockSpec((1, ctx, w), lambda bb: (bb, 0, 0))
    return pl.pallas_call(
        _na_ctx_kernel,
        out_shape=jax.ShapeDtypeStruct((b, ctx, w), BF16),
        grid=(b,),
        in_specs=[spec, spec, spec],
        out_specs=spec,
        compiler_params=_cparams(("parallel",)),
        name="na_ctx_attention",
    )(naq, nak, nav)


def _na_group_geometry(g, rows):
    r = g * NA_RG + np.arange(NA_RG)[:, None]
    us = int(np.clip(g * NA_RG - NA_WIN_ROWS // 2, 0, rows - NA_UB))
    kr = us + np.arange(NA_UB)[None, :]
    rs = np.clip(r - NA_WIN_ROWS // 2, 0, rows - NA_WIN_ROWS)
    valid = (kr >= rs) & (kr < rs + NA_WIN_ROWS)
    dr_idx = np.clip(kr - r + NA_WIN_ROWS - 1, 0, 2 * NA_WIN_ROWS - 2)
    return dr_idx, valid


def _na_bias_table(rpb, rows):
    groups = rows // NA_RG
    geo = [_na_group_geometry(g, rows) for g in range(groups)]
    for dr_g, valid_g in geo[1:groups - 1]:
        assert (valid_g == geo[1][1]).all() and (dr_g[valid_g] == geo[1][0][valid_g]).all()
    dr_idx = np.stack([geo[g][0] for g in (0, 1, groups - 1)])
    valid = np.stack([geo[g][1] for g in (0, 1, groups - 1)])
    kw = NA_WIN_COLS
    col = jnp.arange(GRID_W)
    cs = jnp.clip(col - kw // 2, 0, GRID_W - kw)
    in_win = (col[None, :] >= cs[:, None]) & (col[None, :] < cs[:, None] + kw)
    dc_idx = jnp.clip(col[None, :] - col[:, None], -(kw - 1), kw - 1) + (NA_WIN_COLS - 1)
    rpb_cols = rpb.astype(F32)[:, :, dc_idx] * LOG2E
    rpb_cols = jnp.where(in_win[None, None], rpb_cols, NEG)
    tab = rpb_cols[:, dr_idx]
    tab = jnp.where(jnp.asarray(valid)[None, :, :, :, None, None], tab, NEG)
    tab = tab.transpose(1, 0, 2, 4, 3, 5)
    return tab.reshape(3, NA_HEADS, NA_RG * GRID_W, NA_UB * GRID_W)


def _halo_specs(c, rows_h, ttot, tile0=0):
    per = TM // rows_h
    last = ttot // rows_h - 1
    prev = pl.BlockSpec((1, rows_h, c), lambda b, i: (b, jnp.maximum((i + tile0) * per - 1, 0), 0))
    nxt = pl.BlockSpec((1, rows_h, c), lambda b, i: (b, jnp.minimum((i + tile0 + 1) * per, last), 0))
    return prev, nxt


def _halo_valid(nctx_t, nt, tile0=0):
    i = pl.program_id(1) + tile0
    pv = jnp.where((i == 0) | (i == nctx_t), 0.0, 1.0).astype(F32)
    nv = jnp.where((i == nctx_t - 1) | (i == nt - 1), 0.0, 1.0).astype(F32)
    return pv, nv


def _conv3(u_ref, h, cw, cb, lo, hi):
    return (cb
            + u_ref[h - 1:h - 1 + TM, lo:hi] * cw[0:1]
            + u_ref[h:h + TM, lo:hi] * cw[1:2]
            + u_ref[h + 1:h + 1 + TM, lo:hi] * cw[2:3])


def _ml_conv_kernel(x_ref, xp_ref, xn_ref, cw_ref, cb_ref, sc_ref, o_ref, kt_ref, u_ref, *, nctx_t, nt):
    pv, nv = _halo_valid(nctx_t, nt)
    h = SUBLANES_BF16
    u_ref[0:h] = xp_ref[0].astype(F32) * pv
    u_ref[h:h + TM] = x_ref[0].astype(F32)
    u_ref[h + TM:h + TM + h] = xn_ref[0].astype(F32) * nv
    c = x_ref.shape[-1]
    y = _conv3(u_ref, h, cw_ref[...], cb_ref[...], 0, c)
    y = y * jax.nn.sigmoid(y) * sc_ref[...]
    o_ref[0] = y.astype(BF16)
    kt_ref[0] = y[:, ML_WIDTH:2 * ML_WIDTH].T.astype(BF16)


def _ml_conv_call(mlqk, conv_w, conv_b, scale, nctx_t):
    b, ttot, c = mlqk.shape
    nt = ttot // TM
    prev, nxt = _halo_specs(c, SUBLANES_BF16, ttot)
    return pl.pallas_call(
        functools.partial(_ml_conv_kernel, nctx_t=nctx_t, nt=nt),
        out_shape=(jax.ShapeDtypeStruct((b, ttot, c), BF16),
                   jax.ShapeDtypeStruct((b, ML_WIDTH, ttot), BF16)),
        grid=(b, nt),
        in_specs=[_x_spec(c), prev, nxt, _resident((3, c)), _resident((1, c)), _resident((1, c))],
        out_specs=(_x_spec(c), pl.BlockSpec((1, ML_WIDTH, TM), lambda bb, i: (bb, 0, i))),
        scratch_shapes=[pltpu.VMEM((TM + 2 * SUBLANES_BF16, c), F32)],
        compiler_params=_cparams(("parallel", "parallel")),
        name="mlstm_conv",
    )(mlqk, mlqk, mlqk, conv_w, conv_b, scale)


def _log_sigmoid(x):
    return jnp.minimum(x, 0.0) - jnp.log(1.0 + jnp.exp(-jnp.abs(x)))


def _split_dot(a, b, data_is_rhs):
    data = b if data_is_rhs else a
    hi = data.astype(BF16)
    lo = (data - hi.astype(F32)).astype(BF16)
    if data_is_rhs:
        return _dot(a, hi) + _dot(a, lo)
    return _dot(hi, b) + _dot(lo, b)


def _mlstm_kernel(qkf_ref, ktf_ref, vf_ref, gf_ref, gtf_ref, qkb_ref, ktb_ref, vb_ref, gb_ref, gtb_ref,
                  hf_ref, hb_ref, ct_ref, m_ref):
    L = ML_CHUNK

    @pl.when(pl.program_id(1) == 0)
    def _():
        ct_ref[...] = jnp.zeros_like(ct_ref)
        m_ref[...] = jnp.zeros_like(m_ref)

    ri = lax.broadcasted_iota(jnp.int32, (L, L), 0)
    ci = lax.broadcasted_iota(jnp.int32, (L, L), 1)
    tril = jnp.where(ri >= ci, 1.0, 0.0).astype(BF16)
    triu = jnp.where(ri <= ci, 1.0, 0.0).astype(BF16)
    lane = lax.broadcasted_iota(jnp.int32, (L, LANES), 1)
    ones_col = jnp.where(lane == 0, 1.0, 0.0).astype(BF16)

    dirs = (
        (qkf_ref, ktf_ref, vf_ref, gf_ref, gtf_ref, hf_ref, tril, triu, ri >= ci, L - 1, 0),
        (qkb_ref, ktb_ref, vb_ref, gb_ref, gtb_ref, hb_ref, triu, tril, ri <= ci, 0, 2 * ML_HEADS),
    )
    chains = []
    for dnum, (qk_ref, kt_ref, v_ref, g_ref, gt_ref, h_ref, tcol, trow, mask, tot_row, goff) in enumerate(dirs):
        gt = gt_ref[0]
        cum_col_all = _split_dot(tcol, _log_sigmoid(g_ref[0]), True)
        cum_row_all = _split_dot(_log_sigmoid(gt), trow, False)
        for hd in range(ML_HEADS):
            ic = goff + hd
            fc = goff + ML_HEADS + hd
            cs = slice(hd * ML_HD, (hd + 1) * ML_HD)
            j = dnum * ML_HEADS + hd
            q = qk_ref[0, :, cs]
            k = qk_ref[0, :, ML_WIDTH + hd * ML_HD:ML_WIDTH + (hd + 1) * ML_HD]
            ct = ct_ref[j]
            cum_col = cum_col_all[:, fc:fc + 1]
            chains.append(dict(
                j=j, h_ref=h_ref, cs=cs, kt=kt_ref[0, cs, :], ct=ct, mask=mask,
                v_aug=jnp.concatenate([v_ref[0, :, cs], ones_col], axis=-1),
                s=_dot_nt(q, k), hq=_dot(q, ct.astype(BF16)),
                cum_col=cum_col, cum_row=cum_row_all[fc:fc + 1, :],
                i_row=gt[ic:ic + 1, :], total=cum_col[tot_row:tot_row + 1, :]))
    for ch in chains:
        m_st = m_ref[ch["j"]][0:1, 0:1]
        cum_col, total = ch["cum_col"], ch["total"]
        e_row = ch["i_row"] - ch["cum_row"]
        base = jnp.where(ch["mask"], e_row, NEG)
        m_inter = cum_col + m_st
        m_t = jnp.maximum(m_inter, cum_col + base.max(axis=-1, keepdims=True))
        g_row = total + e_row
        m_new = jnp.maximum(total + m_st, g_row.max(axis=-1, keepdims=True))
        ch.update(base=base, m_inter=m_inter, m_t=m_t, g_row=g_row, m_new=m_new,
                  decay=jnp.exp(total + m_st - m_new))
    for ch in chains:
        ch["qk"] = (ch["s"] * jnp.exp(ch["base"] + (ch["cum_col"] - ch["m_t"]))).astype(BF16)
        ch["a"] = jnp.exp(ch["m_inter"] - ch["m_t"])
        ch["floor"] = jnp.exp(-ch["m_t"])
        ch["kw"] = (ch["kt"].astype(F32) * jnp.exp(ch["g_row"] - ch["m_new"])).astype(BF16)
    for ch in chains:
        ch["p"] = _dot(ch["qk"], ch["v_aug"])
    for ch in chains:
        ch["u"] = _dot(ch["kw"], ch["v_aug"])
    for ch in chains:
        hfull = ch["a"] * ch["hq"] + ch["p"]
        den = hfull[:, ML_HD:ML_HD + 1]
        h_out = hfull[:, 0:ML_HD] / jnp.maximum(jnp.abs(den), ch["floor"])
        ch["h_ref"][0, :, ch["cs"]] = h_out.astype(BF16)
    for ch in chains:
        ct_ref[ch["j"]] = ch["decay"] * ch["ct"] + ch["u"]
        m_ref[ch["j"]] = jnp.broadcast_to(ch["m_new"], m_ref.shape[1:])


def _mlstm_call(qk, kt, v, gates, gates_t, nctx_c):
    b, ttot, _ = qk.shape
    nc = ttot // ML_CHUNK
    L = ML_CHUNK

    def fwd(bb, c):
        return (bb, c, 0)

    def bwd(bb, c):
        return (bb, jnp.where(c < nctx_c, nctx_c - 1 - c, nc - 1 + nctx_c - c), 0)

    def fwd_t(bb, c):
        return (bb, 0, c)

    def bwd_t(bb, c):
        return (bb, 0, jnp.where(c < nctx_c, nctx_c - 1 - c, nc - 1 + nctx_c - c))

    ins = []
    for row_map, col_map in ((fwd, fwd_t), (bwd, bwd_t)):
        ins += [pl.BlockSpec((1, L, 2 * ML_WIDTH), row_map),
                pl.BlockSpec((1, ML_WIDTH, L), col_map),
                pl.BlockSpec((1, L, ML_WIDTH), row_map),
                pl.BlockSpec((1, L, LANES), row_map),
                pl.BlockSpec((1, ML_GATES, L), col_map)]
    out_sds = jax.ShapeDtypeStruct((b, ttot, ML_WIDTH), BF16)
    return pl.pallas_call(
        _mlstm_kernel,
        out_shape=(out_sds, out_sds),
        grid=(b, nc),
        in_specs=ins,
        out_specs=(pl.BlockSpec((1, L, ML_WIDTH), fwd), pl.BlockSpec((1, L, ML_WIDTH), bwd)),
        scratch_shapes=[pltpu.VMEM((2 * ML_HEADS, ML_HD, 2 * ML_HD), F32),
                        pltpu.VMEM((2 * ML_HEADS, 8, LANES), F32)],
        compiler_params=_cparams(("parallel", "arbitrary")),
        name="mlstm_scan",
    )(qk, kt, v, gates, gates_t, qk, kt, v, gates, gates_t)


def _out_tail(x_ref, mod_ref, ng_ref, y, o_ref, d):
    m = mod_ref[0, 0]
    o_ref[0] = x_ref[0] + m[:, 2 * d:3 * d] * _rms(y, ng_ref[...])


def _ctx_lat_specs(c, nctx_t):
    ctx_spec = pl.BlockSpec((1, TM, c), lambda b, i: (b, jnp.minimum(i, nctx_t - 1), 0))
    lat_spec = pl.BlockSpec((1, TM, c), lambda b, i: (b, jnp.maximum(i - nctx_t, 0), 0))
    return ctx_spec, lat_spec


def _ctx_or_lat(c_ref, l_ref, nctx_t):
    return jnp.where(pl.program_id(1) < nctx_t, c_ref[0], l_ref[0])


def _even_out_kernel(x_ref, mod_ref, ng_ref, nac_ref, nal_ref, hf_ref, hb_ref, op_ref, mlg_ref, w_ref,
                     o_ref, *, d, nctx_t):
    hs = jax.nn.sigmoid(op_ref[0].astype(F32)) * (hf_ref[0].astype(F32) + hb_ref[0].astype(F32))
    parts = []
    for hd in range(ML_HEADS):
        seg = hs[:, hd * ML_HD:(hd + 1) * ML_HD]
        mu = jnp.mean(seg, axis=-1, keepdims=True)
        cen = seg - mu
        var = jnp.mean(cen * cen, axis=-1, keepdims=True)
        parts.append(cen * lax.rsqrt(var + EPS))
    ml = (jnp.concatenate(parts, axis=-1) * mlg_ref[...]).astype(BF16)
    na = _ctx_or_lat(nac_ref, nal_ref, nctx_t)
    y = _dot(na, w_ref[0:NA_WIDTH, :]) + _dot(ml, w_ref[NA_WIDTH:NA_WIDTH + ML_WIDTH, :])
    _out_tail(x_ref, mod_ref, ng_ref, y, o_ref, d)


def _even_out_call(x_all, modsel, ng, na_c, na_x, hf, hb, mlo, mlg, w_out, nctx_t):
    b, ttot, d = x_all.shape
    row = lambda c: pl.BlockSpec((1, TM, c), lambda bb, i: (bb, i, 0))
    return pl.pallas_call(
        functools.partial(_even_out_kernel, d=d, nctx_t=nctx_t),
        out_shape=jax.ShapeDtypeStruct((b, ttot, d), F32),
        grid=(b, ttot // TM),
        in_specs=[_x_spec(d), _mod_spec(6 * d, nctx_t), _resident((1, d)),
                  *_ctx_lat_specs(NA_WIDTH, nctx_t), row(ML_WIDTH), row(ML_WIDTH), row(ML_WIDTH),
                  _resident((1, ML_WIDTH)), _resident(w_out.shape)],
        out_specs=_x_spec(d),
        compiler_params=_cparams(("parallel", "parallel")),
        name="even_out_proj",
    )(x_all, modsel, ng, na_c, na_x, hf, hb, mlo, mlg, w_out)


def _odd_out_kernel(x_ref, mod_ref, ng_ref, ac_ref, al_ref, w_ref, o_ref, *, d, nctx_t):
    a = _ctx_or_lat(ac_ref, al_ref, nctx_t)
    _out_tail(x_ref, mod_ref, ng_ref, _dot(a, w_ref[...]), o_ref, d)


def _odd_out_call(x_all, modsel, ng, att_c, att_x, w_o, nctx_t):
    b, ttot, d = x_all.shape
    return pl.pallas_call(
        functools.partial(_odd_out_kernel, d=d, nctx_t=nctx_t),
        out_shape=jax.ShapeDtypeStruct((b, ttot, d), F32),
        grid=(b, ttot // TM),
        in_specs=[_x_spec(d), _mod_spec(6 * d, nctx_t), _resident((1, d)),
                  *_ctx_lat_specs(att_x.shape[-1], nctx_t), _resident(w_o.shape)],
        out_specs=_x_spec(d),
        compiler_params=_cparams(("parallel", "parallel")),
        name="odd_out_proj",
    )(x_all, modsel, ng, att_c, att_x, w_o)


FFN_CK = 256


def _ffn_kernel(x_ref, xp_ref, xn_ref, mod_ref, ng_in_ref, ng_out_ref, wup_ref, cw_ref, cb_ref,
                wdn_ref, o_ref, h_ref, u_ref, *, d, nctx_t, nt, tile0):
    pv, nv = _halo_valid(nctx_t, nt, tile0)
    m = mod_ref[0, 0]
    g_in = ng_in_ref[...]
    hh = SUBLANES_BF16
    h_ref[0:hh] = (_modulated(xp_ref[0], m, g_in, d, 1) * pv).astype(BF16)
    h_ref[hh:hh + TM] = _modulated(x_ref[0], m, g_in, d, 1).astype(BF16)
    h_ref[hh + TM:hh + TM + hh] = (_modulated(xn_ref[0], m, g_in, d, 1) * nv).astype(BF16)
    hcat = h_ref[...]
    ck = FFN_CK
    nslab = ck // LANES

    nchunks = FFN_HIDDEN // ck

    def conv_slab(buf, s, col):
        cw = cw_ref[:, col:col + LANES]
        return (cb_ref[:, col:col + LANES]
                + u_ref[buf, s, hh - 1:hh - 1 + TM, :] * cw[0:1]
                + u_ref[buf, s, hh:hh + TM, :] * cw[1:2]
                + u_ref[buf, s, hh + 1:hh + 1 + TM, :] * cw[2:3])

    y = jnp.zeros((TM, d), F32)
    for c in range(nchunks + 1):
        if c < nchunks:
            lo = c * ck
            glo = FFN_HIDDEN + lo
            ua = _dot(hcat, wup_ref[:, lo:lo + ck])
            ug = _dot(hcat, wup_ref[:, glo:glo + ck])
            for s in range(nslab):
                u_ref[c % 2, s] = ua[:, s * LANES:(s + 1) * LANES]
                u_ref[c % 2, nslab + s] = ug[:, s * LANES:(s + 1) * LANES]
        if c >= 1:
            lo = (c - 1) * ck
            glo = FFN_HIDDEN + lo
            acts = []
            for s in range(nslab):
                a = conv_slab((c - 1) % 2, s, lo + s * LANES)
                g = conv_slab((c - 1) % 2, nslab + s, glo + s * LANES)
                acts.append((a * (g * jax.nn.sigmoid(g))).astype(BF16))
            y = _dot(jnp.concatenate(acts, axis=-1), wdn_ref[lo:lo + ck, :]) + y
    o_ref[0] = x_ref[0] + m[:, 5 * d:6 * d] * _rms(y, ng_out_ref[...])


def _ffn_call(x_all, modsel, ng_in, ng_out, w_up, conv_w, conv_b, w_down, nctx_t, latents_only):
    b, ttot, d = x_all.shape
    nt = ttot // TM
    hh = SUBLANES_BF16
    tile0 = nctx_t if latents_only else 0
    prev, nxt = _halo_specs(d, hh, ttot, tile0)
    return pl.pallas_call(
        functools.partial(_ffn_kernel, d=d, nctx_t=nctx_t, nt=nt, tile0=tile0),
        out_shape=jax.ShapeDtypeStruct((b, ttot - tile0 * TM, d), F32),
        grid=(b, nt - tile0),
        in_specs=[pl.BlockSpec((1, TM, d), lambda bb, i: (bb, i + tile0, 0)), prev, nxt,
                  _mod_spec(6 * d, nctx_t - tile0), _resident((1, d)),
                  _resident((1, d)), _resident(w_up.shape), _resident(conv_w.shape),
                  _resident(conv_b.shape), _resident(w_down.shape)],
        out_specs=_x_spec(d),
        scratch_shapes=[pltpu.VMEM((TM + 2 * hh, d), BF16),
                        pltpu.VMEM((2, 2 * FFN_CK // LANES, TM + 2 * hh, LANES), F32)],
        compiler_params=_cparams(("parallel", "parallel")),
        name="conv_ffn",
    )(x_all, x_all, x_all, modsel, ng_in, ng_out, w_up, conv_w, conv_b, w_down)


MLA_QK_PAD = LANES
MLA_VA = MLA_V + SUBLANES_BF16


def _mla_in_kernel(x_ref, mod_ref, ng_ref, wdq_ref, qg_ref, wuqt_ref, wdkv_ref, kvg_ref,
                   wuk_ref, wuvt_ref, ropeq_ref, ropek_ref, qt_ref, k_ref, vt_ref, *, d):
    m = mod_ref[0, 0]
    hb = _modulated(x_ref[0], m, ng_ref[...], d, 0).astype(BF16)
    cq = _rms(_dot(hb, wdq_ref[...]), qg_ref[...]).astype(BF16)
    qt_all = _dot_nt(wuqt_ref[...], cq)
    rq = ropeq_ref[...]
    cos_r, sin_r, cos_c, sin_c = rq[0:8], rq[8:16], rq[16:24], rq[24:32]
    scale = MLA_QK ** -0.5 * LOG2E
    for hd in range(MLA_HEADS):
        base = hd * MLA_QK_PAD
        nope = qt_all[base:base + MLA_NOPE]
        x1r = qt_all[base + 64:base + 72]
        x2r = qt_all[base + 72:base + 80]
        x1c = qt_all[base + 80:base + 88]
        x2c = qt_all[base + 88:base + 96]
        pad = qt_all[base + 96:base + 128]
        roped = jnp.concatenate([
            nope,
            x1r * cos_r - x2r * sin_r, x1r * sin_r + x2r * cos_r,
            x1c * cos_c - x2c * sin_c, x1c * sin_c + x2c * cos_c,
            pad], axis=0)
        qt_ref[0, hd] = (roped * scale).astype(BF16)
    ckv = _dot(hb, wdkv_ref[...])
    cn = _rms(ckv[:, 0:MLA_KV_RANK], kvg_ref[...]).astype(BF16)
    rk = ropek_ref[...]
    kpe = ckv[:, LANES:2 * LANES] * rk[:, 0:LANES] + ckv[:, 2 * LANES:3 * LANES] * rk[:, LANES:2 * LANES]
    k_all = _dot(cn, wuk_ref[...])
    vt_all = _dot_nt(wuvt_ref[...], cn)
    ones_rows = jnp.where(lax.broadcasted_iota(jnp.int32, (MLA_VA - MLA_V, TM), 0) == 0, 1.0, 0.0)
    for hd in range(MLA_HEADS):
        k_ref[0, hd] = (k_all[:, hd * MLA_QK_PAD:(hd + 1) * MLA_QK_PAD] + kpe).astype(BF16)
        vt_ref[0, hd] = jnp.concatenate(
            [vt_all[hd * MLA_V:(hd + 1) * MLA_V], ones_rows], axis=0).astype(BF16)


def _mla_in_call(x_all, modsel, ng, wdq, qg, wuqt, wdkv, kvg, wuk, wuvt, rope_q, rope_k, nctx_t):
    b, ttot, d = x_all.shape
    nt = ttot // TM
    hn = MLA_HEADS
    return pl.pallas_call(
        functools.partial(_mla_in_kernel, d=d),
        out_shape=(jax.ShapeDtypeStruct((b, hn, MLA_QK_PAD, ttot), BF16),
                   jax.ShapeDtypeStruct((b, hn, ttot, MLA_QK_PAD), BF16),
                   jax.ShapeDtypeStruct((b, hn, MLA_VA, ttot), BF16)),
        grid=(b, nt),
        in_specs=[_x_spec(d), _mod_spec(6 * d, nctx_t), _resident((1, d)),
                  _resident(wdq.shape), _resident(qg.shape), _resident(wuqt.shape),
                  _resident(wdkv.shape), _resident(kvg.shape), _resident(wuk.shape),
                  _resident(wuvt.shape),
                  pl.BlockSpec((32, TM), lambda bb, i: (0, i)),
                  pl.BlockSpec((TM, 2 * LANES), lambda bb, i: (i, 0))],
        out_specs=(pl.BlockSpec((1, hn, MLA_QK_PAD, TM), lambda bb, i: (bb, 0, 0, i)),
                   pl.BlockSpec((1, hn, TM, MLA_QK_PAD), lambda bb, i: (bb, 0, i, 0)),
                   pl.BlockSpec((1, hn, MLA_VA, TM), lambda bb, i: (bb, 0, 0, i))),
        compiler_params=_cparams(("parallel", "parallel")),
        name="mla_in_proj",
    )(x_all, modsel, ng, wdq, qg, wuqt, wdkv, kvg, wuk, wuvt, rope_q, rope_k)


MLA_KC = 256
MLA_LOOKAHEAD = 6
MLA_HPS = 8


def _mla_attn_kernel(qt_ref, k_ref, vt_ref, o_ref, *, nkc):
    outs = []
    for hh in range(MLA_HPS):
        qt = qt_ref[0, hh]
        ms, accs, scores = [], [], []

        def chunk(c):
            return slice(c * MLA_KC, (c + 1) * MLA_KC)

        for c in range(nkc + MLA_LOOKAHEAD):
            if c < nkc:
                scores.append(_dot(k_ref[0, hh, chunk(c), :], qt))
            cc = c - MLA_LOOKAHEAD
            if cc >= 0:
                s = scores[cc]
                m_c = s.max(axis=0, keepdims=True)
                ms.append(m_c)
                accs.append(_dot(vt_ref[0, hh, :, chunk(cc)], jnp.exp2(s - m_c).astype(BF16)))
        m = functools.reduce(jnp.maximum, ms)
        acc = jnp.zeros((MLA_VA, TM), F32)
        for m_c, a_c in zip(ms, accs):
            acc = acc + jnp.exp2(m_c - m) * a_c
        outs.append(acc[0:MLA_V] / acc[MLA_V:MLA_V + 1])
    o_ref[0] = jnp.concatenate(outs, axis=0).T.astype(BF16)


def _mla_attn_call(qt, k, vt, q_tile0, n_qtiles, kv_len):
    b, hn, _, ttot = qt.shape
    return pl.pallas_call(
        functools.partial(_mla_attn_kernel, nkc=kv_len // MLA_KC),
        out_shape=jax.ShapeDtypeStruct((b, n_qtiles * TM, hn * MLA_V), BF16),
        grid=(b, hn // MLA_HPS, n_qtiles),
        in_specs=[
            pl.BlockSpec((1, MLA_HPS, MLA_QK_PAD, TM), lambda bb, hp, qi: (bb, hp, 0, q_tile0 + qi)),
            pl.BlockSpec((1, MLA_HPS, kv_len, MLA_QK_PAD), lambda bb, hp, qi: (bb, hp, 0, 0)),
            pl.BlockSpec((1, MLA_HPS, MLA_VA, kv_len), lambda bb, hp, qi: (bb, hp, 0, 0)),
        ],
        out_specs=pl.BlockSpec((1, TM, MLA_HPS * MLA_V), lambda bb, hp, qi: (bb, qi, hp)),
        compiler_params=_cparams(("parallel", "parallel", "arbitrary")),
        name="mla_attention",
    )(qt, k, vt)


def _rope_tables(ctx, t):
    pos = np.arange(t)
    half = MLA_ROPE // 4
    inv = ROPE_BASE ** (-jnp.arange(half, dtype=F32) / half)
    tabs = []
    for p in (pos // GRID_W, pos % GRID_W):
        ang = jnp.asarray(p, F32)[:, None] * inv
        cos = jnp.concatenate([jnp.ones((ctx, half), F32), jnp.cos(ang)], axis=0)
        sin = jnp.concatenate([jnp.zeros((ctx, half), F32), jnp.sin(ang)], axis=0)
        tabs.append((cos, sin))
    (cr, sr), (cc, sc) = tabs
    rope_q = jnp.concatenate([cr, sr, cc, sc], axis=1).T
    ttot = ctx + t
    zeros = jnp.zeros((ttot, MLA_NOPE), F32)
    tail = jnp.zeros((ttot, LANES - MLA_QK), F32)
    cos_k = jnp.concatenate([zeros, cr, cr, cc, cc, tail], axis=1)
    sin_k = jnp.concatenate([zeros, -sr, sr, -sc, sc, tail], axis=1)
    rope_k = jnp.concatenate([cos_k, sin_k], axis=1)
    return rope_q, rope_k


def _mla_weights(w_uq, w_dkv, w_ukv):
    hn = MLA_HEADS
    q_rank = w_uq.shape[0]
    wq = w_uq.reshape(q_rank, hn, MLA_QK)
    wq = jnp.pad(wq, ((0, 0), (0, 0), (0, MLA_QK_PAD - MLA_QK)))
    wuqt = wq.reshape(q_rank, hn * MLA_QK_PAD).T.astype(BF16)
    d = w_dkv.shape[0]
    kpe = w_dkv[:, MLA_KV_RANK:]
    h8 = MLA_ROPE // 4
    swap = jnp.concatenate([kpe[:, h8:2 * h8], kpe[:, 0:h8], kpe[:, 3 * h8:4 * h8],
                            kpe[:, 2 * h8:3 * h8]], axis=1)

    def place(cols):
        return jnp.concatenate([jnp.zeros((d, MLA_NOPE), F32), cols,
                                jnp.zeros((d, LANES - MLA_QK), F32)], axis=1)

    wdkv = jnp.concatenate([w_dkv[:, :MLA_KV_RANK], place(kpe), place(swap)], axis=1).astype(BF16)
    wkv = w_ukv.reshape(MLA_KV_RANK, hn, MLA_NOPE + MLA_V)
    wuk = jnp.pad(wkv[:, :, :MLA_NOPE], ((0, 0), (0, 0), (0, MLA_QK_PAD - MLA_NOPE)))
    wuk = wuk.reshape(MLA_KV_RANK, hn * MLA_QK_PAD).astype(BF16)
    wuvt = wkv[:, :, MLA_NOPE:].reshape(MLA_KV_RANK, hn * MLA_V).T.astype(BF16)
    return wuqt, wdkv, wuk, wuvt


def kernel(x, c, ctx, c_ctx, ada_w, ada_b, norm_g, ffn_w_up, ffn_conv_w, ffn_conv_b, ffn_w_down,
           ev_w_in, ev_gate_b, ev_conv_w, ev_conv_b, ev_rpb, ev_ml_norm_g, ev_w_out,
           od_w_dq, od_q_norm_g, od_w_uq, od_w_dkv, od_kv_norm_g, od_w_ukv, od_w_o):
    b, t, d = x.shape
    nctx = ctx.shape[1]
    depth = ada_w.shape[0]
    ttot = nctx + t
    assert t % TM == 0 and nctx % TM == 0 and t % GRID_W == 0
    nctx_t = nctx // TM

    x_all = jnp.concatenate([ctx, x], axis=1)

    rows = -(-(b + 1) // 8) * 8
    cvec = jnp.zeros((rows, d), F32).at[:b].set(c).at[b].set(c_ctx)
    mod = _ada_call(cvec, ada_w, ada_b)

    rope_q, rope_k = _rope_tables(nctx, t)
    ml_scale = jnp.concatenate([jnp.full((1, ML_WIDTH), ML_HD ** -0.5, F32),
                                jnp.ones((1, ML_WIDTH), F32)], axis=1)

    for l in range(depth):
        ctx_out = l < depth - 1
        modsel = jnp.stack([jnp.broadcast_to(mod[l, b], (b, 6 * d)), mod[l, :b]], axis=1)
        modsel = modsel.reshape(b, 2, 1, 6 * d)
        ng = norm_g[l].reshape(4, 1, d)
        if l % 2 == 0:
            e = l // 2
            w_in = ev_w_in[e]
            n_main = 3 * NA_WIDTH + 4 * ML_WIDTH
            w_gate = jnp.pad(w_in[:, n_main:], ((0, 0), (0, LANES - ML_GATES))).astype(BF16)
            gate_b = jnp.pad(ev_gate_b[e], (0, LANES - ML_GATES)).reshape(1, LANES)
            naq, nak, nav, mlqk, mlv, mlo, gates = _even_in_call(
                x_all, modsel, ng[0], w_in[:, :n_main].astype(BF16), w_gate, gate_b, nctx_t)
            na_x = _na_call(naq, nak, nav, _na_bias_table(ev_rpb[e], t // GRID_W), nctx, t)
            na_c = _na_ctx_call(naq, nak, nav, nctx)
            qk, kt = _ml_conv_call(mlqk, ev_conv_w[e], ev_conv_b[e].reshape(1, -1), ml_scale, nctx_t)
            gates_t = jnp.swapaxes(gates[:, :, :ML_GATES], 1, 2)
            hf, hb = _mlstm_call(qk, kt, mlv, gates, gates_t, nctx // ML_CHUNK)
            x_all = _even_out_call(x_all, modsel, ng[1], na_c, na_x, hf, hb, mlo,
                                   ev_ml_norm_g[e].reshape(1, -1), ev_w_out[e].astype(BF16), nctx_t)
        else:
            o = l // 2
            wuqt, wdkv, wuk, wuvt = _mla_weights(od_w_uq[o], od_w_dkv[o], od_w_ukv[o])
            qt, kk, vt = _mla_in_call(
                x_all, modsel, ng[0], od_w_dq[o].astype(BF16), od_q_norm_g[o].reshape(1, -1),
                wuqt, wdkv, od_kv_norm_g[o].reshape(1, -1), wuk, wuvt, rope_q, rope_k, nctx_t)
            att_x = _mla_attn_call(qt, kk, vt, nctx_t, t // TM, ttot)
            att_c = _mla_attn_call(qt, kk, vt, 0, nctx_t, nctx) if ctx_out else att_x
            x_all = _odd_out_call(x_all, modsel, ng[1], att_c, att_x, od_w_o[o].astype(BF16), nctx_t)
        x_all = _ffn_call(x_all, modsel, ng[2], ng[3], ffn_w_up[l].astype(BF16), ffn_conv_w[l],
                          ffn_conv_b[l].reshape(1, -1), ffn_w_down[l].astype(BF16), nctx_t,
                          latents_only=not ctx_out)
    return x_all
```

```python
import functools
import math

import jax
import jax.numpy as jnp
import numpy as np
from jax import lax
from jax.experimental import pallas as pl
from jax.experimental.pallas import tpu as pltpu

F32 = jnp.float32
BF16 = jnp.bfloat16

EPS = 1e-6
NEG = -1e30
LOG2E = math.log2(math.e)

GRID_W = 64
NA_HEADS = 8
NA_HD = 64
NA_WIN_ROWS = 8
NA_WIN_COLS = 16
NA_WIDTH = NA_HEADS * NA_HD

ML_HEADS = 4
ML_HD = 128
ML_WIDTH = ML_HEADS * ML_HD
ML_CHUNK = 128
ML_GATES = 4 * ML_HEADS

MLA_HEADS = 16
MLA_NOPE = 64
MLA_ROPE = 32
MLA_V = 64
MLA_QK = MLA_NOPE + MLA_ROPE
MLA_Q_RANK = 256
MLA_KV_RANK = 128
ROPE_BASE = 10000.0

FFN_HIDDEN = 2816

LANES = 128
SUBLANES_BF16 = 16
TM = 256
VMEM_LIMIT = 48 * 1024 * 1024


def _cparams(sem):
    return pltpu.CompilerParams(dimension_semantics=sem, vmem_limit_bytes=VMEM_LIMIT)


def _resident(shape):
    nd = len(shape)
    return pl.BlockSpec(shape, lambda *_: (0,) * nd, pipeline_mode=pl.Buffered(1))


def _rms(xf, g):
    ms = jnp.mean(xf * xf, axis=-1, keepdims=True)
    return xf * lax.rsqrt(ms + EPS) * g


def _dot(a, b):
    return jnp.dot(a, b, preferred_element_type=F32)


def _dot_nt(a, b):
    return lax.dot_general(a, b, (((1,), (1,)), ((), ())), preferred_element_type=F32)


def _ada_kernel(c_ref, w_ref, b_ref, o_ref):
    c = c_ref[...]
    s = c * jax.nn.sigmoid(c)
    o_ref[0] = _dot(s.astype(BF16), w_ref[0].astype(BF16)) + b_ref[0]


def _ada_call(cvec, ada_w, ada_b):
    depth, d, n = ada_w.shape
    rows = cvec.shape[0]
    tn = 1024
    return pl.pallas_call(
        _ada_kernel,
        out_shape=jax.ShapeDtypeStruct((depth, rows, n), F32),
        grid=(depth, n // tn),
        in_specs=[
            pl.BlockSpec((rows, d), lambda l, j: (0, 0)),
            pl.BlockSpec((1, d, tn), lambda l, j: (l, 0, j)),
            pl.BlockSpec((1, 1, tn), lambda l, j: (l, 0, j)),
        ],
        out_specs=pl.BlockSpec((1, rows, tn), lambda l, j: (l, 0, j)),
        compiler_params=_cparams(("parallel", "parallel")),
        name="ada_mod",
    )(cvec, ada_w, ada_b.reshape(depth, 1, n))


def _x_spec(d):
    return pl.BlockSpec((1, TM, d), lambda b, i: (b, i, 0))


def _mod_spec(d6, nctx_t):
    return pl.BlockSpec((1, 1, 1, d6), lambda b, i: (b, jnp.where(i >= nctx_t, 1, 0), 0, 0))


def _modulated(x, m, g, d, which):
    o = 3 * d * which
    return _rms(x, g) * (1.0 + m[:, o + d:o + 2 * d]) + m[:, o:o + d]


def _even_in_kernel(x_ref, mod_ref, ng_ref, w_ref, wg_ref, gb_ref,
                    naq_ref, nak_ref, nav_ref, mlqk_ref, mlv_ref, mlo_ref, g_ref, *, d):
    m = mod_ref[0, 0]
    hb = _modulated(x_ref[0], m, ng_ref[...], d, 0).astype(BF16)

    def seg(lo, hi):
        return _dot(hb, w_ref[:, lo:hi])

    w = NA_WIDTH
    naq_ref[0] = (seg(0, w) * (NA_HD ** -0.5 * LOG2E)).astype(BF16)
    nak_ref[0] = seg(w, 2 * w).astype(BF16)
    nav_ref[0] = seg(2 * w, 3 * w).astype(BF16)
    o = 3 * w
    mlqk_ref[0, :, 0:ML_WIDTH] = seg(o, o + ML_WIDTH).astype(BF16)
    mlqk_ref[0, :, ML_WIDTH:2 * ML_WIDTH] = seg(o + ML_WIDTH, o + 2 * ML_WIDTH).astype(BF16)
    mlv_ref[0] = seg(o + 2 * ML_WIDTH, o + 3 * ML_WIDTH).astype(BF16)
    mlo_ref[0] = seg(o + 3 * ML_WIDTH, o + 4 * ML_WIDTH).astype(BF16)
    g_ref[0] = _dot(hb, wg_ref[...]) + gb_ref[...]


def _even_in_call(x_all, modsel, ng, w_main, w_gate, gate_b, nctx_t):
    b, ttot, d = x_all.shape
    nt = ttot // TM
    n_main = w_main.shape[1]
    row = lambda c: pl.BlockSpec((1, TM, c), lambda bb, i: (bb, i, 0))
    sds = lambda c, dt: jax.ShapeDtypeStruct((b, ttot, c), dt)
    return pl.pallas_call(
        functools.partial(_even_in_kernel, d=d),
        out_shape=(sds(NA_WIDTH, BF16), sds(NA_WIDTH, BF16), sds(NA_WIDTH, BF16),
                   sds(2 * ML_WIDTH, BF16), sds(ML_WIDTH, BF16), sds(ML_WIDTH, BF16),
                   sds(LANES, F32)),
        grid=(b, nt),
        in_specs=[_x_spec(d), _mod_spec(6 * d, nctx_t), _resident((1, d)),
                  _resident((d, n_main)), _resident((d, LANES)), _resident((1, LANES))],
        out_specs=(row(NA_WIDTH), row(NA_WIDTH), row(NA_WIDTH), row(2 * ML_WIDTH),
                   row(ML_WIDTH), row(ML_WIDTH), row(LANES)),
        compiler_params=_cparams(("parallel", "parallel")),
        name="even_in_proj",
    )(x_all, modsel, ng, w_main, w_gate, gate_b)


def _pair_scores(qp, k_parts, bias_parts):
    lane = lax.broadcasted_iota(jnp.int32, qp.shape, 1)
    scores = []
    for hh in range(2):
        keep = (lane >= NA_HD) if hh else (lane < NA_HD)
        qh = jnp.where(keep, qp, jnp.zeros_like(qp))
        s = []
        for kk, bias in zip(k_parts, bias_parts):
            sp = _dot_nt(qh, kk)
            if bias is not None:
                sp = sp + bias[hh]
            s.append(sp)
        scores.append(s)
    return scores


def _pair_finish(scores, v_parts):
    mq = scores[0][0].shape[0]
    lane = lax.broadcasted_iota(jnp.int32, (mq, LANES), 1)
    v_aug = []
    for vv in v_parts:
        ones_col = jnp.where(lax.broadcasted_iota(jnp.int32, vv.shape, 1) == 0, 1.0, 0.0)
        v_aug.append(jnp.concatenate([vv, ones_col.astype(BF16)], axis=-1))
    outs = []
    for s in scores:
        m = functools.reduce(jnp.maximum, [sp.max(axis=-1, keepdims=True) for sp in s])
        o = jnp.zeros((mq, 2 * LANES), F32)
        for sp, vv in zip(s, v_aug):
            o = o + _dot(jnp.exp2(sp - m).astype(BF16), vv)
        outs.append(o[:, 0:LANES] / o[:, LANES:LANES + 1])
    return jnp.where(lane < NA_HD, outs[0], outs[1])


NA_RG = 4
NA_UB = NA_WIN_ROWS + NA_RG - 1


def _na_union_start(r0, rows):
    return jnp.clip(r0 - NA_WIN_ROWS // 2, 0, rows - NA_UB)


def _na_kernel(q_ref, k_ref, v_ref, bias_ref, o_ref, *, ctx, rows):
    us = _na_union_start(pl.program_id(1) * NA_RG, rows)
    start = pl.multiple_of(ctx + us * GRID_W, GRID_W)
    band = NA_UB * GRID_W
    pairs = [slice(p * LANES, (p + 1) * LANES) for p in range(NA_HEADS // 2)]

    def pair_scores(p):
        cs = pairs[p]
        kb = k_ref[0, pl.ds(start, band), cs]
        kc = k_ref[0, 0:ctx, cs]
        bias = (bias_ref[0, 2 * p], bias_ref[0, 2 * p + 1])
        return _pair_scores(q_ref[0, :, cs], (kb, kc), (bias, None))

    scores = [pair_scores(0)]
    for p, cs in enumerate(pairs):
        if p + 1 < len(pairs):
            scores.append(pair_scores(p + 1))
        vb = v_ref[0, pl.ds(start, band), cs]
        vc = v_ref[0, 0:ctx, cs]
        o_ref[0, :, cs] = _pair_finish(scores[p], (vb, vc)).astype(BF16)


def _na_call(naq, nak, nav, bias_tab, ctx, t):
    b, ttot, w = naq.shape
    rows = t // GRID_W
    groups = rows // NA_RG
    assert rows % NA_RG == 0 and groups >= 3 and ctx % (NA_RG * GRID_W) == 0
    q_off = ctx // (NA_RG * GRID_W)
    mq = NA_RG * GRID_W

    def bias_type(bb, g):
        return (jnp.where(g == 0, 0, jnp.where(g == groups - 1, 2, 1)), 0, 0, 0)

    whole = lambda bb, g: (bb, 0, 0)
    return pl.pallas_call(
        functools.partial(_na_kernel, ctx=ctx, rows=rows),
        out_shape=jax.ShapeDtypeStruct((b, t, w), BF16),
        grid=(b, groups),
        in_specs=[
            pl.BlockSpec((1, mq, w), lambda bb, g: (bb, q_off + g, 0)),
            pl.BlockSpec((1, ttot, w), whole, pipeline_mode=pl.Buffered(1)),
            pl.BlockSpec((1, ttot, w), whole, pipeline_mode=pl.Buffered(1)),
            pl.BlockSpec((1, NA_HEADS, mq, NA_UB * GRID_W), bias_type),
        ],
        out_specs=pl.BlockSpec((1, mq, w), lambda bb, g: (bb, g, 0)),
        compiler_params=_cparams(("parallel", "arbitrary")),
        name="na_attention",
    )(naq, nak, nav, bias_tab)


def _na_ctx_kernel(q_ref, k_ref, v_ref, o_ref):
    pairs = [slice(p * LANES, (p + 1) * LANES) for p in range(NA_HEADS // 2)]
    scores = [_pair_scores(q_ref[0, :, cs], (k_ref[0, :, cs],), (None,)) for cs in pairs]
    for sc, cs in zip(scores, pairs):
        o_ref[0, :, cs] = _pair_finish(sc, (v_ref[0, :, cs],)).astype(BF16)


def _na_ctx_call(naq, nak, nav, ctx):
    b, _, w = naq.shape
    spec = pl.BlockSpec((1, ctx, w), lambda bb: (bb, 0, 0))
    return pl.pallas_call(
        _na_ctx_kernel,
        out_shape=jax.ShapeDtypeStruct((b, ctx, w), BF16),
        grid=(b,),
        in_specs=[spec, spec, spec],
        out_specs=spec,
        compiler_params=_cparams(("parallel",)),
        name="na_ctx_attention",
    )(naq, nak, nav)


def _na_group_geometry(g, rows):
    r = g * NA_RG + np.arange(NA_RG)[:, None]
    us = int(np.clip(g * NA_RG - NA_WIN_ROWS // 2, 0, rows - NA_UB))
    kr = us + np.arange(NA_UB)[None, :]
    rs = np.clip(r - NA_WIN_ROWS // 2, 0, rows - NA_WIN_ROWS)
    valid = (kr >= rs) & (kr < rs + NA_WIN_ROWS)
    dr_idx = np.clip(kr - r + NA_WIN_ROWS - 1, 0, 2 * NA_WIN_ROWS - 2)
    return dr_idx, valid


def _na_bias_table(rpb, rows):
    groups = rows // NA_RG
    geo = [_na_group_geometry(g, rows) for g in range(groups)]
    for dr_g, valid_g in geo[1:groups - 1]:
        assert (valid_g == geo[1][1]).all() and (dr_g[valid_g] == geo[1][0][valid_g]).all()
    dr_idx = np.stack([geo[g][0] for g in (0, 1, groups - 1)])
    valid = np.stack([geo[g][1] for g in (0, 1, groups - 1)])
    kw = NA_WIN_COLS
    col = jnp.arange(GRID_W)
    cs = jnp.clip(col - kw // 2, 0, GRID_W - kw)
    in_win = (col[None, :] >= cs[:, None]) & (col[None, :] < cs[:, None] + kw)
    dc_idx = jnp.clip(col[None, :] - col[:, None], -(kw - 1), kw - 1) + (NA_WIN_COLS - 1)
    rpb_cols = rpb.astype(F32)[:, :, dc_idx] * LOG2E
    rpb_cols = jnp.where(in_win[None, None], rpb_cols, NEG)
    tab = rpb_cols[:, dr_idx]
    tab = jnp.where(jnp.asarray(valid)[None, :, :, :, None, None], tab, NEG)
    tab = tab.transpose(1, 0, 2, 4, 3, 5)
    return tab.reshape(3, NA_HEADS, NA_RG * GRID_W, NA_UB * GRID_W)


def _halo_specs(c, rows_h, ttot, tile0=0):
    per = TM // rows_h
    last = ttot // rows_h - 1
    prev = pl.BlockSpec((1, rows_h, c), lambda b, i: (b, jnp.maximum((i + tile0) * per - 1, 0), 0))
    nxt = pl.BlockSpec((1, rows_h, c), lambda b, i: (b, jnp.minimum((i + tile0 + 1) * per, last), 0))
    return prev, nxt


def _halo_valid(nctx_t, nt, tile0=0):
    i = pl.program_id(1) + tile0
    pv = jnp.where((i == 0) | (i == nctx_t), 0.0, 1.0).astype(F32)
    nv = jnp.where((i == nctx_t - 1) | (i == nt - 1), 0.0, 1.0).astype(F32)
    return pv, nv


def _conv3(u_ref, h, cw, cb, lo, hi):
    return (cb
            + u_ref[h - 1:h - 1 + TM, lo:hi] * cw[0:1]
            + u_ref[h:h + TM, lo:hi] * cw[1:2]
            + u_ref[h + 1:h + 1 + TM, lo:hi] * cw[2:3])


def _ml_conv_kernel(x_ref, xp_ref, xn_ref, cw_ref, cb_ref, sc_ref, o_ref, kt_ref, u_ref, *, nctx_t, nt):
    pv, nv = _halo_valid(nctx_t, nt)
    h = SUBLANES_BF16
    u_ref[0:h] = xp_ref[0].astype(F32) * pv
    u_ref[h:h + TM] = x_ref[0].astype(F32)
    u_ref[h + TM:h + TM + h] = xn_ref[0].astype(F32) * nv
    c = x_ref.shape[-1]
    y = _conv3(u_ref, h, cw_ref[...], cb_ref[...], 0, c)
    y = y * jax.nn.sigmoid(y) * sc_ref[...]
    o_ref[0] = y.astype(BF16)
    kt_ref[0] = y[:, ML_WIDTH:2 * ML_WIDTH].T.astype(BF16)


def _ml_conv_call(mlqk, conv_w, conv_b, scale, nctx_t):
    b, ttot, c = mlqk.shape
    nt = ttot // TM
    prev, nxt = _halo_specs(c, SUBLANES_BF16, ttot)
    return pl.pallas_call(
        functools.partial(_ml_conv_kernel, nctx_t=nctx_t, nt=nt),
        out_shape=(jax.ShapeDtypeStruct((b, ttot, c), BF16),
                   jax.ShapeDtypeStruct((b, ML_WIDTH, ttot), BF16)),
        grid=(b, nt),
        in_specs=[_x_spec(c), prev, nxt, _resident((3, c)), _resident((1, c)), _resident((1, c))],
        out_specs=(_x_spec(c), pl.BlockSpec((1, ML_WIDTH, TM), lambda bb, i: (bb, 0, i))),
        scratch_shapes=[pltpu.VMEM((TM + 2 * SUBLANES_BF16, c), F32)],
        compiler_params=_cparams(("parallel", "parallel")),
        name="mlstm_conv",
    )(mlqk, mlqk, mlqk, conv_w, conv_b, scale)


def _log_sigmoid(x):
    return jnp.minimum(x, 0.0) - jnp.log(1.0 + jnp.exp(-jnp.abs(x)))


def _split_dot(a, b, data_is_rhs):
    data = b if data_is_rhs else a
    hi = data.astype(BF16)
    lo = (data - hi.astype(F32)).astype(BF16)
    if data_is_rhs:
        return _dot(a, hi) + _dot(a, lo)
    return _dot(hi, b) + _dot(lo, b)


def _mlstm_kernel(qkf_ref, ktf_ref, vf_ref, gf_ref, gtf_ref, qkb_ref, ktb_ref, vb_ref, gb_ref, gtb_ref,
                  hf_ref, hb_ref, ct_ref, m_ref):
    L = ML_CHUNK

    @pl.when(pl.program_id(1) == 0)
    def _():
        ct_ref[...] = jnp.zeros_like(ct_ref)
        m_ref[...] = jnp.zeros_like(m_ref)

    ri = lax.broadcasted_iota(jnp.int32, (L, L), 0)
    ci = lax.broadcasted_iota(jnp.int32, (L, L), 1)
    tril = jnp.where(ri >= ci, 1.0, 0.0).astype(BF16)
    triu = jnp.where(ri <= ci, 1.0, 0.0).astype(BF16)
    ones_col = jnp.ones((L, LANES), BF16)

    dirs = (
        (qkf_ref, ktf_ref, vf_ref, gf_ref, gtf_ref, hf_ref, tril, triu, ri >= ci, L - 1, 0),
        (qkb_ref, ktb_ref, vb_ref, gb_ref, gtb_ref, hb_ref, triu, tril, ri <= ci, 0, 2 * ML_HEADS),
    )
    chains = []
    for dnum, (qk_ref, kt_ref, v_ref, g_ref, gt_ref, h_ref, tcol, trow, mask, tot_row, goff) in enumerate(dirs):
        gt = gt_ref[0]
        cum_col_all = _split_dot(tcol, _log_sigmoid(g_ref[0]), True)
        cum_row_all = _split_dot(_log_sigmoid(gt), trow, False)
        for hd in range(ML_HEADS):
            ic = goff + hd
            fc = goff + ML_HEADS + hd
            cs = slice(hd * ML_HD, (hd + 1) * ML_HD)
            j = dnum * ML_HEADS + hd
            q = qk_ref[0, :, cs]
            k = qk_ref[0, :, ML_WIDTH + hd * ML_HD:ML_WIDTH + (hd + 1) * ML_HD]
            ct = ct_ref[j]
            cum_col = cum_col_all[:, fc:fc + 1]
            chains.append(dict(
                j=j, h_ref=h_ref, cs=cs, kt=kt_ref[0, cs, :], ct=ct, mask=mask,
                v_aug=jnp.concatenate([v_ref[0, :, cs], ones_col], axis=-1),
                s=_dot_nt(q, k), hq=_dot(q, ct.astype(BF16)),
                cum_col=cum_col, cum_row=cum_row_all[fc:fc + 1, :],
                i_row=gt[ic:ic + 1, :], total=cum_col[tot_row:tot_row + 1, :]))
    for ch in chains:
        m_st = m_ref[ch["j"]][0:1, 0:1]
        cum_col, total = ch["cum_col"], ch["total"]
        e_row = ch["i_row"] - ch["cum_row"]
        base = jnp.where(ch["mask"], e_row, NEG)
        m_inter = cum_col + m_st
        m_t = jnp.maximum(m_inter, cum_col + base.max(axis=-1, keepdims=True))
        g_row = total + e_row
        m_new = jnp.maximum(total + m_st, g_row.max(axis=-1, keepdims=True))
        ch.update(base=base, m_inter=m_inter, m_t=m_t, g_row=g_row, m_new=m_new,
                  decay=jnp.exp(total + m_st - m_new))
    for ch in chains:
        ch["kw"] = (ch["kt"].astype(F32) * jnp.exp(ch["g_row"] - ch["m_new"])).astype(BF16)
    for ch in chains:
        ch["u"] = _dot(ch["kw"], ch["v_aug"])
    for ch in chains:
        ch["qk"] = (ch["s"] * jnp.exp(ch["base"] + (ch["cum_col"] - ch["m_t"]))).astype(BF16)
        ch["a"] = jnp.exp(ch["m_inter"] - ch["m_t"])
        ch["floor"] = jnp.exp(-ch["m_t"])
    for ch in chains:
        ch["p"] = _dot(ch["qk"], ch["v_aug"])
    for ch in chains:
        ct_ref[ch["j"]] = ch["decay"] * ch["ct"] + ch["u"]
        m_ref[ch["j"]] = jnp.broadcast_to(ch["m_new"], m_ref.shape[1:])
    for ch in chains:
        hfull = ch["a"] * ch["hq"] + ch["p"]
        den = hfull[:, ML_HD:2 * ML_HD]
        h_out = hfull[:, 0:ML_HD] / jnp.maximum(jnp.abs(den), ch["floor"])
        ch["h_ref"][0, :, ch["cs"]] = h_out.astype(BF16)


def _mlstm_call(qk, kt, v, gates, gates_t, nctx_c):
    b, ttot, _ = qk.shape
    nc = ttot // ML_CHUNK
    L = ML_CHUNK

    def fwd(bb, c):
        return (bb, c, 0)

    def bwd(bb, c):
        return (bb, jnp.where(c < nctx_c, nctx_c - 1 - c, nc - 1 + nctx_c - c), 0)

    def fwd_t(bb, c):
        return (bb, 0, c)

    def bwd_t(bb, c):
        return (bb, 0, jnp.where(c < nctx_c, nctx_c - 1 - c, nc - 1 + nctx_c - c))

    ins = []
    for row_map, col_map in ((fwd, fwd_t), (bwd, bwd_t)):
        ins += [pl.BlockSpec((1, L, 2 * ML_WIDTH), row_map),
                pl.BlockSpec((1, ML_WIDTH, L), col_map),
                pl.BlockSpec((1, L, ML_WIDTH), row_map),
                pl.BlockSpec((1, L, LANES), row_map),
                pl.BlockSpec((1, ML_GATES, L), col_map)]
    out_sds = jax.ShapeDtypeStruct((b, ttot, ML_WIDTH), BF16)
    return pl.pallas_call(
        _mlstm_kernel,
        out_shape=(out_sds, out_sds),
        grid=(b, nc),
        in_specs=ins,
        out_specs=(pl.BlockSpec((1, L, ML_WIDTH), fwd), pl.BlockSpec((1, L, ML_WIDTH), bwd)),
        scratch_shapes=[pltpu.VMEM((2 * ML_HEADS, ML_HD, 2 * ML_HD), F32),
                        pltpu.VMEM((2 * ML_HEADS, 8, LANES), F32)],
        compiler_params=_cparams(("parallel", "arbitrary")),
        name="mlstm_scan",
    )(qk, kt, v, gates, gates_t, qk, kt, v, gates, gates_t)


def _out_tail(x_ref, mod_ref, ng_ref, y, o_ref, d):
    m = mod_ref[0, 0]
    o_ref[0] = x_ref[0] + m[:, 2 * d:3 * d] * _rms(y, ng_ref[...])


def _ctx_lat_specs(c, nctx_t):
    ctx_spec = pl.BlockSpec((1, TM, c), lambda b, i: (b, jnp.minimum(i, nctx_t - 1), 0))
    lat_spec = pl.BlockSpec((1, TM, c), lambda b, i: (b, jnp.maximum(i - nctx_t, 0), 0))
    return ctx_spec, lat_spec


def _ctx_or_lat(c_ref, l_ref, nctx_t):
    return jnp.where(pl.program_id(1) < nctx_t, c_ref[0], l_ref[0])


def _even_out_kernel(x_ref, mod_ref, ng_ref, nac_ref, nal_ref, hf_ref, hb_ref, op_ref, mlg_ref, w_ref,
                     o_ref, *, d, nctx_t):
    hs = jax.nn.sigmoid(op_ref[0].astype(F32)) * (hf_ref[0].astype(F32) + hb_ref[0].astype(F32))
    parts = []
    for hd in range(ML_HEADS):
        seg = hs[:, hd * ML_HD:(hd + 1) * ML_HD]
        mu = jnp.mean(seg, axis=-1, keepdims=True)
        cen = seg - mu
        var = jnp.mean(cen * cen, axis=-1, keepdims=True)
        parts.append(cen * lax.rsqrt(var + EPS))
    ml = (jnp.concatenate(parts, axis=-1) * mlg_ref[...]).astype(BF16)
    na = _ctx_or_lat(nac_ref, nal_ref, nctx_t)
    y = _dot(na, w_ref[0:NA_WIDTH, :]) + _dot(ml, w_ref[NA_WIDTH:NA_WIDTH + ML_WIDTH, :])
    _out_tail(x_ref, mod_ref, ng_ref, y, o_ref, d)


def _even_out_call(x_all, modsel, ng, na_c, na_x, hf, hb, mlo, mlg, w_out, nctx_t):
    b, ttot, d = x_all.shape
    row = lambda c: pl.BlockSpec((1, TM, c), lambda bb, i: (bb, i, 0))
    return pl.pallas_call(
        functools.partial(_even_out_kernel, d=d, nctx_t=nctx_t),
        out_shape=jax.ShapeDtypeStruct((b, ttot, d), F32),
        grid=(b, ttot // TM),
        in_specs=[_x_spec(d), _mod_spec(6 * d, nctx_t), _resident((1, d)),
                  *_ctx_lat_specs(NA_WIDTH, nctx_t), row(ML_WIDTH), row(ML_WIDTH), row(ML_WIDTH),
                  _resident((1, ML_WIDTH)), _resident(w_out.shape)],
        out_specs=_x_spec(d),
        compiler_params=_cparams(("parallel", "parallel")),
        name="even_out_proj",
    )(x_all, modsel, ng, na_c, na_x, hf, hb, mlo, mlg, w_out)


def _odd_out_kernel(x_ref, mod_ref, ng_ref, ac_ref, al_ref, w_ref, o_ref, *, d, nctx_t):
    a = _ctx_or_lat(ac_ref, al_ref, nctx_t)
    _out_tail(x_ref, mod_ref, ng_ref, _dot(a, w_ref[...]), o_ref, d)


def _odd_out_call(x_all, modsel, ng, att_c, att_x, w_o, nctx_t):
    b, ttot, d = x_all.shape
    return pl.pallas_call(
        functools.partial(_odd_out_kernel, d=d, nctx_t=nctx_t),
        out_shape=jax.ShapeDtypeStruct((b, ttot, d), F32),
        grid=(b, ttot // TM),
        in_specs=[_x_spec(d), _mod_spec(6 * d, nctx_t), _resident((1, d)),
                  *_ctx_lat_specs(att_x.shape[-1], nctx_t), _resident(w_o.shape)],
        out_specs=_x_spec(d),
        compiler_params=_cparams(("parallel", "parallel")),
        name="odd_out_proj",
    )(x_all, modsel, ng, att_c, att_x, w_o)


FFN_CK = 256


def _ffn_kernel(x_ref, xp_ref, xn_ref, mod_ref, ng_in_ref, ng_out_ref, wup_ref, cw_ref, cb_ref,
                wdn_ref, o_ref, h_ref, u_ref, *, d, nctx_t, nt, tile0):
    pv, nv = _halo_valid(nctx_t, nt, tile0)
    m = mod_ref[0, 0]
    g_in = ng_in_ref[...]
    hh = SUBLANES_BF16
    h_ref[0:hh] = (_modulated(xp_ref[0], m, g_in, d, 1) * pv).astype(BF16)
    h_ref[hh:hh + TM] = _modulated(x_ref[0], m, g_in, d, 1).astype(BF16)
    h_ref[hh + TM:hh + TM + hh] = (_modulated(xn_ref[0], m, g_in, d, 1) * nv).astype(BF16)
    hcat = h_ref[...]
    ck = FFN_CK
    nslab = ck // LANES

    nchunks = FFN_HIDDEN // ck

    def conv_slab(buf, s, col):
        cw = cw_ref[:, col:col + LANES]
        return (cb_ref[:, col:col + LANES]
                + u_ref[buf, s, hh - 1:hh - 1 + TM, :] * cw[0:1]
                + u_ref[buf, s, hh:hh + TM, :] * cw[1:2]
                + u_ref[buf, s, hh + 1:hh + 1 + TM, :] * cw[2:3])

    y = jnp.zeros((TM, d), F32)
    for c in range(nchunks + 1):
        if c < nchunks:
            lo = c * ck
            glo = FFN_HIDDEN + lo
            ua = _dot(hcat, wup_ref[:, lo:lo + ck])
            ug = _dot(hcat, wup_ref[:, glo:glo + ck])
            for s in range(nslab):
                u_ref[c % 2, s] = ua[:, s * LANES:(s + 1) * LANES]
                u_ref[c % 2, nslab + s] = ug[:, s * LANES:(s + 1) * LANES]
        if c >= 1:
            lo = (c - 1) * ck
            glo = FFN_HIDDEN + lo
            acts = []
            for s in range(nslab):
                a = conv_slab((c - 1) % 2, s, lo + s * LANES)
                g = conv_slab((c - 1) % 2, nslab + s, glo + s * LANES)
                acts.append((a * (g * jax.nn.sigmoid(g))).astype(BF16))
            y = _dot(jnp.concatenate(acts, axis=-1), wdn_ref[lo:lo + ck, :]) + y
    o_ref[0] = x_ref[0] + m[:, 5 * d:6 * d] * _rms(y, ng_out_ref[...])


def _ffn_call(x_all, modsel, ng_in, ng_out, w_up, conv_w, conv_b, w_down, nctx_t, latents_only):
    b, ttot, d = x_all.shape
    nt = ttot // TM
    hh = SUBLANES_BF16
    tile0 = nctx_t if latents_only else 0
    prev, nxt = _halo_specs(d, hh, ttot, tile0)
    return pl.pallas_call(
        functools.partial(_ffn_kernel, d=d, nctx_t=nctx_t, nt=nt, tile0=tile0),
        out_shape=jax.ShapeDtypeStruct((b, ttot - tile0 * TM, d), F32),
        grid=(b, nt - tile0),
        in_specs=[pl.BlockSpec((1, TM, d), lambda bb, i: (bb, i + tile0, 0)), prev, nxt,
                  _mod_spec(6 * d, nctx_t - tile0), _resident((1, d)),
                  _resident((1, d)), _resident(w_up.shape), _resident(conv_w.shape),
                  _resident(conv_b.shape), _resident(w_down.shape)],
        out_specs=_x_spec(d),
        scratch_shapes=[pltpu.VMEM((TM + 2 * hh, d), BF16),
                        pltpu.VMEM((2, 2 * FFN_CK // LANES, TM + 2 * hh, LANES), F32)],
        compiler_params=_cparams(("parallel", "parallel")),
        name="conv_ffn",
    )(x_all, x_all, x_all, modsel, ng_in, ng_out, w_up, conv_w, conv_b, w_down)


MLA_QK_PAD = LANES
MLA_VA = MLA_V + SUBLANES_BF16


def _mla_in_kernel(x_ref, mod_ref, ng_ref, wdq_ref, qg_ref, wuqt_ref, wdkv_ref, kvg_ref,
                   wuk_ref, wuvt_ref, ropeq_ref, ropek_ref, qt_ref, k_ref, vt_ref, *, d):
    m = mod_ref[0, 0]
    hb = _modulated(x_ref[0], m, ng_ref[...], d, 0).astype(BF16)
    cq = _rms(_dot(hb, wdq_ref[...]), qg_ref[...]).astype(BF16)
    qt_all = _dot_nt(wuqt_ref[...], cq)
    rq = ropeq_ref[...]
    cos_r, sin_r, cos_c, sin_c = rq[0:8], rq[8:16], rq[16:24], rq[24:32]
    scale = MLA_QK ** -0.5 * LOG2E
    for hd in range(MLA_HEADS):
        base = hd * MLA_QK_PAD
        nope = qt_all[base:base + MLA_NOPE]
        x1r = qt_all[base + 64:base + 72]
        x2r = qt_all[base + 72:base + 80]
        x1c = qt_all[base + 80:base + 88]
        x2c = qt_all[base + 88:base + 96]
        pad = qt_all[base + 96:base + 128]
        roped = jnp.concatenate([
            nope,
            x1r * cos_r - x2r * sin_r, x1r * sin_r + x2r * cos_r,
            x1c * cos_c - x2c * sin_c, x1c * sin_c + x2c * cos_c,
            pad], axis=0)
        qt_ref[0, hd] = (roped * scale).astype(BF16)
    ckv = _dot(hb, wdkv_ref[...])
    cn = _rms(ckv[:, 0:MLA_KV_RANK], kvg_ref[...]).astype(BF16)
    rk = ropek_ref[...]
    kpe = ckv[:, LANES:2 * LANES] * rk[:, 0:LANES] + ckv[:, 2 * LANES:3 * LANES] * rk[:, LANES:2 * LANES]
    k_all = _dot(cn, wuk_ref[...])
    vt_all = _dot_nt(wuvt_ref[...], cn)
    ones_rows = jnp.where(lax.broadcasted_iota(jnp.int32, (MLA_VA - MLA_V, TM), 0) == 0, 1.0, 0.0)
    for hd in range(MLA_HEADS):
        k_ref[0, hd] = (k_all[:, hd * MLA_QK_PAD:(hd + 1) * MLA_QK_PAD] + kpe).astype(BF16)
        vt_ref[0, hd] = jnp.concatenate(
            [vt_all[hd * MLA_V:(hd + 1) * MLA_V], ones_rows], axis=0).astype(BF16)


def _mla_in_call(x_all, modsel, ng, wdq, qg, wuqt, wdkv, kvg, wuk, wuvt, rope_q, rope_k, nctx_t):
    b, ttot, d = x_all.shape
    nt = ttot // TM
    hn = MLA_HEADS
    return pl.pallas_call(
        functools.partial(_mla_in_kernel, d=d),
        out_shape=(jax.ShapeDtypeStruct((b, hn, MLA_QK_PAD, ttot), BF16),
                   jax.ShapeDtypeStruct((b, hn, ttot, MLA_QK_PAD), BF16),
                   jax.ShapeDtypeStruct((b, hn, MLA_VA, ttot), BF16)),
        grid=(b, nt),
        in_specs=[_x_spec(d), _mod_spec(6 * d, nctx_t), _resident((1, d)),
                  _resident(wdq.shape), _resident(qg.shape), _resident(wuqt.shape),
                  _resident(wdkv.shape), _resident(kvg.shape), _resident(wuk.shape),
                  _resident(wuvt.shape),
                  pl.BlockSpec((32, TM), lambda bb, i: (0, i)),
                  pl.BlockSpec((TM, 2 * LANES), lambda bb, i: (i, 0))],
        out_specs=(pl.BlockSpec((1, hn, MLA_QK_PAD, TM), lambda bb, i: (bb, 0, 0, i)),
                   pl.BlockSpec((1, hn, TM, MLA_QK_PAD), lambda bb, i: (bb, 0, i, 0)),
                   pl.BlockSpec((1, hn, MLA_VA, TM), lambda bb, i: (bb, 0, 0, i))),
        compiler_params=_cparams(("parallel", "parallel")),
        name="mla_in_proj",
    )(x_all, modsel, ng, wdq, qg, wuqt, wdkv, kvg, wuk, wuvt, rope_q, rope_k)


MLA_KC = 256
MLA_LOOKAHEAD = 6
MLA_HPS = 8


def _mla_attn_kernel(qt_ref, k_ref, vt_ref, o_ref, *, nkc):
    outs = []
    for hh in range(MLA_HPS):
        qt = qt_ref[0, hh]
        ms, accs, scores = [], [], []

        def chunk(c):
            return slice(c * MLA_KC, (c + 1) * MLA_KC)

        for c in range(nkc + MLA_LOOKAHEAD):
            if c < nkc:
                scores.append(_dot(k_ref[0, hh, chunk(c), :], qt))
            cc = c - MLA_LOOKAHEAD
            if cc >= 0:
                s = scores[cc]
                m_c = s.max(axis=0, keepdims=True)
                ms.append(m_c)
                accs.append(_dot(vt_ref[0, hh, :, chunk(cc)], jnp.exp2(s - m_c).astype(BF16)))
        m = functools.reduce(jnp.maximum, ms)
        acc = jnp.zeros((MLA_VA, TM), F32)
        for m_c, a_c in zip(ms, accs):
            acc = acc + jnp.exp2(m_c - m) * a_c
        outs.append(acc[0:MLA_V] / acc[MLA_V:MLA_V + 1])
    o_ref[0] = jnp.concatenate(outs, axis=0).T.astype(BF16)


def _mla_attn_call(qt, k, vt, q_tile0, n_qtiles, kv_len):
    b, hn, _, ttot = qt.shape
    return pl.pallas_call(
        functools.partial(_mla_attn_kernel, nkc=kv_len // MLA_KC),
        out_shape=jax.ShapeDtypeStruct((b, n_qtiles * TM, hn * MLA_V), BF16),
        grid=(b, hn // MLA_HPS, n_qtiles),
        in_specs=[
            pl.BlockSpec((1, MLA_HPS, MLA_QK_PAD, TM), lambda bb, hp, qi: (bb, hp, 0, q_tile0 + qi)),
            pl.BlockSpec((1, MLA_HPS, kv_len, MLA_QK_PAD), lambda bb, hp, qi: (bb, hp, 0, 0)),
            pl.BlockSpec((1, MLA_HPS, MLA_VA, kv_len), lambda bb, hp, qi: (bb, hp, 0, 0)),
        ],
        out_specs=pl.BlockSpec((1, TM, MLA_HPS * MLA_V), lambda bb, hp, qi: (bb, qi, hp)),
        compiler_params=_cparams(("parallel", "parallel", "arbitrary")),
        name="mla_attention",
    )(qt, k, vt)


def _rope_tables(ctx, t):
    pos = np.arange(t)
    half = MLA_ROPE // 4
    inv = ROPE_BASE ** (-jnp.arange(half, dtype=F32) / half)
    tabs = []
    for p in (pos // GRID_W, pos % GRID_W):
        ang = jnp.asarray(p, F32)[:, None] * inv
        cos = jnp.concatenate([jnp.ones((ctx, half), F32), jnp.cos(ang)], axis=0)
        sin = jnp.concatenate([jnp.zeros((ctx, half), F32), jnp.sin(ang)], axis=0)
        tabs.append((cos, sin))
    (cr, sr), (cc, sc) = tabs
    rope_q = jnp.concatenate([cr, sr, cc, sc], axis=1).T
    ttot = ctx + t
    zeros = jnp.zeros((ttot, MLA_NOPE), F32)
    tail = jnp.zeros((ttot, LANES - MLA_QK), F32)
    cos_k = jnp.concatenate([zeros, cr, cr, cc, cc, tail], axis=1)
    sin_k = jnp.concatenate([zeros, -sr, sr, -sc, sc, tail], axis=1)
    rope_k = jnp.concatenate([cos_k, sin_k], axis=1)
    return rope_q, rope_k


def _mla_weights(w_uq, w_dkv, w_ukv):
    hn = MLA_HEADS
    q_rank = w_uq.shape[0]
    wq = w_uq.reshape(q_rank, hn, MLA_QK)
    wq = jnp.pad(wq, ((0, 0), (0, 0), (0, MLA_QK_PAD - MLA_QK)))
    wuqt = wq.reshape(q_rank, hn * MLA_QK_PAD).T.astype(BF16)
    d = w_dkv.shape[0]
    kpe = w_dkv[:, MLA_KV_RANK:]
    h8 = MLA_ROPE // 4
    swap = jnp.concatenate([kpe[:, h8:2 * h8], kpe[:, 0:h8], kpe[:, 3 * h8:4 * h8],
                            kpe[:, 2 * h8:3 * h8]], axis=1)

    def place(cols):
        return jnp.concatenate([jnp.zeros((d, MLA_NOPE), F32), cols,
                                jnp.zeros((d, LANES - MLA_QK), F32)], axis=1)

    wdkv = jnp.concatenate([w_dkv[:, :MLA_KV_RANK], place(kpe), place(swap)], axis=1).astype(BF16)
    wkv = w_ukv.reshape(MLA_KV_RANK, hn, MLA_NOPE + MLA_V)
    wuk = jnp.pad(wkv[:, :, :MLA_NOPE], ((0, 0), (0, 0), (0, MLA_QK_PAD - MLA_NOPE)))
    wuk = wuk.reshape(MLA_KV_RANK, hn * MLA_QK_PAD).astype(BF16)
    wuvt = wkv[:, :, MLA_NOPE:].reshape(MLA_KV_RANK, hn * MLA_V).T.astype(BF16)
    return wuqt, wdkv, wuk, wuvt


def kernel(x, c, ctx, c_ctx, ada_w, ada_b, norm_g, ffn_w_up, ffn_conv_w, ffn_conv_b, ffn_w_down,
           ev_w_in, ev_gate_b, ev_conv_w, ev_conv_b, ev_rpb, ev_ml_norm_g, ev_w_out,
           od_w_dq, od_q_norm_g, od_w_uq, od_w_dkv, od_kv_norm_g, od_w_ukv, od_w_o):
    b, t, d = x.shape
    nctx = ctx.shape[1]
    depth = ada_w.shape[0]
    ttot = nctx + t
    assert t % TM == 0 and nctx % TM == 0 and t % GRID_W == 0
    nctx_t = nctx // TM

    x_all = jnp.concatenate([ctx, x], axis=1)

    rows = -(-(b + 1) // 8) * 8
    cvec = jnp.zeros((rows, d), F32).at[:b].set(c).at[b].set(c_ctx)
    mod = _ada_call(cvec, ada_w, ada_b)

    rope_q, rope_k = _rope_tables(nctx, t)
    ml_scale = jnp.concatenate([jnp.full((1, ML_WIDTH), ML_HD ** -0.5, F32),
                                jnp.ones((1, ML_WIDTH), F32)], axis=1)

    for l in range(depth):
        ctx_out = l < depth - 1
        modsel = jnp.stack([jnp.broadcast_to(mod[l, b], (b, 6 * d)), mod[l, :b]], axis=1)
        modsel = modsel.reshape(b, 2, 1, 6 * d)
        ng = norm_g[l].reshape(4, 1, d)
        if l % 2 == 0:
            e = l // 2
            w_in = ev_w_in[e]
            n_main = 3 * NA_WIDTH + 4 * ML_WIDTH
            w_gate = jnp.pad(w_in[:, n_main:], ((0, 0), (0, LANES - ML_GATES))).astype(BF16)
            gate_b = jnp.pad(ev_gate_b[e], (0, LANES - ML_GATES)).reshape(1, LANES)
            naq, nak, nav, mlqk, mlv, mlo, gates = _even_in_call(
                x_all, modsel, ng[0], w_in[:, :n_main].astype(BF16), w_gate, gate_b, nctx_t)
            na_x = _na_call(naq, nak, nav, _na_bias_table(ev_rpb[e], t // GRID_W), nctx, t)
            na_c = _na_ctx_call(naq, nak, nav, nctx)
            qk, kt = _ml_conv_call(mlqk, ev_conv_w[e], ev_conv_b[e].reshape(1, -1), ml_scale, nctx_t)
            gates_t = jnp.swapaxes(gates[:, :, :ML_GATES], 1, 2)
            hf, hb = _mlstm_call(qk, kt, mlv, gates, gates_t, nctx // ML_CHUNK)
            x_all = _even_out_call(x_all, modsel, ng[1], na_c, na_x, hf, hb, mlo,
                                   ev_ml_norm_g[e].reshape(1, -1), ev_w_out[e].astype(BF16), nctx_t)
        else:
            o = l // 2
            wuqt, wdkv, wuk, wuvt = _mla_weights(od_w_uq[o], od_w_dkv[o], od_w_ukv[o])
            qt, kk, vt = _mla_in_call(
                x_all, modsel, ng[0], od_w_dq[o].astype(BF16), od_q_norm_g[o].reshape(1, -1),
                wuqt, wdkv, od_kv_norm_g[o].reshape(1, -1), wuk, wuvt, rope_q, rope_k, nctx_t)
            att_x = _mla_attn_call(qt, kk, vt, nctx_t, t // TM, ttot)
            att_c = _mla_attn_call(qt, kk, vt, 0, nctx_t, nctx) if ctx_out else att_x
            x_all = _odd_out_call(x_all, modsel, ng[1], att_c, att_x, od_w_o[o].astype(BF16), nctx_t)
        x_all = _ffn_call(x_all, modsel, ng[2], ng[3], ffn_w_up[l].astype(BF16), ffn_conv_w[l],
                          ffn_conv_b[l].reshape(1, -1), ffn_w_down[l].astype(BF16), nctx_t,
                          latents_only=not ctx_out)
    return x_all
```

```python
import functools
import math

import jax
import jax.numpy as jnp
import numpy as np
from jax import lax
from jax.experimental import pallas as pl
from jax.experimental.pallas import tpu as pltpu

F32 = jnp.float32
BF16 = jnp.bfloat16

EPS = 1e-6
NEG = -1e30
LOG2E = math.log2(math.e)

GRID_W = 64
NA_HEADS = 8
NA_HD = 64
NA_WIN_ROWS = 8
NA_WIN_COLS = 16
NA_WIDTH = NA_HEADS * NA_HD

ML_HEADS = 4
ML_HD = 128
ML_WIDTH = ML_HEADS * ML_HD
ML_CHUNK = 128
ML_GATES = 4 * ML_HEADS

MLA_HEADS = 16
MLA_NOPE = 64
MLA_ROPE = 32
MLA_V = 64
MLA_QK = MLA_NOPE + MLA_ROPE
MLA_Q_RANK = 256
MLA_KV_RANK = 128
ROPE_BASE = 10000.0

FFN_HIDDEN = 2816

LANES = 128
SUBLANES_BF16 = 16
TM = 256
VMEM_LIMIT = 48 * 1024 * 1024


def _cparams(sem):
    return pltpu.CompilerParams(dimension_semantics=sem, vmem_limit_bytes=VMEM_LIMIT)


def _resident(shape):
    nd = len(shape)
    return pl.BlockSpec(shape, lambda *_: (0,) * nd, pipeline_mode=pl.Buffered(1))


def _rms(xf, g):
    ms = jnp.mean(xf * xf, axis=-1, keepdims=True)
    return xf * lax.rsqrt(ms + EPS) * g


def _dot(a, b):
    return jnp.dot(a, b, preferred_element_type=F32)


def _dot_nt(a, b):
    return lax.dot_general(a, b, (((1,), (1,)), ((), ())), preferred_element_type=F32)


def _ada_kernel(c_ref, w_ref, b_ref, o_ref):
    c = c_ref[...]
    s = c * jax.nn.sigmoid(c)
    o_ref[0] = _dot(s.astype(BF16), w_ref[0].astype(BF16)) + b_ref[0]


def _ada_call(cvec, ada_w, ada_b):
    depth, d, n = ada_w.shape
    rows = cvec.shape[0]
    tn = 1024
    return pl.pallas_call(
        _ada_kernel,
        out_shape=jax.ShapeDtypeStruct((depth, rows, n), F32),
        grid=(depth, n // tn),
        in_specs=[
            pl.BlockSpec((rows, d), lambda l, j: (0, 0)),
            pl.BlockSpec((1, d, tn), lambda l, j: (l, 0, j)),
            pl.BlockSpec((1, 1, tn), lambda l, j: (l, 0, j)),
        ],
        out_specs=pl.BlockSpec((1, rows, tn), lambda l, j: (l, 0, j)),
        compiler_params=_cparams(("parallel", "parallel")),
        name="ada_mod",
    )(cvec, ada_w, ada_b.reshape(depth, 1, n))


def _x_spec(d):
    return pl.BlockSpec((1, TM, d), lambda b, i: (b, i, 0))


def _mod_spec(d6, nctx_t):
    return pl.BlockSpec((1, 1, 1, d6), lambda b, i: (b, jnp.where(i >= nctx_t, 1, 0), 0, 0))


def _modulated(x, m, g, d, which):
    o = 3 * d * which
    return _rms(x, g) * (1.0 + m[:, o + d:o + 2 * d]) + m[:, o:o + d]


def _even_in_kernel(x_ref, mod_ref, ng_ref, w_ref, wg_ref, gb_ref,
                    naq_ref, nak_ref, nav_ref, mlqk_ref, mlv_ref, mlo_ref, g_ref, *, d):
    m = mod_ref[0, 0]
    hb = _modulated(x_ref[0], m, ng_ref[...], d, 0).astype(BF16)

    def seg(lo, hi):
        return _dot(hb, w_ref[:, lo:hi])

    w = NA_WIDTH
    naq_ref[0] = (seg(0, w) * (NA_HD ** -0.5 * LOG2E)).astype(BF16)
    nak_ref[0] = seg(w, 2 * w).astype(BF16)
    nav_ref[0] = seg(2 * w, 3 * w).astype(BF16)
    o = 3 * w
    mlqk_ref[0, :, 0:ML_WIDTH] = seg(o, o + ML_WIDTH).astype(BF16)
    mlqk_ref[0, :, ML_WIDTH:2 * ML_WIDTH] = seg(o + ML_WIDTH, o + 2 * ML_WIDTH).astype(BF16)
    mlv_ref[0] = seg(o + 2 * ML_WIDTH, o + 3 * ML_WIDTH).astype(BF16)
    mlo_ref[0] = seg(o + 3 * ML_WIDTH, o + 4 * ML_WIDTH).astype(BF16)
    g_ref[0] = _dot(hb, wg_ref[...]) + gb_ref[...]


def _even_in_call(x_all, modsel, ng, w_main, w_gate, gate_b, nctx_t):
    b, ttot, d = x_all.shape
    nt = ttot // TM
    n_main = w_main.shape[1]
    row = lambda c: pl.BlockSpec((1, TM, c), lambda bb, i: (bb, i, 0))
    sds = lambda c, dt: jax.ShapeDtypeStruct((b, ttot, c), dt)
    return pl.pallas_call(
        functools.partial(_even_in_kernel, d=d),
        out_shape=(sds(NA_WIDTH, BF16), sds(NA_WIDTH, BF16), sds(NA_WIDTH, BF16),
                   sds(2 * ML_WIDTH, BF16), sds(ML_WIDTH, BF16), sds(ML_WIDTH, BF16),
                   sds(LANES, F32)),
        grid=(b, nt),
        in_specs=[_x_spec(d), _mod_spec(6 * d, nctx_t), _resident((1, d)),
                  _resident((d, n_main)), _resident((d, LANES)), _resident((1, LANES))],
        out_specs=(row(NA_WIDTH), row(NA_WIDTH), row(NA_WIDTH), row(2 * ML_WIDTH),
                   row(ML_WIDTH), row(ML_WIDTH), row(LANES)),
        compiler_params=_cparams(("parallel", "parallel")),
        name="even_in_proj",
    )(x_all, modsel, ng, w_main, w_gate, gate_b)


def _pair_scores(qp, k_parts, bias_parts):
    lane = lax.broadcasted_iota(jnp.int32, qp.shape, 1)
    scores = []
    for hh in range(2):
        keep = (lane >= NA_HD) if hh else (lane < NA_HD)
        qh = jnp.where(keep, qp, jnp.zeros_like(qp))
        s = []
        for kk, bias in zip(k_parts, bias_parts):
            sp = _dot_nt(qh, kk)
            if bias is not None:
                sp = sp + bias[hh]
            s.append(sp)
        scores.append(s)
    return scores


def _pair_finish(scores, v_parts):
    mq = scores[0][0].shape[0]
    lane = lax.broadcasted_iota(jnp.int32, (mq, LANES), 1)
    v_aug = []
    for vv in v_parts:
        ones_col = jnp.where(lax.broadcasted_iota(jnp.int32, vv.shape, 1) == 0, 1.0, 0.0)
        v_aug.append(jnp.concatenate([vv, ones_col.astype(BF16)], axis=-1))
    outs = []
    for s in scores:
        m = functools.reduce(jnp.maximum, [sp.max(axis=-1, keepdims=True) for sp in s])
        o = jnp.zeros((mq, 2 * LANES), F32)
        for sp, vv in zip(s, v_aug):
            o = o + _dot(jnp.exp2(sp - m).astype(BF16), vv)
        outs.append(o[:, 0:LANES] / o[:, LANES:LANES + 1])
    return jnp.where(lane < NA_HD, outs[0], outs[1])


NA_RG = 4
NA_UB = NA_WIN_ROWS + NA_RG - 1


def _na_union_start(r0, rows):
    return jnp.clip(r0 - NA_WIN_ROWS // 2, 0, rows - NA_UB)


def _na_kernel(q_ref, k_ref, v_ref, bias_ref, o_ref, *, ctx, rows):
    us = _na_union_start(pl.program_id(1) * NA_RG, rows)
    start = pl.multiple_of(ctx + us * GRID_W, GRID_W)
    band = NA_UB * GRID_W
    pairs = [slice(p * LANES, (p + 1) * LANES) for p in range(NA_HEADS // 2)]

    def pair_scores(p):
        cs = pairs[p]
        kb = k_ref[0, pl.ds(start, band), cs]
        kc = k_ref[0, 0:ctx, cs]
        bias = (bias_ref[0, 2 * p], bias_ref[0, 2 * p + 1])
        return _pair_scores(q_ref[0, :, cs], (kb, kc), (bias, None))

    scores = [pair_scores(0)]
    for p, cs in enumerate(pairs):
        if p + 1 < len(pairs):
            scores.append(pair_scores(p + 1))
        vb = v_ref[0, pl.ds(start, band), cs]
        vc = v_ref[0, 0:ctx, cs]
        o_ref[0, :, cs] = _pair_finish(scores[p], (vb, vc)).astype(BF16)


def _na_call(naq, nak, nav, bias_tab, ctx, t):
    b, ttot, w = naq.shape
    rows = t // GRID_W
    groups = rows // NA_RG
    assert rows % NA_RG == 0 and groups >= 3 and ctx % (NA_RG * GRID_W) == 0
    q_off = ctx // (NA_RG * GRID_W)
    mq = NA_RG * GRID_W

    def bias_type(bb, g):
        return (jnp.where(g == 0, 0, jnp.where(g == groups - 1, 2, 1)), 0, 0, 0)

    whole = lambda bb, g: (bb, 0, 0)
    return pl.pallas_call(
        functools.partial(_na_kernel, ctx=ctx, rows=rows),
        out_shape=jax.ShapeDtypeStruct((b, t, w), BF16),
        grid=(b, groups),
        in_specs=[
            pl.BlockSpec((1, mq, w), lambda bb, g: (bb, q_off + g, 0)),
            pl.BlockSpec((1, ttot, w), whole, pipeline_mode=pl.Buffered(1)),
            pl.BlockSpec((1, ttot, w), whole, pipeline_mode=pl.Buffered(1)),
            pl.BlockSpec((1, NA_HEADS, mq, NA_UB * GRID_W), bias_type),
        ],
        out_specs=pl.BlockSpec((1, mq, w), lambda bb, g: (bb, g, 0)),
        compiler_params=_cparams(("parallel", "arbitrary")),
        name="na_attention",
    )(naq, nak, nav, bias_tab)


def _na_ctx_kernel(q_ref, k_ref, v_ref, o_ref):
    pairs = [slice(p * LANES, (p + 1) * LANES) for p in range(NA_HEADS // 2)]
    scores = [_pair_scores(q_ref[0, :, cs], (k_ref[0, :, cs],), (None,)) for cs in pairs]
    for sc, cs in zip(scores, pairs):
        o_ref[0, :, cs] = _pair_finish(sc, (v_ref[0, :, cs],)).astype(BF16)


def _na_ctx_call(naq, nak, nav, ctx):
    b, _, w = naq.shape
    spec = pl.BlockSpec((1, ctx, w), lambda bb: (bb, 0, 0))
    return pl.pallas_call(
        _na_ctx_kernel,
        out_shape=jax.ShapeDtypeStruct((b, ctx, w), BF16),
        grid=(b,),
        in_specs=[spec, spec, spec],
        out_specs=spec,
        compiler_params=_cparams(("parallel",)),
        name="na_ctx_attention",
    )(naq, nak, nav)


def _na_group_geometry(g, rows):
    r = g * NA_RG + np.arange(NA_RG)[:, None]
    us = int(np.clip(g * NA_RG - NA_WIN_ROWS // 2, 0, rows - NA_UB))
    kr = us + np.arange(NA_UB)[None, :]
    rs = np.clip(r - NA_WIN_ROWS // 2, 0, rows - NA_WIN_ROWS)
    valid = (kr >= rs) & (kr < rs + NA_WIN_ROWS)
    dr_idx = np.clip(kr - r + NA_WIN_ROWS - 1, 0, 2 * NA_WIN_ROWS - 2)
    return dr_idx, valid


def _na_bias_table(rpb, rows):
    groups = rows // NA_RG
    geo = [_na_group_geometry(g, rows) for g in range(groups)]
    for dr_g, valid_g in geo[1:groups - 1]:
        assert (valid_g == geo[1][1]).all() and (dr_g[valid_g] == geo[1][0][valid_g]).all()
    dr_idx = np.stack([geo[g][0] for g in (0, 1, groups - 1)])
    valid = np.stack([geo[g][1] for g in (0, 1, groups - 1)])
    kw = NA_WIN_COLS
    col = jnp.arange(GRID_W)
    cs = jnp.clip(col - kw // 2, 0, GRID_W - kw)
    in_win = (col[None, :] >= cs[:, None]) & (col[None, :] < cs[:, None] + kw)
    dc_idx = jnp.clip(col[None, :] - col[:, None], -(kw - 1), kw - 1) + (NA_WIN_COLS - 1)
    rpb_cols = rpb.astype(F32)[:, :, dc_idx] * LOG2E
    rpb_cols = jnp.where(in_win[None, None], rpb_cols, NEG)
    tab = rpb_cols[:, dr_idx]
    tab = jnp.where(jnp.asarray(valid)[None, :, :, :, None, None], tab, NEG)
    tab = tab.transpose(1, 0, 2, 4, 3, 5)
    return tab.reshape(3, NA_HEADS, NA_RG * GRID_W, NA_UB * GRID_W)


def _halo_specs(c, rows_h, ttot, tile0=0):
    per = TM // rows_h
    last = ttot // rows_h - 1
    prev = pl.BlockSpec((1, rows_h, c), lambda b, i: (b, jnp.maximum((i + tile0) * per - 1, 0), 0))
    nxt = pl.BlockSpec((1, rows_h, c), lambda b, i: (b, jnp.minimum((i + tile0 + 1) * per, last), 0))
    return prev, nxt


def _halo_valid(nctx_t, nt, tile0=0):
    i = pl.program_id(1) + tile0
    pv = jnp.where((i == 0) | (i == nctx_t), 0.0, 1.0).astype(F32)
    nv = jnp.where((i == nctx_t - 1) | (i == nt - 1), 0.0, 1.0).astype(F32)
    return pv, nv


def _conv3(u_ref, h, cw, cb, lo, hi):
    return (cb
            + u_ref[h - 1:h - 1 + TM, lo:hi] * cw[0:1]
            + u_ref[h:h + TM, lo:hi] * cw[1:2]
            + u_ref[h + 1:h + 1 + TM, lo:hi] * cw[2:3])


def _ml_conv_kernel(x_ref, xp_ref, xn_ref, cw_ref, cb_ref, sc_ref, o_ref, kt_ref, u_ref, *, nctx_t, nt):
    pv, nv = _halo_valid(nctx_t, nt)
    h = SUBLANES_BF16
    u_ref[0:h] = xp_ref[0].astype(F32) * pv
    u_ref[h:h + TM] = x_ref[0].astype(F32)
    u_ref[h + TM:h + TM + h] = xn_ref[0].astype(F32) * nv
    c = x_ref.shape[-1]
    y = _conv3(u_ref, h, cw_ref[...], cb_ref[...], 0, c)
    y = y * jax.nn.sigmoid(y) * sc_ref[...]
    o_ref[0] = y.astype(BF16)
    kt_ref[0] = y[:, ML_WIDTH:2 * ML_WIDTH].T.astype(BF16)


def _ml_conv_call(mlqk, conv_w, conv_b, scale, nctx_t):
    b, ttot, c = mlqk.shape
    nt = ttot // TM
    prev, nxt = _halo_specs(c, SUBLANES_BF16, ttot)
    return pl.pallas_call(
        functools.partial(_ml_conv_kernel, nctx_t=nctx_t, nt=nt),
        out_shape=(jax.ShapeDtypeStruct((b, ttot, c), BF16),
                   jax.ShapeDtypeStruct((b, ML_WIDTH, ttot), BF16)),
        grid=(b, nt),
        in_specs=[_x_spec(c), prev, nxt, _resident((3, c)), _resident((1, c)), _resident((1, c))],
        out_specs=(_x_spec(c), pl.BlockSpec((1, ML_WIDTH, TM), lambda bb, i: (bb, 0, i))),
        scratch_shapes=[pltpu.VMEM((TM + 2 * SUBLANES_BF16, c), F32)],
        compiler_params=_cparams(("parallel", "parallel")),
        name="mlstm_conv",
    )(mlqk, mlqk, mlqk, conv_w, conv_b, scale)


def _log_sigmoid(x):
    return jnp.minimum(x, 0.0) - jnp.log(1.0 + jnp.exp(-jnp.abs(x)))


def _split_dot(a, b, data_is_rhs):
    data = b if data_is_rhs else a
    hi = data.astype(BF16)
    lo = (data - hi.astype(F32)).astype(BF16)
    if data_is_rhs:
        return _dot(a, hi) + _dot(a, lo)
    return _dot(hi, b) + _dot(lo, b)


def _mlstm_kernel(qkf_ref, ktf_ref, vf_ref, gf_ref, gtf_ref, qkb_ref, ktb_ref, vb_ref, gb_ref, gtb_ref,
                  hf_ref, hb_ref, ct_ref, m_ref):
    L = ML_CHUNK

    @pl.when(pl.program_id(1) == 0)
    def _():
        ct_ref[...] = jnp.zeros_like(ct_ref)
        m_ref[...] = jnp.zeros_like(m_ref)

    ri = lax.broadcasted_iota(jnp.int32, (L, L), 0)
    ci = lax.broadcasted_iota(jnp.int32, (L, L), 1)
    tril = jnp.where(ri >= ci, 1.0, 0.0).astype(BF16)
    triu = jnp.where(ri <= ci, 1.0, 0.0).astype(BF16)
    ones_col = jnp.ones((L, LANES), BF16)

    dirs = (
        (qkf_ref, ktf_ref, vf_ref, gf_ref, gtf_ref, hf_ref, tril, triu, ri >= ci, L - 1, 0),
        (qkb_ref, ktb_ref, vb_ref, gb_ref, gtb_ref, hb_ref, triu, tril, ri <= ci, 0, 2 * ML_HEADS),
    )
    chains = []
    for dnum, (qk_ref, kt_ref, v_ref, g_ref, gt_ref, h_ref, tcol, trow, mask, tot_row, goff) in enumerate(dirs):
        gt = gt_ref[0]
        cum_col_all = _split_dot(tcol, _log_sigmoid(g_ref[0]), True)
        cum_row_all = _split_dot(_log_sigmoid(gt), trow, False)
        for hd in range(ML_HEADS):
            ic = goff + hd
            fc = goff + ML_HEADS + hd
            cs = slice(hd * ML_HD, (hd + 1) * ML_HD)
            j = dnum * ML_HEADS + hd
            q = qk_ref[0, :, cs]
            k = qk_ref[0, :, ML_WIDTH + hd * ML_HD:ML_WIDTH + (hd + 1) * ML_HD]
            ct = ct_ref[j]
            cum_col = cum_col_all[:, fc:fc + 1]
            chains.append(dict(
                j=j, h_ref=h_ref, cs=cs, kt=kt_ref[0, cs, :], ct=ct, mask=mask,
                v_aug=jnp.concatenate([v_ref[0, :, cs], ones_col], axis=-1),
                s=_dot_nt(q, k), hq=_dot(q, ct.astype(BF16)),
                cum_col=cum_col, cum_row=cum_row_all[fc:fc + 1, :],
                i_row=gt[ic:ic + 1, :], total=cum_col[tot_row:tot_row + 1, :]))
    for ch in chains:
        m_st = m_ref[ch["j"]][0:1, 0:1]
        cum_col, total = ch["cum_col"], ch["total"]
        e_row = ch["i_row"] - ch["cum_row"]
        base = jnp.where(ch["mask"], e_row, NEG)
        m_inter = cum_col + m_st
        m_t = jnp.maximum(m_inter, cum_col + base.max(axis=-1, keepdims=True))
        g_row = total + e_row
        m_new = jnp.maximum(total + m_st, g_row.max(axis=-1, keepdims=True))
        ch.update(base=base, m_inter=m_inter, m_t=m_t, g_row=g_row, m_new=m_new,
                  decay=jnp.exp(total + m_st - m_new))
    for ch in chains:
        ch["kw"] = (ch["kt"].astype(F32) * jnp.exp(ch["g_row"] - ch["m_new"])).astype(BF16)
    for ch in chains:
        ch["u"] = _dot(ch["kw"], ch["v_aug"])
    for ch in chains:
        ch["qk"] = (ch["s"] * jnp.exp(ch["base"] + (ch["cum_col"] - ch["m_t"]))).astype(BF16)
        ch["a"] = jnp.exp(ch["m_inter"] - ch["m_t"])
        ch["floor"] = jnp.exp(-ch["m_t"])
    for ch in chains:
        ch["p"] = _dot(ch["qk"], ch["v_aug"])
    for ch in chains:
        ct_ref[ch["j"]] = ch["decay"] * ch["ct"] + ch["u"]
        m_ref[ch["j"]] = jnp.broadcast_to(ch["m_new"], m_ref.shape[1:])
    for ch in chains:
        hfull = ch["a"] * ch["hq"] + ch["p"]
        den = hfull[:, ML_HD:2 * ML_HD]
        h_out = hfull[:, 0:ML_HD] / jnp.maximum(jnp.abs(den), ch["floor"])
        ch["h_ref"][0, :, ch["cs"]] = h_out.astype(BF16)


def _mlstm_call(qk, kt, v, gates, gates_t, nctx_c):
    b, ttot, _ = qk.shape
    nc = ttot // ML_CHUNK
    L = ML_CHUNK

    def fwd(bb, c):
        return (bb, c, 0)

    def bwd(bb, c):
        return (bb, jnp.where(c < nctx_c, nctx_c - 1 - c, nc - 1 + nctx_c - c), 0)

    def fwd_t(bb, c):
        return (bb, 0, c)

    def bwd_t(bb, c):
        return (bb, 0, jnp.where(c < nctx_c, nctx_c - 1 - c, nc - 1 + nctx_c - c))

    ins = []
    for row_map, col_map in ((fwd, fwd_t), (bwd, bwd_t)):
        ins += [pl.BlockSpec((1, L, 2 * ML_WIDTH), row_map),
                pl.BlockSpec((1, ML_WIDTH, L), col_map),
                pl.BlockSpec((1, L, ML_WIDTH), row_map),
                pl.BlockSpec((1, L, LANES), row_map),
                pl.BlockSpec((1, ML_GATES, L), col_map)]
    out_sds = jax.ShapeDtypeStruct((b, ttot, ML_WIDTH), BF16)
    return pl.pallas_call(
        _mlstm_kernel,
        out_shape=(out_sds, out_sds),
        grid=(b, nc),
        in_specs=ins,
        out_specs=(pl.BlockSpec((1, L, ML_WIDTH), fwd), pl.BlockSpec((1, L, ML_WIDTH), bwd)),
        scratch_shapes=[pltpu.VMEM((2 * ML_HEADS, ML_HD, 2 * ML_HD), F32),
                        pltpu.VMEM((2 * ML_HEADS, 8, LANES), F32)],
        compiler_params=_cparams(("parallel", "arbitrary")),
        name="mlstm_scan",
    )(qk, kt, v, gates, gates_t, qk, kt, v, gates, gates_t)


def _out_tail(x_ref, mod_ref, ng_ref, y, o_ref, d):
    m = mod_ref[0, 0]
    o_ref[0] = x_ref[0] + m[:, 2 * d:3 * d] * _rms(y, ng_ref[...])


def _ctx_lat_specs(c, nctx_t):
    ctx_spec = pl.BlockSpec((1, TM, c), lambda b, i: (b, jnp.minimum(i, nctx_t - 1), 0))
    lat_spec = pl.BlockSpec((1, TM, c), lambda b, i: (b, jnp.maximum(i - nctx_t, 0), 0))
    return ctx_spec, lat_spec


def _ctx_or_lat(c_ref, l_ref, nctx_t):
    return jnp.where(pl.program_id(1) < nctx_t, c_ref[0], l_ref[0])


def _even_out_kernel(x_ref, mod_ref, ng_ref, nac_ref, nal_ref, hf_ref, hb_ref, op_ref, mlg_ref, w_ref,
                     o_ref, *, d, nctx_t):
    hs = jax.nn.sigmoid(op_ref[0].astype(F32)) * (hf_ref[0].astype(F32) + hb_ref[0].astype(F32))
    parts = []
    for hd in range(ML_HEADS):
        seg = hs[:, hd * ML_HD:(hd + 1) * ML_HD]
        mu = jnp.mean(seg, axis=-1, keepdims=True)
        cen = seg - mu
        var = jnp.mean(cen * cen, axis=-1, keepdims=True)
        parts.append(cen * lax.rsqrt(var + EPS))
    ml = (jnp.concatenate(parts, axis=-1) * mlg_ref[...]).astype(BF16)
    na = _ctx_or_lat(nac_ref, nal_ref, nctx_t)
    y = _dot(na, w_ref[0:NA_WIDTH, :]) + _dot(ml, w_ref[NA_WIDTH:NA_WIDTH + ML_WIDTH, :])
    _out_tail(x_ref, mod_ref, ng_ref, y, o_ref, d)


def _even_out_call(x_all, modsel, ng, na_c, na_x, hf, hb, mlo, mlg, w_out, nctx_t):
    b, ttot, d = x_all.shape
    row = lambda c: pl.BlockSpec((1, TM, c), lambda bb, i: (bb, i, 0))
    return pl.pallas_call(
        functools.partial(_even_out_kernel, d=d, nctx_t=nctx_t),
        out_shape=jax.ShapeDtypeStruct((b, ttot, d), F32),
        grid=(b, ttot // TM),
        in_specs=[_x_spec(d), _mod_spec(6 * d, nctx_t), _resident((1, d)),
                  *_ctx_lat_specs(NA_WIDTH, nctx_t), row(ML_WIDTH), row(ML_WIDTH), row(ML_WIDTH),
                  _resident((1, ML_WIDTH)), _resident(w_out.shape)],
        out_specs=_x_spec(d),
        compiler_params=_cparams(("parallel", "parallel")),
        name="even_out_proj",
    )(x_all, modsel, ng, na_c, na_x, hf, hb, mlo, mlg, w_out)


def _odd_out_kernel(x_ref, mod_ref, ng_ref, ac_ref, al_ref, w_ref, o_ref, *, d, nctx_t):
    a = _ctx_or_lat(ac_ref, al_ref, nctx_t)
    _out_tail(x_ref, mod_ref, ng_ref, _dot(a, w_ref[...]), o_ref, d)


def _odd_out_call(x_all, modsel, ng, att_c, att_x, w_o, nctx_t):
    b, ttot, d = x_all.shape
    return pl.pallas_call(
        functools.partial(_odd_out_kernel, d=d, nctx_t=nctx_t),
        out_shape=jax.ShapeDtypeStruct((b, ttot, d), F32),
        grid=(b, ttot // TM),
        in_specs=[_x_spec(d), _mod_spec(6 * d, nctx_t), _resident((1, d)),
                  *_ctx_lat_specs(att_x.shape[-1], nctx_t), _resident(w_o.shape)],
        out_specs=_x_spec(d),
        compiler_params=_cparams(("parallel", "parallel")),
        name="odd_out_proj",
    )(x_all, modsel, ng, att_c, att_x, w_o)


FFN_CK = 256


def _ffn_kernel(x_ref, xp_ref, xn_ref, mod_ref, ng_in_ref, ng_out_ref, wup_ref, cw_ref, cb_ref,
                wdn_ref, o_ref, h_ref, u_ref, *, d, nctx_t, nt, tile0):
    pv, nv = _halo_valid(nctx_t, nt, tile0)
    m = mod_ref[0, 0]
    g_in = ng_in_ref[...]
    hh = SUBLANES_BF16
    h_ref[0:hh] = (_modulated(xp_ref[0], m, g_in, d, 1) * pv).astype(BF16)
    h_ref[hh:hh + TM] = _modulated(x_ref[0], m, g_in, d, 1).astype(BF16)
    h_ref[hh + TM:hh + TM + hh] = (_modulated(xn_ref[0], m, g_in, d, 1) * nv).astype(BF16)
    hcat = h_ref[...]
    ck = FFN_CK
    nslab = ck // LANES

    nchunks = FFN_HIDDEN // ck

    def conv_slab(buf, s, col):
        cw = cw_ref[:, col:col + LANES]
        return (cb_ref[:, col:col + LANES]
                + u_ref[buf, s, hh - 1:hh - 1 + TM, :] * cw[0:1]
                + u_ref[buf, s, hh:hh + TM, :] * cw[1:2]
                + u_ref[buf, s, hh + 1:hh + 1 + TM, :] * cw[2:3])

    y = jnp.zeros((TM, d), F32)
    for c in range(nchunks + 1):
        if c < nchunks:
            lo = c * ck
            glo = FFN_HIDDEN + lo
            ua = _dot(hcat, wup_ref[:, lo:lo + ck])
            ug = _dot(hcat, wup_ref[:, glo:glo + ck])
            for s in range(nslab):
                u_ref[c % 2, s] = ua[:, s * LANES:(s + 1) * LANES]
                u_ref[c % 2, nslab + s] = ug[:, s * LANES:(s + 1) * LANES]
        if c >= 1:
            lo = (c - 1) * ck
            glo = FFN_HIDDEN + lo
            acts = []
            for s in range(nslab):
                a = conv_slab((c - 1) % 2, s, lo + s * LANES)
                g = conv_slab((c - 1) % 2, nslab + s, glo + s * LANES)
                acts.append((a * (g * jax.nn.sigmoid(g))).astype(BF16))
            y = _dot(jnp.concatenate(acts, axis=-1), wdn_ref[lo:lo + ck, :]) + y
    o_ref[0] = x_ref[0] + m[:, 5 * d:6 * d] * _rms(y, ng_out_ref[...])


def _ffn_call(x_all, modsel, ng_in, ng_out, w_up, conv_w, conv_b, w_down, nctx_t, latents_only):
    b, ttot, d = x_all.shape
    nt = ttot // TM
    hh = SUBLANES_BF16
    tile0 = nctx_t if latents_only else 0
    prev, nxt = _halo_specs(d, hh, ttot, tile0)
    return pl.pallas_call(
        functools.partial(_ffn_kernel, d=d, nctx_t=nctx_t, nt=nt, tile0=tile0),
        out_shape=jax.ShapeDtypeStruct((b, ttot - tile0 * TM, d), F32),
        grid=(b, nt - tile0),
        in_specs=[pl.BlockSpec((1, TM, d), lambda bb, i: (bb, i + tile0, 0)), prev, nxt,
                  _mod_spec(6 * d, nctx_t - tile0), _resident((1, d)),
                  _resident((1, d)), _resident(w_up.shape), _resident(conv_w.shape),
                  _resident(conv_b.shape), _resident(w_down.shape)],
        out_specs=_x_spec(d),
        scratch_shapes=[pltpu.VMEM((TM + 2 * hh, d), BF16),
                        pltpu.VMEM((2, 2 * FFN_CK // LANES, TM + 2 * hh, LANES), F32)],
        compiler_params=_cparams(("parallel", "parallel")),
        name="conv_ffn",
    )(x_all, x_all, x_all, modsel, ng_in, ng_out, w_up, conv_w, conv_b, w_down)


MLA_QK_PAD = LANES
MLA_VA = MLA_V + SUBLANES_BF16


def _mla_in_kernel(x_ref, mod_ref, ng_ref, wdq_ref, qg_ref, wuqt_ref, wdkv_ref, kvg_ref,
                   wuk_ref, wuvt_ref, ropeq_ref, ropek_ref, qt_ref, k_ref, vt_ref, *, d):
    m = mod_ref[0, 0]
    hb = _modulated(x_ref[0], m, ng_ref[...], d, 0).astype(BF16)
    cq = _rms(_dot(hb, wdq_ref[...]), qg_ref[...]).astype(BF16)
    qt_all = _dot_nt(wuqt_ref[...], cq)
    rq = ropeq_ref[...]
    cos_r, sin_r, cos_c, sin_c = rq[0:8], rq[8:16], rq[16:24], rq[24:32]
    scale = MLA_QK ** -0.5 * LOG2E
    for hd in range(MLA_HEADS):
        base = hd * MLA_QK_PAD
        nope = qt_all[base:base + MLA_NOPE]
        x1r = qt_all[base + 64:base + 72]
        x2r = qt_all[base + 72:base + 80]
        x1c = qt_all[base + 80:base + 88]
        x2c = qt_all[base + 88:base + 96]
        pad = qt_all[base + 96:base + 128]
        roped = jnp.concatenate([
            nope,
            x1r * cos_r - x2r * sin_r, x1r * sin_r + x2r * cos_r,
            x1c * cos_c - x2c * sin_c, x1c * sin_c + x2c * cos_c,
            pad], axis=0)
        qt_ref[0, hd] = (roped * scale).astype(BF16)
    ckv = _dot(hb, wdkv_ref[...])
    cn = _rms(ckv[:, 0:MLA_KV_RANK], kvg_ref[...]).astype(BF16)
    rk = ropek_ref[...]
    kpe = ckv[:, LANES:2 * LANES] * rk[:, 0:LANES] + ckv[:, 2 * LANES:3 * LANES] * rk[:, LANES:2 * LANES]
    k_all = _dot(cn, wuk_ref[...])
    vt_all = _dot_nt(wuvt_ref[...], cn)
    ones_rows = jnp.where(lax.broadcasted_iota(jnp.int32, (MLA_VA - MLA_V, TM), 0) == 0, 1.0, 0.0)
    for hd in range(MLA_HEADS):
        k_ref[0, hd] = (k_all[:, hd * MLA_QK_PAD:(hd + 1) * MLA_QK_PAD] + kpe).astype(BF16)
        vt_ref[0, hd] = jnp.concatenate(
            [vt_all[hd * MLA_V:(hd + 1) * MLA_V], ones_rows], axis=0).astype(BF16)


def _mla_in_call(x_all, modsel, ng, wdq, qg, wuqt, wdkv, kvg, wuk, wuvt, rope_q, rope_k, nctx_t):
    b, ttot, d = x_all.shape
    nt = ttot // TM
    hn = MLA_HEADS
    return pl.pallas_call(
        functools.partial(_mla_in_kernel, d=d),
        out_shape=(jax.ShapeDtypeStruct((b, hn, MLA_QK_PAD, ttot), BF16),
                   jax.ShapeDtypeStruct((b, hn, ttot, MLA_QK_PAD), BF16),
                   jax.ShapeDtypeStruct((b, hn, MLA_VA, ttot), BF16)),
        grid=(b, nt),
        in_specs=[_x_spec(d), _mod_spec(6 * d, nctx_t), _resident((1, d)),
                  _resident(wdq.shape), _resident(qg.shape), _resident(wuqt.shape),
                  _resident(wdkv.shape), _resident(kvg.shape), _resident(wuk.shape),
                  _resident(wuvt.shape),
                  pl.BlockSpec((32, TM), lambda bb, i: (0, i)),
                  pl.BlockSpec((TM, 2 * LANES), lambda bb, i: (i, 0))],
        out_specs=(pl.BlockSpec((1, hn, MLA_QK_PAD, TM), lambda bb, i: (bb, 0, 0, i)),
                   pl.BlockSpec((1, hn, TM, MLA_QK_PAD), lambda bb, i: (bb, 0, i, 0)),
                   pl.BlockSpec((1, hn, MLA_VA, TM), lambda bb, i: (bb, 0, 0, i))),
        compiler_params=_cparams(("parallel", "parallel")),
        name="mla_in_proj",
    )(x_all, modsel, ng, wdq, qg, wuqt, wdkv, kvg, wuk, wuvt, rope_q, rope_k)


MLA_KC = 256
MLA_LOOKAHEAD = 8
MLA_HPS = 8


def _mla_attn_kernel(qt_ref, k_ref, vt_ref, o_ref, *, nkc):
    def chunk(c):
        return slice(c * MLA_KC, (c + 1) * MLA_KC)

    items = [(hh, c) for hh in range(MLA_HPS) for c in range(nkc)]
    scores, ms, accs, outs = [], [], [], []
    for i in range(len(items) + MLA_LOOKAHEAD):
        if i < len(items):
            hh, c = items[i]
            scores.append(_dot(k_ref[0, hh, chunk(c), :], qt_ref[0, hh]))
        if i >= MLA_LOOKAHEAD:
            hh, c = items[i - MLA_LOOKAHEAD]
            s = scores[i - MLA_LOOKAHEAD]
            m_c = s.max(axis=0, keepdims=True)
            ms.append(m_c)
            accs.append(_dot(vt_ref[0, hh, :, chunk(c)], jnp.exp2(s - m_c).astype(BF16)))
            if c == nkc - 1:
                m = functools.reduce(jnp.maximum, ms)
                acc = jnp.zeros((MLA_VA, TM), F32)
                for m_c, a_c in zip(ms, accs):
                    acc = acc + jnp.exp2(m_c - m) * a_c
                outs.append(acc[0:MLA_V] / acc[MLA_V:MLA_V + 1])
                ms, accs = [], []
    o_ref[0] = jnp.concatenate(outs, axis=0).T.astype(BF16)


def _mla_attn_call(qt, k, vt, q_tile0, n_qtiles, kv_len):
    b, hn, _, ttot = qt.shape
    return pl.pallas_call(
        functools.partial(_mla_attn_kernel, nkc=kv_len // MLA_KC),
        out_shape=jax.ShapeDtypeStruct((b, n_qtiles * TM, hn * MLA_V), BF16),
        grid=(b, hn // MLA_HPS, n_qtiles),
        in_specs=[
            pl.BlockSpec((1, MLA_HPS, MLA_QK_PAD, TM), lambda bb, hp, qi: (bb, hp, 0, q_tile0 + qi)),
            pl.BlockSpec((1, MLA_HPS, kv_len, MLA_QK_PAD), lambda bb, hp, qi: (bb, hp, 0, 0)),
            pl.BlockSpec((1, MLA_HPS, MLA_VA, kv_len), lambda bb, hp, qi: (bb, hp, 0, 0)),
        ],
        out_specs=pl.BlockSpec((1, TM, MLA_HPS * MLA_V), lambda bb, hp, qi: (bb, qi, hp)),
        compiler_params=_cparams(("parallel", "parallel", "arbitrary")),
        name="mla_attention",
    )(qt, k, vt)


def _rope_tables(ctx, t):
    pos = np.arange(t)
    half = MLA_ROPE // 4
    inv = ROPE_BASE ** (-jnp.arange(half, dtype=F32) / half)
    tabs = []
    for p in (pos // GRID_W, pos % GRID_W):
        ang = jnp.asarray(p, F32)[:, None] * inv
        cos = jnp.concatenate([jnp.ones((ctx, half), F32), jnp.cos(ang)], axis=0)
        sin = jnp.concatenate([jnp.zeros((ctx, half), F32), jnp.sin(ang)], axis=0)
        tabs.append((cos, sin))
    (cr, sr), (cc, sc) = tabs
    rope_q = jnp.concatenate([cr, sr, cc, sc], axis=1).T
    ttot = ctx + t
    zeros = jnp.zeros((ttot, MLA_NOPE), F32)
    tail = jnp.zeros((ttot, LANES - MLA_QK), F32)
    cos_k = jnp.concatenate([zeros, cr, cr, cc, cc, tail], axis=1)
    sin_k = jnp.concatenate([zeros, -sr, sr, -sc, sc, tail], axis=1)
    rope_k = jnp.concatenate([cos_k, sin_k], axis=1)
    return rope_q, rope_k


def _mla_weights(w_uq, w_dkv, w_ukv):
    hn = MLA_HEADS
    q_rank = w_uq.shape[0]
    wq = w_uq.reshape(q_rank, hn, MLA_QK)
    wq = jnp.pad(wq, ((0, 0), (0, 0), (0, MLA_QK_PAD - MLA_QK)))
    wuqt = wq.reshape(q_rank, hn * MLA_QK_PAD).T.astype(BF16)
    d = w_dkv.shape[0]
    kpe = w_dkv[:, MLA_KV_RANK:]
    h8 = MLA_ROPE // 4
    swap = jnp.concatenate([kpe[:, h8:2 * h8], kpe[:, 0:h8], kpe[:, 3 * h8:4 * h8],
                            kpe[:, 2 * h8:3 * h8]], axis=1)

    def place(cols):
        return jnp.concatenate([jnp.zeros((d, MLA_NOPE), F32), cols,
                                jnp.zeros((d, LANES - MLA_QK), F32)], axis=1)

    wdkv = jnp.concatenate([w_dkv[:, :MLA_KV_RANK], place(kpe), place(swap)], axis=1).astype(BF16)
    wkv = w_ukv.reshape(MLA_KV_RANK, hn, MLA_NOPE + MLA_V)
    wuk = jnp.pad(wkv[:, :, :MLA_NOPE], ((0, 0), (0, 0), (0, MLA_QK_PAD - MLA_NOPE)))
    wuk = wuk.reshape(MLA_KV_RANK, hn * MLA_QK_PAD).astype(BF16)
    wuvt = wkv[:, :, MLA_NOPE:].reshape(MLA_KV_RANK, hn * MLA_V).T.astype(BF16)
    return wuqt, wdkv, wuk, wuvt


def kernel(x, c, ctx, c_ctx, ada_w, ada_b, norm_g, ffn_w_up, ffn_conv_w, ffn_conv_b, ffn_w_down,
           ev_w_in, ev_gate_b, ev_conv_w, ev_conv_b, ev_rpb, ev_ml_norm_g, ev_w_out,
           od_w_dq, od_q_norm_g, od_w_uq, od_w_dkv, od_kv_norm_g, od_w_ukv, od_w_o):
    b, t, d = x.shape
    nctx = ctx.shape[1]
    depth = ada_w.shape[0]
    ttot = nctx + t
    assert t % TM == 0 and nctx % TM == 0 and t % GRID_W == 0
    nctx_t = nctx // TM

    x_all = jnp.concatenate([ctx, x], axis=1)

    rows = -(-(b + 1) // 8) * 8
    cvec = jnp.zeros((rows, d), F32).at[:b].set(c).at[b].set(c_ctx)
    mod = _ada_call(cvec, ada_w, ada_b)

    rope_q, rope_k = _rope_tables(nctx, t)
    ml_scale = jnp.concatenate([jnp.full((1, ML_WIDTH), ML_HD ** -0.5, F32),
                                jnp.ones((1, ML_WIDTH), F32)], axis=1)

    for l in range(depth):
        ctx_out = l < depth - 1
        modsel = jnp.stack([jnp.broadcast_to(mod[l, b], (b, 6 * d)), mod[l, :b]], axis=1)
        modsel = modsel.reshape(b, 2, 1, 6 * d)
        ng = norm_g[l].reshape(4, 1, d)
        if l % 2 == 0:
            e = l // 2
            w_in = ev_w_in[e]
            n_main = 3 * NA_WIDTH + 4 * ML_WIDTH
            w_gate = jnp.pad(w_in[:, n_main:], ((0, 0), (0, LANES - ML_GATES))).astype(BF16)
            gate_b = jnp.pad(ev_gate_b[e], (0, LANES - ML_GATES)).reshape(1, LANES)
            naq, nak, nav, mlqk, mlv, mlo, gates = _even_in_call(
                x_all, modsel, ng[0], w_in[:, :n_main].astype(BF16), w_gate, gate_b, nctx_t)
            na_x = _na_call(naq, nak, nav, _na_bias_table(ev_rpb[e], t // GRID_W), nctx, t)
            na_c = _na_ctx_call(naq, nak, nav, nctx)
            qk, kt = _ml_conv_call(mlqk, ev_conv_w[e], ev_conv_b[e].reshape(1, -1), ml_scale, nctx_t)
            gates_t = jnp.swapaxes(gates[:, :, :ML_GATES], 1, 2)
            hf, hb = _mlstm_call(qk, kt, mlv, gates, gates_t, nctx // ML_CHUNK)
            x_all = _even_out_call(x_all, modsel, ng[1], na_c, na_x, hf, hb, mlo,
                                   ev_ml_norm_g[e].reshape(1, -1), ev_w_out[e].astype(BF16), nctx_t)
        else:
            o = l // 2
            wuqt, wdkv, wuk, wuvt = _mla_weights(od_w_uq[o], od_w_dkv[o], od_w_ukv[o])
            qt, kk, vt = _mla_in_call(
                x_all, modsel, ng[0], od_w_dq[o].astype(BF16), od_q_norm_g[o].reshape(1, -1),
                wuqt, wdkv, od_kv_norm_g[o].reshape(1, -1), wuk, wuvt, rope_q, rope_k, nctx_t)
            att_x = _mla_attn_call(qt, kk, vt, nctx_t, t // TM, ttot)
            att_c = _mla_attn_call(qt, kk, vt, 0, nctx_t, nctx) if ctx_out else att_x
            x_all = _odd_out_call(x_all, modsel, ng[1], att_c, att_x, od_w_o[o].astype(BF16), nctx_t)
        x_all = _ffn_call(x_all, modsel, ng[2], ng[3], ffn_w_up[l].astype(BF16), ffn_conv_w[l],
                          ffn_conv_b[l].reshape(1, -1), ffn_w_down[l].astype(BF16), nctx_t,
                          latents_only=not ctx_out)
    return x_all
```

```python
import functools
import math

import jax
import jax.numpy as jnp
import numpy as np
from jax import lax
from jax.experimental import pallas as pl
from jax.experimental.pallas import tpu as pltpu

F32 = jnp.float32
BF16 = jnp.bfloat16

EPS = 1e-6
NEG = -1e30
LOG2E = math.log2(math.e)

GRID_W = 64
NA_HEADS = 8
NA_HD = 64
NA_WIN_ROWS = 8
NA_WIN_COLS = 16
NA_WIDTH = NA_HEADS * NA_HD

ML_HEADS = 4
ML_HD = 128
ML_WIDTH = ML_HEADS * ML_HD
ML_CHUNK = 128
ML_GATES = 4 * ML_HEADS

MLA_HEADS = 16
MLA_NOPE = 64
MLA_ROPE = 32
MLA_V = 64
MLA_QK = MLA_NOPE + MLA_ROPE
MLA_Q_RANK = 256
MLA_KV_RANK = 128
ROPE_BASE = 10000.0

FFN_HIDDEN = 2816

LANES = 128
SUBLANES_BF16 = 16
TM = 256
VMEM_LIMIT = 48 * 1024 * 1024


def _cparams(sem):
    return pltpu.CompilerParams(dimension_semantics=sem, vmem_limit_bytes=VMEM_LIMIT)


def _resident(shape):
    nd = len(shape)
    return pl.BlockSpec(shape, lambda *_: (0,) * nd, pipeline_mode=pl.Buffered(1))


def _rms(xf, g):
    ms = jnp.mean(xf * xf, axis=-1, keepdims=True)
    return xf * lax.rsqrt(ms + EPS) * g


def _dot(a, b):
    return jnp.dot(a, b, preferred_element_type=F32)


def _dot_nt(a, b):
    return lax.dot_general(a, b, (((1,), (1,)), ((), ())), preferred_element_type=F32)


def _ada_kernel(c_ref, w_ref, b_ref, o_ref):
    c = c_ref[...]
    s = c * jax.nn.sigmoid(c)
    o_ref[0] = _dot(s.astype(BF16), w_ref[0].astype(BF16)) + b_ref[0]


def _ada_call(cvec, ada_w, ada_b):
    depth, d, n = ada_w.shape
    rows = cvec.shape[0]
    tn = 1024
    return pl.pallas_call(
        _ada_kernel,
        out_shape=jax.ShapeDtypeStruct((depth, rows, n), F32),
        grid=(depth, n // tn),
        in_specs=[
            pl.BlockSpec((rows, d), lambda l, j: (0, 0)),
            pl.BlockSpec((1, d, tn), lambda l, j: (l, 0, j)),
            pl.BlockSpec((1, 1, tn), lambda l, j: (l, 0, j)),
        ],
        out_specs=pl.BlockSpec((1, rows, tn), lambda l, j: (l, 0, j)),
        compiler_params=_cparams(("parallel", "parallel")),
        name="ada_mod",
    )(cvec, ada_w, ada_b.reshape(depth, 1, n))


def _x_spec(d):
    return pl.BlockSpec((1, TM, d), lambda b, i: (b, i, 0))


def _mod_spec(d6, nctx_t):
    return pl.BlockSpec((1, 1, 1, d6), lambda b, i: (b, jnp.where(i >= nctx_t, 1, 0), 0, 0))


def _modulated(x, m, g, d, which):
    o = 3 * d * which
    return _rms(x, g) * (1.0 + m[:, o + d:o + 2 * d]) + m[:, o:o + d]


def _even_in_kernel(x_ref, mod_ref, ng_ref, w_ref, wg_ref, gb_ref,
                    naq_ref, nak_ref, nav_ref, mlqk_ref, mlv_ref, mlo_ref, g_ref, *, d):
    m = mod_ref[0, 0]
    hb = _modulated(x_ref[0], m, ng_ref[...], d, 0).astype(BF16)

    def seg(lo, hi):
        return _dot(hb, w_ref[:, lo:hi])

    w = NA_WIDTH
    naq_ref[0] = (seg(0, w) * (NA_HD ** -0.5 * LOG2E)).astype(BF16)
    nak_ref[0] = seg(w, 2 * w).astype(BF16)
    nav_ref[0] = seg(2 * w, 3 * w).astype(BF16)
    o = 3 * w
    mlqk_ref[0, :, 0:ML_WIDTH] = seg(o, o + ML_WIDTH).astype(BF16)
    mlqk_ref[0, :, ML_WIDTH:2 * ML_WIDTH] = seg(o + ML_WIDTH, o + 2 * ML_WIDTH).astype(BF16)
    mlv_ref[0] = seg(o + 2 * ML_WIDTH, o + 3 * ML_WIDTH).astype(BF16)
    mlo_ref[0] = seg(o + 3 * ML_WIDTH, o + 4 * ML_WIDTH).astype(BF16)
    g_ref[0] = _dot(hb, wg_ref[...]) + gb_ref[...]


def _even_in_call(x_all, modsel, ng, w_main, w_gate, gate_b, nctx_t):
    b, ttot, d = x_all.shape
    nt = ttot // TM
    n_main = w_main.shape[1]
    row = lambda c: pl.BlockSpec((1, TM, c), lambda bb, i: (bb, i, 0))
    sds = lambda c, dt: jax.ShapeDtypeStruct((b, ttot, c), dt)
    return pl.pallas_call(
        functools.partial(_even_in_kernel, d=d),
        out_shape=(sds(NA_WIDTH, BF16), sds(NA_WIDTH, BF16), sds(NA_WIDTH, BF16),
                   sds(2 * ML_WIDTH, BF16), sds(ML_WIDTH, BF16), sds(ML_WIDTH, BF16),
                   sds(LANES, F32)),
        grid=(b, nt),
        in_specs=[_x_spec(d), _mod_spec(6 * d, nctx_t), _resident((1, d)),
                  _resident((d, n_main)), _resident((d, LANES)), _resident((1, LANES))],
        out_specs=(row(NA_WIDTH), row(NA_WIDTH), row(NA_WIDTH), row(2 * ML_WIDTH),
                   row(ML_WIDTH), row(ML_WIDTH), row(LANES)),
        compiler_params=_cparams(("parallel", "parallel")),
        name="even_in_proj",
    )(x_all, modsel, ng, w_main, w_gate, gate_b)


def _pair_scores(qp, k_parts, bias_parts):
    lane = lax.broadcasted_iota(jnp.int32, qp.shape, 1)
    scores = []
    for hh in range(2):
        keep = (lane >= NA_HD) if hh else (lane < NA_HD)
        qh = jnp.where(keep, qp, jnp.zeros_like(qp))
        s = []
        for kk, bias in zip(k_parts, bias_parts):
            sp = _dot_nt(qh, kk)
            if bias is not None:
                sp = sp + bias[hh]
            s.append(sp)
        scores.append(s)
    return scores


def _pair_finish(scores, v_parts):
    mq = scores[0][0].shape[0]
    lane = lax.broadcasted_iota(jnp.int32, (mq, LANES), 1)
    v_aug = []
    for vv in v_parts:
        ones_col = jnp.where(lax.broadcasted_iota(jnp.int32, vv.shape, 1) == 0, 1.0, 0.0)
        v_aug.append(jnp.concatenate([vv, ones_col.astype(BF16)], axis=-1))
    outs = []
    for s in scores:
        m = functools.reduce(jnp.maximum, [sp.max(axis=-1, keepdims=True) for sp in s])
        o = jnp.zeros((mq, 2 * LANES), F32)
        for sp, vv in zip(s, v_aug):
            o = o + _dot(jnp.exp2(sp - m).astype(BF16), vv)
        outs.append(o[:, 0:LANES] / o[:, LANES:LANES + 1])
    return jnp.where(lane < NA_HD, outs[0], outs[1])


NA_RG = 4
NA_UB = NA_WIN_ROWS + NA_RG - 1


def _na_union_start(r0, rows):
    return jnp.clip(r0 - NA_WIN_ROWS // 2, 0, rows - NA_UB)


def _na_kernel(q_ref, k_ref, v_ref, bias_ref, o_ref, *, ctx, rows):
    us = _na_union_start(pl.program_id(1) * NA_RG, rows)
    start = pl.multiple_of(ctx + us * GRID_W, GRID_W)
    band = NA_UB * GRID_W
    pairs = [slice(p * LANES, (p + 1) * LANES) for p in range(NA_HEADS // 2)]

    def pair_scores(p):
        cs = pairs[p]
        kb = k_ref[0, pl.ds(start, band), cs]
        kc = k_ref[0, 0:ctx, cs]
        bias = (bias_ref[0, 2 * p], bias_ref[0, 2 * p + 1])
        return _pair_scores(q_ref[0, :, cs], (kb, kc), (bias, None))

    scores = [pair_scores(0)]
    for p, cs in enumerate(pairs):
        if p + 1 < len(pairs):
            scores.append(pair_scores(p + 1))
        vb = v_ref[0, pl.ds(start, band), cs]
        vc = v_ref[0, 0:ctx, cs]
        o_ref[0, :, cs] = _pair_finish(scores[p], (vb, vc)).astype(BF16)


def _na_call(naq, nak, nav, bias_tab, ctx, t):
    b, ttot, w = naq.shape
    rows = t // GRID_W
    groups = rows // NA_RG
    assert rows % NA_RG == 0 and groups >= 3 and ctx % (NA_RG * GRID_W) == 0
    q_off = ctx // (NA_RG * GRID_W)
    mq = NA_RG * GRID_W

    def bias_type(bb, g):
        return (jnp.where(g == 0, 0, jnp.where(g == groups - 1, 2, 1)), 0, 0, 0)

    whole = lambda bb, g: (bb, 0, 0)
    return pl.pallas_call(
        functools.partial(_na_kernel, ctx=ctx, rows=rows),
        out_shape=jax.ShapeDtypeStruct((b, t, w), BF16),
        grid=(b, groups),
        in_specs=[
            pl.BlockSpec((1, mq, w), lambda bb, g: (bb, q_off + g, 0)),
            pl.BlockSpec((1, ttot, w), whole, pipeline_mode=pl.Buffered(1)),
            pl.BlockSpec((1, ttot, w), whole, pipeline_mode=pl.Buffered(1)),
            pl.BlockSpec((1, NA_HEADS, mq, NA_UB * GRID_W), bias_type),
        ],
        out_specs=pl.BlockSpec((1, mq, w), lambda bb, g: (bb, g, 0)),
        compiler_params=_cparams(("parallel", "arbitrary")),
        name="na_attention",
    )(naq, nak, nav, bias_tab)


def _na_ctx_kernel(q_ref, k_ref, v_ref, o_ref):
    pairs = [slice(p * LANES, (p + 1) * LANES) for p in range(NA_HEADS // 2)]
    scores = [_pair_scores(q_ref[0, :, cs], (k_ref[0, :, cs],), (None,)) for cs in pairs]
    for sc, cs in zip(scores, pairs):
        o_ref[0, :, cs] = _pair_finish(sc, (v_ref[0, :, cs],)).astype(BF16)


def _na_ctx_call(naq, nak, nav, ctx):
    b, _, w = naq.shape
    spec = pl.BlockSpec((1, ctx, w), lambda bb: (bb, 0, 0))
    return pl.pallas_call(
        _na_ctx_kernel,
        out_shape=jax.ShapeDtypeStruct((b, ctx, w), BF16),
        grid=(b,),
        in_specs=[spec, spec, spec],
        out_specs=spec,
        compiler_params=_cparams(("parallel",)),
        name="na_ctx_attention",
    )(naq, nak, nav)


def _na_group_geometry(g, rows):
    r = g * NA_RG + np.arange(NA_RG)[:, None]
    us = int(np.clip(g * NA_RG - NA_WIN_ROWS // 2, 0, rows - NA_UB))
    kr = us + np.arange(NA_UB)[None, :]
    rs = np.clip(r - NA_WIN_ROWS // 2, 0, rows - NA_WIN_ROWS)
    valid = (kr >= rs) & (kr < rs + NA_WIN_ROWS)
    dr_idx = np.clip(kr - r + NA_WIN_ROWS - 1, 0, 2 * NA_WIN_ROWS - 2)
    return dr_idx, valid


def _na_bias_table(rpb, rows):
    groups = rows // NA_RG
    geo = [_na_group_geometry(g, rows) for g in range(groups)]
    for dr_g, valid_g in geo[1:groups - 1]:
        assert (valid_g == geo[1][1]).all() and (dr_g[valid_g] == geo[1][0][valid_g]).all()
    dr_idx = np.stack([geo[g][0] for g in (0, 1, groups - 1)])
    valid = np.stack([geo[g][1] for g in (0, 1, groups - 1)])
    kw = NA_WIN_COLS
    col = jnp.arange(GRID_W)
    cs = jnp.clip(col - kw // 2, 0, GRID_W - kw)
    in_win = (col[None, :] >= cs[:, None]) & (col[None, :] < cs[:, None] + kw)
    dc_idx = jnp.clip(col[None, :] - col[:, None], -(kw - 1), kw - 1) + (NA_WIN_COLS - 1)
    rpb_cols = rpb.astype(F32)[:, :, dc_idx] * LOG2E
    rpb_cols = jnp.where(in_win[None, None], rpb_cols, NEG)
    tab = rpb_cols[:, dr_idx]
    tab = jnp.where(jnp.asarray(valid)[None, :, :, :, None, None], tab, NEG)
    tab = tab.transpose(1, 0, 2, 4, 3, 5)
    return tab.reshape(3, NA_HEADS, NA_RG * GRID_W, NA_UB * GRID_W)


def _halo_specs(c, rows_h, ttot, tile0=0):
    per = TM // rows_h
    last = ttot // rows_h - 1
    prev = pl.BlockSpec((1, rows_h, c), lambda b, i: (b, jnp.maximum((i + tile0) * per - 1, 0), 0))
    nxt = pl.BlockSpec((1, rows_h, c), lambda b, i: (b, jnp.minimum((i + tile0 + 1) * per, last), 0))
    return prev, nxt


def _halo_valid(nctx_t, nt, tile0=0):
    i = pl.program_id(1) + tile0
    pv = jnp.where((i == 0) | (i == nctx_t), 0.0, 1.0).astype(F32)
    nv = jnp.where((i == nctx_t - 1) | (i == nt - 1), 0.0, 1.0).astype(F32)
    return pv, nv


def _conv3(u_ref, h, cw, cb, lo, hi):
    return (cb
            + u_ref[h - 1:h - 1 + TM, lo:hi] * cw[0:1]
            + u_ref[h:h + TM, lo:hi] * cw[1:2]
            + u_ref[h + 1:h + 1 + TM, lo:hi] * cw[2:3])


def _ml_conv_kernel(x_ref, xp_ref, xn_ref, cw_ref, cb_ref, sc_ref, o_ref, kt_ref, u_ref, *, nctx_t, nt):
    pv, nv = _halo_valid(nctx_t, nt)
    h = SUBLANES_BF16
    u_ref[0:h] = xp_ref[0].astype(F32) * pv
    u_ref[h:h + TM] = x_ref[0].astype(F32)
    u_ref[h + TM:h + TM + h] = xn_ref[0].astype(F32) * nv
    c = x_ref.shape[-1]
    y = _conv3(u_ref, h, cw_ref[...], cb_ref[...], 0, c)
    y = y * jax.nn.sigmoid(y) * sc_ref[...]
    o_ref[0] = y.astype(BF16)
    kt_ref[0] = y[:, ML_WIDTH:2 * ML_WIDTH].T.astype(BF16)


def _ml_conv_call(mlqk, conv_w, conv_b, scale, nctx_t):
    b, ttot, c = mlqk.shape
    nt = ttot // TM
    prev, nxt = _halo_specs(c, SUBLANES_BF16, ttot)
    return pl.pallas_call(
        functools.partial(_ml_conv_kernel, nctx_t=nctx_t, nt=nt),
        out_shape=(jax.ShapeDtypeStruct((b, ttot, c), BF16),
                   jax.ShapeDtypeStruct((b, ML_WIDTH, ttot), BF16)),
        grid=(b, nt),
        in_specs=[_x_spec(c), prev, nxt, _resident((3, c)), _resident((1, c)), _resident((1, c))],
        out_specs=(_x_spec(c), pl.BlockSpec((1, ML_WIDTH, TM), lambda bb, i: (bb, 0, i))),
        scratch_shapes=[pltpu.VMEM((TM + 2 * SUBLANES_BF16, c), F32)],
        compiler_params=_cparams(("parallel", "parallel")),
        name="mlstm_conv",
    )(mlqk, mlqk, mlqk, conv_w, conv_b, scale)


def _log_sigmoid(x):
    return jnp.minimum(x, 0.0) - jnp.log(1.0 + jnp.exp(-jnp.abs(x)))


def _split_dot(a, b, data_is_rhs):
    data = b if data_is_rhs else a
    hi = data.astype(BF16)
    lo = (data - hi.astype(F32)).astype(BF16)
    if data_is_rhs:
        return _dot(a, hi) + _dot(a, lo)
    return _dot(hi, b) + _dot(lo, b)


def _mlstm_kernel(qkf_ref, ktf_ref, vf_ref, gf_ref, gtf_ref, qkb_ref, ktb_ref, vb_ref, gb_ref, gtb_ref,
                  hf_ref, hb_ref, ct_ref, m_ref):
    L = ML_CHUNK

    @pl.when(pl.program_id(1) == 0)
    def _():
        ct_ref[...] = jnp.zeros_like(ct_ref)
        m_ref[...] = jnp.zeros_like(m_ref)

    ri = lax.broadcasted_iota(jnp.int32, (L, L), 0)
    ci = lax.broadcasted_iota(jnp.int32, (L, L), 1)
    tril = jnp.where(ri >= ci, 1.0, 0.0).astype(BF16)
    triu = jnp.where(ri <= ci, 1.0, 0.0).astype(BF16)
    ones_col = jnp.ones((L, LANES), BF16)

    dirs = (
        (qkf_ref, ktf_ref, vf_ref, gf_ref, gtf_ref, hf_ref, tril, triu, ri >= ci, L - 1, 0),
        (qkb_ref, ktb_ref, vb_ref, gb_ref, gtb_ref, hb_ref, triu, tril, ri <= ci, 0, 2 * ML_HEADS),
    )
    chains = []
    for dnum, (qk_ref, kt_ref, v_ref, g_ref, gt_ref, h_ref, tcol, trow, mask, tot_row, goff) in enumerate(dirs):
        gt = gt_ref[0]
        cum_col_all = _split_dot(tcol, _log_sigmoid(g_ref[0]), True)
        cum_row_all = _split_dot(_log_sigmoid(gt), trow, False)
        for hd in range(ML_HEADS):
            ic = goff + hd
            fc = goff + ML_HEADS + hd
            cs = slice(hd * ML_HD, (hd + 1) * ML_HD)
            j = dnum * ML_HEADS + hd
            q = qk_ref[0, :, cs]
            k = qk_ref[0, :, ML_WIDTH + hd * ML_HD:ML_WIDTH + (hd + 1) * ML_HD]
            ct = ct_ref[j]
            cum_col = cum_col_all[:, fc:fc + 1]
            chains.append(dict(
                j=j, h_ref=h_ref, cs=cs, kt=kt_ref[0, cs, :], ct=ct, mask=mask,
                v_aug=jnp.concatenate([v_ref[0, :, cs], ones_col], axis=-1),
                s=_dot_nt(q, k), hq=_dot(q, ct.astype(BF16)),
                cum_col=cum_col, cum_row=cum_row_all[fc:fc + 1, :],
                i_row=gt[ic:ic + 1, :], total=cum_col[tot_row:tot_row + 1, :]))
    for ch in chains:
        m_st = m_ref[ch["j"]][0:1, 0:1]
        cum_col, total = ch["cum_col"], ch["total"]
        e_row = ch["i_row"] - ch["cum_row"]
        base = jnp.where(ch["mask"], e_row, NEG)
        m_inter = cum_col + m_st
        m_t = jnp.maximum(m_inter, cum_col + base.max(axis=-1, keepdims=True))
        g_row = total + e_row
        m_new = jnp.maximum(total + m_st, g_row.max(axis=-1, keepdims=True))
        ch.update(base=base, m_inter=m_inter, m_t=m_t, g_row=g_row, m_new=m_new,
                  decay=jnp.exp(total + m_st - m_new))
    for ch in chains:
        ch["kw"] = (ch["kt"].astype(F32) * jnp.exp(ch["g_row"] - ch["m_new"])).astype(BF16)
    for ch in chains:
        ch["u"] = _dot(ch["kw"], ch["v_aug"])
    for ch in chains:
        ch["qk"] = (ch["s"] * jnp.exp(ch["base"] + (ch["cum_col"] - ch["m_t"]))).astype(BF16)
        ch["a"] = jnp.exp(ch["m_inter"] - ch["m_t"])
        ch["floor"] = jnp.exp(-ch["m_t"])
    for ch in chains:
        ch["p"] = _dot(ch["qk"], ch["v_aug"])
    for ch in chains:
        ct_ref[ch["j"]] = ch["decay"] * ch["ct"] + ch["u"]
        m_ref[ch["j"]] = jnp.broadcast_to(ch["m_new"], m_ref.shape[1:])
    for ch in chains:
        hfull = ch["a"] * ch["hq"] + ch["p"]
        den = hfull[:, ML_HD:2 * ML_HD]
        h_out = hfull[:, 0:ML_HD] / jnp.maximum(jnp.abs(den), ch["floor"])
        ch["h_ref"][0, :, ch["cs"]] = h_out.astype(BF16)


def _mlstm_call(qk, kt, v, gates, gates_t, nctx_c):
    b, ttot, _ = qk.shape
    nc = ttot // ML_CHUNK
    L = ML_CHUNK

    def fwd(bb, c):
        return (bb, c, 0)

    def bwd(bb, c):
        return (bb, jnp.where(c < nctx_c, nctx_c - 1 - c, nc - 1 + nctx_c - c), 0)

    def fwd_t(bb, c):
        return (bb, 0, c)

    def bwd_t(bb, c):
        return (bb, 0, jnp.where(c < nctx_c, nctx_c - 1 - c, nc - 1 + nctx_c - c))

    ins = []
    for row_map, col_map in ((fwd, fwd_t), (bwd, bwd_t)):
        ins += [pl.BlockSpec((1, L, 2 * ML_WIDTH), row_map),
                pl.BlockSpec((1, ML_WIDTH, L), col_map),
                pl.BlockSpec((1, L, ML_WIDTH), row_map),
                pl.BlockSpec((1, L, LANES), row_map),
                pl.BlockSpec((1, ML_GATES, L), col_map)]
    out_sds = jax.ShapeDtypeStruct((b, ttot, ML_WIDTH), BF16)
    return pl.pallas_call(
        _mlstm_kernel,
        out_shape=(out_sds, out_sds),
        grid=(b, nc),
        in_specs=ins,
        out_specs=(pl.BlockSpec((1, L, ML_WIDTH), fwd), pl.BlockSpec((1, L, ML_WIDTH), bwd)),
        scratch_shapes=[pltpu.VMEM((2 * ML_HEADS, ML_HD, 2 * ML_HD), F32),
                        pltpu.VMEM((2 * ML_HEADS, 8, LANES), F32)],
        compiler_params=_cparams(("parallel", "arbitrary")),
        name="mlstm_scan",
    )(qk, kt, v, gates, gates_t, qk, kt, v, gates, gates_t)


def _out_tail(x_ref, mod_ref, ng_ref, y, o_ref, d):
    m = mod_ref[0, 0]
    o_ref[0] = x_ref[0] + m[:, 2 * d:3 * d] * _rms(y, ng_ref[...])


def _ctx_lat_specs(c, nctx_t):
    ctx_spec = pl.BlockSpec((1, TM, c), lambda b, i: (b, jnp.minimum(i, nctx_t - 1), 0))
    lat_spec = pl.BlockSpec((1, TM, c), lambda b, i: (b, jnp.maximum(i - nctx_t, 0), 0))
    return ctx_spec, lat_spec


def _ctx_or_lat(c_ref, l_ref, nctx_t):
    return jnp.where(pl.program_id(1) < nctx_t, c_ref[0], l_ref[0])


def _even_out_kernel(x_ref, mod_ref, ng_ref, nac_ref, nal_ref, hf_ref, hb_ref, op_ref, mlg_ref, w_ref,
                     o_ref, *, d, nctx_t):
    hs = jax.nn.sigmoid(op_ref[0].astype(F32)) * (hf_ref[0].astype(F32) + hb_ref[0].astype(F32))
    parts = []
    for hd in range(ML_HEADS):
        seg = hs[:, hd * ML_HD:(hd + 1) * ML_HD]
        mu = jnp.mean(seg, axis=-1, keepdims=True)
        cen = seg - mu
        var = jnp.mean(cen * cen, axis=-1, keepdims=True)
        parts.append(cen * lax.rsqrt(var + EPS))
    ml = (jnp.concatenate(parts, axis=-1) * mlg_ref[...]).astype(BF16)
    na = _ctx_or_lat(nac_ref, nal_ref, nctx_t)
    y = _dot(na, w_ref[0:NA_WIDTH, :]) + _dot(ml, w_ref[NA_WIDTH:NA_WIDTH + ML_WIDTH, :])
    _out_tail(x_ref, mod_ref, ng_ref, y, o_ref, d)


def _even_out_call(x_all, modsel, ng, na_c, na_x, hf, hb, mlo, mlg, w_out, nctx_t):
    b, ttot, d = x_all.shape
    row = lambda c: pl.BlockSpec((1, TM, c), lambda bb, i: (bb, i, 0))
    return pl.pallas_call(
        functools.partial(_even_out_kernel, d=d, nctx_t=nctx_t),
        out_shape=jax.ShapeDtypeStruct((b, ttot, d), F32),
        grid=(b, ttot // TM),
        in_specs=[_x_spec(d), _mod_spec(6 * d, nctx_t), _resident((1, d)),
                  *_ctx_lat_specs(NA_WIDTH, nctx_t), row(ML_WIDTH), row(ML_WIDTH), row(ML_WIDTH),
                  _resident((1, ML_WIDTH)), _resident(w_out.shape)],
        out_specs=_x_spec(d),
        compiler_params=_cparams(("parallel", "parallel")),
        name="even_out_proj",
    )(x_all, modsel, ng, na_c, na_x, hf, hb, mlo, mlg, w_out)


def _odd_out_kernel(x_ref, mod_ref, ng_ref, ac_ref, al_ref, w_ref, o_ref, *, d, nctx_t):
    a = _ctx_or_lat(ac_ref, al_ref, nctx_t)
    _out_tail(x_ref, mod_ref, ng_ref, _dot(a, w_ref[...]), o_ref, d)


def _odd_out_call(x_all, modsel, ng, att_c, att_x, w_o, nctx_t):
    b, ttot, d = x_all.shape
    return pl.pallas_call(
        functools.partial(_odd_out_kernel, d=d, nctx_t=nctx_t),
        out_shape=jax.ShapeDtypeStruct((b, ttot, d), F32),
        grid=(b, ttot // TM),
        in_specs=[_x_spec(d), _mod_spec(6 * d, nctx_t), _resident((1, d)),
                  *_ctx_lat_specs(att_x.shape[-1], nctx_t), _resident(w_o.shape)],
        out_specs=_x_spec(d),
        compiler_params=_cparams(("parallel", "parallel")),
        name="odd_out_proj",
    )(x_all, modsel, ng, att_c, att_x, w_o)


FFN_CK = 256
FFN_CPAD = 8


def _ffn_kernel(x_ref, xp_ref, xn_ref, mod_ref, ng_in_ref, ng_out_ref, wup_ref, cw_ref, cb_ref,
                wdn_ref, o_ref, h_ref, u_ref, *, d, nctx_t, nt, tile0):
    pv, nv = _halo_valid(nctx_t, nt, tile0)
    m = mod_ref[0, 0]
    g_in = ng_in_ref[...]
    hh = SUBLANES_BF16
    hp = _modulated(xp_ref[0, hh - 8:hh], m, g_in, d, 1)[7:8] * pv
    hn = _modulated(xn_ref[0, 0:8], m, g_in, d, 1)[0:1] * nv
    rid = lax.broadcasted_iota(jnp.int32, (hh, d), 0)
    h_ref[0:TM] = _modulated(x_ref[0], m, g_in, d, 1).astype(BF16)
    h_ref[TM:TM + hh] = jnp.where(rid == 0, hp, jnp.where(rid == 1, hn, 0.0)).astype(BF16)
    hcat = h_ref[...]
    ck = FFN_CK
    nslab = ck // LANES
    nchunks = FFN_HIDDEN // ck
    r0 = FFN_CPAD

    def store_u(buf, s, u):
        u_ref[buf, s, r0:r0 + TM, :] = u[0:TM]
        u_ref[buf, s, r0 - 1:r0, :] = u[TM:TM + 1]
        u_ref[buf, s, r0 + TM:r0 + TM + 1, :] = u[TM + 1:TM + 2]

    def conv_slab(buf, s, col):
        cw = cw_ref[:, col:col + LANES]
        return (cb_ref[:, col:col + LANES]
                + u_ref[buf, s, r0 - 1:r0 - 1 + TM, :] * cw[0:1]
                + u_ref[buf, s, r0:r0 + TM, :] * cw[1:2]
                + u_ref[buf, s, r0 + 1:r0 + 1 + TM, :] * cw[2:3])

    y = jnp.zeros((TM, d), F32)
    for c in range(nchunks + 1):
        if c < nchunks:
            lo = c * ck
            glo = FFN_HIDDEN + lo
            ua = _dot(hcat, wup_ref[:, lo:lo + ck])
            ug = _dot(hcat, wup_ref[:, glo:glo + ck])
            for s in range(nslab):
                store_u(c % 2, s, ua[:, s * LANES:(s + 1) * LANES])
                store_u(c % 2, nslab + s, ug[:, s * LANES:(s + 1) * LANES])
        if c >= 1:
            lo = (c - 1) * ck
            glo = FFN_HIDDEN + lo
            acts = []
            for s in range(nslab):
                a = conv_slab((c - 1) % 2, s, lo + s * LANES)
                g = conv_slab((c - 1) % 2, nslab + s, glo + s * LANES)
                acts.append((a * (g * jax.nn.sigmoid(g))).astype(BF16))
            y = _dot(jnp.concatenate(acts, axis=-1), wdn_ref[lo:lo + ck, :]) + y
    o_ref[0] = x_ref[0] + m[:, 5 * d:6 * d] * _rms(y, ng_out_ref[...])


def _ffn_call(x_all, modsel, ng_in, ng_out, w_up, conv_w, conv_b, w_down, nctx_t, latents_only):
    b, ttot, d = x_all.shape
    nt = ttot // TM
    hh = SUBLANES_BF16
    tile0 = nctx_t if latents_only else 0
    prev, nxt = _halo_specs(d, hh, ttot, tile0)
    return pl.pallas_call(
        functools.partial(_ffn_kernel, d=d, nctx_t=nctx_t, nt=nt, tile0=tile0),
        out_shape=jax.ShapeDtypeStruct((b, ttot - tile0 * TM, d), F32),
        grid=(b, nt - tile0),
        in_specs=[pl.BlockSpec((1, TM, d), lambda bb, i: (bb, i + tile0, 0)), prev, nxt,
                  _mod_spec(6 * d, nctx_t - tile0), _resident((1, d)),
                  _resident((1, d)), _resident(w_up.shape), _resident(conv_w.shape),
                  _resident(conv_b.shape), _resident(w_down.shape)],
        out_specs=_x_spec(d),
        scratch_shapes=[pltpu.VMEM((TM + hh, d), BF16),
                        pltpu.VMEM((2, 2 * FFN_CK // LANES, TM + 2 * FFN_CPAD, LANES), F32)],
        compiler_params=_cparams(("parallel", "parallel")),
        name="conv_ffn",
    )(x_all, x_all, x_all, modsel, ng_in, ng_out, w_up, conv_w, conv_b, w_down)


MLA_QK_PAD = LANES
MLA_VA = MLA_V + SUBLANES_BF16


def _mla_in_kernel(x_ref, mod_ref, ng_ref, wdq_ref, qg_ref, wuqt_ref, wdkv_ref, kvg_ref,
                   wuk_ref, wuvt_ref, ropeq_ref, ropek_ref, qt_ref, k_ref, vt_ref, *, d):
    m = mod_ref[0, 0]
    hb = _modulated(x_ref[0], m, ng_ref[...], d, 0).astype(BF16)
    cq_pre = _dot(hb, wdq_ref[...])
    ckv = _dot(hb, wdkv_ref[...])
    cq = _rms(cq_pre, qg_ref[...]).astype(BF16)
    cn = _rms(ckv[:, 0:MLA_KV_RANK], kvg_ref[...]).astype(BF16)
    qt_all = _dot_nt(wuqt_ref[...], cq)
    k_all = _dot(cn, wuk_ref[...])
    vt_all = _dot_nt(wuvt_ref[...], cn)
    rq = ropeq_ref[...]
    cos_r, sin_r, cos_c, sin_c = rq[0:8], rq[8:16], rq[16:24], rq[24:32]
    scale = MLA_QK ** -0.5 * LOG2E
    for hd in range(MLA_HEADS):
        base = hd * MLA_QK_PAD
        nope = qt_all[base:base + MLA_NOPE]
        x1r = qt_all[base + 64:base + 72]
        x2r = qt_all[base + 72:base + 80]
        x1c = qt_all[base + 80:base + 88]
        x2c = qt_all[base + 88:base + 96]
        pad = qt_all[base + 96:base + 128]
        roped = jnp.concatenate([
            nope,
            x1r * cos_r - x2r * sin_r, x1r * sin_r + x2r * cos_r,
            x1c * cos_c - x2c * sin_c, x1c * sin_c + x2c * cos_c,
            pad], axis=0)
        qt_ref[0, hd] = (roped * scale).astype(BF16)
    rk = ropek_ref[...]
    kpe = ckv[:, LANES:2 * LANES] * rk[:, 0:LANES] + ckv[:, 2 * LANES:3 * LANES] * rk[:, LANES:2 * LANES]
    ones_rows = jnp.where(lax.broadcasted_iota(jnp.int32, (MLA_VA - MLA_V, TM), 0) == 0, 1.0, 0.0)
    for hd in range(MLA_HEADS):
        k_ref[0, hd] = (k_all[:, hd * MLA_QK_PAD:(hd + 1) * MLA_QK_PAD] + kpe).astype(BF16)
        vt_ref[0, hd] = jnp.concatenate(
            [vt_all[hd * MLA_V:(hd + 1) * MLA_V], ones_rows], axis=0).astype(BF16)


def _mla_in_call(x_all, modsel, ng, wdq, qg, wuqt, wdkv, kvg, wuk, wuvt, rope_q, rope_k, nctx_t):
    b, ttot, d = x_all.shape
    nt = ttot // TM
    hn = MLA_HEADS
    return pl.pallas_call(
        functools.partial(_mla_in_kernel, d=d),
        out_shape=(jax.ShapeDtypeStruct((b, hn, MLA_QK_PAD, ttot), BF16),
                   jax.ShapeDtypeStruct((b, hn, ttot, MLA_QK_PAD), BF16),
                   jax.ShapeDtypeStruct((b, hn, MLA_VA, ttot), BF16)),
        grid=(b, nt),
        in_specs=[_x_spec(d), _mod_spec(6 * d, nctx_t), _resident((1, d)),
                  _resident(wdq.shape), _resident(qg.shape), _resident(wuqt.shape),
                  _resident(wdkv.shape), _resident(kvg.shape), _resident(wuk.shape),
                  _resident(wuvt.shape),
                  pl.BlockSpec((32, TM), lambda bb, i: (0, i)),
                  pl.BlockSpec((TM, 2 * LANES), lambda bb, i: (i, 0))],
        out_specs=(pl.BlockSpec((1, hn, MLA_QK_PAD, TM), lambda bb, i: (bb, 0, 0, i)),
                   pl.BlockSpec((1, hn, TM, MLA_QK_PAD), lambda bb, i: (bb, 0, i, 0)),
                   pl.BlockSpec((1, hn, MLA_VA, TM), lambda bb, i: (bb, 0, 0, i))),
        compiler_params=_cparams(("parallel", "parallel")),
        name="mla_in_proj",
    )(x_all, modsel, ng, wdq, qg, wuqt, wdkv, kvg, wuk, wuvt, rope_q, rope_k)


MLA_KC = 256
MLA_LOOKAHEAD = 8
MLA_HPS = 8


def _mla_attn_kernel(qt_ref, k_ref, vt_ref, o_ref, *, nkc):
    def chunk(c):
        return slice(c * MLA_KC, (c + 1) * MLA_KC)

    items = [(hh, c) for hh in range(MLA_HPS) for c in range(nkc)]
    scores, ms, accs, outs = [], [], [], []
    for i in range(len(items) + MLA_LOOKAHEAD):
        if i < len(items):
            hh, c = items[i]
            scores.append(_dot(k_ref[0, hh, chunk(c), :], qt_ref[0, hh]))
        if i >= MLA_LOOKAHEAD:
            hh, c = items[i - MLA_LOOKAHEAD]
            s = scores[i - MLA_LOOKAHEAD]
            m_c = s.max(axis=0, keepdims=True)
            ms.append(m_c)
            accs.append(_dot(vt_ref[0, hh, :, chunk(c)], jnp.exp2(s - m_c).astype(BF16)))
            if c == nkc - 1:
                m = functools.reduce(jnp.maximum, ms)
                acc = jnp.zeros((MLA_VA, TM), F32)
                for m_c, a_c in zip(ms, accs):
                    acc = acc + jnp.exp2(m_c - m) * a_c
                outs.append(acc[0:MLA_V] / acc[MLA_V:MLA_V + 1])
                ms, accs = [], []
    o_ref[0] = jnp.concatenate(outs, axis=0).T.astype(BF16)


def _mla_attn_call(qt, k, vt, q_tile0, n_qtiles, kv_len):
    b, hn, _, ttot = qt.shape
    return pl.pallas_call(
        functools.partial(_mla_attn_kernel, nkc=kv_len // MLA_KC),
        out_shape=jax.ShapeDtypeStruct((b, n_qtiles * TM, hn * MLA_V), BF16),
        grid=(b, hn // MLA_HPS, n_qtiles),
        in_specs=[
            pl.BlockSpec((1, MLA_HPS, MLA_QK_PAD, TM), lambda bb, hp, qi: (bb, hp, 0, q_tile0 + qi)),
            pl.BlockSpec((1, MLA_HPS, kv_len, MLA_QK_PAD), lambda bb, hp, qi: (bb, hp, 0, 0)),
            pl.BlockSpec((1, MLA_HPS, MLA_VA, kv_len), lambda bb, hp, qi: (bb, hp, 0, 0)),
        ],
        out_specs=pl.BlockSpec((1, TM, MLA_HPS * MLA_V), lambda bb, hp, qi: (bb, qi, hp)),
        compiler_params=_cparams(("parallel", "parallel", "arbitrary")),
        name="mla_attention",
    )(qt, k, vt)


def _rope_tables(ctx, t):
    pos = np.arange(t)
    half = MLA_ROPE // 4
    inv = ROPE_BASE ** (-jnp.arange(half, dtype=F32) / half)
    tabs = []
    for p in (pos // GRID_W, pos % GRID_W):
        ang = jnp.asarray(p, F32)[:, None] * inv
        cos = jnp.concatenate([jnp.ones((ctx, half), F32), jnp.cos(ang)], axis=0)
        sin = jnp.concatenate([jnp.zeros((ctx, half), F32), jnp.sin(ang)], axis=0)
        tabs.append((cos, sin))
    (cr, sr), (cc, sc) = tabs
    rope_q = jnp.concatenate([cr, sr, cc, sc], axis=1).T
    ttot = ctx + t
    zeros = jnp.zeros((ttot, MLA_NOPE), F32)
    tail = jnp.zeros((ttot, LANES - MLA_QK), F32)
    cos_k = jnp.concatenate([zeros, cr, cr, cc, cc, tail], axis=1)
    sin_k = jnp.concatenate([zeros, -sr, sr, -sc, sc, tail], axis=1)
    rope_k = jnp.concatenate([cos_k, sin_k], axis=1)
    return rope_q, rope_k


def _mla_weights(w_uq, w_dkv, w_ukv):
    hn = MLA_HEADS
    q_rank = w_uq.shape[0]
    wq = w_uq.reshape(q_rank, hn, MLA_QK)
    wq = jnp.pad(wq, ((0, 0), (0, 0), (0, MLA_QK_PAD - MLA_QK)))
    wuqt = wq.reshape(q_rank, hn * MLA_QK_PAD).T.astype(BF16)
    d = w_dkv.shape[0]
    kpe = w_dkv[:, MLA_KV_RANK:]
    h8 = MLA_ROPE // 4
    swap = jnp.concatenate([kpe[:, h8:2 * h8], kpe[:, 0:h8], kpe[:, 3 * h8:4 * h8],
                            kpe[:, 2 * h8:3 * h8]], axis=1)

    def place(cols):
        return jnp.concatenate([jnp.zeros((d, MLA_NOPE), F32), cols,
                                jnp.zeros((d, LANES - MLA_QK), F32)], axis=1)

    wdkv = jnp.concatenate([w_dkv[:, :MLA_KV_RANK], place(kpe), place(swap)], axis=1).astype(BF16)
    wkv = w_ukv.reshape(MLA_KV_RANK, hn, MLA_NOPE + MLA_V)
    wuk = jnp.pad(wkv[:, :, :MLA_NOPE], ((0, 0), (0, 0), (0, MLA_QK_PAD - MLA_NOPE)))
    wuk = wuk.reshape(MLA_KV_RANK, hn * MLA_QK_PAD).astype(BF16)
    wuvt = wkv[:, :, MLA_NOPE:].reshape(MLA_KV_RANK, hn * MLA_V).T.astype(BF16)
    return wuqt, wdkv, wuk, wuvt


def kernel(x, c, ctx, c_ctx, ada_w, ada_b, norm_g, ffn_w_up, ffn_conv_w, ffn_conv_b, ffn_w_down,
           ev_w_in, ev_gate_b, ev_conv_w, ev_conv_b, ev_rpb, ev_ml_norm_g, ev_w_out,
           od_w_dq, od_q_norm_g, od_w_uq, od_w_dkv, od_kv_norm_g, od_w_ukv, od_w_o):
    b, t, d = x.shape
    nctx = ctx.shape[1]
    depth = ada_w.shape[0]
    ttot = nctx + t
    assert t % TM == 0 and nctx % TM == 0 and t % GRID_W == 0
    nctx_t = nctx // TM

    x_all = jnp.concatenate([ctx, x], axis=1)

    rows = -(-(b + 1) // 8) * 8
    cvec = jnp.zeros((rows, d), F32).at[:b].set(c).at[b].set(c_ctx)
    mod = _ada_call(cvec, ada_w, ada_b)

    rope_q, rope_k = _rope_tables(nctx, t)
    ml_scale = jnp.concatenate([jnp.full((1, ML_WIDTH), ML_HD ** -0.5, F32),
                                jnp.ones((1, ML_WIDTH), F32)], axis=1)

    for l in range(depth):
        ctx_out = l < depth - 1
        modsel = jnp.stack([jnp.broadcast_to(mod[l, b], (b, 6 * d)), mod[l, :b]], axis=1)
        modsel = modsel.reshape(b, 2, 1, 6 * d)
        ng = norm_g[l].reshape(4, 1, d)
        if l % 2 == 0:
            e = l // 2
            w_in = ev_w_in[e]
            n_main = 3 * NA_WIDTH + 4 * ML_WIDTH
            w_gate = jnp.pad(w_in[:, n_main:], ((0, 0), (0, LANES - ML_GATES))).astype(BF16)
            gate_b = jnp.pad(ev_gate_b[e], (0, LANES - ML_GATES)).reshape(1, LANES)
            naq, nak, nav, mlqk, mlv, mlo, gates = _even_in_call(
                x_all, modsel, ng[0], w_in[:, :n_main].astype(BF16), w_gate, gate_b, nctx_t)
            na_x = _na_call(naq, nak, nav, _na_bias_table(ev_rpb[e], t // GRID_W), nctx, t)
            na_c = _na_ctx_call(naq, nak, nav, nctx)
            qk, kt = _ml_conv_call(mlqk, ev_conv_w[e], ev_conv_b[e].reshape(1, -1), ml_scale, nctx_t)
            gates_t = jnp.swapaxes(gates[:, :, :ML_GATES], 1, 2)
            hf, hb = _mlstm_call(qk, kt, mlv, gates, gates_t, nctx // ML_CHUNK)
            x_all = _even_out_call(x_all, modsel, ng[1], na_c, na_x, hf, hb, mlo,
                                   ev_ml_norm_g[e].reshape(1, -1), ev_w_out[e].astype(BF16), nctx_t)
        else:
            o = l // 2
            wuqt, wdkv, wuk, wuvt = _mla_weights(od_w_uq[o], od_w_dkv[o], od_w_ukv[o])
            qt, kk, vt = _mla_in_call(
                x_all, modsel, ng[0], od_w_dq[o].astype(BF16), od_q_norm_g[o].reshape(1, -1),
                wuqt, wdkv, od_kv_norm_g[o].reshape(1, -1), wuk, wuvt, rope_q, rope_k, nctx_t)
            att_x = _mla_attn_call(qt, kk, vt, nctx_t, t // TM, ttot)
            att_c = _mla_attn_call(qt, kk, vt, 0, nctx_t, nctx) if ctx_out else att_x
            x_all = _odd_out_call(x_all, modsel, ng[1], att_c, att_x, od_w_o[o].astype(BF16), nctx_t)
        x_all = _ffn_call(x_all, modsel, ng[2], ng[3], ffn_w_up[l].astype(BF16), ffn_conv_w[l],
                          ffn_conv_b[l].reshape(1, -1), ffn_w_down[l].astype(BF16), nctx_t,
                          latents_only=not ctx_out)
    return x_all
```

```python
import functools
import math

import jax
import jax.numpy as jnp
import numpy as np
from jax import lax
from jax.experimental import pallas as pl
from jax.experimental.pallas import tpu as pltpu

F32 = jnp.float32
BF16 = jnp.bfloat16

EPS = 1e-6
NEG = -1e30
LOG2E = math.log2(math.e)

GRID_W = 64
NA_HEADS = 8
NA_HD = 64
NA_WIN_ROWS = 8
NA_WIN_COLS = 16
NA_WIDTH = NA_HEADS * NA_HD

ML_HEADS = 4
ML_HD = 128
ML_WIDTH = ML_HEADS * ML_HD
ML_CHUNK = 128
ML_GATES = 4 * ML_HEADS

MLA_HEADS = 16
MLA_NOPE = 64
MLA_ROPE = 32
MLA_V = 64
MLA_QK = MLA_NOPE + MLA_ROPE
MLA_Q_RANK = 256
MLA_KV_RANK = 128
ROPE_BASE = 10000.0

FFN_HIDDEN = 2816

LANES = 128
SUBLANES_BF16 = 16
TM = 256
VMEM_LIMIT = 48 * 1024 * 1024


def _cparams(sem):
    return pltpu.CompilerParams(dimension_semantics=sem, vmem_limit_bytes=VMEM_LIMIT)


def _resident(shape):
    nd = len(shape)
    return pl.BlockSpec(shape, lambda *_: (0,) * nd, pipeline_mode=pl.Buffered(1))


def _rms(xf, g):
    ms = jnp.mean(xf * xf, axis=-1, keepdims=True)
    return xf * lax.rsqrt(ms + EPS) * g


def _dot(a, b):
    return jnp.dot(a, b, preferred_element_type=F32)


def _dot_nt(a, b):
    return lax.dot_general(a, b, (((1,), (1,)), ((), ())), preferred_element_type=F32)


def _ada_kernel(c_ref, w_ref, b_ref, o_ref):
    c = c_ref[...]
    s = c * jax.nn.sigmoid(c)
    o_ref[0] = _dot(s.astype(BF16), w_ref[0].astype(BF16)) + b_ref[0]


def _ada_call(cvec, ada_w, ada_b):
    depth, d, n = ada_w.shape
    rows = cvec.shape[0]
    tn = 1024
    return pl.pallas_call(
        _ada_kernel,
        out_shape=jax.ShapeDtypeStruct((depth, rows, n), F32),
        grid=(depth, n // tn),
        in_specs=[
            pl.BlockSpec((rows, d), lambda l, j: (0, 0)),
            pl.BlockSpec((1, d, tn), lambda l, j: (l, 0, j)),
            pl.BlockSpec((1, 1, tn), lambda l, j: (l, 0, j)),
        ],
        out_specs=pl.BlockSpec((1, rows, tn), lambda l, j: (l, 0, j)),
        compiler_params=_cparams(("parallel", "parallel")),
        name="ada_mod",
    )(cvec, ada_w, ada_b.reshape(depth, 1, n))


BB = 2


def _bb(b):
    return BB if b % BB == 0 else 1


def _x_spec(d, bb=1):
    return pl.BlockSpec((bb, TM, d), lambda b, i: (b, i, 0))


def _mod_spec(d6, nctx_t, bb=1):
    return pl.BlockSpec((bb, 1, 1, d6), lambda b, i: (b, jnp.where(i >= nctx_t, 1, 0), 0, 0))


def _modulated(x, m, g, d, which):
    o = 3 * d * which
    return _rms(x, g) * (1.0 + m[:, o + d:o + 2 * d]) + m[:, o:o + d]


def _rows(ref):
    return jnp.concatenate([ref[s] for s in range(ref.shape[0])], axis=0)


def _store_rows(ref, val):
    for s in range(ref.shape[0]):
        ref[s] = val[s * TM:(s + 1) * TM].astype(ref.dtype)


def _modulated_rows(x_ref, mod_ref, g, d, which):
    return jnp.concatenate([_modulated(x_ref[s], mod_ref[s, 0], g, d, which)
                            for s in range(x_ref.shape[0])], axis=0)


def _even_in_kernel(x_ref, mod_ref, ng_ref, w_ref, wg_ref, gb_ref,
                    naq_ref, nak_ref, nav_ref, mlqk_ref, mlv_ref, mlo_ref, g_ref, *, d):
    hb = _modulated_rows(x_ref, mod_ref, ng_ref[...], d, 0).astype(BF16)

    def seg(lo, hi):
        return _dot(hb, w_ref[:, lo:hi])

    w = NA_WIDTH
    _store_rows(naq_ref, seg(0, w) * (NA_HD ** -0.5 * LOG2E))
    _store_rows(nak_ref, seg(w, 2 * w))
    _store_rows(nav_ref, seg(2 * w, 3 * w))
    o = 3 * w
    _store_rows(mlqk_ref, jnp.concatenate([seg(o, o + ML_WIDTH).astype(BF16),
                                           seg(o + ML_WIDTH, o + 2 * ML_WIDTH).astype(BF16)], axis=-1))
    _store_rows(mlv_ref, seg(o + 2 * ML_WIDTH, o + 3 * ML_WIDTH))
    _store_rows(mlo_ref, seg(o + 3 * ML_WIDTH, o + 4 * ML_WIDTH))
    _store_rows(g_ref, _dot(hb, wg_ref[...]) + gb_ref[...])


def _even_in_call(x_all, modsel, ng, w_main, w_gate, gate_b, nctx_t):
    b, ttot, d = x_all.shape
    nt = ttot // TM
    bb = _bb(b)
    n_main = w_main.shape[1]
    row = lambda c: _x_spec(c, bb)
    sds = lambda c, dt: jax.ShapeDtypeStruct((b, ttot, c), dt)
    return pl.pallas_call(
        functools.partial(_even_in_kernel, d=d),
        out_shape=(sds(NA_WIDTH, BF16), sds(NA_WIDTH, BF16), sds(NA_WIDTH, BF16),
                   sds(2 * ML_WIDTH, BF16), sds(ML_WIDTH, BF16), sds(ML_WIDTH, BF16),
                   sds(LANES, F32)),
        grid=(b // bb, nt),
        in_specs=[_x_spec(d, bb), _mod_spec(6 * d, nctx_t, bb), _resident((1, d)),
                  _resident((d, n_main)), _resident((d, LANES)), _resident((1, LANES))],
        out_specs=(row(NA_WIDTH), row(NA_WIDTH), row(NA_WIDTH), row(2 * ML_WIDTH),
                   row(ML_WIDTH), row(ML_WIDTH), row(LANES)),
        compiler_params=_cparams(("parallel", "parallel")),
        name="even_in_proj",
    )(x_all, modsel, ng, w_main, w_gate, gate_b)


def _pair_scores(qp, k_parts, bias_parts):
    lane = lax.broadcasted_iota(jnp.int32, qp.shape, 1)
    scores = []
    for hh in range(2):
        keep = (lane >= NA_HD) if hh else (lane < NA_HD)
        qh = jnp.where(keep, qp, jnp.zeros_like(qp))
        s = []
        for kk, bias in zip(k_parts, bias_parts):
            sp = _dot_nt(qh, kk)
            if bias is not None:
                sp = sp + bias[hh]
            s.append(sp)
        scores.append(s)
    return scores


def _pair_finish(scores, v_parts):
    mq = scores[0][0].shape[0]
    lane = lax.broadcasted_iota(jnp.int32, (mq, LANES), 1)
    v_aug = []
    for vv in v_parts:
        ones_col = jnp.where(lax.broadcasted_iota(jnp.int32, vv.shape, 1) == 0, 1.0, 0.0)
        v_aug.append(jnp.concatenate([vv, ones_col.astype(BF16)], axis=-1))
    outs = []
    for s in scores:
        m = functools.reduce(jnp.maximum, [sp.max(axis=-1, keepdims=True) for sp in s])
        o = jnp.zeros((mq, 2 * LANES), F32)
        for sp, vv in zip(s, v_aug):
            o = o + _dot(jnp.exp2(sp - m).astype(BF16), vv)
        outs.append(o[:, 0:LANES] / o[:, LANES:LANES + 1])
    return jnp.where(lane < NA_HD, outs[0], outs[1])


NA_RG = 4
NA_UB = NA_WIN_ROWS + NA_RG - 1


def _na_union_start(r0, rows):
    return jnp.clip(r0 - NA_WIN_ROWS // 2, 0, rows - NA_UB)


def _na_kernel(q_ref, k_ref, v_ref, bias_ref, o_ref, *, ctx, rows):
    us = _na_union_start(pl.program_id(1) * NA_RG, rows)
    start = pl.multiple_of(ctx + us * GRID_W, GRID_W)
    band = NA_UB * GRID_W
    pairs = [slice(p * LANES, (p + 1) * LANES) for p in range(NA_HEADS // 2)]

    def pair_scores(p):
        cs = pairs[p]
        kb = k_ref[0, pl.ds(start, band), cs]
        kc = k_ref[0, 0:ctx, cs]
        bias = (bias_ref[0, 2 * p], bias_ref[0, 2 * p + 1])
        return _pair_scores(q_ref[0, :, cs], (kb, kc), (bias, None))

    scores = [pair_scores(0)]
    for p, cs in enumerate(pairs):
        if p + 1 < len(pairs):
            scores.append(pair_scores(p + 1))
        vb = v_ref[0, pl.ds(start, band), cs]
        vc = v_ref[0, 0:ctx, cs]
        o_ref[0, :, cs] = _pair_finish(scores[p], (vb, vc)).astype(BF16)


def _na_call(naq, nak, nav, bias_tab, ctx, t):
    b, ttot, w = naq.shape
    rows = t // GRID_W
    groups = rows // NA_RG
    assert rows % NA_RG == 0 and groups >= 3 and ctx % (NA_RG * GRID_W) == 0
    q_off = ctx // (NA_RG * GRID_W)
    mq = NA_RG * GRID_W

    def bias_type(bb, g):
        return (jnp.where(g == 0, 0, jnp.where(g == groups - 1, 2, 1)), 0, 0, 0)

    whole = lambda bb, g: (bb, 0, 0)
    return pl.pallas_call(
        functools.partial(_na_kernel, ctx=ctx, rows=rows),
        out_shape=jax.ShapeDtypeStruct((b, t, w), BF16),
        grid=(b, groups),
        in_specs=[
            pl.BlockSpec((1, mq, w), lambda bb, g: (bb, q_off + g, 0)),
            pl.BlockSpec((1, ttot, w), whole, pipeline_mode=pl.Buffered(1)),
            pl.BlockSpec((1, ttot, w), whole, pipeline_mode=pl.Buffered(1)),
            pl.BlockSpec((1, NA_HEADS, mq, NA_UB * GRID_W), bias_type),
        ],
        out_specs=pl.BlockSpec((1, mq, w), lambda bb, g: (bb, g, 0)),
        compiler_params=_cparams(("parallel", "arbitrary")),
        name="na_attention",
    )(naq, nak, nav, bias_tab)


def _na_ctx_kernel(q_ref, k_ref, v_ref, o_ref):
    pairs = [slice(p * LANES, (p + 1) * LANES) for p in range(NA_HEADS // 2)]
    scores = [_pair_scores(q_ref[0, :, cs], (k_ref[0, :, cs],), (None,)) for cs in pairs]
    for sc, cs in zip(scores, pairs):
        o_ref[0, :, cs] = _pair_finish(sc, (v_ref[0, :, cs],)).astype(BF16)


def _na_ctx_call(naq, nak, nav, ctx):
    b, _, w = naq.shape
    spec = pl.BlockSpec((1, ctx, w), lambda bb: (bb, 0, 0))
    return pl.pallas_call(
        _na_ctx_kernel,
        out_shape=jax.ShapeDtypeStruct((b, ctx, w), BF16),
        grid=(b,),
        in_specs=[spec, spec, spec],
        out_specs=spec,
        compiler_params=_cparams(("parallel",)),
        name="na_ctx_attention",
    )(naq, nak, nav)


def _na_group_geometry(g, rows):
    r = g * NA_RG + np.arange(NA_RG)[:, None]
    us = int(np.clip(g * NA_RG - NA_WIN_ROWS // 2, 0, rows - NA_UB))
    kr = us + np.arange(NA_UB)[None, :]
    rs = np.clip(r - NA_WIN_ROWS // 2, 0, rows - NA_WIN_ROWS)
    valid = (kr >= rs) & (kr < rs + NA_WIN_ROWS)
    dr_idx = np.clip(kr - r + NA_WIN_ROWS - 1, 0, 2 * NA_WIN_ROWS - 2)
    return dr_idx, valid


def _na_bias_table(rpb, rows):
    groups = rows // NA_RG
    geo = [_na_group_geometry(g, rows) for g in range(groups)]
    for dr_g, valid_g in geo[1:groups - 1]:
        assert (valid_g == geo[1][1]).all() and (dr_g[valid_g] == geo[1][0][valid_g]).all()
    dr_idx = np.stack([geo[g][0] for g in (0, 1, groups - 1)])
    valid = np.stack([geo[g][1] for g in (0, 1, groups - 1)])
    kw = NA_WIN_COLS
    col = jnp.arange(GRID_W)
    cs = jnp.clip(col - kw // 2, 0, GRID_W - kw)
    in_win = (col[None, :] >= cs[:, None]) & (col[None, :] < cs[:, None] + kw)
    dc_idx = jnp.clip(col[None, :] - col[:, None], -(kw - 1), kw - 1) + (NA_WIN_COLS - 1)
    rpb_cols = rpb.astype(F32)[:, :, dc_idx] * LOG2E
    rpb_cols = jnp.where(in_win[None, None], rpb_cols, NEG)
    tab = rpb_cols[:, dr_idx]
    tab = jnp.where(jnp.asarray(valid)[None, :, :, :, None, None], tab, NEG)
    tab = tab.transpose(1, 0, 2, 4, 3, 5)
    return tab.reshape(3, NA_HEADS, NA_RG * GRID_W, NA_UB * GRID_W)


def _halo_specs(c, rows_h, ttot, tile0=0, bb=1):
    per = TM // rows_h
    last = ttot // rows_h - 1
    prev = pl.BlockSpec((bb, rows_h, c), lambda b, i: (b, jnp.maximum((i + tile0) * per - 1, 0), 0))
    nxt = pl.BlockSpec((bb, rows_h, c), lambda b, i: (b, jnp.minimum((i + tile0 + 1) * per, last), 0))
    return prev, nxt


def _halo_valid(nctx_t, nt, tile0=0):
    i = pl.program_id(1) + tile0
    pv = jnp.where((i == 0) | (i == nctx_t), 0.0, 1.0).astype(F32)
    nv = jnp.where((i == nctx_t - 1) | (i == nt - 1), 0.0, 1.0).astype(F32)
    return pv, nv


def _conv3(u_ref, h, cw, cb, lo, hi):
    return (cb
            + u_ref[h - 1:h - 1 + TM, lo:hi] * cw[0:1]
            + u_ref[h:h + TM, lo:hi] * cw[1:2]
            + u_ref[h + 1:h + 1 + TM, lo:hi] * cw[2:3])


def _ml_conv_kernel(x_ref, xp_ref, xn_ref, cw_ref, cb_ref, sc_ref, o_ref, kt_ref, u_ref, *, nctx_t, nt):
    pv, nv = _halo_valid(nctx_t, nt)
    h = SUBLANES_BF16
    c = x_ref.shape[-1]
    for s in range(x_ref.shape[0]):
        us_ref = u_ref.at[s]
        us_ref[0:h] = xp_ref[s].astype(F32) * pv
        us_ref[h:h + TM] = x_ref[s].astype(F32)
        us_ref[h + TM:h + TM + h] = xn_ref[s].astype(F32) * nv
        y = _conv3(us_ref, h, cw_ref[...], cb_ref[...], 0, c)
        y = y * jax.nn.sigmoid(y) * sc_ref[...]
        o_ref[s] = y.astype(BF16)
        kt_ref[s] = y[:, ML_WIDTH:2 * ML_WIDTH].T.astype(BF16)


def _ml_conv_call(mlqk, conv_w, conv_b, scale, nctx_t):
    b, ttot, c = mlqk.shape
    nt = ttot // TM
    bb = _bb(b)
    prev, nxt = _halo_specs(c, SUBLANES_BF16, ttot, bb=bb)
    return pl.pallas_call(
        functools.partial(_ml_conv_kernel, nctx_t=nctx_t, nt=nt),
        out_shape=(jax.ShapeDtypeStruct((b, ttot, c), BF16),
                   jax.ShapeDtypeStruct((b, ML_WIDTH, ttot), BF16)),
        grid=(b // bb, nt),
        in_specs=[_x_spec(c, bb), prev, nxt, _resident((3, c)), _resident((1, c)), _resident((1, c))],
        out_specs=(_x_spec(c, bb), pl.BlockSpec((bb, ML_WIDTH, TM), lambda g, i: (g, 0, i))),
        scratch_shapes=[pltpu.VMEM((bb, TM + 2 * SUBLANES_BF16, c), F32)],
        compiler_params=_cparams(("parallel", "parallel")),
        name="mlstm_conv",
    )(mlqk, mlqk, mlqk, conv_w, conv_b, scale)


def _log_sigmoid(x):
    return jnp.minimum(x, 0.0) - jnp.log(1.0 + jnp.exp(-jnp.abs(x)))


def _split_dot(a, b, data_is_rhs):
    data = b if data_is_rhs else a
    hi = data.astype(BF16)
    lo = (data - hi.astype(F32)).astype(BF16)
    if data_is_rhs:
        return _dot(a, hi) + _dot(a, lo)
    return _dot(hi, b) + _dot(lo, b)


def _mlstm_kernel(qkf_ref, ktf_ref, vf_ref, gf_ref, gtf_ref, qkb_ref, ktb_ref, vb_ref, gb_ref, gtb_ref,
                  hf_ref, hb_ref, ct_ref, m_ref):
    L = ML_CHUNK

    @pl.when(pl.program_id(1) == 0)
    def _():
        ct_ref[...] = jnp.zeros_like(ct_ref)
        m_ref[...] = jnp.zeros_like(m_ref)

    ri = lax.broadcasted_iota(jnp.int32, (L, L), 0)
    ci = lax.broadcasted_iota(jnp.int32, (L, L), 1)
    tril = jnp.where(ri >= ci, 1.0, 0.0).astype(BF16)
    triu = jnp.where(ri <= ci, 1.0, 0.0).astype(BF16)
    ones_col = jnp.ones((L, LANES), BF16)

    dirs = (
        (qkf_ref, ktf_ref, vf_ref, gf_ref, gtf_ref, hf_ref, tril, triu, ri >= ci, L - 1, 0),
        (qkb_ref, ktb_ref, vb_ref, gb_ref, gtb_ref, hb_ref, triu, tril, ri <= ci, 0, 2 * ML_HEADS),
    )
    chains = []
    for dnum, (qk_ref, kt_ref, v_ref, g_ref, gt_ref, h_ref, tcol, trow, mask, tot_row, goff) in enumerate(dirs):
        gt = gt_ref[0]
        cum_col_all = _split_dot(tcol, _log_sigmoid(g_ref[0]), True)
        cum_row_all = _split_dot(_log_sigmoid(gt), trow, False)
        for hd in range(ML_HEADS):
            ic = goff + hd
            fc = goff + ML_HEADS + hd
            cs = slice(hd * ML_HD, (hd + 1) * ML_HD)
            j = dnum * ML_HEADS + hd
            q = qk_ref[0, :, cs]
            k = qk_ref[0, :, ML_WIDTH + hd * ML_HD:ML_WIDTH + (hd + 1) * ML_HD]
            ct = ct_ref[j]
            cum_col = cum_col_all[:, fc:fc + 1]
            chains.append(dict(
                j=j, h_ref=h_ref, cs=cs, kt=kt_ref[0, cs, :], ct=ct, mask=mask,
                v_aug=jnp.concatenate([v_ref[0, :, cs], ones_col], axis=-1),
                s=_dot_nt(q, k), hq=_dot(q, ct.astype(BF16)),
                cum_col=cum_col, cum_row=cum_row_all[fc:fc + 1, :],
                i_row=gt[ic:ic + 1, :], total=cum_col[tot_row:tot_row + 1, :]))
    for ch in chains:
        m_st = m_ref[ch["j"]][0:1, 0:1]
        cum_col, total = ch["cum_col"], ch["total"]
        e_row = ch["i_row"] - ch["cum_row"]
        base = jnp.where(ch["mask"], e_row, NEG)
        m_inter = cum_col + m_st
        m_t = jnp.maximum(m_inter, cum_col + base.max(axis=-1, keepdims=True))
        g_row = total + e_row
        m_new = jnp.maximum(total + m_st, g_row.max(axis=-1, keepdims=True))
        ch.update(base=base, m_inter=m_inter, m_t=m_t, g_row=g_row, m_new=m_new,
                  decay=jnp.exp(total + m_st - m_new))
    for ch in chains:
        ch["kw"] = (ch["kt"].astype(F32) * jnp.exp(ch["g_row"] - ch["m_new"])).astype(BF16)
    for ch in chains:
        ch["u"] = _dot(ch["kw"], ch["v_aug"])
    for ch in chains:
        ch["qk"] = (ch["s"] * jnp.exp(ch["base"] + (ch["cum_col"] - ch["m_t"]))).astype(BF16)
        ch["a"] = jnp.exp(ch["m_inter"] - ch["m_t"])
        ch["floor"] = jnp.exp(-ch["m_t"])
    for ch in chains:
        ch["p"] = _dot(ch["qk"], ch["v_aug"])
    for ch in chains:
        ct_ref[ch["j"]] = ch["decay"] * ch["ct"] + ch["u"]
        m_ref[ch["j"]] = jnp.broadcast_to(ch["m_new"], m_ref.shape[1:])
    for ch in chains:
        hfull = ch["a"] * ch["hq"] + ch["p"]
        den = hfull[:, ML_HD:2 * ML_HD]
        h_out = hfull[:, 0:ML_HD] / jnp.maximum(jnp.abs(den), ch["floor"])
        ch["h_ref"][0, :, ch["cs"]] = h_out.astype(BF16)


def _mlstm_call(qk, kt, v, gates, gates_t, nctx_c):
    b, ttot, _ = qk.shape
    nc = ttot // ML_CHUNK
    L = ML_CHUNK

    def fwd(bb, c):
        return (bb, c, 0)

    def bwd(bb, c):
        return (bb, jnp.where(c < nctx_c, nctx_c - 1 - c, nc - 1 + nctx_c - c), 0)

    def fwd_t(bb, c):
        return (bb, 0, c)

    def bwd_t(bb, c):
        return (bb, 0, jnp.where(c < nctx_c, nctx_c - 1 - c, nc - 1 + nctx_c - c))

    ins = []
    for row_map, col_map in ((fwd, fwd_t), (bwd, bwd_t)):
        ins += [pl.BlockSpec((1, L, 2 * ML_WIDTH), row_map),
                pl.BlockSpec((1, ML_WIDTH, L), col_map),
                pl.BlockSpec((1, L, ML_WIDTH), row_map),
                pl.BlockSpec((1, L, LANES), row_map),
                pl.BlockSpec((1, ML_GATES, L), col_map)]
    out_sds = jax.ShapeDtypeStruct((b, ttot, ML_WIDTH), BF16)
    return pl.pallas_call(
        _mlstm_kernel,
        out_shape=(out_sds, out_sds),
        grid=(b, nc),
        in_specs=ins,
        out_specs=(pl.BlockSpec((1, L, ML_WIDTH), fwd), pl.BlockSpec((1, L, ML_WIDTH), bwd)),
        scratch_shapes=[pltpu.VMEM((2 * ML_HEADS, ML_HD, 2 * ML_HD), F32),
                        pltpu.VMEM((2 * ML_HEADS, 8, LANES), F32)],
        compiler_params=_cparams(("parallel", "arbitrary")),
        name="mlstm_scan",
    )(qk, kt, v, gates, gates_t, qk, kt, v, gates, gates_t)


def _out_tail(x_ref, mod_ref, ng_ref, y, o_ref, d):
    for s in range(x_ref.shape[0]):
        m = mod_ref[s, 0]
        o_ref[s] = x_ref[s] + m[:, 2 * d:3 * d] * _rms(y[s * TM:(s + 1) * TM], ng_ref[...])


def _ctx_lat_specs(c, nctx_t, bb=1):
    ctx_spec = pl.BlockSpec((bb, TM, c), lambda b, i: (b, jnp.minimum(i, nctx_t - 1), 0))
    lat_spec = pl.BlockSpec((bb, TM, c), lambda b, i: (b, jnp.maximum(i - nctx_t, 0), 0))
    return ctx_spec, lat_spec


def _ctx_or_lat(c_ref, l_ref, nctx_t):
    return jnp.where(pl.program_id(1) < nctx_t, _rows(c_ref), _rows(l_ref))


def _even_out_kernel(x_ref, mod_ref, ng_ref, nac_ref, nal_ref, hf_ref, hb_ref, op_ref, mlg_ref, w_ref,
                     o_ref, *, d, nctx_t):
    hs = jax.nn.sigmoid(_rows(op_ref).astype(F32)) * (_rows(hf_ref).astype(F32) + _rows(hb_ref).astype(F32))
    parts = []
    for hd in range(ML_HEADS):
        seg = hs[:, hd * ML_HD:(hd + 1) * ML_HD]
        mu = jnp.mean(seg, axis=-1, keepdims=True)
        cen = seg - mu
        var = jnp.mean(cen * cen, axis=-1, keepdims=True)
        parts.append(cen * lax.rsqrt(var + EPS))
    ml = (jnp.concatenate(parts, axis=-1) * mlg_ref[...]).astype(BF16)
    na = _ctx_or_lat(nac_ref, nal_ref, nctx_t)
    y = _dot(na, w_ref[0:NA_WIDTH, :]) + _dot(ml, w_ref[NA_WIDTH:NA_WIDTH + ML_WIDTH, :])
    _out_tail(x_ref, mod_ref, ng_ref, y, o_ref, d)


def _even_out_call(x_all, modsel, ng, na_c, na_x, hf, hb, mlo, mlg, w_out, nctx_t):
    b, ttot, d = x_all.shape
    bb = _bb(b)
    row = lambda c: _x_spec(c, bb)
    return pl.pallas_call(
        functools.partial(_even_out_kernel, d=d, nctx_t=nctx_t),
        out_shape=jax.ShapeDtypeStruct((b, ttot, d), F32),
        grid=(b // bb, ttot // TM),
        in_specs=[_x_spec(d, bb), _mod_spec(6 * d, nctx_t, bb), _resident((1, d)),
                  *_ctx_lat_specs(NA_WIDTH, nctx_t, bb), row(ML_WIDTH), row(ML_WIDTH), row(ML_WIDTH),
                  _resident((1, ML_WIDTH)), _resident(w_out.shape)],
        out_specs=_x_spec(d, bb),
        compiler_params=_cparams(("parallel", "parallel")),
        name="even_out_proj",
    )(x_all, modsel, ng, na_c, na_x, hf, hb, mlo, mlg, w_out)


def _odd_out_kernel(x_ref, mod_ref, ng_ref, ac_ref, al_ref, w_ref, o_ref, *, d, nctx_t):
    a = _ctx_or_lat(ac_ref, al_ref, nctx_t)
    _out_tail(x_ref, mod_ref, ng_ref, _dot(a, w_ref[...]), o_ref, d)


def _odd_out_call(x_all, modsel, ng, att_c, att_x, w_o, nctx_t):
    b, ttot, d = x_all.shape
    bb = _bb(b)
    return pl.pallas_call(
        functools.partial(_odd_out_kernel, d=d, nctx_t=nctx_t),
        out_shape=jax.ShapeDtypeStruct((b, ttot, d), F32),
        grid=(b // bb, ttot // TM),
        in_specs=[_x_spec(d, bb), _mod_spec(6 * d, nctx_t, bb), _resident((1, d)),
                  *_ctx_lat_specs(att_x.shape[-1], nctx_t, bb), _resident(w_o.shape)],
        out_specs=_x_spec(d, bb),
        compiler_params=_cparams(("parallel", "parallel")),
        name="odd_out_proj",
    )(x_all, modsel, ng, att_c, att_x, w_o)


FFN_CK = 256
FFN_CPAD = 8


def _ffn_kernel(x_ref, xp_ref, xn_ref, mod_ref, ng_in_ref, ng_out_ref, wup_ref, cw_ref, cb_ref,
                wdn_ref, o_ref, h_ref, u_ref, *, d, nctx_t, nt, tile0):
    pv, nv = _halo_valid(nctx_t, nt, tile0)
    g_in = ng_in_ref[...]
    hh = SUBLANES_BF16
    nb = x_ref.shape[0]
    assert 2 * nb <= hh
    rid = lax.broadcasted_iota(jnp.int32, (hh, d), 0)
    halo = jnp.zeros((hh, d), F32)
    for e in range(nb):
        m = mod_ref[e, 0]
        hp = _modulated(xp_ref[e, hh - 8:hh], m, g_in, d, 1)[7:8] * pv
        hn = _modulated(xn_ref[e, 0:8], m, g_in, d, 1)[0:1] * nv
        halo = jnp.where(rid == 2 * e, hp, jnp.where(rid == 2 * e + 1, hn, halo))
        h_ref[e * TM:(e + 1) * TM] = _modulated(x_ref[e], m, g_in, d, 1).astype(BF16)
    h_ref[nb * TM:nb * TM + hh] = halo.astype(BF16)
    hcat = h_ref[...]
    ck = FFN_CK
    nslab = ck // LANES
    nchunks = FFN_HIDDEN // ck
    r0 = FFN_CPAD

    def store_u(buf, s, u):
        for e in range(nb):
            u_ref[buf, e, s, r0:r0 + TM, :] = u[e * TM:(e + 1) * TM]
            u_ref[buf, e, s, r0 - 1:r0, :] = u[nb * TM + 2 * e:nb * TM + 2 * e + 1]
            u_ref[buf, e, s, r0 + TM:r0 + TM + 1, :] = u[nb * TM + 2 * e + 1:nb * TM + 2 * e + 2]

    def conv_slab(buf, e, s, col):
        cw = cw_ref[:, col:col + LANES]
        return (cb_ref[:, col:col + LANES]
                + u_ref[buf, e, s, r0 - 1:r0 - 1 + TM, :] * cw[0:1]
                + u_ref[buf, e, s, r0:r0 + TM, :] * cw[1:2]
                + u_ref[buf, e, s, r0 + 1:r0 + 1 + TM, :] * cw[2:3])

    y = jnp.zeros((nb * TM, d), F32)
    for c in range(nchunks + 1):
        if c < nchunks:
            lo = c * ck
            glo = FFN_HIDDEN + lo
            ua = _dot(hcat, wup_ref[:, lo:lo + ck])
            ug = _dot(hcat, wup_ref[:, glo:glo + ck])
            for s in range(nslab):
                store_u(c % 2, s, ua[:, s * LANES:(s + 1) * LANES])
                store_u(c % 2, nslab + s, ug[:, s * LANES:(s + 1) * LANES])
        if c >= 1:
            lo = (c - 1) * ck
            glo = FFN_HIDDEN + lo
            rows = []
            for e in range(nb):
                acts = []
                for s in range(nslab):
                    a = conv_slab((c - 1) % 2, e, s, lo + s * LANES)
                    g = conv_slab((c - 1) % 2, e, nslab + s, glo + s * LANES)
                    acts.append((a * (g * jax.nn.sigmoid(g))).astype(BF16))
                rows.append(jnp.concatenate(acts, axis=-1))
            y = _dot(jnp.concatenate(rows, axis=0), wdn_ref[lo:lo + ck, :]) + y
    for e in range(nb):
        o_ref[e] = x_ref[e] + mod_ref[e, 0][:, 5 * d:6 * d] * _rms(y[e * TM:(e + 1) * TM], ng_out_ref[...])


def _ffn_call(x_all, modsel, ng_in, ng_out, w_up, conv_w, conv_b, w_down, nctx_t, latents_only):
    b, ttot, d = x_all.shape
    nt = ttot // TM
    hh = SUBLANES_BF16
    tile0 = nctx_t if latents_only else 0
    bb = _bb(b)
    prev, nxt = _halo_specs(d, hh, ttot, tile0, bb)
    return pl.pallas_call(
        functools.partial(_ffn_kernel, d=d, nctx_t=nctx_t, nt=nt, tile0=tile0),
        out_shape=jax.ShapeDtypeStruct((b, ttot - tile0 * TM, d), F32),
        grid=(b // bb, nt - tile0),
        in_specs=[pl.BlockSpec((bb, TM, d), lambda g, i: (g, i + tile0, 0)), prev, nxt,
                  _mod_spec(6 * d, nctx_t - tile0, bb), _resident((1, d)),
                  _resident((1, d)), _resident(w_up.shape), _resident(conv_w.shape),
                  _resident(conv_b.shape), _resident(w_down.shape)],
        out_specs=_x_spec(d, bb),
        scratch_shapes=[pltpu.VMEM((bb * TM + hh, d), BF16),
                        pltpu.VMEM((2, bb, 2 * FFN_CK // LANES, TM + 2 * FFN_CPAD, LANES), F32)],
        compiler_params=_cparams(("parallel", "parallel")),
        name="conv_ffn",
    )(x_all, x_all, x_all, modsel, ng_in, ng_out, w_up, conv_w, conv_b, w_down)


MLA_QK_PAD = LANES
MLA_VA = MLA_V + SUBLANES_BF16


def _mla_in_kernel(x_ref, mod_ref, ng_ref, wdq_ref, qg_ref, wuqt_ref, wdkv_ref, kvg_ref,
                   wuk_ref, wuvt_ref, ropeq_ref, ropek_ref, qt_ref, k_ref, vt_ref, *, d):
    nb = x_ref.shape[0]
    hb = _modulated_rows(x_ref, mod_ref, ng_ref[...], d, 0).astype(BF16)
    cq_pre = _dot(hb, wdq_ref[...])
    ckv = _dot(hb, wdkv_ref[...])
    cq = _rms(cq_pre, qg_ref[...]).astype(BF16)
    cn = _rms(ckv[:, 0:MLA_KV_RANK], kvg_ref[...]).astype(BF16)
    qt_all = _dot_nt(wuqt_ref[...], cq)
    k_all = _dot(cn, wuk_ref[...])
    vt_all = _dot_nt(wuvt_ref[...], cn)
    rq = jnp.concatenate([ropeq_ref[...]] * nb, axis=1)
    cos_r, sin_r, cos_c, sin_c = rq[0:8], rq[8:16], rq[16:24], rq[24:32]
    scale = MLA_QK ** -0.5 * LOG2E
    for hd in range(MLA_HEADS):
        base = hd * MLA_QK_PAD
        nope = qt_all[base:base + MLA_NOPE]
        x1r = qt_all[base + 64:base + 72]
        x2r = qt_all[base + 72:base + 80]
        x1c = qt_all[base + 80:base + 88]
        x2c = qt_all[base + 88:base + 96]
        pad = qt_all[base + 96:base + 128]
        roped = jnp.concatenate([
            nope,
            x1r * cos_r - x2r * sin_r, x1r * sin_r + x2r * cos_r,
            x1c * cos_c - x2c * sin_c, x1c * sin_c + x2c * cos_c,
            pad], axis=0)
        roped = (roped * scale).astype(BF16)
        for e in range(nb):
            qt_ref[e, hd] = roped[:, e * TM:(e + 1) * TM]
    rk = jnp.concatenate([ropek_ref[...]] * nb, axis=0)
    kpe = ckv[:, LANES:2 * LANES] * rk[:, 0:LANES] + ckv[:, 2 * LANES:3 * LANES] * rk[:, LANES:2 * LANES]
    ones_rows = jnp.where(lax.broadcasted_iota(jnp.int32, (MLA_VA - MLA_V, nb * TM), 0) == 0, 1.0, 0.0)
    for hd in range(MLA_HEADS):
        kh = (k_all[:, hd * MLA_QK_PAD:(hd + 1) * MLA_QK_PAD] + kpe).astype(BF16)
        vth = jnp.concatenate([vt_all[hd * MLA_V:(hd + 1) * MLA_V], ones_rows], axis=0).astype(BF16)
        for e in range(nb):
            k_ref[e, hd] = kh[e * TM:(e + 1) * TM]
            vt_ref[e, hd] = vth[:, e * TM:(e + 1) * TM]


def _mla_in_call(x_all, modsel, ng, wdq, qg, wuqt, wdkv, kvg, wuk, wuvt, rope_q, rope_k, nctx_t):
    b, ttot, d = x_all.shape
    nt = ttot // TM
    hn = MLA_HEADS
    bb = _bb(b)
    return pl.pallas_call(
        functools.partial(_mla_in_kernel, d=d),
        out_shape=(jax.ShapeDtypeStruct((b, hn, MLA_QK_PAD, ttot), BF16),
                   jax.ShapeDtypeStruct((b, hn, ttot, MLA_QK_PAD), BF16),
                   jax.ShapeDtypeStruct((b, hn, MLA_VA, ttot), BF16)),
        grid=(b // bb, nt),
        in_specs=[_x_spec(d, bb), _mod_spec(6 * d, nctx_t, bb), _resident((1, d)),
                  _resident(wdq.shape), _resident(qg.shape), _resident(wuqt.shape),
                  _resident(wdkv.shape), _resident(kvg.shape), _resident(wuk.shape),
                  _resident(wuvt.shape),
                  pl.BlockSpec((32, TM), lambda g, i: (0, i)),
                  pl.BlockSpec((TM, 2 * LANES), lambda g, i: (i, 0))],
        out_specs=(pl.BlockSpec((bb, hn, MLA_QK_PAD, TM), lambda g, i: (g, 0, 0, i)),
                   pl.BlockSpec((bb, hn, TM, MLA_QK_PAD), lambda g, i: (g, 0, i, 0)),
                   pl.BlockSpec((bb, hn, MLA_VA, TM), lambda g, i: (g, 0, 0, i))),
        compiler_params=_cparams(("parallel", "parallel")),
        name="mla_in_proj",
    )(x_all, modsel, ng, wdq, qg, wuqt, wdkv, kvg, wuk, wuvt, rope_q, rope_k)


MLA_KC = 256
MLA_LOOKAHEAD = 8
MLA_HPS = 8


def _mla_attn_kernel(qt_ref, k_ref, vt_ref, o_ref, *, nkc):
    def chunk(c):
        return slice(c * MLA_KC, (c + 1) * MLA_KC)

    items = [(hh, c) for hh in range(MLA_HPS) for c in range(nkc)]
    scores, ms, accs, outs = [], [], [], []
    for i in range(len(items) + MLA_LOOKAHEAD):
        if i < len(items):
            hh, c = items[i]
            scores.append(_dot(k_ref[0, hh, chunk(c), :], qt_ref[0, hh]))
        if i >= MLA_LOOKAHEAD:
            hh, c = items[i - MLA_LOOKAHEAD]
            s = scores[i - MLA_LOOKAHEAD]
            m_c = s.max(axis=0, keepdims=True)
            ms.append(m_c)
            accs.append(_dot(vt_ref[0, hh, :, chunk(c)], jnp.exp2(s - m_c).astype(BF16)))
            if c == nkc - 1:
                m = functools.reduce(jnp.maximum, ms)
                acc = jnp.zeros((MLA_VA, TM), F32)
                for m_c, a_c in zip(ms, accs):
                    acc = acc + jnp.exp2(m_c - m) * a_c
                outs.append(acc[0:MLA_V] / acc[MLA_V:MLA_V + 1])
                ms, accs = [], []
    o_ref[0] = jnp.concatenate(outs, axis=0).T.astype(BF16)


def _mla_attn_call(qt, k, vt, q_tile0, n_qtiles, kv_len):
    b, hn, _, ttot = qt.shape
    return pl.pallas_call(
        functools.partial(_mla_attn_kernel, nkc=kv_len // MLA_KC),
        out_shape=jax.ShapeDtypeStruct((b, n_qtiles * TM, hn * MLA_V), BF16),
        grid=(b, hn // MLA_HPS, n_qtiles),
        in_specs=[
            pl.BlockSpec((1, MLA_HPS, MLA_QK_PAD, TM), lambda bb, hp, qi: (bb, hp, 0, q_tile0 + qi)),
            pl.BlockSpec((1, MLA_HPS, kv_len, MLA_QK_PAD), lambda bb, hp, qi: (bb, hp, 0, 0)),
            pl.BlockSpec((1, MLA_HPS, MLA_VA, kv_len), lambda bb, hp, qi: (bb, hp, 0, 0)),
        ],
        out_specs=pl.BlockSpec((1, TM, MLA_HPS * MLA_V), lambda bb, hp, qi: (bb, qi, hp)),
        compiler_params=_cparams(("parallel", "parallel", "arbitrary")),
        name="mla_attention",
    )(qt, k, vt)


def _rope_tables(ctx, t):
    pos = np.arange(t)
    half = MLA_ROPE // 4
    inv = ROPE_BASE ** (-jnp.arange(half, dtype=F32) / half)
    tabs = []
    for p in (pos // GRID_W, pos % GRID_W):
        ang = jnp.asarray(p, F32)[:, None] * inv
        cos = jnp.concatenate([jnp.ones((ctx, half), F32), jnp.cos(ang)], axis=0)
        sin = jnp.concatenate([jnp.zeros((ctx, half), F32), jnp.sin(ang)], axis=0)
        tabs.append((cos, sin))
    (cr, sr), (cc, sc) = tabs
    rope_q = jnp.concatenate([cr, sr, cc, sc], axis=1).T
    ttot = ctx + t
    zeros = jnp.zeros((ttot, MLA_NOPE), F32)
    tail = jnp.zeros((ttot, LANES - MLA_QK), F32)
    cos_k = jnp.concatenate([zeros, cr, cr, cc, cc, tail], axis=1)
    sin_k = jnp.concatenate([zeros, -sr, sr, -sc, sc, tail], axis=1)
    rope_k = jnp.concatenate([cos_k, sin_k], axis=1)
    return rope_q, rope_k


def _mla_weights(w_uq, w_dkv, w_ukv):
    hn = MLA_HEADS
    q_rank = w_uq.shape[0]
    wq = w_uq.reshape(q_rank, hn, MLA_QK)
    wq = jnp.pad(wq, ((0, 0), (0, 0), (0, MLA_QK_PAD - MLA_QK)))
    wuqt = wq.reshape(q_rank, hn * MLA_QK_PAD).T.astype(BF16)
    d = w_dkv.shape[0]
    kpe = w_dkv[:, MLA_KV_RANK:]
    h8 = MLA_ROPE // 4
    swap = jnp.concatenate([kpe[:, h8:2 * h8], kpe[:, 0:h8], kpe[:, 3 * h8:4 * h8],
                            kpe[:, 2 * h8:3 * h8]], axis=1)

    def place(cols):
        return jnp.concatenate([jnp.zeros((d, MLA_NOPE), F32), cols,
                                jnp.zeros((d, LANES - MLA_QK), F32)], axis=1)

    wdkv = jnp.concatenate([w_dkv[:, :MLA_KV_RANK], place(kpe), place(swap)], axis=1).astype(BF16)
    wkv = w_ukv.reshape(MLA_KV_RANK, hn, MLA_NOPE + MLA_V)
    wuk = jnp.pad(wkv[:, :, :MLA_NOPE], ((0, 0), (0, 0), (0, MLA_QK_PAD - MLA_NOPE)))
    wuk = wuk.reshape(MLA_KV_RANK, hn * MLA_QK_PAD).astype(BF16)
    wuvt = wkv[:, :, MLA_NOPE:].reshape(MLA_KV_RANK, hn * MLA_V).T.astype(BF16)
    return wuqt, wdkv, wuk, wuvt


def kernel(x, c, ctx, c_ctx, ada_w, ada_b, norm_g, ffn_w_up, ffn_conv_w, ffn_conv_b, ffn_w_down,
           ev_w_in, ev_gate_b, ev_conv_w, ev_conv_b, ev_rpb, ev_ml_norm_g, ev_w_out,
           od_w_dq, od_q_norm_g, od_w_uq, od_w_dkv, od_kv_norm_g, od_w_ukv, od_w_o):
    b, t, d = x.shape
    nctx = ctx.shape[1]
    depth = ada_w.shape[0]
    ttot = nctx + t
    assert t % TM == 0 and nctx % TM == 0 and t % GRID_W == 0
    nctx_t = nctx // TM

    x_all = jnp.concatenate([ctx, x], axis=1)

    rows = -(-(b + 1) // 8) * 8
    cvec = jnp.zeros((rows, d), F32).at[:b].set(c).at[b].set(c_ctx)
    mod = _ada_call(cvec, ada_w, ada_b)

    rope_q, rope_k = _rope_tables(nctx, t)
    ml_scale = jnp.concatenate([jnp.full((1, ML_WIDTH), ML_HD ** -0.5, F32),
                                jnp.ones((1, ML_WIDTH), F32)], axis=1)

    for l in range(depth):
        ctx_out = l < depth - 1
        modsel = jnp.stack([jnp.broadcast_to(mod[l, b], (b, 6 * d)), mod[l, :b]], axis=1)
        modsel = modsel.reshape(b, 2, 1, 6 * d)
        ng = norm_g[l].reshape(4, 1, d)
        if l % 2 == 0:
            e = l // 2
            w_in = ev_w_in[e]
            n_main = 3 * NA_WIDTH + 4 * ML_WIDTH
            w_gate = jnp.pad(w_in[:, n_main:], ((0, 0), (0, LANES - ML_GATES))).astype(BF16)
            gate_b = jnp.pad(ev_gate_b[e], (0, LANES - ML_GATES)).reshape(1, LANES)
            naq, nak, nav, mlqk, mlv, mlo, gates = _even_in_call(
                x_all, modsel, ng[0], w_in[:, :n_main].astype(BF16), w_gate, gate_b, nctx_t)
            na_x = _na_call(naq, nak, nav, _na_bias_table(ev_rpb[e], t // GRID_W), nctx, t)
            na_c = _na_ctx_call(naq, nak, nav, nctx)
            qk, kt = _ml_conv_call(mlqk, ev_conv_w[e], ev_conv_b[e].reshape(1, -1), ml_scale, nctx_t)
            gates_t = jnp.swapaxes(gates[:, :, :ML_GATES], 1, 2)
            hf, hb = _mlstm_call(qk, kt, mlv, gates, gates_t, nctx // ML_CHUNK)
            x_all = _even_out_call(x_all, modsel, ng[1], na_c, na_x, hf, hb, mlo,
                                   ev_ml_norm_g[e].reshape(1, -1), ev_w_out[e].astype(BF16), nctx_t)
        else:
            o = l // 2
            wuqt, wdkv, wuk, wuvt = _mla_weights(od_w_uq[o], od_w_dkv[o], od_w_ukv[o])
            qt, kk, vt = _mla_in_call(
                x_all, modsel, ng[0], od_w_dq[o].astype(BF16), od_q_norm_g[o].reshape(1, -1),
                wuqt, wdkv, od_kv_norm_g[o].reshape(1, -1), wuk, wuvt, rope_q, rope_k, nctx_t)
            att_x = _mla_attn_call(qt, kk, vt, nctx_t, t // TM, ttot)
            att_c = _mla_attn_call(qt, kk, vt, 0, nctx_t, nctx) if ctx_out else att_x
            x_all = _odd_out_call(x_all, modsel, ng[1], att_c, att_x, od_w_o[o].astype(BF16), nctx_t)
        x_all = _ffn_call(x_all, modsel, ng[2], ng[3], ffn_w_up[l].astype(BF16), ffn_conv_w[l],
                          ffn_conv_b[l].reshape(1, -1), ffn_w_down[l].astype(BF16), nctx_t,
                          latents_only=not ctx_out)
    return x_all
```

```python
import functools
import math

import jax
import jax.numpy as jnp
import numpy as np
from jax import lax
from jax.experimental import pallas as pl
from jax.experimental.pallas import tpu as pltpu

F32 = jnp.float32
BF16 = jnp.bfloat16

EPS = 1e-6
NEG = -1e30
LOG2E = math.log2(math.e)

GRID_W = 64
NA_HEADS = 8
NA_HD = 64
NA_WIN_ROWS = 8
NA_WIN_COLS = 16
NA_WIDTH = NA_HEADS * NA_HD

ML_HEADS = 4
ML_HD = 128
ML_WIDTH = ML_HEADS * ML_HD
ML_CHUNK = 128
ML_GATES = 4 * ML_HEADS

MLA_HEADS = 16
MLA_NOPE = 64
MLA_ROPE = 32
MLA_V = 64
MLA_QK = MLA_NOPE + MLA_ROPE
MLA_Q_RANK = 256
MLA_KV_RANK = 128
ROPE_BASE = 10000.0

FFN_HIDDEN = 2816

LANES = 128
SUBLANES_BF16 = 16
TM = 256
VMEM_LIMIT = 48 * 1024 * 1024


def _cparams(sem):
    return pltpu.CompilerParams(dimension_semantics=sem, vmem_limit_bytes=VMEM_LIMIT)


def _resident(shape):
    nd = len(shape)
    return pl.BlockSpec(shape, lambda *_: (0,) * nd, pipeline_mode=pl.Buffered(1))


def _rms(xf, g):
    ms = jnp.mean(xf * xf, axis=-1, keepdims=True)
    return xf * lax.rsqrt(ms + EPS) * g


def _dot(a, b):
    return jnp.dot(a, b, preferred_element_type=F32)


def _dot_nt(a, b):
    return lax.dot_general(a, b, (((1,), (1,)), ((), ())), preferred_element_type=F32)


def _ada_kernel(c_ref, w_ref, b_ref, o_ref):
    c = c_ref[...]
    s = c * jax.nn.sigmoid(c)
    o_ref[0] = _dot(s.astype(BF16), w_ref[0].astype(BF16)) + b_ref[0]


def _ada_call(cvec, ada_w, ada_b):
    depth, d, n = ada_w.shape
    rows = cvec.shape[0]
    tn = 1024
    return pl.pallas_call(
        _ada_kernel,
        out_shape=jax.ShapeDtypeStruct((depth, rows, n), F32),
        grid=(depth, n // tn),
        in_specs=[
            pl.BlockSpec((rows, d), lambda l, j: (0, 0)),
            pl.BlockSpec((1, d, tn), lambda l, j: (l, 0, j)),
            pl.BlockSpec((1, 1, tn), lambda l, j: (l, 0, j)),
        ],
        out_specs=pl.BlockSpec((1, rows, tn), lambda l, j: (l, 0, j)),
        compiler_params=_cparams(("parallel", "parallel")),
        name="ada_mod",
    )(cvec, ada_w, ada_b.reshape(depth, 1, n))


BB = 4
BB_FFN = 2
BB_SCAN = 2


def _bb(b, want=BB):
    while b % want:
        want //= 2
    return want


def _x_spec(d, bb=1):
    return pl.BlockSpec((bb, TM, d), lambda b, i: (b, i, 0))


def _mod_spec(d6, nctx_t, bb=1):
    return pl.BlockSpec((bb, 1, 1, d6), lambda b, i: (b, jnp.where(i >= nctx_t, 1, 0), 0, 0))


def _modulated(x, m, g, d, which):
    o = 3 * d * which
    return _rms(x, g) * (1.0 + m[:, o + d:o + 2 * d]) + m[:, o:o + d]


def _rows(ref):
    return jnp.concatenate([ref[s] for s in range(ref.shape[0])], axis=0)


def _store_rows(ref, val):
    for s in range(ref.shape[0]):
        ref[s] = val[s * TM:(s + 1) * TM].astype(ref.dtype)


def _modulated_rows(x_ref, mod_ref, g, d, which):
    return jnp.concatenate([_modulated(x_ref[s], mod_ref[s, 0], g, d, which)
                            for s in range(x_ref.shape[0])], axis=0)


def _even_in_kernel(x_ref, mod_ref, ng_ref, w_ref, wg_ref, gb_ref,
                    naq_ref, nak_ref, nav_ref, mlqk_ref, mlv_ref, mlo_ref, g_ref, *, d):
    hb = _modulated_rows(x_ref, mod_ref, ng_ref[...], d, 0).astype(BF16)

    def seg(lo, hi):
        return _dot(hb, w_ref[:, lo:hi])

    w = NA_WIDTH
    _store_rows(naq_ref, seg(0, w) * (NA_HD ** -0.5 * LOG2E))
    _store_rows(nak_ref, seg(w, 2 * w))
    _store_rows(nav_ref, seg(2 * w, 3 * w))
    o = 3 * w
    _store_rows(mlqk_ref, jnp.concatenate([seg(o, o + ML_WIDTH).astype(BF16),
                                           seg(o + ML_WIDTH, o + 2 * ML_WIDTH).astype(BF16)], axis=-1))
    _store_rows(mlv_ref, seg(o + 2 * ML_WIDTH, o + 3 * ML_WIDTH))
    _store_rows(mlo_ref, seg(o + 3 * ML_WIDTH, o + 4 * ML_WIDTH))
    _store_rows(g_ref, _dot(hb, wg_ref[...]) + gb_ref[...])


def _even_in_call(x_all, modsel, ng, w_main, w_gate, gate_b, nctx_t):
    b, ttot, d = x_all.shape
    nt = ttot // TM
    bb = _bb(b)
    n_main = w_main.shape[1]
    row = lambda c: _x_spec(c, bb)
    sds = lambda c, dt: jax.ShapeDtypeStruct((b, ttot, c), dt)
    return pl.pallas_call(
        functools.partial(_even_in_kernel, d=d),
        out_shape=(sds(NA_WIDTH, BF16), sds(NA_WIDTH, BF16), sds(NA_WIDTH, BF16),
                   sds(2 * ML_WIDTH, BF16), sds(ML_WIDTH, BF16), sds(ML_WIDTH, BF16),
                   sds(LANES, F32)),
        grid=(b // bb, nt),
        in_specs=[_x_spec(d, bb), _mod_spec(6 * d, nctx_t, bb), _resident((1, d)),
                  _resident((d, n_main)), _resident((d, LANES)), _resident((1, LANES))],
        out_specs=(row(NA_WIDTH), row(NA_WIDTH), row(NA_WIDTH), row(2 * ML_WIDTH),
                   row(ML_WIDTH), row(ML_WIDTH), row(LANES)),
        compiler_params=_cparams(("parallel", "parallel")),
        name="even_in_proj",
    )(x_all, modsel, ng, w_main, w_gate, gate_b)


def _pair_scores(qp, k_parts, bias_parts):
    lane = lax.broadcasted_iota(jnp.int32, qp.shape, 1)
    scores = []
    for hh in range(2):
        keep = (lane >= NA_HD) if hh else (lane < NA_HD)
        qh = jnp.where(keep, qp, jnp.zeros_like(qp))
        s = []
        for kk, bias in zip(k_parts, bias_parts):
            sp = _dot_nt(qh, kk)
            if bias is not None:
                sp = sp + bias[hh]
            s.append(sp)
        scores.append(s)
    return scores


def _pair_finish(scores, v_parts):
    mq = scores[0][0].shape[0]
    lane = lax.broadcasted_iota(jnp.int32, (mq, LANES), 1)
    v_aug = []
    for vv in v_parts:
        ones_col = jnp.where(lax.broadcasted_iota(jnp.int32, vv.shape, 1) == 0, 1.0, 0.0)
        v_aug.append(jnp.concatenate([vv, ones_col.astype(BF16)], axis=-1))
    outs = []
    for s in scores:
        m = functools.reduce(jnp.maximum, [sp.max(axis=-1, keepdims=True) for sp in s])
        o = jnp.zeros((mq, 2 * LANES), F32)
        for sp, vv in zip(s, v_aug):
            o = o + _dot(jnp.exp2(sp - m).astype(BF16), vv)
        outs.append(o[:, 0:LANES] / o[:, LANES:LANES + 1])
    return jnp.where(lane < NA_HD, outs[0], outs[1])


NA_RG = 4
NA_UB = NA_WIN_ROWS + NA_RG - 1


def _na_union_start(r0, rows):
    return jnp.clip(r0 - NA_WIN_ROWS // 2, 0, rows - NA_UB)


def _na_kernel(q_ref, k_ref, v_ref, bias_ref, o_ref, *, ctx, rows):
    us = _na_union_start(pl.program_id(1) * NA_RG, rows)
    start = pl.multiple_of(ctx + us * GRID_W, GRID_W)
    band = NA_UB * GRID_W
    pairs = [slice(p * LANES, (p + 1) * LANES) for p in range(NA_HEADS // 2)]

    def pair_scores(p):
        cs = pairs[p]
        kb = k_ref[0, pl.ds(start, band), cs]
        kc = k_ref[0, 0:ctx, cs]
        bias = (bias_ref[0, 2 * p], bias_ref[0, 2 * p + 1])
        return _pair_scores(q_ref[0, :, cs], (kb, kc), (bias, None))

    scores = [pair_scores(0)]
    for p, cs in enumerate(pairs):
        if p + 1 < len(pairs):
            scores.append(pair_scores(p + 1))
        vb = v_ref[0, pl.ds(start, band), cs]
        vc = v_ref[0, 0:ctx, cs]
        o_ref[0, :, cs] = _pair_finish(scores[p], (vb, vc)).astype(BF16)


def _na_call(naq, nak, nav, bias_tab, ctx, t):
    b, ttot, w = naq.shape
    rows = t // GRID_W
    groups = rows // NA_RG
    assert rows % NA_RG == 0 and groups >= 3 and ctx % (NA_RG * GRID_W) == 0
    q_off = ctx // (NA_RG * GRID_W)
    mq = NA_RG * GRID_W

    def bias_type(bb, g):
        return (jnp.where(g == 0, 0, jnp.where(g == groups - 1, 2, 1)), 0, 0, 0)

    whole = lambda bb, g: (bb, 0, 0)
    return pl.pallas_call(
        functools.partial(_na_kernel, ctx=ctx, rows=rows),
        out_shape=jax.ShapeDtypeStruct((b, t, w), BF16),
        grid=(b, groups),
        in_specs=[
            pl.BlockSpec((1, mq, w), lambda bb, g: (bb, q_off + g, 0)),
            pl.BlockSpec((1, ttot, w), whole, pipeline_mode=pl.Buffered(1)),
            pl.BlockSpec((1, ttot, w), whole, pipeline_mode=pl.Buffered(1)),
            pl.BlockSpec((1, NA_HEADS, mq, NA_UB * GRID_W), bias_type),
        ],
        out_specs=pl.BlockSpec((1, mq, w), lambda bb, g: (bb, g, 0)),
        compiler_params=_cparams(("parallel", "arbitrary")),
        name="na_attention",
    )(naq, nak, nav, bias_tab)


def _na_ctx_kernel(q_ref, k_ref, v_ref, o_ref):
    pairs = [slice(p * LANES, (p + 1) * LANES) for p in range(NA_HEADS // 2)]
    scores = [_pair_scores(q_ref[0, :, cs], (k_ref[0, :, cs],), (None,)) for cs in pairs]
    for sc, cs in zip(scores, pairs):
        o_ref[0, :, cs] = _pair_finish(sc, (v_ref[0, :, cs],)).astype(BF16)


def _na_ctx_call(naq, nak, nav, ctx):
    b, _, w = naq.shape
    spec = pl.BlockSpec((1, ctx, w), lambda bb: (bb, 0, 0))
    return pl.pallas_call(
        _na_ctx_kernel,
        out_shape=jax.ShapeDtypeStruct((b, ctx, w), BF16),
        grid=(b,),
        in_specs=[spec, spec, spec],
        out_specs=spec,
        compiler_params=_cparams(("parallel",)),
        name="na_ctx_attention",
    )(naq, nak, nav)


def _na_group_geometry(g, rows):
    r = g * NA_RG + np.arange(NA_RG)[:, None]
    us = int(np.clip(g * NA_RG - NA_WIN_ROWS // 2, 0, rows - NA_UB))
    kr = us + np.arange(NA_UB)[None, :]
    rs = np.clip(r - NA_WIN_ROWS // 2, 0, rows - NA_WIN_ROWS)
    valid = (kr >= rs) & (kr < rs + NA_WIN_ROWS)
    dr_idx = np.clip(kr - r + NA_WIN_ROWS - 1, 0, 2 * NA_WIN_ROWS - 2)
    return dr_idx, valid


def _na_bias_table(rpb, rows):
    groups = rows // NA_RG
    geo = [_na_group_geometry(g, rows) for g in range(groups)]
    for dr_g, valid_g in geo[1:groups - 1]:
        assert (valid_g == geo[1][1]).all() and (dr_g[valid_g] == geo[1][0][valid_g]).all()
    dr_idx = np.stack([geo[g][0] for g in (0, 1, groups - 1)])
    valid = np.stack([geo[g][1] for g in (0, 1, groups - 1)])
    kw = NA_WIN_COLS
    col = jnp.arange(GRID_W)
    cs = jnp.clip(col - kw // 2, 0, GRID_W - kw)
    in_win = (col[None, :] >= cs[:, None]) & (col[None, :] < cs[:, None] + kw)
    dc_idx = jnp.clip(col[None, :] - col[:, None], -(kw - 1), kw - 1) + (NA_WIN_COLS - 1)
    rpb_cols = rpb.astype(F32)[:, :, dc_idx] * LOG2E
    rpb_cols = jnp.where(in_win[None, None], rpb_cols, NEG)
    tab = rpb_cols[:, dr_idx]
    tab = jnp.where(jnp.asarray(valid)[None, :, :, :, None, None], tab, NEG)
    tab = tab.transpose(1, 0, 2, 4, 3, 5)
    return tab.reshape(3, NA_HEADS, NA_RG * GRID_W, NA_UB * GRID_W)


def _halo_specs(c, rows_h, ttot, tile0=0, bb=1):
    per = TM // rows_h
    last = ttot // rows_h - 1
    prev = pl.BlockSpec((bb, rows_h, c), lambda b, i: (b, jnp.maximum((i + tile0) * per - 1, 0), 0))
    nxt = pl.BlockSpec((bb, rows_h, c), lambda b, i: (b, jnp.minimum((i + tile0 + 1) * per, last), 0))
    return prev, nxt


def _halo_valid(nctx_t, nt, tile0=0):
    i = pl.program_id(1) + tile0
    pv = jnp.where((i == 0) | (i == nctx_t), 0.0, 1.0).astype(F32)
    nv = jnp.where((i == nctx_t - 1) | (i == nt - 1), 0.0, 1.0).astype(F32)
    return pv, nv


def _conv3(u_ref, h, cw, cb, lo, hi):
    return (cb
            + u_ref[h - 1:h - 1 + TM, lo:hi] * cw[0:1]
            + u_ref[h:h + TM, lo:hi] * cw[1:2]
            + u_ref[h + 1:h + 1 + TM, lo:hi] * cw[2:3])


def _ml_conv_kernel(x_ref, xp_ref, xn_ref, cw_ref, cb_ref, sc_ref, o_ref, kt_ref, u_ref, *, nctx_t, nt):
    pv, nv = _halo_valid(nctx_t, nt)
    h = SUBLANES_BF16
    c = x_ref.shape[-1]
    for s in range(x_ref.shape[0]):
        us_ref = u_ref.at[s]
        us_ref[0:h] = xp_ref[s].astype(F32) * pv
        us_ref[h:h + TM] = x_ref[s].astype(F32)
        us_ref[h + TM:h + TM + h] = xn_ref[s].astype(F32) * nv
        y = _conv3(us_ref, h, cw_ref[...], cb_ref[...], 0, c)
        y = y * jax.nn.sigmoid(y) * sc_ref[...]
        o_ref[s] = y.astype(BF16)
        kt_ref[s] = y[:, ML_WIDTH:2 * ML_WIDTH].T.astype(BF16)


def _ml_conv_call(mlqk, conv_w, conv_b, scale, nctx_t):
    b, ttot, c = mlqk.shape
    nt = ttot // TM
    bb = _bb(b)
    prev, nxt = _halo_specs(c, SUBLANES_BF16, ttot, bb=bb)
    return pl.pallas_call(
        functools.partial(_ml_conv_kernel, nctx_t=nctx_t, nt=nt),
        out_shape=(jax.ShapeDtypeStruct((b, ttot, c), BF16),
                   jax.ShapeDtypeStruct((b, ML_WIDTH, ttot), BF16)),
        grid=(b // bb, nt),
        in_specs=[_x_spec(c, bb), prev, nxt, _resident((3, c)), _resident((1, c)), _resident((1, c))],
        out_specs=(_x_spec(c, bb), pl.BlockSpec((bb, ML_WIDTH, TM), lambda g, i: (g, 0, i))),
        scratch_shapes=[pltpu.VMEM((bb, TM + 2 * SUBLANES_BF16, c), F32)],
        compiler_params=_cparams(("parallel", "parallel")),
        name="mlstm_conv",
    )(mlqk, mlqk, mlqk, conv_w, conv_b, scale)


def _log_sigmoid(x):
    return jnp.minimum(x, 0.0) - jnp.log(1.0 + jnp.exp(-jnp.abs(x)))


def _split_dot(a, b, data_is_rhs):
    data = b if data_is_rhs else a
    hi = data.astype(BF16)
    lo = (data - hi.astype(F32)).astype(BF16)
    if data_is_rhs:
        return _dot(a, hi) + _dot(a, lo)
    return _dot(hi, b) + _dot(lo, b)


def _mlstm_kernel(qkf_ref, ktf_ref, vf_ref, gf_ref, gtf_ref, qkb_ref, ktb_ref, vb_ref, gb_ref, gtb_ref,
                  hf_ref, hb_ref, ct_ref, m_ref):
    L = ML_CHUNK

    @pl.when(pl.program_id(1) == 0)
    def _():
        ct_ref[...] = jnp.zeros_like(ct_ref)
        m_ref[...] = jnp.zeros_like(m_ref)

    ri = lax.broadcasted_iota(jnp.int32, (L, L), 0)
    ci = lax.broadcasted_iota(jnp.int32, (L, L), 1)
    tril = jnp.where(ri >= ci, 1.0, 0.0).astype(BF16)
    triu = jnp.where(ri <= ci, 1.0, 0.0).astype(BF16)
    ones_col = jnp.ones((L, LANES), BF16)

    dirs = (
        (qkf_ref, ktf_ref, vf_ref, gf_ref, gtf_ref, hf_ref, tril, triu, ri >= ci, L - 1, 0),
        (qkb_ref, ktb_ref, vb_ref, gb_ref, gtb_ref, hb_ref, triu, tril, ri <= ci, 0, 2 * ML_HEADS),
    )
    chains = []
    for e in range(qkf_ref.shape[0]):
        for dnum, (qk_ref, kt_ref, v_ref, g_ref, gt_ref, h_ref, tcol, trow, mask, tot_row, goff) in enumerate(dirs):
            gt = gt_ref[e]
            cum_col_all = _split_dot(tcol, _log_sigmoid(g_ref[e]), True)
            cum_row_all = _split_dot(_log_sigmoid(gt), trow, False)
            for hd in range(ML_HEADS):
                ic = goff + hd
                fc = goff + ML_HEADS + hd
                cs = slice(hd * ML_HD, (hd + 1) * ML_HD)
                j = (2 * e + dnum) * ML_HEADS + hd
                q = qk_ref[e, :, cs]
                k = qk_ref[e, :, ML_WIDTH + hd * ML_HD:ML_WIDTH + (hd + 1) * ML_HD]
                ct = ct_ref[j]
                cum_col = cum_col_all[:, fc:fc + 1]
                chains.append(dict(
                    j=j, e=e, h_ref=h_ref, cs=cs, kt=kt_ref[e, cs, :], ct=ct, mask=mask,
                    v_aug=jnp.concatenate([v_ref[e, :, cs], ones_col], axis=-1),
                    s=_dot_nt(q, k), hq=_dot(q, ct.astype(BF16)),
                    cum_col=cum_col, cum_row=cum_row_all[fc:fc + 1, :],
                    i_row=gt[ic:ic + 1, :], total=cum_col[tot_row:tot_row + 1, :]))
    for ch in chains:
        m_st = m_ref[ch["j"]][0:1, 0:1]
        cum_col, total = ch["cum_col"], ch["total"]
        e_row = ch["i_row"] - ch["cum_row"]
        base = jnp.where(ch["mask"], e_row, NEG)
        m_inter = cum_col + m_st
        m_t = jnp.maximum(m_inter, cum_col + base.max(axis=-1, keepdims=True))
        g_row = total + e_row
        m_new = jnp.maximum(total + m_st, g_row.max(axis=-1, keepdims=True))
        ch.update(base=base, m_inter=m_inter, m_t=m_t, g_row=g_row, m_new=m_new,
                  decay=jnp.exp(total + m_st - m_new))
    for ch in chains:
        ch["kw"] = (ch["kt"].astype(F32) * jnp.exp(ch["g_row"] - ch["m_new"])).astype(BF16)
    for ch in chains:
        ch["u"] = _dot(ch["kw"], ch["v_aug"])
    for ch in chains:
        ch["qk"] = (ch["s"] * jnp.exp(ch["base"] + (ch["cum_col"] - ch["m_t"]))).astype(BF16)
        ch["a"] = jnp.exp(ch["m_inter"] - ch["m_t"])
        ch["floor"] = jnp.exp(-ch["m_t"])
    for ch in chains:
        ch["p"] = _dot(ch["qk"], ch["v_aug"])
    for ch in chains:
        ct_ref[ch["j"]] = ch["decay"] * ch["ct"] + ch["u"]
        m_ref[ch["j"]] = jnp.broadcast_to(ch["m_new"], m_ref.shape[1:])
    for ch in chains:
        hfull = ch["a"] * ch["hq"] + ch["p"]
        den = hfull[:, ML_HD:2 * ML_HD]
        h_out = hfull[:, 0:ML_HD] / jnp.maximum(jnp.abs(den), ch["floor"])
        ch["h_ref"][ch["e"], :, ch["cs"]] = h_out.astype(BF16)


def _mlstm_call(qk, kt, v, gates, gates_t, nctx_c):
    b, ttot, _ = qk.shape
    nc = ttot // ML_CHUNK
    L = ML_CHUNK

    def fwd(bb, c):
        return (bb, c, 0)

    def bwd(bb, c):
        return (bb, jnp.where(c < nctx_c, nctx_c - 1 - c, nc - 1 + nctx_c - c), 0)

    def fwd_t(bb, c):
        return (bb, 0, c)

    def bwd_t(bb, c):
        return (bb, 0, jnp.where(c < nctx_c, nctx_c - 1 - c, nc - 1 + nctx_c - c))

    bb = _bb(b, BB_SCAN)
    ins = []
    for row_map, col_map in ((fwd, fwd_t), (bwd, bwd_t)):
        ins += [pl.BlockSpec((bb, L, 2 * ML_WIDTH), row_map),
                pl.BlockSpec((bb, ML_WIDTH, L), col_map),
                pl.BlockSpec((bb, L, ML_WIDTH), row_map),
                pl.BlockSpec((bb, L, LANES), row_map),
                pl.BlockSpec((bb, ML_GATES, L), col_map)]
    out_sds = jax.ShapeDtypeStruct((b, ttot, ML_WIDTH), BF16)
    return pl.pallas_call(
        _mlstm_kernel,
        out_shape=(out_sds, out_sds),
        grid=(b // bb, nc),
        in_specs=ins,
        out_specs=(pl.BlockSpec((bb, L, ML_WIDTH), fwd), pl.BlockSpec((bb, L, ML_WIDTH), bwd)),
        scratch_shapes=[pltpu.VMEM((bb * 2 * ML_HEADS, ML_HD, 2 * ML_HD), F32),
                        pltpu.VMEM((bb * 2 * ML_HEADS, 8, LANES), F32)],
        compiler_params=_cparams(("parallel", "arbitrary")),
        name="mlstm_scan",
    )(qk, kt, v, gates, gates_t, qk, kt, v, gates, gates_t)


def _out_tail(x_ref, mod_ref, ng_ref, y, o_ref, d):
    for s in range(x_ref.shape[0]):
        m = mod_ref[s, 0]
        o_ref[s] = x_ref[s] + m[:, 2 * d:3 * d] * _rms(y[s * TM:(s + 1) * TM], ng_ref[...])


def _ctx_lat_specs(c, nctx_t, bb=1):
    ctx_spec = pl.BlockSpec((bb, TM, c), lambda b, i: (b, jnp.minimum(i, nctx_t - 1), 0))
    lat_spec = pl.BlockSpec((bb, TM, c), lambda b, i: (b, jnp.maximum(i - nctx_t, 0), 0))
    return ctx_spec, lat_spec


def _ctx_or_lat(c_ref, l_ref, nctx_t):
    return jnp.where(pl.program_id(1) < nctx_t, _rows(c_ref), _rows(l_ref))


def _even_out_kernel(x_ref, mod_ref, ng_ref, nac_ref, nal_ref, hf_ref, hb_ref, op_ref, mlg_ref, w_ref,
                     o_ref, *, d, nctx_t):
    hs = jax.nn.sigmoid(_rows(op_ref).astype(F32)) * (_rows(hf_ref).astype(F32) + _rows(hb_ref).astype(F32))
    parts = []
    for hd in range(ML_HEADS):
        seg = hs[:, hd * ML_HD:(hd + 1) * ML_HD]
        mu = jnp.mean(seg, axis=-1, keepdims=True)
        cen = seg - mu
        var = jnp.mean(cen * cen, axis=-1, keepdims=True)
        parts.append(cen * lax.rsqrt(var + EPS))
    ml = (jnp.concatenate(parts, axis=-1) * mlg_ref[...]).astype(BF16)
    na = _ctx_or_lat(nac_ref, nal_ref, nctx_t)
    y = _dot(na, w_ref[0:NA_WIDTH, :]) + _dot(ml, w_ref[NA_WIDTH:NA_WIDTH + ML_WIDTH, :])
    _out_tail(x_ref, mod_ref, ng_ref, y, o_ref, d)


def _even_out_call(x_all, modsel, ng, na_c, na_x, hf, hb, mlo, mlg, w_out, nctx_t):
    b, ttot, d = x_all.shape
    bb = _bb(b)
    row = lambda c: _x_spec(c, bb)
    return pl.pallas_call(
        functools.partial(_even_out_kernel, d=d, nctx_t=nctx_t),
        out_shape=jax.ShapeDtypeStruct((b, ttot, d), F32),
        grid=(b // bb, ttot // TM),
        in_specs=[_x_spec(d, bb), _mod_spec(6 * d, nctx_t, bb), _resident((1, d)),
                  *_ctx_lat_specs(NA_WIDTH, nctx_t, bb), row(ML_WIDTH), row(ML_WIDTH), row(ML_WIDTH),
                  _resident((1, ML_WIDTH)), _resident(w_out.shape)],
        out_specs=_x_spec(d, bb),
        compiler_params=_cparams(("parallel", "parallel")),
        name="even_out_proj",
    )(x_all, modsel, ng, na_c, na_x, hf, hb, mlo, mlg, w_out)


def _odd_out_kernel(x_ref, mod_ref, ng_ref, ac_ref, al_ref, w_ref, o_ref, *, d, nctx_t):
    a = _ctx_or_lat(ac_ref, al_ref, nctx_t)
    _out_tail(x_ref, mod_ref, ng_ref, _dot(a, w_ref[...]), o_ref, d)


def _odd_out_call(x_all, modsel, ng, att_c, att_x, w_o, nctx_t):
    b, ttot, d = x_all.shape
    bb = _bb(b)
    return pl.pallas_call(
        functools.partial(_odd_out_kernel, d=d, nctx_t=nctx_t),
        out_shape=jax.ShapeDtypeStruct((b, ttot, d), F32),
        grid=(b // bb, ttot // TM),
        in_specs=[_x_spec(d, bb), _mod_spec(6 * d, nctx_t, bb), _resident((1, d)),
                  *_ctx_lat_specs(att_x.shape[-1], nctx_t, bb), _resident(w_o.shape)],
        out_specs=_x_spec(d, bb),
        compiler_params=_cparams(("parallel", "parallel")),
        name="odd_out_proj",
    )(x_all, modsel, ng, att_c, att_x, w_o)


FFN_CK = 256
FFN_CPAD = 8


def _ffn_kernel(x_ref, xp_ref, xn_ref, mod_ref, ng_in_ref, ng_out_ref, wup_ref, cw_ref, cb_ref,
                wdn_ref, o_ref, h_ref, u_ref, *, d, nctx_t, nt, tile0):
    pv, nv = _halo_valid(nctx_t, nt, tile0)
    g_in = ng_in_ref[...]
    hh = SUBLANES_BF16
    nb = x_ref.shape[0]
    assert 2 * nb <= hh
    rid = lax.broadcasted_iota(jnp.int32, (hh, d), 0)
    halo = jnp.zeros((hh, d), F32)
    for e in range(nb):
        m = mod_ref[e, 0]
        hp = _modulated(xp_ref[e, hh - 8:hh], m, g_in, d, 1)[7:8] * pv
        hn = _modulated(xn_ref[e, 0:8], m, g_in, d, 1)[0:1] * nv
        halo = jnp.where(rid == 2 * e, hp, jnp.where(rid == 2 * e + 1, hn, halo))
        h_ref[e * TM:(e + 1) * TM] = _modulated(x_ref[e], m, g_in, d, 1).astype(BF16)
    h_ref[nb * TM:nb * TM + hh] = halo.astype(BF16)
    hcat = h_ref[...]
    ck = FFN_CK
    nslab = ck // LANES
    nchunks = FFN_HIDDEN // ck
    r0 = FFN_CPAD

    def store_u(buf, s, u):
        for e in range(nb):
            u_ref[buf, e, s, r0:r0 + TM, :] = u[e * TM:(e + 1) * TM]
            u_ref[buf, e, s, r0 - 1:r0, :] = u[nb * TM + 2 * e:nb * TM + 2 * e + 1]
            u_ref[buf, e, s, r0 + TM:r0 + TM + 1, :] = u[nb * TM + 2 * e + 1:nb * TM + 2 * e + 2]

    def conv_slab(buf, e, s, col):
        cw = cw_ref[:, col:col + LANES]
        return (cb_ref[:, col:col + LANES]
                + u_ref[buf, e, s, r0 - 1:r0 - 1 + TM, :] * cw[0:1]
                + u_ref[buf, e, s, r0:r0 + TM, :] * cw[1:2]
                + u_ref[buf, e, s, r0 + 1:r0 + 1 + TM, :] * cw[2:3])

    y = jnp.zeros((nb * TM, d), F32)
    for c in range(nchunks + 1):
        if c < nchunks:
            lo = c * ck
            glo = FFN_HIDDEN + lo
            ua = _dot(hcat, wup_ref[:, lo:lo + ck])
            ug = _dot(hcat, wup_ref[:, glo:glo + ck])
            for s in range(nslab):
                store_u(c % 2, s, ua[:, s * LANES:(s + 1) * LANES])
                store_u(c % 2, nslab + s, ug[:, s * LANES:(s + 1) * LANES])
        if c >= 1:
            lo = (c - 1) * ck
            glo = FFN_HIDDEN + lo
            rows = []
            for e in range(nb):
                acts = []
                for s in range(nslab):
                    a = conv_slab((c - 1) % 2, e, s, lo + s * LANES)
                    g = conv_slab((c - 1) % 2, e, nslab + s, glo + s * LANES)
                    acts.append((a * (g * jax.nn.sigmoid(g))).astype(BF16))
                rows.append(jnp.concatenate(acts, axis=-1))
            y = _dot(jnp.concatenate(rows, axis=0), wdn_ref[lo:lo + ck, :]) + y
    for e in range(nb):
        o_ref[e] = x_ref[e] + mod_ref[e, 0][:, 5 * d:6 * d] * _rms(y[e * TM:(e + 1) * TM], ng_out_ref[...])


def _ffn_call(x_all, modsel, ng_in, ng_out, w_up, conv_w, conv_b, w_down, nctx_t, latents_only):
    b, ttot, d = x_all.shape
    nt = ttot // TM
    hh = SUBLANES_BF16
    tile0 = nctx_t if latents_only else 0
    bb = _bb(b, BB_FFN)
    prev, nxt = _halo_specs(d, hh, ttot, tile0, bb)
    return pl.pallas_call(
        functools.partial(_ffn_kernel, d=d, nctx_t=nctx_t, nt=nt, tile0=tile0),
        out_shape=jax.ShapeDtypeStruct((b, ttot - tile0 * TM, d), F32),
        grid=(b // bb, nt - tile0),
        in_specs=[pl.BlockSpec((bb, TM, d), lambda g, i: (g, i + tile0, 0)), prev, nxt,
                  _mod_spec(6 * d, nctx_t - tile0, bb), _resident((1, d)),
                  _resident((1, d)), _resident(w_up.shape), _resident(conv_w.shape),
                  _resident(conv_b.shape), _resident(w_down.shape)],
        out_specs=_x_spec(d, bb),
        scratch_shapes=[pltpu.VMEM((bb * TM + hh, d), BF16),
                        pltpu.VMEM((2, bb, 2 * FFN_CK // LANES, TM + 2 * FFN_CPAD, LANES), F32)],
        compiler_params=_cparams(("parallel", "parallel")),
        name="conv_ffn",
    )(x_all, x_all, x_all, modsel, ng_in, ng_out, w_up, conv_w, conv_b, w_down)


MLA_QK_PAD = LANES
MLA_VA = MLA_V + SUBLANES_BF16


def _mla_in_kernel(x_ref, mod_ref, ng_ref, wdq_ref, qg_ref, wuqt_ref, wdkv_ref, kvg_ref,
                   wuk_ref, wuvt_ref, ropeq_ref, ropek_ref, qt_ref, k_ref, vt_ref, *, d):
    nb = x_ref.shape[0]
    hb = _modulated_rows(x_ref, mod_ref, ng_ref[...], d, 0).astype(BF16)
    cq_pre = _dot(hb, wdq_ref[...])
    ckv = _dot(hb, wdkv_ref[...])
    cq = _rms(cq_pre, qg_ref[...]).astype(BF16)
    cn = _rms(ckv[:, 0:MLA_KV_RANK], kvg_ref[...]).astype(BF16)
    qt_all = _dot_nt(wuqt_ref[...], cq)
    k_all = _dot(cn, wuk_ref[...])
    vt_all = _dot_nt(wuvt_ref[...], cn)
    rq = jnp.concatenate([ropeq_ref[...]] * nb, axis=1)
    cos_r, sin_r, cos_c, sin_c = rq[0:8], rq[8:16], rq[16:24], rq[24:32]
    scale = MLA_QK ** -0.5 * LOG2E
    for hd in range(MLA_HEADS):
        base = hd * MLA_QK_PAD
        nope = qt_all[base:base + MLA_NOPE]
        x1r = qt_all[base + 64:base + 72]
        x2r = qt_all[base + 72:base + 80]
        x1c = qt_all[base + 80:base + 88]
        x2c = qt_all[base + 88:base + 96]
        pad = qt_all[base + 96:base + 128]
        roped = jnp.concatenate([
            nope,
            x1r * cos_r - x2r * sin_r, x1r * sin_r + x2r * cos_r,
            x1c * cos_c - x2c * sin_c, x1c * sin_c + x2c * cos_c,
            pad], axis=0)
        roped = (roped * scale).astype(BF16)
        for e in range(nb):
            qt_ref[e, hd] = roped[:, e * TM:(e + 1) * TM]
    rk = jnp.concatenate([ropek_ref[...]] * nb, axis=0)
    kpe = ckv[:, LANES:2 * LANES] * rk[:, 0:LANES] + ckv[:, 2 * LANES:3 * LANES] * rk[:, LANES:2 * LANES]
    ones_rows = jnp.where(lax.broadcasted_iota(jnp.int32, (MLA_VA - MLA_V, nb * TM), 0) == 0, 1.0, 0.0)
    for hd in range(MLA_HEADS):
        kh = (k_all[:, hd * MLA_QK_PAD:(hd + 1) * MLA_QK_PAD] + kpe).astype(BF16)
        vth = jnp.concatenate([vt_all[hd * MLA_V:(hd + 1) * MLA_V], ones_rows], axis=0).astype(BF16)
        for e in range(nb):
            k_ref[e, hd] = kh[e * TM:(e + 1) * TM]
            vt_ref[e, hd] = vth[:, e * TM:(e + 1) * TM]


def _mla_in_call(x_all, modsel, ng, wdq, qg, wuqt, wdkv, kvg, wuk, wuvt, rope_q, rope_k, nctx_t):
    b, ttot, d = x_all.shape
    nt = ttot // TM
    hn = MLA_HEADS
    bb = _bb(b)
    return pl.pallas_call(
        functools.partial(_mla_in_kernel, d=d),
        out_shape=(jax.ShapeDtypeStruct((b, hn, MLA_QK_PAD, ttot), BF16),
                   jax.ShapeDtypeStruct((b, hn, ttot, MLA_QK_PAD), BF16),
                   jax.ShapeDtypeStruct((b, hn, MLA_VA, ttot), BF16)),
        grid=(b // bb, nt),
        in_specs=[_x_spec(d, bb), _mod_spec(6 * d, nctx_t, bb), _resident((1, d)),
                  _resident(wdq.shape), _resident(qg.shape), _resident(wuqt.shape),
                  _resident(wdkv.shape), _resident(kvg.shape), _resident(wuk.shape),
                  _resident(wuvt.shape),
                  pl.BlockSpec((32, TM), lambda g, i: (0, i)),
                  pl.BlockSpec((TM, 2 * LANES), lambda g, i: (i, 0))],
        out_specs=(pl.BlockSpec((bb, hn, MLA_QK_PAD, TM), lambda g, i: (g, 0, 0, i)),
                   pl.BlockSpec((bb, hn, TM, MLA_QK_PAD), lambda g, i: (g, 0, i, 0)),
                   pl.BlockSpec((bb, hn, MLA_VA, TM), lambda g, i: (g, 0, 0, i))),
        compiler_params=_cparams(("parallel", "parallel")),
        name="mla_in_proj",
    )(x_all, modsel, ng, wdq, qg, wuqt, wdkv, kvg, wuk, wuvt, rope_q, rope_k)


MLA_KC = 256
MLA_LOOKAHEAD = 8
MLA_HPS = 8


def _mla_attn_kernel(qt_ref, k_ref, vt_ref, o_ref, *, nkc):
    def chunk(c):
        return slice(c * MLA_KC, (c + 1) * MLA_KC)

    items = [(hh, c) for hh in range(MLA_HPS) for c in range(nkc)]
    scores, ms, accs, outs = [], [], [], []
    for i in range(len(items) + MLA_LOOKAHEAD):
        if i < len(items):
            hh, c = items[i]
            scores.append(_dot(k_ref[0, hh, chunk(c), :], qt_ref[0, hh]))
        if i >= MLA_LOOKAHEAD:
            hh, c = items[i - MLA_LOOKAHEAD]
            s = scores[i - MLA_LOOKAHEAD]
            m_c = s.max(axis=0, keepdims=True)
            ms.append(m_c)
            accs.append(_dot(vt_ref[0, hh, :, chunk(c)], jnp.exp2(s - m_c).astype(BF16)))
            if c == nkc - 1:
                m = functools.reduce(jnp.maximum, ms)
                acc = jnp.zeros((MLA_VA, TM), F32)
                for m_c, a_c in zip(ms, accs):
                    acc = acc + jnp.exp2(m_c - m) * a_c
                outs.append(acc[0:MLA_V] / acc[MLA_V:MLA_V + 1])
                ms, accs = [], []
    o_ref[0] = jnp.concatenate(outs, axis=0).T.astype(BF16)


def _mla_attn_call(qt, k, vt, q_tile0, n_qtiles, kv_len):
    b, hn, _, ttot = qt.shape
    return pl.pallas_call(
        functools.partial(_mla_attn_kernel, nkc=kv_len // MLA_KC),
        out_shape=jax.ShapeDtypeStruct((b, n_qtiles * TM, hn * MLA_V), BF16),
        grid=(b, hn // MLA_HPS, n_qtiles),
        in_specs=[
            pl.BlockSpec((1, MLA_HPS, MLA_QK_PAD, TM), lambda bb, hp, qi: (bb, hp, 0, q_tile0 + qi)),
            pl.BlockSpec((1, MLA_HPS, kv_len, MLA_QK_PAD), lambda bb, hp, qi: (bb, hp, 0, 0)),
            pl.BlockSpec((1, MLA_HPS, MLA_VA, kv_len), lambda bb, hp, qi: (bb, hp, 0, 0)),
        ],
        out_specs=pl.BlockSpec((1, TM, MLA_HPS * MLA_V), lambda bb, hp, qi: (bb, qi, hp)),
        compiler_params=_cparams(("parallel", "parallel", "arbitrary")),
        name="mla_attention",
    )(qt, k, vt)


def _rope_tables(ctx, t):
    pos = np.arange(t)
    half = MLA_ROPE // 4
    inv = ROPE_BASE ** (-jnp.arange(half, dtype=F32) / half)
    tabs = []
    for p in (pos // GRID_W, pos % GRID_W):
        ang = jnp.asarray(p, F32)[:, None] * inv
        cos = jnp.concatenate([jnp.ones((ctx, half), F32), jnp.cos(ang)], axis=0)
        sin = jnp.concatenate([jnp.zeros((ctx, half), F32), jnp.sin(ang)], axis=0)
        tabs.append((cos, sin))
    (cr, sr), (cc, sc) = tabs
    rope_q = jnp.concatenate([cr, sr, cc, sc], axis=1).T
    ttot = ctx + t
    zeros = jnp.zeros((ttot, MLA_NOPE), F32)
    tail = jnp.zeros((ttot, LANES - MLA_QK), F32)
    cos_k = jnp.concatenate([zeros, cr, cr, cc, cc, tail], axis=1)
    sin_k = jnp.concatenate([zeros, -sr, sr, -sc, sc, tail], axis=1)
    rope_k = jnp.concatenate([cos_k, sin_k], axis=1)
    return rope_q, rope_k


def _mla_weights(w_uq, w_dkv, w_ukv):
    hn = MLA_HEADS
    q_rank = w_uq.shape[0]
    wq = w_uq.reshape(q_rank, hn, MLA_QK)
    wq = jnp.pad(wq, ((0, 0), (0, 0), (0, MLA_QK_PAD - MLA_QK)))
    wuqt = wq.reshape(q_rank, hn * MLA_QK_PAD).T.astype(BF16)
    d = w_dkv.shape[0]
    kpe = w_dkv[:, MLA_KV_RANK:]
    h8 = MLA_ROPE // 4
    swap = jnp.concatenate([kpe[:, h8:2 * h8], kpe[:, 0:h8], kpe[:, 3 * h8:4 * h8],
                            kpe[:, 2 * h8:3 * h8]], axis=1)

    def place(cols):
        return jnp.concatenate([jnp.zeros((d, MLA_NOPE), F32), cols,
                                jnp.zeros((d, LANES - MLA_QK), F32)], axis=1)

    wdkv = jnp.concatenate([w_dkv[:, :MLA_KV_RANK], place(kpe), place(swap)], axis=1).astype(BF16)
    wkv = w_ukv.reshape(MLA_KV_RANK, hn, MLA_NOPE + MLA_V)
    wuk = jnp.pad(wkv[:, :, :MLA_NOPE], ((0, 0), (0, 0), (0, MLA_QK_PAD - MLA_NOPE)))
    wuk = wuk.reshape(MLA_KV_RANK, hn * MLA_QK_PAD).astype(BF16)
    wuvt = wkv[:, :, MLA_NOPE:].reshape(MLA_KV_RANK, hn * MLA_V).T.astype(BF16)
    return wuqt, wdkv, wuk, wuvt


def kernel(x, c, ctx, c_ctx, ada_w, ada_b, norm_g, ffn_w_up, ffn_conv_w, ffn_conv_b, ffn_w_down,
           ev_w_in, ev_gate_b, ev_conv_w, ev_conv_b, ev_rpb, ev_ml_norm_g, ev_w_out,
           od_w_dq, od_q_norm_g, od_w_uq, od_w_dkv, od_kv_norm_g, od_w_ukv, od_w_o):
    b, t, d = x.shape
    nctx = ctx.shape[1]
    depth = ada_w.shape[0]
    ttot = nctx + t
    assert t % TM == 0 and nctx % TM == 0 and t % GRID_W == 0
    nctx_t = nctx // TM

    x_all = jnp.concatenate([ctx, x], axis=1)

    rows = -(-(b + 1) // 8) * 8
    cvec = jnp.zeros((rows, d), F32).at[:b].set(c).at[b].set(c_ctx)
    mod = _ada_call(cvec, ada_w, ada_b)

    rope_q, rope_k = _rope_tables(nctx, t)
    ml_scale = jnp.concatenate([jnp.full((1, ML_WIDTH), ML_HD ** -0.5, F32),
                                jnp.ones((1, ML_WIDTH), F32)], axis=1)

    for l in range(depth):
        ctx_out = l < depth - 1
        modsel = jnp.stack([jnp.broadcast_to(mod[l, b], (b, 6 * d)), mod[l, :b]], axis=1)
        modsel = modsel.reshape(b, 2, 1, 6 * d)
        ng = norm_g[l].reshape(4, 1, d)
        if l % 2 == 0:
            e = l // 2
            w_in = ev_w_in[e]
            n_main = 3 * NA_WIDTH + 4 * ML_WIDTH
            w_gate = jnp.pad(w_in[:, n_main:], ((0, 0), (0, LANES - ML_GATES))).astype(BF16)
            gate_b = jnp.pad(ev_gate_b[e], (0, LANES - ML_GATES)).reshape(1, LANES)
            naq, nak, nav, mlqk, mlv, mlo, gates = _even_in_call(
                x_all, modsel, ng[0], w_in[:, :n_main].astype(BF16), w_gate, gate_b, nctx_t)
            na_x = _na_call(naq, nak, nav, _na_bias_table(ev_rpb[e], t // GRID_W), nctx, t)
            na_c = _na_ctx_call(naq, nak, nav, nctx)
            qk, kt = _ml_conv_call(mlqk, ev_conv_w[e], ev_conv_b[e].reshape(1, -1), ml_scale, nctx_t)
            gates_t = jnp.swapaxes(gates[:, :, :ML_GATES], 1, 2)
            hf, hb = _mlstm_call(qk, kt, mlv, gates, gates_t, nctx // ML_CHUNK)
            x_all = _even_out_call(x_all, modsel, ng[1], na_c, na_x, hf, hb, mlo,
                                   ev_ml_norm_g[e].reshape(1, -1), ev_w_out[e].astype(BF16), nctx_t)
        else:
            o = l // 2
            wuqt, wdkv, wuk, wuvt = _mla_weights(od_w_uq[o], od_w_dkv[o], od_w_ukv[o])
            qt, kk, vt = _mla_in_call(
                x_all, modsel, ng[0], od_w_dq[o].astype(BF16), od_q_norm_g[o].reshape(1, -1),
                wuqt, wdkv, od_kv_norm_g[o].reshape(1, -1), wuk, wuvt, rope_q, rope_k, nctx_t)
            att_x = _mla_attn_call(qt, kk, vt, nctx_t, t // TM, ttot)
            att_c = _mla_attn_call(qt, kk, vt, 0, nctx_t, nctx) if ctx_out else att_x
            x_all = _odd_out_call(x_all, modsel, ng[1], att_c, att_x, od_w_o[o].astype(BF16), nctx_t)
        x_all = _ffn_call(x_all, modsel, ng[2], ng[3], ffn_w_up[l].astype(BF16), ffn_conv_w[l],
                          ffn_conv_b[l].reshape(1, -1), ffn_w_down[l].astype(BF16), nctx_t,
                          latents_only=not ctx_out)
    return x_all
```

```python
import functools
import math

import jax
import jax.numpy as jnp
import numpy as np
from jax import lax
from jax.experimental import pallas as pl
from jax.experimental.pallas import tpu as pltpu

F32 = jnp.float32
BF16 = jnp.bfloat16

EPS = 1e-6
NEG = -1e30
LOG2E = math.log2(math.e)

GRID_W = 64
NA_HEADS = 8
NA_HD = 64
NA_WIN_ROWS = 8
NA_WIN_COLS = 16
NA_WIDTH = NA_HEADS * NA_HD

ML_HEADS = 4
ML_HD = 128
ML_WIDTH = ML_HEADS * ML_HD
ML_CHUNK = 128
ML_GATES = 4 * ML_HEADS

MLA_HEADS = 16
MLA_NOPE = 64
MLA_ROPE = 32
MLA_V = 64
MLA_QK = MLA_NOPE + MLA_ROPE
MLA_Q_RANK = 256
MLA_KV_RANK = 128
ROPE_BASE = 10000.0

FFN_HIDDEN = 2816

LANES = 128
SUBLANES_BF16 = 16
TM = 256
VMEM_LIMIT = 48 * 1024 * 1024


def _cparams(sem):
    return pltpu.CompilerParams(dimension_semantics=sem, vmem_limit_bytes=VMEM_LIMIT)


def _resident(shape):
    nd = len(shape)
    return pl.BlockSpec(shape, lambda *_: (0,) * nd, pipeline_mode=pl.Buffered(1))


def _rms(xf, g):
    ms = jnp.mean(xf * xf, axis=-1, keepdims=True)
    return xf * lax.rsqrt(ms + EPS) * g


def _dot(a, b):
    return jnp.dot(a, b, preferred_element_type=F32)


def _dot_nt(a, b):
    return lax.dot_general(a, b, (((1,), (1,)), ((), ())), preferred_element_type=F32)


def _ada_kernel(c_ref, w_ref, b_ref, o_ref):
    c = c_ref[...]
    s = c * jax.nn.sigmoid(c)
    o_ref[0] = _dot(s.astype(BF16), w_ref[0].astype(BF16)) + b_ref[0]


def _ada_call(cvec, ada_w, ada_b):
    depth, d, n = ada_w.shape
    rows = cvec.shape[0]
    tn = 1024
    return pl.pallas_call(
        _ada_kernel,
        out_shape=jax.ShapeDtypeStruct((depth, rows, n), F32),
        grid=(depth, n // tn),
        in_specs=[
            pl.BlockSpec((rows, d), lambda l, j: (0, 0)),
            pl.BlockSpec((1, d, tn), lambda l, j: (l, 0, j)),
            pl.BlockSpec((1, 1, tn), lambda l, j: (l, 0, j)),
        ],
        out_specs=pl.BlockSpec((1, rows, tn), lambda l, j: (l, 0, j)),
        compiler_params=_cparams(("parallel", "parallel")),
        name="ada_mod",
    )(cvec, ada_w, ada_b.reshape(depth, 1, n))


BB = 4
BB_FFN = 2
BB_SCAN = 2


def _bb(b, want=BB):
    while b % want:
        want //= 2
    return want


def _x_spec(d, bb=1):
    return pl.BlockSpec((bb, TM, d), lambda b, i: (b, i, 0))


def _mod_spec(d6, nctx_t, bb=1):
    return pl.BlockSpec((bb, 1, 1, d6), lambda b, i: (b, jnp.where(i >= nctx_t, 1, 0), 0, 0))


def _modulated(x, m, g, d, which):
    o = 3 * d * which
    return _rms(x, g) * (1.0 + m[:, o + d:o + 2 * d]) + m[:, o:o + d]


def _rows(ref):
    return jnp.concatenate([ref[s] for s in range(ref.shape[0])], axis=0)


def _store_rows(ref, val):
    for s in range(ref.shape[0]):
        ref[s] = val[s * TM:(s + 1) * TM].astype(ref.dtype)


def _modulated_rows(x_ref, mod_ref, g, d, which):
    return jnp.concatenate([_modulated(x_ref[s], mod_ref[s, 0], g, d, which)
                            for s in range(x_ref.shape[0])], axis=0)


class _CtxLatRows:
    def __init__(self, c_ref, l_ref, nctx_t):
        self.c_ref, self.l_ref, self.nctx_t, self.shape = c_ref, l_ref, nctx_t, c_ref.shape

    def __getitem__(self, s):
        return jnp.where(pl.program_id(1) < self.nctx_t, self.c_ref[s], self.l_ref[s])


def _x_rows(refs, nctx_t):
    if nctx_t is None:
        return refs[0], refs[1:]
    return _CtxLatRows(refs[0], refs[1], nctx_t), refs[2:]


def _x_specs(x_parts, nctx_t, bb):
    d = x_parts[0].shape[-1]
    return [_x_spec(d, bb)] if len(x_parts) == 1 else list(_ctx_lat_specs(d, nctx_t, bb))


def _even_in_kernel(*refs, d, split_at):
    x_ref, refs = _x_rows(refs, split_at)
    (mod_ref, ng_ref, w_ref, wg_ref, gb_ref,
     naq_ref, nak_ref, nav_ref, mlqk_ref, mlv_ref, mlo_ref, g_ref) = refs
    hb = _modulated_rows(x_ref, mod_ref, ng_ref[...], d, 0).astype(BF16)

    def seg(lo, hi):
        return _dot(hb, w_ref[:, lo:hi])

    w = NA_WIDTH
    _store_rows(naq_ref, seg(0, w) * (NA_HD ** -0.5 * LOG2E))
    _store_rows(nak_ref, seg(w, 2 * w))
    _store_rows(nav_ref, seg(2 * w, 3 * w))
    o = 3 * w
    _store_rows(mlqk_ref, jnp.concatenate([seg(o, o + ML_WIDTH).astype(BF16),
                                           seg(o + ML_WIDTH, o + 2 * ML_WIDTH).astype(BF16)], axis=-1))
    _store_rows(mlv_ref, seg(o + 2 * ML_WIDTH, o + 3 * ML_WIDTH))
    _store_rows(mlo_ref, seg(o + 3 * ML_WIDTH, o + 4 * ML_WIDTH))
    _store_rows(g_ref, _dot(hb, wg_ref[...]) + gb_ref[...])


def _even_in_call(x_parts, modsel, ng, w_main, w_gate, gate_b, nctx_t):
    b, _, d = x_parts[0].shape
    ttot = sum(p.shape[1] for p in x_parts)
    nt = ttot // TM
    bb = _bb(b)
    n_main = w_main.shape[1]
    row = lambda c: _x_spec(c, bb)
    sds = lambda c, dt: jax.ShapeDtypeStruct((b, ttot, c), dt)
    return pl.pallas_call(
        functools.partial(_even_in_kernel, d=d, split_at=nctx_t if len(x_parts) == 2 else None),
        out_shape=(sds(NA_WIDTH, BF16), sds(NA_WIDTH, BF16), sds(NA_WIDTH, BF16),
                   sds(2 * ML_WIDTH, BF16), sds(ML_WIDTH, BF16), sds(ML_WIDTH, BF16),
                   sds(LANES, F32)),
        grid=(b // bb, nt),
        in_specs=[*_x_specs(x_parts, nctx_t, bb), _mod_spec(6 * d, nctx_t, bb), _resident((1, d)),
                  _resident((d, n_main)), _resident((d, LANES)), _resident((1, LANES))],
        out_specs=(row(NA_WIDTH), row(NA_WIDTH), row(NA_WIDTH), row(2 * ML_WIDTH),
                   row(ML_WIDTH), row(ML_WIDTH), row(LANES)),
        compiler_params=_cparams(("parallel", "parallel")),
        name="even_in_proj",
    )(*x_parts, modsel, ng, w_main, w_gate, gate_b)


def _pair_scores(qp, k_parts, bias_parts):
    lane = lax.broadcasted_iota(jnp.int32, qp.shape, 1)
    scores = []
    for hh in range(2):
        keep = (lane >= NA_HD) if hh else (lane < NA_HD)
        qh = jnp.where(keep, qp, jnp.zeros_like(qp))
        s = []
        for kk, bias in zip(k_parts, bias_parts):
            sp = _dot_nt(qh, kk)
            if bias is not None:
                sp = sp + bias[hh]
            s.append(sp)
        scores.append(s)
    return scores


def _pair_finish(scores, v_parts):
    mq = scores[0][0].shape[0]
    lane = lax.broadcasted_iota(jnp.int32, (mq, LANES), 1)
    v_aug = []
    for vv in v_parts:
        ones_col = jnp.where(lax.broadcasted_iota(jnp.int32, vv.shape, 1) == 0, 1.0, 0.0)
        v_aug.append(jnp.concatenate([vv, ones_col.astype(BF16)], axis=-1))
    outs = []
    for s in scores:
        m = functools.reduce(jnp.maximum, [sp.max(axis=-1, keepdims=True) for sp in s])
        o = jnp.zeros((mq, 2 * LANES), F32)
        for sp, vv in zip(s, v_aug):
            o = o + _dot(jnp.exp2(sp - m).astype(BF16), vv)
        outs.append(o[:, 0:LANES] / o[:, LANES:LANES + 1])
    return jnp.where(lane < NA_HD, outs[0], outs[1])


NA_RG = 4
NA_UB = NA_WIN_ROWS + NA_RG - 1


def _na_union_start(r0, rows):
    return jnp.clip(r0 - NA_WIN_ROWS // 2, 0, rows - NA_UB)


def _na_kernel(q_ref, k_ref, v_ref, bias_ref, o_ref, *, ctx, rows):
    us = _na_union_start(pl.program_id(1) * NA_RG, rows)
    start = pl.multiple_of(ctx + us * GRID_W, GRID_W)
    band = NA_UB * GRID_W
    pairs = [slice(p * LANES, (p + 1) * LANES) for p in range(NA_HEADS // 2)]

    def pair_scores(p):
        cs = pairs[p]
        kb = k_ref[0, pl.ds(start, band), cs]
        kc = k_ref[0, 0:ctx, cs]
        bias = (bias_ref[0, 2 * p], bias_ref[0, 2 * p + 1])
        return _pair_scores(q_ref[0, :, cs], (kb, kc), (bias, None))

    scores = [pair_scores(0)]
    for p, cs in enumerate(pairs):
        if p + 1 < len(pairs):
            scores.append(pair_scores(p + 1))
        vb = v_ref[0, pl.ds(start, band), cs]
        vc = v_ref[0, 0:ctx, cs]
        o_ref[0, :, cs] = _pair_finish(scores[p], (vb, vc)).astype(BF16)


def _na_call(naq, nak, nav, bias_tab, ctx, t):
    b, ttot, w = naq.shape
    rows = t // GRID_W
    groups = rows // NA_RG
    assert rows % NA_RG == 0 and groups >= 3 and ctx % (NA_RG * GRID_W) == 0
    q_off = ctx // (NA_RG * GRID_W)
    mq = NA_RG * GRID_W

    def bias_type(bb, g):
        return (jnp.where(g == 0, 0, jnp.where(g == groups - 1, 2, 1)), 0, 0, 0)

    whole = lambda bb, g: (bb, 0, 0)
    return pl.pallas_call(
        functools.partial(_na_kernel, ctx=ctx, rows=rows),
        out_shape=jax.ShapeDtypeStruct((b, t, w), BF16),
        grid=(b, groups),
        in_specs=[
            pl.BlockSpec((1, mq, w), lambda bb, g: (bb, q_off + g, 0)),
            pl.BlockSpec((1, ttot, w), whole, pipeline_mode=pl.Buffered(1)),
            pl.BlockSpec((1, ttot, w), whole, pipeline_mode=pl.Buffered(1)),
            pl.BlockSpec((1, NA_HEADS, mq, NA_UB * GRID_W), bias_type),
        ],
        out_specs=pl.BlockSpec((1, mq, w), lambda bb, g: (bb, g, 0)),
        compiler_params=_cparams(("parallel", "arbitrary")),
        name="na_attention",
    )(naq, nak, nav, bias_tab)


def _na_ctx_kernel(q_ref, k_ref, v_ref, o_ref):
    pairs = [slice(p * LANES, (p + 1) * LANES) for p in range(NA_HEADS // 2)]
    scores = [_pair_scores(q_ref[0, :, cs], (k_ref[0, :, cs],), (None,)) for cs in pairs]
    for sc, cs in zip(scores, pairs):
        o_ref[0, :, cs] = _pair_finish(sc, (v_ref[0, :, cs],)).astype(BF16)


def _na_ctx_call(naq, nak, nav, ctx):
    b, _, w = naq.shape
    spec = pl.BlockSpec((1, ctx, w), lambda bb: (bb, 0, 0))
    return pl.pallas_call(
        _na_ctx_kernel,
        out_shape=jax.ShapeDtypeStruct((b, ctx, w), BF16),
        grid=(b,),
        in_specs=[spec, spec, spec],
        out_specs=spec,
        compiler_params=_cparams(("parallel",)),
        name="na_ctx_attention",
    )(naq, nak, nav)


def _na_group_geometry(g, rows):
    r = g * NA_RG + np.arange(NA_RG)[:, None]
    us = int(np.clip(g * NA_RG - NA_WIN_ROWS // 2, 0, rows - NA_UB))
    kr = us + np.arange(NA_UB)[None, :]
    rs = np.clip(r - NA_WIN_ROWS // 2, 0, rows - NA_WIN_ROWS)
    valid = (kr >= rs) & (kr < rs + NA_WIN_ROWS)
    dr_idx = np.clip(kr - r + NA_WIN_ROWS - 1, 0, 2 * NA_WIN_ROWS - 2)
    return dr_idx, valid


def _na_bias_table(rpb, rows):
    groups = rows // NA_RG
    geo = [_na_group_geometry(g, rows) for g in range(groups)]
    for dr_g, valid_g in geo[1:groups - 1]:
        assert (valid_g == geo[1][1]).all() and (dr_g[valid_g] == geo[1][0][valid_g]).all()
    kw = NA_WIN_COLS
    col = jnp.arange(GRID_W)
    cs = jnp.clip(col - kw // 2, 0, GRID_W - kw)
    in_win = (col[None, :] >= cs[:, None]) & (col[None, :] < cs[:, None] + kw)
    dc_idx = jnp.clip(col[None, :] - col[:, None], -(kw - 1), kw - 1) + (NA_WIN_COLS - 1)
    rpb_cols = rpb.astype(F32)[:, :, dc_idx] * LOG2E
    rpb_cols = jnp.where(in_win[None, None], rpb_cols, NEG)
    rpb_t = rpb_cols.transpose(0, 2, 1, 3)

    def masked(n):
        return jnp.full((NA_HEADS, GRID_W, n, GRID_W), NEG, F32)

    types = []
    for g in (0, 1, groups - 1):
        dr_idx, valid = geo[g]
        slabs = []
        for i in range(NA_RG):
            js = np.nonzero(valid[i])[0]
            assert (np.diff(js) == 1).all() and (np.diff(dr_idx[i, js]) == 1).all()
            piece = rpb_t[:, :, dr_idx[i, js[0]]:dr_idx[i, js[-1]] + 1]
            slab = jnp.concatenate([masked(js[0]), piece, masked(NA_UB - 1 - js[-1])], axis=2)
            slabs.append(slab.reshape(NA_HEADS, GRID_W, NA_UB * GRID_W))
        types.append(jnp.stack(slabs, axis=1).reshape(NA_HEADS, NA_RG * GRID_W, NA_UB * GRID_W))
    return jnp.stack(types)


def _halo_specs(c, rows_h, ttot, tile0=0, bb=1):
    per = TM // rows_h
    last = ttot // rows_h - 1
    prev = pl.BlockSpec((bb, rows_h, c), lambda b, i: (b, jnp.maximum((i + tile0) * per - 1, 0), 0))
    nxt = pl.BlockSpec((bb, rows_h, c), lambda b, i: (b, jnp.minimum((i + tile0 + 1) * per, last), 0))
    return prev, nxt


def _halo_valid(nctx_t, nt, tile0=0):
    i = pl.program_id(1) + tile0
    pv = jnp.where((i == 0) | (i == nctx_t), 0.0, 1.0).astype(F32)
    nv = jnp.where((i == nctx_t - 1) | (i == nt - 1), 0.0, 1.0).astype(F32)
    return pv, nv


def _conv3(u_ref, h, cw, cb, lo, hi):
    return (cb
            + u_ref[h - 1:h - 1 + TM, lo:hi] * cw[0:1]
            + u_ref[h:h + TM, lo:hi] * cw[1:2]
            + u_ref[h + 1:h + 1 + TM, lo:hi] * cw[2:3])


def _ml_conv_kernel(x_ref, xp_ref, xn_ref, cw_ref, cb_ref, sc_ref, o_ref, kt_ref, u_ref, *, nctx_t, nt):
    pv, nv = _halo_valid(nctx_t, nt)
    h = SUBLANES_BF16
    c = x_ref.shape[-1]
    for s in range(x_ref.shape[0]):
        us_ref = u_ref.at[s]
        us_ref[0:h] = xp_ref[s].astype(F32) * pv
        us_ref[h:h + TM] = x_ref[s].astype(F32)
        us_ref[h + TM:h + TM + h] = xn_ref[s].astype(F32) * nv
        y = _conv3(us_ref, h, cw_ref[...], cb_ref[...], 0, c)
        y = y * jax.nn.sigmoid(y) * sc_ref[...]
        o_ref[s] = y.astype(BF16)
        kt_ref[s] = y[:, ML_WIDTH:2 * ML_WIDTH].T.astype(BF16)


def _ml_conv_call(mlqk, conv_w, conv_b, scale, nctx_t):
    b, ttot, c = mlqk.shape
    nt = ttot // TM
    bb = _bb(b)
    prev, nxt = _halo_specs(c, SUBLANES_BF16, ttot, bb=bb)
    return pl.pallas_call(
        functools.partial(_ml_conv_kernel, nctx_t=nctx_t, nt=nt),
        out_shape=(jax.ShapeDtypeStruct((b, ttot, c), BF16),
                   jax.ShapeDtypeStruct((b, ML_WIDTH, ttot), BF16)),
        grid=(b // bb, nt),
        in_specs=[_x_spec(c, bb), prev, nxt, _resident((3, c)), _resident((1, c)), _resident((1, c))],
        out_specs=(_x_spec(c, bb), pl.BlockSpec((bb, ML_WIDTH, TM), lambda g, i: (g, 0, i))),
        scratch_shapes=[pltpu.VMEM((bb, TM + 2 * SUBLANES_BF16, c), F32)],
        compiler_params=_cparams(("parallel", "parallel")),
        name="mlstm_conv",
    )(mlqk, mlqk, mlqk, conv_w, conv_b, scale)


def _log_sigmoid(x):
    return jnp.minimum(x, 0.0) - jnp.log(1.0 + jnp.exp(-jnp.abs(x)))


def _split_dot(a, b, data_is_rhs):
    data = b if data_is_rhs else a
    hi = data.astype(BF16)
    lo = (data - hi.astype(F32)).astype(BF16)
    if data_is_rhs:
        return _dot(a, hi) + _dot(a, lo)
    return _dot(hi, b) + _dot(lo, b)


def _mlstm_kernel(qkf_ref, ktf_ref, vf_ref, gf_ref, gtf_ref, qkb_ref, ktb_ref, vb_ref, gb_ref, gtb_ref,
                  hf_ref, hb_ref, ct_ref, m_ref):
    L = ML_CHUNK

    @pl.when(pl.program_id(1) == 0)
    def _():
        ct_ref[...] = jnp.zeros_like(ct_ref)
        m_ref[...] = jnp.zeros_like(m_ref)

    ri = lax.broadcasted_iota(jnp.int32, (L, L), 0)
    ci = lax.broadcasted_iota(jnp.int32, (L, L), 1)
    tril = jnp.where(ri >= ci, 1.0, 0.0).astype(BF16)
    triu = jnp.where(ri <= ci, 1.0, 0.0).astype(BF16)
    ones_col = jnp.ones((L, LANES), BF16)

    dirs = (
        (qkf_ref, ktf_ref, vf_ref, gf_ref, gtf_ref, hf_ref, tril, triu, ri >= ci, L - 1, 0),
        (qkb_ref, ktb_ref, vb_ref, gb_ref, gtb_ref, hb_ref, triu, tril, ri <= ci, 0, 2 * ML_HEADS),
    )
    chains = []
    for e in range(qkf_ref.shape[0]):
        for dnum, (qk_ref, kt_ref, v_ref, g_ref, gt_ref, h_ref, tcol, trow, mask, tot_row, goff) in enumerate(dirs):
            gt = gt_ref[e]
            cum_col_all = _split_dot(tcol, _log_sigmoid(g_ref[e]), True)
            cum_row_all = _split_dot(_log_sigmoid(gt), trow, False)
            for hd in range(ML_HEADS):
                ic = goff + hd
                fc = goff + ML_HEADS + hd
                cs = slice(hd * ML_HD, (hd + 1) * ML_HD)
                j = (2 * e + dnum) * ML_HEADS + hd
                q = qk_ref[e, :, cs]
                k = qk_ref[e, :, ML_WIDTH + hd * ML_HD:ML_WIDTH + (hd + 1) * ML_HD]
                ct = ct_ref[j]
                cum_col = cum_col_all[:, fc:fc + 1]
                chains.append(dict(
                    j=j, e=e, h_ref=h_ref, cs=cs, kt=kt_ref[e, cs, :], ct=ct, mask=mask,
                    v_aug=jnp.concatenate([v_ref[e, :, cs], ones_col], axis=-1),
                    s=_dot_nt(q, k), hq=_dot(q, ct.astype(BF16)),
                    cum_col=cum_col, cum_row=cum_row_all[fc:fc + 1, :],
                    i_row=gt[ic:ic + 1, :], total=cum_col[tot_row:tot_row + 1, :]))
    for ch in chains:
        m_st = m_ref[ch["j"]][0:1, 0:1]
        cum_col, total = ch["cum_col"], ch["total"]
        e_row = ch["i_row"] - ch["cum_row"]
        base = jnp.where(ch["mask"], e_row, NEG)
        m_inter = cum_col + m_st
        m_t = jnp.maximum(m_inter, cum_col + base.max(axis=-1, keepdims=True))
        g_row = total + e_row
        m_new = jnp.maximum(total + m_st, g_row.max(axis=-1, keepdims=True))
        ch.update(base=base, m_inter=m_inter, m_t=m_t, g_row=g_row, m_new=m_new,
                  decay=jnp.exp(total + m_st - m_new))
    for ch in chains:
        ch["kw"] = (ch["kt"].astype(F32) * jnp.exp(ch["g_row"] - ch["m_new"])).astype(BF16)
    for ch in chains:
        ch["u"] = _dot(ch["kw"], ch["v_aug"])
    for ch in chains:
        ch["qk"] = (ch["s"] * jnp.exp(ch["base"] + (ch["cum_col"] - ch["m_t"]))).astype(BF16)
        ch["a"] = jnp.exp(ch["m_inter"] - ch["m_t"])
        ch["floor"] = jnp.exp(-ch["m_t"])
    for ch in chains:
        ch["p"] = _dot(ch["qk"], ch["v_aug"])
    for ch in chains:
        ct_ref[ch["j"]] = ch["decay"] * ch["ct"] + ch["u"]
        m_ref[ch["j"]] = jnp.broadcast_to(ch["m_new"], m_ref.shape[1:])
    for ch in chains:
        hfull = ch["a"] * ch["hq"] + ch["p"]
        den = hfull[:, ML_HD:2 * ML_HD]
        h_out = hfull[:, 0:ML_HD] / jnp.maximum(jnp.abs(den), ch["floor"])
        ch["h_ref"][ch["e"], :, ch["cs"]] = h_out.astype(BF16)


def _mlstm_call(qk, kt, v, gates, gates_t, nctx_c):
    b, ttot, _ = qk.shape
    nc = ttot // ML_CHUNK
    L = ML_CHUNK

    def fwd(bb, c):
        return (bb, c, 0)

    def bwd(bb, c):
        return (bb, jnp.where(c < nctx_c, nctx_c - 1 - c, nc - 1 + nctx_c - c), 0)

    def fwd_t(bb, c):
        return (bb, 0, c)

    def bwd_t(bb, c):
        return (bb, 0, jnp.where(c < nctx_c, nctx_c - 1 - c, nc - 1 + nctx_c - c))

    bb = _bb(b, BB_SCAN)
    ins = []
    for row_map, col_map in ((fwd, fwd_t), (bwd, bwd_t)):
        ins += [pl.BlockSpec((bb, L, 2 * ML_WIDTH), row_map),
                pl.BlockSpec((bb, ML_WIDTH, L), col_map),
                pl.BlockSpec((bb, L, ML_WIDTH), row_map),
                pl.BlockSpec((bb, L, LANES), row_map),
                pl.BlockSpec((bb, ML_GATES, L), col_map)]
    out_sds = jax.ShapeDtypeStruct((b, ttot, ML_WIDTH), BF16)
    return pl.pallas_call(
        _mlstm_kernel,
        out_shape=(out_sds, out_sds),
        grid=(b // bb, nc),
        in_specs=ins,
        out_specs=(pl.BlockSpec((bb, L, ML_WIDTH), fwd), pl.BlockSpec((bb, L, ML_WIDTH), bwd)),
        scratch_shapes=[pltpu.VMEM((bb * 2 * ML_HEADS, ML_HD, 2 * ML_HD), F32),
                        pltpu.VMEM((bb * 2 * ML_HEADS, 8, LANES), F32)],
        compiler_params=_cparams(("parallel", "arbitrary")),
        name="mlstm_scan",
    )(qk, kt, v, gates, gates_t, qk, kt, v, gates, gates_t)


def _out_tail(x_ref, mod_ref, ng_ref, y, o_ref, d):
    for s in range(x_ref.shape[0]):
        m = mod_ref[s, 0]
        o_ref[s] = x_ref[s] + m[:, 2 * d:3 * d] * _rms(y[s * TM:(s + 1) * TM], ng_ref[...])


def _ctx_lat_specs(c, nctx_t, bb=1):
    ctx_spec = pl.BlockSpec((bb, TM, c), lambda b, i: (b, jnp.minimum(i, nctx_t - 1), 0))
    lat_spec = pl.BlockSpec((bb, TM, c), lambda b, i: (b, jnp.maximum(i - nctx_t, 0), 0))
    return ctx_spec, lat_spec


def _ctx_or_lat(c_ref, l_ref, nctx_t):
    return jnp.where(pl.program_id(1) < nctx_t, _rows(c_ref), _rows(l_ref))


def _even_out_kernel(*refs, d, nctx_t, split_at):
    x_ref, refs = _x_rows(refs, split_at)
    mod_ref, ng_ref, nac_ref, nal_ref, hf_ref, hb_ref, op_ref, mlg_ref, w_ref, o_ref = refs
    hs = jax.nn.sigmoid(_rows(op_ref).astype(F32)) * (_rows(hf_ref).astype(F32) + _rows(hb_ref).astype(F32))
    parts = []
    for hd in range(ML_HEADS):
        seg = hs[:, hd * ML_HD:(hd + 1) * ML_HD]
        mu = jnp.mean(seg, axis=-1, keepdims=True)
        cen = seg - mu
        var = jnp.mean(cen * cen, axis=-1, keepdims=True)
        parts.append(cen * lax.rsqrt(var + EPS))
    ml = (jnp.concatenate(parts, axis=-1) * mlg_ref[...]).astype(BF16)
    na = _ctx_or_lat(nac_ref, nal_ref, nctx_t)
    y = _dot(na, w_ref[0:NA_WIDTH, :]) + _dot(ml, w_ref[NA_WIDTH:NA_WIDTH + ML_WIDTH, :])
    _out_tail(x_ref, mod_ref, ng_ref, y, o_ref, d)


def _even_out_call(x_parts, modsel, ng, na_c, na_x, hf, hb, mlo, mlg, w_out, nctx_t):
    b, _, d = x_parts[0].shape
    ttot = sum(p.shape[1] for p in x_parts)
    bb = _bb(b)
    row = lambda c: _x_spec(c, bb)
    return pl.pallas_call(
        functools.partial(_even_out_kernel, d=d, nctx_t=nctx_t,
                          split_at=nctx_t if len(x_parts) == 2 else None),
        out_shape=jax.ShapeDtypeStruct((b, ttot, d), F32),
        grid=(b // bb, ttot // TM),
        in_specs=[*_x_specs(x_parts, nctx_t, bb), _mod_spec(6 * d, nctx_t, bb), _resident((1, d)),
                  *_ctx_lat_specs(NA_WIDTH, nctx_t, bb), row(ML_WIDTH), row(ML_WIDTH), row(ML_WIDTH),
                  _resident((1, ML_WIDTH)), _resident(w_out.shape)],
        out_specs=_x_spec(d, bb),
        compiler_params=_cparams(("parallel", "parallel")),
        name="even_out_proj",
    )(*x_parts, modsel, ng, na_c, na_x, hf, hb, mlo, mlg, w_out)


def _odd_out_kernel(x_ref, mod_ref, ng_ref, ac_ref, al_ref, w_ref, o_ref, *, d, nctx_t):
    a = _ctx_or_lat(ac_ref, al_ref, nctx_t)
    _out_tail(x_ref, mod_ref, ng_ref, _dot(a, w_ref[...]), o_ref, d)


def _odd_out_call(x_all, modsel, ng, att_c, att_x, w_o, nctx_t):
    b, ttot, d = x_all.shape
    bb = _bb(b)
    return pl.pallas_call(
        functools.partial(_odd_out_kernel, d=d, nctx_t=nctx_t),
        out_shape=jax.ShapeDtypeStruct((b, ttot, d), F32),
        grid=(b // bb, ttot // TM),
        in_specs=[_x_spec(d, bb), _mod_spec(6 * d, nctx_t, bb), _resident((1, d)),
                  *_ctx_lat_specs(att_x.shape[-1], nctx_t, bb), _resident(w_o.shape)],
        out_specs=_x_spec(d, bb),
        compiler_params=_cparams(("parallel", "parallel")),
        name="odd_out_proj",
    )(x_all, modsel, ng, att_c, att_x, w_o)


FFN_CK = 256
FFN_CPAD = 8


def _ffn_kernel(x_ref, xp_ref, xn_ref, mod_ref, ng_in_ref, ng_out_ref, wup_ref, cw_ref, cb_ref,
                wdn_ref, o_ref, h_ref, u_ref, *, d, nctx_t, nt, tile0):
    pv, nv = _halo_valid(nctx_t, nt, tile0)
    g_in = ng_in_ref[...]
    hh = SUBLANES_BF16
    nb = x_ref.shape[0]
    assert 2 * nb <= hh
    rid = lax.broadcasted_iota(jnp.int32, (hh, d), 0)
    halo = jnp.zeros((hh, d), F32)
    for e in range(nb):
        m = mod_ref[e, 0]
        hp = _modulated(xp_ref[e, hh - 8:hh], m, g_in, d, 1)[7:8] * pv
        hn = _modulated(xn_ref[e, 0:8], m, g_in, d, 1)[0:1] * nv
        halo = jnp.where(rid == 2 * e, hp, jnp.where(rid == 2 * e + 1, hn, halo))
        h_ref[e * TM:(e + 1) * TM] = _modulated(x_ref[e], m, g_in, d, 1).astype(BF16)
    h_ref[nb * TM:nb * TM + hh] = halo.astype(BF16)
    hcat = h_ref[...]
    ck = FFN_CK
    nslab = ck // LANES
    nchunks = FFN_HIDDEN // ck
    r0 = FFN_CPAD

    def store_u(buf, s, u):
        for e in range(nb):
            u_ref[buf, e, s, r0:r0 + TM, :] = u[e * TM:(e + 1) * TM]
            u_ref[buf, e, s, r0 - 1:r0, :] = u[nb * TM + 2 * e:nb * TM + 2 * e + 1]
            u_ref[buf, e, s, r0 + TM:r0 + TM + 1, :] = u[nb * TM + 2 * e + 1:nb * TM + 2 * e + 2]

    def conv_slab(buf, e, s, col):
        cw = cw_ref[:, col:col + LANES]
        return (cb_ref[:, col:col + LANES]
                + u_ref[buf, e, s, r0 - 1:r0 - 1 + TM, :] * cw[0:1]
                + u_ref[buf, e, s, r0:r0 + TM, :] * cw[1:2]
                + u_ref[buf, e, s, r0 + 1:r0 + 1 + TM, :] * cw[2:3])

    y = jnp.zeros((nb * TM, d), F32)
    for c in range(nchunks + 1):
        if c < nchunks:
            lo = c * ck
            glo = FFN_HIDDEN + lo
            ua = _dot(hcat, wup_ref[:, lo:lo + ck])
            ug = _dot(hcat, wup_ref[:, glo:glo + ck])
            for s in range(nslab):
                store_u(c % 2, s, ua[:, s * LANES:(s + 1) * LANES])
                store_u(c % 2, nslab + s, ug[:, s * LANES:(s + 1) * LANES])
        if c >= 1:
            lo = (c - 1) * ck
            glo = FFN_HIDDEN + lo
            rows = []
            for e in range(nb):
                acts = []
                for s in range(nslab):
                    a = conv_slab((c - 1) % 2, e, s, lo + s * LANES)
                    g = conv_slab((c - 1) % 2, e, nslab + s, glo + s * LANES)
                    acts.append((a * (g * jax.nn.sigmoid(g))).astype(BF16))
                rows.append(jnp.concatenate(acts, axis=-1))
            y = _dot(jnp.concatenate(rows, axis=0), wdn_ref[lo:lo + ck, :]) + y
    for e in range(nb):
        o_ref[e] = x_ref[e] + mod_ref[e, 0][:, 5 * d:6 * d] * _rms(y[e * TM:(e + 1) * TM], ng_out_ref[...])


def _ffn_call(x_all, modsel, ng_in, ng_out, w_up, conv_w, conv_b, w_down, nctx_t, latents_only):
    b, ttot, d = x_all.shape
    nt = ttot // TM
    hh = SUBLANES_BF16
    tile0 = nctx_t if latents_only else 0
    bb = _bb(b, BB_FFN)
    prev, nxt = _halo_specs(d, hh, ttot, tile0, bb)
    return pl.pallas_call(
        functools.partial(_ffn_kernel, d=d, nctx_t=nctx_t, nt=nt, tile0=tile0),
        out_shape=jax.ShapeDtypeStruct((b, ttot - tile0 * TM, d), F32),
        grid=(b // bb, nt - tile0),
        in_specs=[pl.BlockSpec((bb, TM, d), lambda g, i: (g, i + tile0, 0)), prev, nxt,
                  _mod_spec(6 * d, nctx_t - tile0, bb), _resident((1, d)),
                  _resident((1, d)), _resident(w_up.shape), _resident(conv_w.shape),
                  _resident(conv_b.shape), _resident(w_down.shape)],
        out_specs=_x_spec(d, bb),
        scratch_shapes=[pltpu.VMEM((bb * TM + hh, d), BF16),
                        pltpu.VMEM((2, bb, 2 * FFN_CK // LANES, TM + 2 * FFN_CPAD, LANES), F32)],
        compiler_params=_cparams(("parallel", "parallel")),
        name="conv_ffn",
    )(x_all, x_all, x_all, modsel, ng_in, ng_out, w_up, conv_w, conv_b, w_down)


MLA_QK_PAD = LANES
MLA_VA = MLA_V + SUBLANES_BF16


def _mla_in_kernel(x_ref, mod_ref, ng_ref, wdq_ref, qg_ref, wuqt_ref, wdkv_ref, kvg_ref,
                   wuk_ref, wuvt_ref, ropeq_ref, ropek_ref, qt_ref, k_ref, vt_ref, *, d):
    nb = x_ref.shape[0]
    hb = _modulated_rows(x_ref, mod_ref, ng_ref[...], d, 0).astype(BF16)
    cq_pre = _dot(hb, wdq_ref[...])
    ckv = _dot(hb, wdkv_ref[...])
    cq = _rms(cq_pre, qg_ref[...]).astype(BF16)
    cn = _rms(ckv[:, 0:MLA_KV_RANK], kvg_ref[...]).astype(BF16)
    qt_all = _dot_nt(wuqt_ref[...], cq)
    k_all = _dot(cn, wuk_ref[...])
    vt_all = _dot_nt(wuvt_ref[...], cn)
    rq = jnp.concatenate([ropeq_ref[...]] * nb, axis=1)
    cos_r, sin_r, cos_c, sin_c = rq[0:8], rq[8:16], rq[16:24], rq[24:32]
    scale = MLA_QK ** -0.5 * LOG2E
    for hd in range(MLA_HEADS):
        base = hd * MLA_QK_PAD
        nope = qt_all[base:base + MLA_NOPE]
        x1r = qt_all[base + 64:base + 72]
        x2r = qt_all[base + 72:base + 80]
        x1c = qt_all[base + 80:base + 88]
        x2c = qt_all[base + 88:base + 96]
        pad = qt_all[base + 96:base + 128]
        roped = jnp.concatenate([
            nope,
            x1r * cos_r - x2r * sin_r, x1r * sin_r + x2r * cos_r,
            x1c * cos_c - x2c * sin_c, x1c * sin_c + x2c * cos_c,
            pad], axis=0)
        roped = (roped * scale).astype(BF16)
        for e in range(nb):
            qt_ref[e, hd] = roped[:, e * TM:(e + 1) * TM]
    rk = jnp.concatenate([ropek_ref[...]] * nb, axis=0)
    kpe = ckv[:, LANES:2 * LANES] * rk[:, 0:LANES] + ckv[:, 2 * LANES:3 * LANES] * rk[:, LANES:2 * LANES]
    ones_rows = jnp.where(lax.broadcasted_iota(jnp.int32, (MLA_VA - MLA_V, nb * TM), 0) == 0, 1.0, 0.0)
    for hd in range(MLA_HEADS):
        kh = (k_all[:, hd * MLA_QK_PAD:(hd + 1) * MLA_QK_PAD] + kpe).astype(BF16)
        vth = jnp.concatenate([vt_all[hd * MLA_V:(hd + 1) * MLA_V], ones_rows], axis=0).astype(BF16)
        for e in range(nb):
            k_ref[e, hd] = kh[e * TM:(e + 1) * TM]
            vt_ref[e, hd] = vth[:, e * TM:(e + 1) * TM]


def _mla_in_call(x_all, modsel, ng, wdq, qg, wuqt, wdkv, kvg, wuk, wuvt, rope_q, rope_k, nctx_t):
    b, ttot, d = x_all.shape
    nt = ttot // TM
    hn = MLA_HEADS
    bb = _bb(b)
    return pl.pallas_call(
        functools.partial(_mla_in_kernel, d=d),
        out_shape=(jax.ShapeDtypeStruct((b, hn, MLA_QK_PAD, ttot), BF16),
                   jax.ShapeDtypeStruct((b, hn, ttot, MLA_QK_PAD), BF16),
                   jax.ShapeDtypeStruct((b, hn, MLA_VA, ttot), BF16)),
        grid=(b // bb, nt),
        in_specs=[_x_spec(d, bb), _mod_spec(6 * d, nctx_t, bb), _resident((1, d)),
                  _resident(wdq.shape), _resident(qg.shape), _resident(wuqt.shape),
                  _resident(wdkv.shape), _resident(kvg.shape), _resident(wuk.shape),
                  _resident(wuvt.shape),
                  pl.BlockSpec((32, TM), lambda g, i: (0, i)),
                  pl.BlockSpec((TM, 2 * LANES), lambda g, i: (i, 0))],
        out_specs=(pl.BlockSpec((bb, hn, MLA_QK_PAD, TM), lambda g, i: (g, 0, 0, i)),
                   pl.BlockSpec((bb, hn, TM, MLA_QK_PAD), lambda g, i: (g, 0, i, 0)),
                   pl.BlockSpec((bb, hn, MLA_VA, TM), lambda g, i: (g, 0, 0, i))),
        compiler_params=_cparams(("parallel", "parallel")),
        name="mla_in_proj",
    )(x_all, modsel, ng, wdq, qg, wuqt, wdkv, kvg, wuk, wuvt, rope_q, rope_k)


MLA_KC = 256
MLA_LOOKAHEAD = 8
MLA_HPS = 8


def _mla_attn_kernel(qt_ref, k_ref, vt_ref, o_ref, *, kv_len):
    nkc = -(-kv_len // MLA_KC)

    def chunk(c):
        return slice(c * MLA_KC, min((c + 1) * MLA_KC, kv_len))

    items = [(hh, c) for hh in range(MLA_HPS) for c in range(nkc)]
    scores, ms, accs, outs = [], [], [], []
    for i in range(len(items) + MLA_LOOKAHEAD):
        if i < len(items):
            hh, c = items[i]
            scores.append(_dot(k_ref[0, hh, chunk(c), :], qt_ref[0, hh]))
        if i >= MLA_LOOKAHEAD:
            hh, c = items[i - MLA_LOOKAHEAD]
            s = scores[i - MLA_LOOKAHEAD]
            m_c = s.max(axis=0, keepdims=True)
            ms.append(m_c)
            accs.append(_dot(vt_ref[0, hh, :, chunk(c)], jnp.exp2(s - m_c).astype(BF16)))
            if c == nkc - 1:
                m = functools.reduce(jnp.maximum, ms)
                acc = jnp.zeros((MLA_VA, TM), F32)
                for m_c, a_c in zip(ms, accs):
                    acc = acc + jnp.exp2(m_c - m) * a_c
                outs.append(acc[0:MLA_V] / acc[MLA_V:MLA_V + 1])
                ms, accs = [], []
    o_ref[0] = jnp.concatenate(outs, axis=0).T.astype(BF16)


def _mla_attn_call(qt, k, vt, q_tile0, n_qtiles, kv_len):
    b, hn, _, ttot = qt.shape
    return pl.pallas_call(
        functools.partial(_mla_attn_kernel, kv_len=kv_len),
        out_shape=jax.ShapeDtypeStruct((b, n_qtiles * TM, hn * MLA_V), BF16),
        grid=(b, hn // MLA_HPS, n_qtiles),
        in_specs=[
            pl.BlockSpec((1, MLA_HPS, MLA_QK_PAD, TM), lambda bb, hp, qi: (bb, hp, 0, q_tile0 + qi)),
            pl.BlockSpec((1, MLA_HPS, kv_len, MLA_QK_PAD), lambda bb, hp, qi: (bb, hp, 0, 0)),
            pl.BlockSpec((1, MLA_HPS, MLA_VA, kv_len), lambda bb, hp, qi: (bb, hp, 0, 0)),
        ],
        out_specs=pl.BlockSpec((1, TM, MLA_HPS * MLA_V), lambda bb, hp, qi: (bb, qi, hp)),
        compiler_params=_cparams(("parallel", "parallel", "arbitrary")),
        name="mla_attention",
    )(qt, k, vt)


def _rope_tables(ctx, t):
    pos = np.arange(t)
    half = MLA_ROPE // 4
    inv = ROPE_BASE ** (-jnp.arange(half, dtype=F32) / half)
    tabs = []
    for p in (pos // GRID_W, pos % GRID_W):
        ang = jnp.asarray(p, F32)[:, None] * inv
        cos = jnp.concatenate([jnp.ones((ctx, half), F32), jnp.cos(ang)], axis=0)
        sin = jnp.concatenate([jnp.zeros((ctx, half), F32), jnp.sin(ang)], axis=0)
        tabs.append((cos, sin))
    (cr, sr), (cc, sc) = tabs
    rope_q = jnp.concatenate([cr, sr, cc, sc], axis=1).T
    ttot = ctx + t
    zeros = jnp.zeros((ttot, MLA_NOPE), F32)
    tail = jnp.zeros((ttot, LANES - MLA_QK), F32)
    cos_k = jnp.concatenate([zeros, cr, cr, cc, cc, tail], axis=1)
    sin_k = jnp.concatenate([zeros, -sr, sr, -sc, sc, tail], axis=1)
    rope_k = jnp.concatenate([cos_k, sin_k], axis=1)
    return rope_q, rope_k


def _mla_weights(w_uq, w_dkv, w_ukv):
    hn = MLA_HEADS
    q_rank = w_uq.shape[0]
    wq = w_uq.reshape(q_rank, hn, MLA_QK)
    wq = jnp.pad(wq, ((0, 0), (0, 0), (0, MLA_QK_PAD - MLA_QK)))
    wuqt = wq.reshape(q_rank, hn * MLA_QK_PAD).T.astype(BF16)
    d = w_dkv.shape[0]
    kpe = w_dkv[:, MLA_KV_RANK:]
    h8 = MLA_ROPE // 4
    swap = jnp.concatenate([kpe[:, h8:2 * h8], kpe[:, 0:h8], kpe[:, 3 * h8:4 * h8],
                            kpe[:, 2 * h8:3 * h8]], axis=1)

    def place(cols):
        return jnp.concatenate([jnp.zeros((d, MLA_NOPE), F32), cols,
                                jnp.zeros((d, LANES - MLA_QK), F32)], axis=1)

    wdkv = jnp.concatenate([w_dkv[:, :MLA_KV_RANK], place(kpe), place(swap)], axis=1).astype(BF16)
    wkv = w_ukv.reshape(MLA_KV_RANK, hn, MLA_NOPE + MLA_V)
    wuk = jnp.pad(wkv[:, :, :MLA_NOPE], ((0, 0), (0, 0), (0, MLA_QK_PAD - MLA_NOPE)))
    wuk = wuk.reshape(MLA_KV_RANK, hn * MLA_QK_PAD).astype(BF16)
    wuvt = wkv[:, :, MLA_NOPE:].reshape(MLA_KV_RANK, hn * MLA_V).T.astype(BF16)
    return wuqt, wdkv, wuk, wuvt


def kernel(x, c, ctx, c_ctx, ada_w, ada_b, norm_g, ffn_w_up, ffn_conv_w, ffn_conv_b, ffn_w_down,
           ev_w_in, ev_gate_b, ev_conv_w, ev_conv_b, ev_rpb, ev_ml_norm_g, ev_w_out,
           od_w_dq, od_q_norm_g, od_w_uq, od_w_dkv, od_kv_norm_g, od_w_ukv, od_w_o):
    b, t, d = x.shape
    nctx = ctx.shape[1]
    depth = ada_w.shape[0]
    ttot = nctx + t
    assert t % TM == 0 and nctx % TM == 0 and t % GRID_W == 0
    nctx_t = nctx // TM

    x_parts = (ctx, x)

    rows = -(-(b + 1) // 8) * 8
    cvec = jnp.zeros((rows, d), F32).at[:b].set(c).at[b].set(c_ctx)
    mod = _ada_call(cvec, ada_w, ada_b)

    rope_q, rope_k = _rope_tables(nctx, t)
    ml_scale = jnp.concatenate([jnp.full((1, ML_WIDTH), ML_HD ** -0.5, F32),
                                jnp.ones((1, ML_WIDTH), F32)], axis=1)

    for l in range(depth):
        ctx_out = l < depth - 1
        modsel = jnp.stack([jnp.broadcast_to(mod[l, b], (b, 6 * d)), mod[l, :b]], axis=1)
        modsel = modsel.reshape(b, 2, 1, 6 * d)
        ng = norm_g[l].reshape(4, 1, d)
        if l % 2 == 0:
            e = l // 2
            w_in = ev_w_in[e]
            n_main = 3 * NA_WIDTH + 4 * ML_WIDTH
            w_gate = jnp.pad(w_in[:, n_main:], ((0, 0), (0, LANES - ML_GATES))).astype(BF16)
            gate_b = jnp.pad(ev_gate_b[e], (0, LANES - ML_GATES)).reshape(1, LANES)
            naq, nak, nav, mlqk, mlv, mlo, gates = _even_in_call(
                x_parts, modsel, ng[0], w_in[:, :n_main].astype(BF16), w_gate, gate_b, nctx_t)
            na_x = _na_call(naq, nak, nav, _na_bias_table(ev_rpb[e], t // GRID_W), nctx, t)
            na_c = _na_ctx_call(naq, nak, nav, nctx)
            qk, kt = _ml_conv_call(mlqk, ev_conv_w[e], ev_conv_b[e].reshape(1, -1), ml_scale, nctx_t)
            gates_t = jnp.swapaxes(gates[:, :, :ML_GATES], 1, 2)
            hf, hb = _mlstm_call(qk, kt, mlv, gates, gates_t, nctx // ML_CHUNK)
            x_all = _even_out_call(x_parts, modsel, ng[1], na_c, na_x, hf, hb, mlo,
                                   ev_ml_norm_g[e].reshape(1, -1), ev_w_out[e].astype(BF16), nctx_t)
        else:
            (x_all,) = x_parts
            o = l // 2
            wuqt, wdkv, wuk, wuvt = _mla_weights(od_w_uq[o], od_w_dkv[o], od_w_ukv[o])
            qt, kk, vt = _mla_in_call(
                x_all, modsel, ng[0], od_w_dq[o].astype(BF16), od_q_norm_g[o].reshape(1, -1),
                wuqt, wdkv, od_kv_norm_g[o].reshape(1, -1), wuk, wuvt, rope_q, rope_k, nctx_t)
            att_x = _mla_attn_call(qt, kk, vt, nctx_t, t // TM, ttot)
            att_c = _mla_attn_call(qt, kk, vt, 0, nctx_t, nctx) if ctx_out else att_x
            x_all = _odd_out_call(x_all, modsel, ng[1], att_c, att_x, od_w_o[o].astype(BF16), nctx_t)
        x_all = _ffn_call(x_all, modsel, ng[2], ng[3], ffn_w_up[l].astype(BF16), ffn_conv_w[l],
                          ffn_conv_b[l].reshape(1, -1), ffn_w_down[l].astype(BF16), nctx_t,
                          latents_only=not ctx_out)
        x_parts = (x_all,)
    return x_all
```

```python
import functools
import math

import jax
import jax.numpy as jnp
import numpy as np
from jax import lax
from jax.experimental import pallas as pl
from jax.experimental.pallas import tpu as pltpu

F32 = jnp.float32
BF16 = jnp.bfloat16

EPS = 1e-6
NEG = -1e30
LOG2E = math.log2(math.e)

GRID_W = 64
NA_HEADS = 8
NA_HD = 64
NA_WIN_ROWS = 8
NA_WIN_COLS = 16
NA_WIDTH = NA_HEADS * NA_HD

ML_HEADS = 4
ML_HD = 128
ML_WIDTH = ML_HEADS * ML_HD
ML_CHUNK = 128
ML_GATES = 4 * ML_HEADS

MLA_HEADS = 16
MLA_NOPE = 64
MLA_ROPE = 32
MLA_V = 64
MLA_QK = MLA_NOPE + MLA_ROPE
MLA_Q_RANK = 256
MLA_KV_RANK = 128
ROPE_BASE = 10000.0

FFN_HIDDEN = 2816

LANES = 128
SUBLANES_BF16 = 16
TM = 256
VMEM_LIMIT = 48 * 1024 * 1024


def _cparams(sem):
    return pltpu.CompilerParams(dimension_semantics=sem, vmem_limit_bytes=VMEM_LIMIT)


def _resident(shape):
    nd = len(shape)
    return pl.BlockSpec(shape, lambda *_: (0,) * nd, pipeline_mode=pl.Buffered(1))


def _rms(xf, g):
    ms = jnp.mean(xf * xf, axis=-1, keepdims=True)
    return xf * lax.rsqrt(ms + EPS) * g


def _dot(a, b):
    return jnp.dot(a, b, preferred_element_type=F32)


def _dot_nt(a, b):
    return lax.dot_general(a, b, (((1,), (1,)), ((), ())), preferred_element_type=F32)


def _ada_kernel(c_ref, w_ref, b_ref, o_ref):
    c = c_ref[...]
    s = c * jax.nn.sigmoid(c)
    o_ref[0] = _dot(s.astype(BF16), w_ref[0].astype(BF16)) + b_ref[0]


def _ada_call(cvec, ada_w, ada_b):
    depth, d, n = ada_w.shape
    rows = cvec.shape[0]
    tn = 1024
    return pl.pallas_call(
        _ada_kernel,
        out_shape=jax.ShapeDtypeStruct((depth, rows, n), F32),
        grid=(depth, n // tn),
        in_specs=[
            pl.BlockSpec((rows, d), lambda l, j: (0, 0)),
            pl.BlockSpec((1, d, tn), lambda l, j: (l, 0, j)),
            pl.BlockSpec((1, 1, tn), lambda l, j: (l, 0, j)),
        ],
        out_specs=pl.BlockSpec((1, rows, tn), lambda l, j: (l, 0, j)),
        compiler_params=_cparams(("parallel", "parallel")),
        name="ada_mod",
    )(cvec, ada_w, ada_b.reshape(depth, 1, n))


BB = 4
BB_FFN = 2
BB_SCAN = 2


def _bb(b, want=BB):
    while b % want:
        want //= 2
    return want


def _x_spec(d, bb=1):
    return pl.BlockSpec((bb, TM, d), lambda b, i: (b, i, 0))


def _mod_spec(d6, nctx_t, bb=1):
    return pl.BlockSpec((bb, 1, 1, d6), lambda b, i: (b, jnp.where(i >= nctx_t, 1, 0), 0, 0))


def _modulated(x, m, g, d, which):
    o = 3 * d * which
    return _rms(x, g) * (1.0 + m[:, o + d:o + 2 * d]) + m[:, o:o + d]


def _rows(ref):
    return jnp.concatenate([ref[s] for s in range(ref.shape[0])], axis=0)


def _store_rows(ref, val):
    for s in range(ref.shape[0]):
        ref[s] = val[s * TM:(s + 1) * TM].astype(ref.dtype)


def _modulated_rows(x_ref, mod_ref, g, d, which):
    return jnp.concatenate([_modulated(x_ref[s], mod_ref[s, 0], g, d, which)
                            for s in range(x_ref.shape[0])], axis=0)


class _CtxLatRows:
    def __init__(self, c_ref, l_ref, nctx_t):
        self.c_ref, self.l_ref, self.nctx_t, self.shape = c_ref, l_ref, nctx_t, c_ref.shape

    def __getitem__(self, s):
        return jnp.where(pl.program_id(1) < self.nctx_t, self.c_ref[s], self.l_ref[s])


def _x_rows(refs, nctx_t):
    if nctx_t is None:
        return refs[0], refs[1:]
    return _CtxLatRows(refs[0], refs[1], nctx_t), refs[2:]


def _x_specs(x_parts, nctx_t, bb):
    d = x_parts[0].shape[-1]
    return [_x_spec(d, bb)] if len(x_parts) == 1 else list(_ctx_lat_specs(d, nctx_t, bb))


def _even_in_kernel(*refs, d, split_at):
    x_ref, refs = _x_rows(refs, split_at)
    (mod_ref, ng_ref, w_ref, wg_ref, gb_ref,
     naq_ref, nak_ref, nav_ref, mlqk_ref, mlv_ref, mlo_ref, g_ref) = refs
    hb = _modulated_rows(x_ref, mod_ref, ng_ref[...], d, 0).astype(BF16)

    def seg(lo, hi):
        return _dot(hb, w_ref[:, lo:hi])

    w = NA_WIDTH
    _store_rows(naq_ref, seg(0, w) * (NA_HD ** -0.5 * LOG2E))
    _store_rows(nak_ref, seg(w, 2 * w))
    _store_rows(nav_ref, seg(2 * w, 3 * w))
    o = 3 * w
    _store_rows(mlqk_ref, jnp.concatenate([seg(o, o + ML_WIDTH).astype(BF16),
                                           seg(o + ML_WIDTH, o + 2 * ML_WIDTH).astype(BF16)], axis=-1))
    _store_rows(mlv_ref, seg(o + 2 * ML_WIDTH, o + 3 * ML_WIDTH))
    _store_rows(mlo_ref, seg(o + 3 * ML_WIDTH, o + 4 * ML_WIDTH))
    _store_rows(g_ref, _dot(hb, wg_ref[...]) + gb_ref[...])


def _even_in_call(x_parts, modsel, ng, w_main, w_gate, gate_b, nctx_t):
    b, _, d = x_parts[0].shape
    ttot = sum(p.shape[1] for p in x_parts)
    nt = ttot // TM
    bb = _bb(b)
    n_main = w_main.shape[1]
    row = lambda c: _x_spec(c, bb)
    sds = lambda c, dt: jax.ShapeDtypeStruct((b, ttot, c), dt)
    return pl.pallas_call(
        functools.partial(_even_in_kernel, d=d, split_at=nctx_t if len(x_parts) == 2 else None),
        out_shape=(sds(NA_WIDTH, BF16), sds(NA_WIDTH, BF16), sds(NA_WIDTH, BF16),
                   sds(2 * ML_WIDTH, BF16), sds(ML_WIDTH, BF16), sds(ML_WIDTH, BF16),
                   sds(LANES, F32)),
        grid=(b // bb, nt),
        in_specs=[*_x_specs(x_parts, nctx_t, bb), _mod_spec(6 * d, nctx_t, bb), _resident((1, d)),
                  _resident((d, n_main)), _resident((d, LANES)), _resident((1, LANES))],
        out_specs=(row(NA_WIDTH), row(NA_WIDTH), row(NA_WIDTH), row(2 * ML_WIDTH),
                   row(ML_WIDTH), row(ML_WIDTH), row(LANES)),
        compiler_params=_cparams(("parallel", "parallel")),
        name="even_in_proj",
    )(*x_parts, modsel, ng, w_main, w_gate, gate_b)


def _pair_scores(qp, k_parts, bias_parts):
    lane = lax.broadcasted_iota(jnp.int32, qp.shape, 1)
    scores = []
    for hh in range(2):
        keep = (lane >= NA_HD) if hh else (lane < NA_HD)
        qh = jnp.where(keep, qp, jnp.zeros_like(qp))
        s = []
        for kk, bias in zip(k_parts, bias_parts):
            sp = _dot_nt(qh, kk)
            if bias is not None:
                sp = sp + bias[hh]
            s.append(sp)
        scores.append(s)
    return scores


def _pair_finish(scores, v_parts):
    mq = scores[0][0].shape[0]
    lane = lax.broadcasted_iota(jnp.int32, (mq, LANES), 1)
    v_aug = []
    for vv in v_parts:
        ones_col = jnp.where(lax.broadcasted_iota(jnp.int32, vv.shape, 1) == 0, 1.0, 0.0)
        v_aug.append(jnp.concatenate([vv, ones_col.astype(BF16)], axis=-1))
    outs = []
    for s in scores:
        m = functools.reduce(jnp.maximum, [sp.max(axis=-1, keepdims=True) for sp in s])
        o = jnp.zeros((mq, 2 * LANES), F32)
        for sp, vv in zip(s, v_aug):
            o = o + _dot(jnp.exp2(sp - m).astype(BF16), vv)
        outs.append(o[:, 0:LANES] / o[:, LANES:LANES + 1])
    return jnp.where(lane < NA_HD, outs[0], outs[1])


NA_RG = 4
NA_UB = NA_WIN_ROWS + NA_RG - 1


def _na_union_start(r0, rows):
    return jnp.clip(r0 - NA_WIN_ROWS // 2, 0, rows - NA_UB)


def _na_kernel(q_ref, k_ref, v_ref, bias_ref, o_ref, *, ctx, rows):
    us = _na_union_start(pl.program_id(1) * NA_RG, rows)
    start = pl.multiple_of(ctx + us * GRID_W, GRID_W)
    band = NA_UB * GRID_W
    pairs = [slice(p * LANES, (p + 1) * LANES) for p in range(NA_HEADS // 2)]

    def pair_scores(p):
        cs = pairs[p]
        kb = k_ref[0, pl.ds(start, band), cs]
        kc = k_ref[0, 0:ctx, cs]
        bias = (bias_ref[0, 2 * p], bias_ref[0, 2 * p + 1])
        return _pair_scores(q_ref[0, :, cs], (kb, kc), (bias, None))

    scores = [pair_scores(0)]
    for p, cs in enumerate(pairs):
        if p + 1 < len(pairs):
            scores.append(pair_scores(p + 1))
        vb = v_ref[0, pl.ds(start, band), cs]
        vc = v_ref[0, 0:ctx, cs]
        o_ref[0, :, cs] = _pair_finish(scores[p], (vb, vc)).astype(BF16)


def _na_call(naq, nak, nav, bias_tab, ctx, t):
    b, ttot, w = naq.shape
    rows = t // GRID_W
    groups = rows // NA_RG
    assert rows % NA_RG == 0 and groups >= 3 and ctx % (NA_RG * GRID_W) == 0
    q_off = ctx // (NA_RG * GRID_W)
    mq = NA_RG * GRID_W

    def bias_type(bb, g):
        return (jnp.where(g == 0, 0, jnp.where(g == groups - 1, 2, 1)), 0, 0, 0)

    whole = lambda bb, g: (bb, 0, 0)
    return pl.pallas_call(
        functools.partial(_na_kernel, ctx=ctx, rows=rows),
        out_shape=jax.ShapeDtypeStruct((b, t, w), BF16),
        grid=(b, groups),
        in_specs=[
            pl.BlockSpec((1, mq, w), lambda bb, g: (bb, q_off + g, 0)),
            pl.BlockSpec((1, ttot, w), whole, pipeline_mode=pl.Buffered(1)),
            pl.BlockSpec((1, ttot, w), whole, pipeline_mode=pl.Buffered(1)),
            pl.BlockSpec((1, NA_HEADS, mq, NA_UB * GRID_W), bias_type),
        ],
        out_specs=pl.BlockSpec((1, mq, w), lambda bb, g: (bb, g, 0)),
        compiler_params=_cparams(("parallel", "arbitrary")),
        name="na_attention",
    )(naq, nak, nav, bias_tab)


def _na_ctx_kernel(q_ref, k_ref, v_ref, o_ref):
    pairs = [slice(p * LANES, (p + 1) * LANES) for p in range(NA_HEADS // 2)]
    scores = [_pair_scores(q_ref[0, :, cs], (k_ref[0, :, cs],), (None,)) for cs in pairs]
    for sc, cs in zip(scores, pairs):
        o_ref[0, :, cs] = _pair_finish(sc, (v_ref[0, :, cs],)).astype(BF16)


def _na_ctx_call(naq, nak, nav, ctx):
    b, _, w = naq.shape
    spec = pl.BlockSpec((1, ctx, w), lambda bb: (bb, 0, 0))
    return pl.pallas_call(
        _na_ctx_kernel,
        out_shape=jax.ShapeDtypeStruct((b, ctx, w), BF16),
        grid=(b,),
        in_specs=[spec, spec, spec],
        out_specs=spec,
        compiler_params=_cparams(("parallel",)),
        name="na_ctx_attention",
    )(naq, nak, nav)


def _na_group_geometry(g, rows):
    r = g * NA_RG + np.arange(NA_RG)[:, None]
    us = int(np.clip(g * NA_RG - NA_WIN_ROWS // 2, 0, rows - NA_UB))
    kr = us + np.arange(NA_UB)[None, :]
    rs = np.clip(r - NA_WIN_ROWS // 2, 0, rows - NA_WIN_ROWS)
    valid = (kr >= rs) & (kr < rs + NA_WIN_ROWS)
    dr_idx = np.clip(kr - r + NA_WIN_ROWS - 1, 0, 2 * NA_WIN_ROWS - 2)
    return dr_idx, valid


def _na_bias_table(rpb, rows):
    groups = rows // NA_RG
    geo = [_na_group_geometry(g, rows) for g in range(groups)]
    for dr_g, valid_g in geo[1:groups - 1]:
        assert (valid_g == geo[1][1]).all() and (dr_g[valid_g] == geo[1][0][valid_g]).all()
    kw = NA_WIN_COLS
    col = jnp.arange(GRID_W)
    cs = jnp.clip(col - kw // 2, 0, GRID_W - kw)
    in_win = (col[None, :] >= cs[:, None]) & (col[None, :] < cs[:, None] + kw)
    dc_idx = jnp.clip(col[None, :] - col[:, None], -(kw - 1), kw - 1) + (NA_WIN_COLS - 1)
    rpb_cols = rpb.astype(F32)[:, :, dc_idx] * LOG2E
    rpb_cols = jnp.where(in_win[None, None], rpb_cols, NEG)
    rpb_t = rpb_cols.transpose(0, 2, 1, 3)

    def masked(n):
        return jnp.full((NA_HEADS, GRID_W, n, GRID_W), NEG, F32)

    types = []
    for g in (0, 1, groups - 1):
        dr_idx, valid = geo[g]
        slabs = []
        for i in range(NA_RG):
            js = np.nonzero(valid[i])[0]
            assert (np.diff(js) == 1).all() and (np.diff(dr_idx[i, js]) == 1).all()
            piece = rpb_t[:, :, dr_idx[i, js[0]]:dr_idx[i, js[-1]] + 1]
            slab = jnp.concatenate([masked(js[0]), piece, masked(NA_UB - 1 - js[-1])], axis=2)
            slabs.append(slab.reshape(NA_HEADS, GRID_W, NA_UB * GRID_W))
        types.append(jnp.stack(slabs, axis=1).reshape(NA_HEADS, NA_RG * GRID_W, NA_UB * GRID_W))
    return jnp.stack(types)


def _halo_specs(c, rows_h, ttot, tile0=0, bb=1):
    per = TM // rows_h
    last = ttot // rows_h - 1
    prev = pl.BlockSpec((bb, rows_h, c), lambda b, i: (b, jnp.maximum((i + tile0) * per - 1, 0), 0))
    nxt = pl.BlockSpec((bb, rows_h, c), lambda b, i: (b, jnp.minimum((i + tile0 + 1) * per, last), 0))
    return prev, nxt


def _halo_valid(nctx_t, nt, tile0=0):
    i = pl.program_id(1) + tile0
    pv = jnp.where((i == 0) | (i == nctx_t), 0.0, 1.0).astype(F32)
    nv = jnp.where((i == nctx_t - 1) | (i == nt - 1), 0.0, 1.0).astype(F32)
    return pv, nv


def _conv3(u_ref, h, cw, cb, lo, hi):
    return (cb
            + u_ref[h - 1:h - 1 + TM, lo:hi] * cw[0:1]
            + u_ref[h:h + TM, lo:hi] * cw[1:2]
            + u_ref[h + 1:h + 1 + TM, lo:hi] * cw[2:3])


def _ml_conv_kernel(x_ref, xp_ref, xn_ref, cw_ref, cb_ref, sc_ref, o_ref, kt_ref, u_ref, *, nctx_t, nt):
    pv, nv = _halo_valid(nctx_t, nt)
    h = SUBLANES_BF16
    c = x_ref.shape[-1]
    for s in range(x_ref.shape[0]):
        ys = []
        for j in range(c // LANES):
            cs = slice(j * LANES, (j + 1) * LANES)
            us_ref = u_ref.at[s, j]
            us_ref[0:h] = xp_ref[s, :, cs].astype(F32) * pv
            us_ref[h:h + TM] = x_ref[s, :, cs].astype(F32)
            us_ref[h + TM:h + TM + h] = xn_ref[s, :, cs].astype(F32) * nv
            y = _conv3(us_ref, h, cw_ref[:, cs], cb_ref[:, cs], 0, LANES)
            ys.append(y * jax.nn.sigmoid(y) * sc_ref[:, cs])
        y = jnp.concatenate(ys, axis=-1)
        o_ref[s] = y.astype(BF16)
        kt_ref[s] = y[:, ML_WIDTH:2 * ML_WIDTH].T.astype(BF16)


def _ml_conv_call(mlqk, conv_w, conv_b, scale, nctx_t):
    b, ttot, c = mlqk.shape
    nt = ttot // TM
    bb = _bb(b)
    prev, nxt = _halo_specs(c, SUBLANES_BF16, ttot, bb=bb)
    return pl.pallas_call(
        functools.partial(_ml_conv_kernel, nctx_t=nctx_t, nt=nt),
        out_shape=(jax.ShapeDtypeStruct((b, ttot, c), BF16),
                   jax.ShapeDtypeStruct((b, ML_WIDTH, ttot), BF16)),
        grid=(b // bb, nt),
        in_specs=[_x_spec(c, bb), prev, nxt, _resident((3, c)), _resident((1, c)), _resident((1, c))],
        out_specs=(_x_spec(c, bb), pl.BlockSpec((bb, ML_WIDTH, TM), lambda g, i: (g, 0, i))),
        scratch_shapes=[pltpu.VMEM((bb, c // LANES, TM + 2 * SUBLANES_BF16, LANES), F32)],
        compiler_params=_cparams(("parallel", "parallel")),
        name="mlstm_conv",
    )(mlqk, mlqk, mlqk, conv_w, conv_b, scale)


def _log_sigmoid(x):
    return jnp.minimum(x, 0.0) - jnp.log(1.0 + jnp.exp(-jnp.abs(x)))


def _split_dot(a, b, data_is_rhs):
    data = b if data_is_rhs else a
    hi = data.astype(BF16)
    lo = (data - hi.astype(F32)).astype(BF16)
    if data_is_rhs:
        return _dot(a, hi) + _dot(a, lo)
    return _dot(hi, b) + _dot(lo, b)


def _mlstm_kernel(qkf_ref, ktf_ref, vf_ref, gf_ref, gtf_ref, qkb_ref, ktb_ref, vb_ref, gb_ref, gtb_ref,
                  hf_ref, hb_ref, ct_ref, m_ref):
    L = ML_CHUNK

    @pl.when(pl.program_id(1) == 0)
    def _():
        ct_ref[...] = jnp.zeros_like(ct_ref)
        m_ref[...] = jnp.zeros_like(m_ref)

    ri = lax.broadcasted_iota(jnp.int32, (L, L), 0)
    ci = lax.broadcasted_iota(jnp.int32, (L, L), 1)
    tril = jnp.where(ri >= ci, 1.0, 0.0).astype(BF16)
    triu = jnp.where(ri <= ci, 1.0, 0.0).astype(BF16)
    ones_col = jnp.ones((L, LANES), BF16)

    dirs = (
        (qkf_ref, ktf_ref, vf_ref, gf_ref, gtf_ref, hf_ref, tril, triu, ri >= ci, L - 1, 0),
        (qkb_ref, ktb_ref, vb_ref, gb_ref, gtb_ref, hb_ref, triu, tril, ri <= ci, 0, 2 * ML_HEADS),
    )
    chains = []
    for e in range(qkf_ref.shape[0]):
        for dnum, (qk_ref, kt_ref, v_ref, g_ref, gt_ref, h_ref, tcol, trow, mask, tot_row, goff) in enumerate(dirs):
            gt = gt_ref[e]
            cum_col_all = _split_dot(tcol, _log_sigmoid(g_ref[e]), True)
            cum_row_all = _split_dot(_log_sigmoid(gt), trow, False)
            for hd in range(ML_HEADS):
                ic = goff + hd
                fc = goff + ML_HEADS + hd
                cs = slice(hd * ML_HD, (hd + 1) * ML_HD)
                j = (2 * e + dnum) * ML_HEADS + hd
                q = qk_ref[e, :, cs]
                k = qk_ref[e, :, ML_WIDTH + hd * ML_HD:ML_WIDTH + (hd + 1) * ML_HD]
                ct = ct_ref[j]
                cum_col = cum_col_all[:, fc:fc + 1]
                chains.append(dict(
                    j=j, e=e, h_ref=h_ref, cs=cs, kt=kt_ref[e, cs, :], ct=ct, mask=mask,
                    v_aug=jnp.concatenate([v_ref[e, :, cs], ones_col], axis=-1),
                    s=_dot_nt(q, k), hq=_dot(q, ct.astype(BF16)),
                    cum_col=cum_col, cum_row=cum_row_all[fc:fc + 1, :],
                    i_row=gt[ic:ic + 1, :], total=cum_col[tot_row:tot_row + 1, :]))
    for ch in chains:
        m_st = m_ref[ch["j"]][0:1, 0:1]
        cum_col, total = ch["cum_col"], ch["total"]
        e_row = ch["i_row"] - ch["cum_row"]
        base = jnp.where(ch["mask"], e_row, NEG)
        m_inter = cum_col + m_st
        m_t = jnp.maximum(m_inter, cum_col + base.max(axis=-1, keepdims=True))
        g_row = total + e_row
        m_new = jnp.maximum(total + m_st, g_row.max(axis=-1, keepdims=True))
        ch.update(base=base, m_inter=m_inter, m_t=m_t, g_row=g_row, m_new=m_new,
                  decay=jnp.exp(total + m_st - m_new))
    for ch in chains:
        ch["kw"] = (ch["kt"].astype(F32) * jnp.exp(ch["g_row"] - ch["m_new"])).astype(BF16)
    for ch in chains:
        ch["u"] = _dot(ch["kw"], ch["v_aug"])
    for ch in chains:
        ch["qk"] = (ch["s"] * jnp.exp(ch["base"] + (ch["cum_col"] - ch["m_t"]))).astype(BF16)
        ch["a"] = jnp.exp(ch["m_inter"] - ch["m_t"])
        ch["floor"] = jnp.exp(-ch["m_t"])
    for ch in chains:
        ch["p"] = _dot(ch["qk"], ch["v_aug"])
    for ch in chains:
        ct_ref[ch["j"]] = ch["decay"] * ch["ct"] + ch["u"]
        m_ref[ch["j"]] = jnp.broadcast_to(ch["m_new"], m_ref.shape[1:])
    for ch in chains:
        hfull = ch["a"] * ch["hq"] + ch["p"]
        den = hfull[:, ML_HD:2 * ML_HD]
        h_out = hfull[:, 0:ML_HD] / jnp.maximum(jnp.abs(den), ch["floor"])
        ch["h_ref"][ch["e"], :, ch["cs"]] = h_out.astype(BF16)


def _mlstm_call(qk, kt, v, gates, gates_t, nctx_c):
    b, ttot, _ = qk.shape
    nc = ttot // ML_CHUNK
    L = ML_CHUNK

    def fwd(bb, c):
        return (bb, c, 0)

    def bwd(bb, c):
        return (bb, jnp.where(c < nctx_c, nctx_c - 1 - c, nc - 1 + nctx_c - c), 0)

    def fwd_t(bb, c):
        return (bb, 0, c)

    def bwd_t(bb, c):
        return (bb, 0, jnp.where(c < nctx_c, nctx_c - 1 - c, nc - 1 + nctx_c - c))

    bb = _bb(b, BB_SCAN)
    ins = []
    for row_map, col_map in ((fwd, fwd_t), (bwd, bwd_t)):
        ins += [pl.BlockSpec((bb, L, 2 * ML_WIDTH), row_map),
                pl.BlockSpec((bb, ML_WIDTH, L), col_map),
                pl.BlockSpec((bb, L, ML_WIDTH), row_map),
                pl.BlockSpec((bb, L, LANES), row_map),
                pl.BlockSpec((bb, ML_GATES, L), col_map)]
    out_sds = jax.ShapeDtypeStruct((b, ttot, ML_WIDTH), BF16)
    return pl.pallas_call(
        _mlstm_kernel,
        out_shape=(out_sds, out_sds),
        grid=(b // bb, nc),
        in_specs=ins,
        out_specs=(pl.BlockSpec((bb, L, ML_WIDTH), fwd), pl.BlockSpec((bb, L, ML_WIDTH), bwd)),
        scratch_shapes=[pltpu.VMEM((bb * 2 * ML_HEADS, ML_HD, 2 * ML_HD), F32),
                        pltpu.VMEM((bb * 2 * ML_HEADS, 8, LANES), F32)],
        compiler_params=_cparams(("parallel", "arbitrary")),
        name="mlstm_scan",
    )(qk, kt, v, gates, gates_t, qk, kt, v, gates, gates_t)


def _out_tail(x_ref, mod_ref, ng_ref, y, o_ref, d):
    for s in range(x_ref.shape[0]):
        m = mod_ref[s, 0]
        o_ref[s] = x_ref[s] + m[:, 2 * d:3 * d] * _rms(y[s * TM:(s + 1) * TM], ng_ref[...])


def _ctx_lat_specs(c, nctx_t, bb=1):
    ctx_spec = pl.BlockSpec((bb, TM, c), lambda b, i: (b, jnp.minimum(i, nctx_t - 1), 0))
    lat_spec = pl.BlockSpec((bb, TM, c), lambda b, i: (b, jnp.maximum(i - nctx_t, 0), 0))
    return ctx_spec, lat_spec


def _ctx_or_lat(c_ref, l_ref, nctx_t):
    return jnp.where(pl.program_id(1) < nctx_t, _rows(c_ref), _rows(l_ref))


def _even_out_kernel(*refs, d, nctx_t, split_at):
    x_ref, refs = _x_rows(refs, split_at)
    mod_ref, ng_ref, nac_ref, nal_ref, hf_ref, hb_ref, op_ref, mlg_ref, w_ref, o_ref = refs
    hs = jax.nn.sigmoid(_rows(op_ref).astype(F32)) * (_rows(hf_ref).astype(F32) + _rows(hb_ref).astype(F32))
    parts = []
    for hd in range(ML_HEADS):
        seg = hs[:, hd * ML_HD:(hd + 1) * ML_HD]
        mu = jnp.mean(seg, axis=-1, keepdims=True)
        cen = seg - mu
        var = jnp.mean(cen * cen, axis=-1, keepdims=True)
        parts.append(cen * lax.rsqrt(var + EPS))
    ml = (jnp.concatenate(parts, axis=-1) * mlg_ref[...]).astype(BF16)
    na = _ctx_or_lat(nac_ref, nal_ref, nctx_t)
    y = _dot(na, w_ref[0:NA_WIDTH, :]) + _dot(ml, w_ref[NA_WIDTH:NA_WIDTH + ML_WIDTH, :])
    _out_tail(x_ref, mod_ref, ng_ref, y, o_ref, d)


def _even_out_call(x_parts, modsel, ng, na_c, na_x, hf, hb, mlo, mlg, w_out, nctx_t):
    b, _, d = x_parts[0].shape
    ttot = sum(p.shape[1] for p in x_parts)
    bb = _bb(b)
    row = lambda c: _x_spec(c, bb)
    return pl.pallas_call(
        functools.partial(_even_out_kernel, d=d, nctx_t=nctx_t,
                          split_at=nctx_t if len(x_parts) == 2 else None),
        out_shape=jax.ShapeDtypeStruct((b, ttot, d), F32),
        grid=(b // bb, ttot // TM),
        in_specs=[*_x_specs(x_parts, nctx_t, bb), _mod_spec(6 * d, nctx_t, bb), _resident((1, d)),
                  *_ctx_lat_specs(NA_WIDTH, nctx_t, bb), row(ML_WIDTH), row(ML_WIDTH), row(ML_WIDTH),
                  _resident((1, ML_WIDTH)), _resident(w_out.shape)],
        out_specs=_x_spec(d, bb),
        compiler_params=_cparams(("parallel", "parallel")),
        name="even_out_proj",
    )(*x_parts, modsel, ng, na_c, na_x, hf, hb, mlo, mlg, w_out)


def _odd_out_kernel(x_ref, mod_ref, ng_ref, ac_ref, al_ref, w_ref, o_ref, *, d, nctx_t):
    a = _ctx_or_lat(ac_ref, al_ref, nctx_t)
    _out_tail(x_ref, mod_ref, ng_ref, _dot(a, w_ref[...]), o_ref, d)


def _odd_out_call(x_all, modsel, ng, att_c, att_x, w_o, nctx_t):
    b, ttot, d = x_all.shape
    bb = _bb(b)
    return pl.pallas_call(
        functools.partial(_odd_out_kernel, d=d, nctx_t=nctx_t),
        out_shape=jax.ShapeDtypeStruct((b, ttot, d), F32),
        grid=(b // bb, ttot // TM),
        in_specs=[_x_spec(d, bb), _mod_spec(6 * d, nctx_t, bb), _resident((1, d)),
                  *_ctx_lat_specs(att_x.shape[-1], nctx_t, bb), _resident(w_o.shape)],
        out_specs=_x_spec(d, bb),
        compiler_params=_cparams(("parallel", "parallel")),
        name="odd_out_proj",
    )(x_all, modsel, ng, att_c, att_x, w_o)


FFN_CK = 256
FFN_CPAD = 8


def _ffn_kernel(x_ref, xp_ref, xn_ref, mod_ref, ng_in_ref, ng_out_ref, wup_ref, cw_ref, cb_ref,
                wdn_ref, o_ref, h_ref, u_ref, *, d, nctx_t, nt, tile0):
    pv, nv = _halo_valid(nctx_t, nt, tile0)
    g_in = ng_in_ref[...]
    hh = SUBLANES_BF16
    nb = x_ref.shape[0]
    assert 2 * nb <= hh
    rid = lax.broadcasted_iota(jnp.int32, (hh, d), 0)
    halo = jnp.zeros((hh, d), F32)
    for e in range(nb):
        m = mod_ref[e, 0]
        hp = _modulated(xp_ref[e, hh - 8:hh], m, g_in, d, 1)[7:8] * pv
        hn = _modulated(xn_ref[e, 0:8], m, g_in, d, 1)[0:1] * nv
        halo = jnp.where(rid == 2 * e, hp, jnp.where(rid == 2 * e + 1, hn, halo))
        h_ref[e * TM:(e + 1) * TM] = _modulated(x_ref[e], m, g_in, d, 1).astype(BF16)
    h_ref[nb * TM:nb * TM + hh] = halo.astype(BF16)
    hcat = h_ref[...]
    ck = FFN_CK
    nslab = ck // LANES
    nchunks = FFN_HIDDEN // ck
    r0 = FFN_CPAD

    def store_u(buf, s, u):
        for e in range(nb):
            u_ref[buf, e, s, r0:r0 + TM, :] = u[e * TM:(e + 1) * TM]
            u_ref[buf, e, s, r0 - 1:r0, :] = u[nb * TM + 2 * e:nb * TM + 2 * e + 1]
            u_ref[buf, e, s, r0 + TM:r0 + TM + 1, :] = u[nb * TM + 2 * e + 1:nb * TM + 2 * e + 2]

    def conv_slab(buf, e, s, col):
        cw = cw_ref[:, col:col + LANES]
        return (cb_ref[:, col:col + LANES]
                + u_ref[buf, e, s, r0 - 1:r0 - 1 + TM, :] * cw[0:1]
                + u_ref[buf, e, s, r0:r0 + TM, :] * cw[1:2]
                + u_ref[buf, e, s, r0 + 1:r0 + 1 + TM, :] * cw[2:3])

    y = jnp.zeros((nb * TM, d), F32)
    for c in range(nchunks + 1):
        if c < nchunks:
            lo = c * ck
            glo = FFN_HIDDEN + lo
            ua = _dot(hcat, wup_ref[:, lo:lo + ck])
            ug = _dot(hcat, wup_ref[:, glo:glo + ck])
            for s in range(nslab):
                store_u(c % 2, s, ua[:, s * LANES:(s + 1) * LANES])
                store_u(c % 2, nslab + s, ug[:, s * LANES:(s + 1) * LANES])
        if c >= 1:
            lo = (c - 1) * ck
            glo = FFN_HIDDEN + lo
            rows = []
            for e in range(nb):
                acts = []
                for s in range(nslab):
                    a = conv_slab((c - 1) % 2, e, s, lo + s * LANES)
                    g = conv_slab((c - 1) % 2, e, nslab + s, glo + s * LANES)
                    acts.append((a * (g * jax.nn.sigmoid(g))).astype(BF16))
                rows.append(jnp.concatenate(acts, axis=-1))
            y = _dot(jnp.concatenate(rows, axis=0), wdn_ref[lo:lo + ck, :]) + y
    for e in range(nb):
        o_ref[e] = x_ref[e] + mod_ref[e, 0][:, 5 * d:6 * d] * _rms(y[e * TM:(e + 1) * TM], ng_out_ref[...])


def _ffn_call(x_all, modsel, ng_in, ng_out, w_up, conv_w, conv_b, w_down, nctx_t, latents_only):
    b, ttot, d = x_all.shape
    nt = ttot // TM
    hh = SUBLANES_BF16
    tile0 = nctx_t if latents_only else 0
    bb = _bb(b, BB_FFN)
    prev, nxt = _halo_specs(d, hh, ttot, tile0, bb)
    return pl.pallas_call(
        functools.partial(_ffn_kernel, d=d, nctx_t=nctx_t, nt=nt, tile0=tile0),
        out_shape=jax.ShapeDtypeStruct((b, ttot - tile0 * TM, d), F32),
        grid=(b // bb, nt - tile0),
        in_specs=[pl.BlockSpec((bb, TM, d), lambda g, i: (g, i + tile0, 0)), prev, nxt,
                  _mod_spec(6 * d, nctx_t - tile0, bb), _resident((1, d)),
                  _resident((1, d)), _resident(w_up.shape), _resident(conv_w.shape),
                  _resident(conv_b.shape), _resident(w_down.shape)],
        out_specs=_x_spec(d, bb),
        scratch_shapes=[pltpu.VMEM((bb * TM + hh, d), BF16),
                        pltpu.VMEM((2, bb, 2 * FFN_CK // LANES, TM + 2 * FFN_CPAD, LANES), F32)],
        compiler_params=_cparams(("parallel", "parallel")),
        name="conv_ffn",
    )(x_all, x_all, x_all, modsel, ng_in, ng_out, w_up, conv_w, conv_b, w_down)


MLA_QK_PAD = LANES
MLA_VA = MLA_V + SUBLANES_BF16


def _mla_in_kernel(x_ref, mod_ref, ng_ref, wdq_ref, qg_ref, wuqt_ref, wdkv_ref, kvg_ref,
                   wuk_ref, wuvt_ref, ropeq_ref, ropek_ref, qt_ref, k_ref, vt_ref, *, d):
    nb = x_ref.shape[0]
    hb = _modulated_rows(x_ref, mod_ref, ng_ref[...], d, 0).astype(BF16)
    cq_pre = _dot(hb, wdq_ref[...])
    ckv = _dot(hb, wdkv_ref[...])
    cq = _rms(cq_pre, qg_ref[...]).astype(BF16)
    cn = _rms(ckv[:, 0:MLA_KV_RANK], kvg_ref[...]).astype(BF16)
    qt_all = _dot_nt(wuqt_ref[...], cq)
    k_all = _dot(cn, wuk_ref[...])
    vt_all = _dot_nt(wuvt_ref[...], cn)
    rq = jnp.concatenate([ropeq_ref[...]] * nb, axis=1)
    cos_r, sin_r, cos_c, sin_c = rq[0:8], rq[8:16], rq[16:24], rq[24:32]
    scale = MLA_QK ** -0.5 * LOG2E
    for hd in range(MLA_HEADS):
        base = hd * MLA_QK_PAD
        nope = qt_all[base:base + MLA_NOPE]
        x1r = qt_all[base + 64:base + 72]
        x2r = qt_all[base + 72:base + 80]
        x1c = qt_all[base + 80:base + 88]
        x2c = qt_all[base + 88:base + 96]
        pad = qt_all[base + 96:base + 128]
        roped = jnp.concatenate([
            nope,
            x1r * cos_r - x2r * sin_r, x1r * sin_r + x2r * cos_r,
            x1c * cos_c - x2c * sin_c, x1c * sin_c + x2c * cos_c,
            pad], axis=0)
        roped = (roped * scale).astype(BF16)
        for e in range(nb):
            qt_ref[e, hd] = roped[:, e * TM:(e + 1) * TM]
    rk = jnp.concatenate([ropek_ref[...]] * nb, axis=0)
    kpe = ckv[:, LANES:2 * LANES] * rk[:, 0:LANES] + ckv[:, 2 * LANES:3 * LANES] * rk[:, LANES:2 * LANES]
    ones_rows = jnp.where(lax.broadcasted_iota(jnp.int32, (MLA_VA - MLA_V, nb * TM), 0) == 0, 1.0, 0.0)
    for hd in range(MLA_HEADS):
        kh = (k_all[:, hd * MLA_QK_PAD:(hd + 1) * MLA_QK_PAD] + kpe).astype(BF16)
        vth = jnp.concatenate([vt_all[hd * MLA_V:(hd + 1) * MLA_V], ones_rows], axis=0).astype(BF16)
        for e in range(nb):
            k_ref[e, hd] = kh[e * TM:(e + 1) * TM]
            vt_ref[e, hd] = vth[:, e * TM:(e + 1) * TM]


def _mla_in_call(x_all, modsel, ng, wdq, qg, wuqt, wdkv, kvg, wuk, wuvt, rope_q, rope_k, nctx_t):
    b, ttot, d = x_all.shape
    nt = ttot // TM
    hn = MLA_HEADS
    bb = _bb(b)
    return pl.pallas_call(
        functools.partial(_mla_in_kernel, d=d),
        out_shape=(jax.ShapeDtypeStruct((b, hn, MLA_QK_PAD, ttot), BF16),
                   jax.ShapeDtypeStruct((b, hn, ttot, MLA_QK_PAD), BF16),
                   jax.ShapeDtypeStruct((b, hn, MLA_VA, ttot), BF16)),
        grid=(b // bb, nt),
        in_specs=[_x_spec(d, bb), _mod_spec(6 * d, nctx_t, bb), _resident((1, d)),
                  _resident(wdq.shape), _resident(qg.shape), _resident(wuqt.shape),
                  _resident(wdkv.shape), _resident(kvg.shape), _resident(wuk.shape),
                  _resident(wuvt.shape),
                  pl.BlockSpec((32, TM), lambda g, i: (0, i)),
                  pl.BlockSpec((TM, 2 * LANES), lambda g, i: (i, 0))],
        out_specs=(pl.BlockSpec((bb, hn, MLA_QK_PAD, TM), lambda g, i: (g, 0, 0, i)),
                   pl.BlockSpec((bb, hn, TM, MLA_QK_PAD), lambda g, i: (g, 0, i, 0)),
                   pl.BlockSpec((bb, hn, MLA_VA, TM), lambda g, i: (g, 0, 0, i))),
        compiler_params=_cparams(("parallel", "parallel")),
        name="mla_in_proj",
    )(x_all, modsel, ng, wdq, qg, wuqt, wdkv, kvg, wuk, wuvt, rope_q, rope_k)


MLA_KC = 256
MLA_LOOKAHEAD = 8
MLA_HPS = 8


def _mla_attn_kernel(qt_ref, k_ref, vt_ref, o_ref, *, kv_len):
    nkc = -(-kv_len // MLA_KC)

    def chunk(c):
        return slice(c * MLA_KC, min((c + 1) * MLA_KC, kv_len))

    items = [(hh, c) for hh in range(MLA_HPS) for c in range(nkc)]
    scores, ms, accs, outs = [], [], [], []
    for i in range(len(items) + MLA_LOOKAHEAD):
        if i < len(items):
            hh, c = items[i]
            scores.append(_dot(k_ref[0, hh, chunk(c), :], qt_ref[0, hh]))
        if i >= MLA_LOOKAHEAD:
            hh, c = items[i - MLA_LOOKAHEAD]
            s = scores[i - MLA_LOOKAHEAD]
            m_c = s.max(axis=0, keepdims=True)
            ms.append(m_c)
            accs.append(_dot(vt_ref[0, hh, :, chunk(c)], jnp.exp2(s - m_c).astype(BF16)))
            if c == nkc - 1:
                m = functools.reduce(jnp.maximum, ms)
                acc = jnp.zeros((MLA_VA, TM), F32)
                for m_c, a_c in zip(ms, accs):
                    acc = acc + jnp.exp2(m_c - m) * a_c
                outs.append(acc[0:MLA_V] / acc[MLA_V:MLA_V + 1])
                ms, accs = [], []
    o_ref[0] = jnp.concatenate(outs, axis=0).T.astype(BF16)


def _mla_attn_call(qt, k, vt, q_tile0, n_qtiles, kv_len):
    b, hn, _, ttot = qt.shape
    return pl.pallas_call(
        functools.partial(_mla_attn_kernel, kv_len=kv_len),
        out_shape=jax.ShapeDtypeStruct((b, n_qtiles * TM, hn * MLA_V), BF16),
        grid=(b, hn // MLA_HPS, n_qtiles),
        in_specs=[
            pl.BlockSpec((1, MLA_HPS, MLA_QK_PAD, TM), lambda bb, hp, qi: (bb, hp, 0, q_tile0 + qi)),
            pl.BlockSpec((1, MLA_HPS, kv_len, MLA_QK_PAD), lambda bb, hp, qi: (bb, hp, 0, 0)),
            pl.BlockSpec((1, MLA_HPS, MLA_VA, kv_len), lambda bb, hp, qi: (bb, hp, 0, 0)),
        ],
        out_specs=pl.BlockSpec((1, TM, MLA_HPS * MLA_V), lambda bb, hp, qi: (bb, qi, hp)),
        compiler_params=_cparams(("parallel", "parallel", "arbitrary")),
        name="mla_attention",
    )(qt, k, vt)


def _rope_tables(ctx, t):
    pos = np.arange(t)
    half = MLA_ROPE // 4
    inv = ROPE_BASE ** (-jnp.arange(half, dtype=F32) / half)
    tabs = []
    for p in (pos // GRID_W, pos % GRID_W):
        ang = jnp.asarray(p, F32)[:, None] * inv
        cos = jnp.concatenate([jnp.ones((ctx, half), F32), jnp.cos(ang)], axis=0)
        sin = jnp.concatenate([jnp.zeros((ctx, half), F32), jnp.sin(ang)], axis=0)
        tabs.append((cos, sin))
    (cr, sr), (cc, sc) = tabs
    rope_q = jnp.concatenate([cr, sr, cc, sc], axis=1).T
    ttot = ctx + t
    zeros = jnp.zeros((ttot, MLA_NOPE), F32)
    tail = jnp.zeros((ttot, LANES - MLA_QK), F32)
    cos_k = jnp.concatenate([zeros, cr, cr, cc, cc, tail], axis=1)
    sin_k = jnp.concatenate([zeros, -sr, sr, -sc, sc, tail], axis=1)
    rope_k = jnp.concatenate([cos_k, sin_k], axis=1)
    return rope_q, rope_k


def _mla_weights(w_uq, w_dkv, w_ukv):
    hn = MLA_HEADS
    q_rank = w_uq.shape[0]
    wq = w_uq.reshape(q_rank, hn, MLA_QK)
    wq = jnp.pad(wq, ((0, 0), (0, 0), (0, MLA_QK_PAD - MLA_QK)))
    wuqt = wq.reshape(q_rank, hn * MLA_QK_PAD).T.astype(BF16)
    d = w_dkv.shape[0]
    kpe = w_dkv[:, MLA_KV_RANK:]
    h8 = MLA_ROPE // 4
    swap = jnp.concatenate([kpe[:, h8:2 * h8], kpe[:, 0:h8], kpe[:, 3 * h8:4 * h8],
                            kpe[:, 2 * h8:3 * h8]], axis=1)

    def place(cols):
        return jnp.concatenate([jnp.zeros((d, MLA_NOPE), F32), cols,
                                jnp.zeros((d, LANES - MLA_QK), F32)], axis=1)

    wdkv = jnp.concatenate([w_dkv[:, :MLA_KV_RANK], place(kpe), place(swap)], axis=1).astype(BF16)
    wkv = w_ukv.reshape(MLA_KV_RANK, hn, MLA_NOPE + MLA_V)
    wuk = jnp.pad(wkv[:, :, :MLA_NOPE], ((0, 0), (0, 0), (0, MLA_QK_PAD - MLA_NOPE)))
    wuk = wuk.reshape(MLA_KV_RANK, hn * MLA_QK_PAD).astype(BF16)
    wuvt = wkv[:, :, MLA_NOPE:].reshape(MLA_KV_RANK, hn * MLA_V).T.astype(BF16)
    return wuqt, wdkv, wuk, wuvt


def kernel(x, c, ctx, c_ctx, ada_w, ada_b, norm_g, ffn_w_up, ffn_conv_w, ffn_conv_b, ffn_w_down,
           ev_w_in, ev_gate_b, ev_conv_w, ev_conv_b, ev_rpb, ev_ml_norm_g, ev_w_out,
           od_w_dq, od_q_norm_g, od_w_uq, od_w_dkv, od_kv_norm_g, od_w_ukv, od_w_o):
    b, t, d = x.shape
    nctx = ctx.shape[1]
    depth = ada_w.shape[0]
    ttot = nctx + t
    assert t % TM == 0 and nctx % TM == 0 and t % GRID_W == 0
    nctx_t = nctx // TM

    x_parts = (ctx, x)

    rows = -(-(b + 1) // 8) * 8
    cvec = jnp.zeros((rows, d), F32).at[:b].set(c).at[b].set(c_ctx)
    mod = _ada_call(cvec, ada_w, ada_b)

    rope_q, rope_k = _rope_tables(nctx, t)
    ml_scale = jnp.concatenate([jnp.full((1, ML_WIDTH), ML_HD ** -0.5, F32),
                                jnp.ones((1, ML_WIDTH), F32)], axis=1)

    for l in range(depth):
        ctx_out = l < depth - 1
        modsel = jnp.stack([jnp.broadcast_to(mod[l, b], (b, 6 * d)), mod[l, :b]], axis=1)
        modsel = modsel.reshape(b, 2, 1, 6 * d)
        ng = norm_g[l].reshape(4, 1, d)
        if l % 2 == 0:
            e = l // 2
            w_in = ev_w_in[e]
            n_main = 3 * NA_WIDTH + 4 * ML_WIDTH
            w_gate = jnp.pad(w_in[:, n_main:], ((0, 0), (0, LANES - ML_GATES))).astype(BF16)
            gate_b = jnp.pad(ev_gate_b[e], (0, LANES - ML_GATES)).reshape(1, LANES)
            naq, nak, nav, mlqk, mlv, mlo, gates = _even_in_call(
                x_parts, modsel, ng[0], w_in[:, :n_main].astype(BF16), w_gate, gate_b, nctx_t)
            na_x = _na_call(naq, nak, nav, _na_bias_table(ev_rpb[e], t // GRID_W), nctx, t)
            na_c = _na_ctx_call(naq, nak, nav, nctx)
            qk, kt = _ml_conv_call(mlqk, ev_conv_w[e], ev_conv_b[e].reshape(1, -1), ml_scale, nctx_t)
            gates_t = jnp.swapaxes(gates[:, :, :ML_GATES], 1, 2)
            hf, hb = _mlstm_call(qk, kt, mlv, gates, gates_t, nctx // ML_CHUNK)
            x_all = _even_out_call(x_parts, modsel, ng[1], na_c, na_x, hf, hb, mlo,
                                   ev_ml_norm_g[e].reshape(1, -1), ev_w_out[e].astype(BF16), nctx_t)
        else:
            (x_all,) = x_parts
            o = l // 2
            wuqt, wdkv, wuk, wuvt = _mla_weights(od_w_uq[o], od_w_dkv[o], od_w_ukv[o])
            qt, kk, vt = _mla_in_call(
                x_all, modsel, ng[0], od_w_dq[o].astype(BF16), od_q_norm_g[o].reshape(1, -1),
                wuqt, wdkv, od_kv_norm_g[o].reshape(1, -1), wuk, wuvt, rope_q, rope_k, nctx_t)
            att_x = _mla_attn_call(qt, kk, vt, nctx_t, t // TM, ttot)
            att_c = _mla_attn_call(qt, kk, vt, 0, nctx_t, nctx) if ctx_out else att_x
            x_all = _odd_out_call(x_all, modsel, ng[1], att_c, att_x, od_w_o[o].astype(BF16), nctx_t)
        x_all = _ffn_call(x_all, modsel, ng[2], ng[3], ffn_w_up[l].astype(BF16), ffn_conv_w[l],
                          ffn_conv_b[l].reshape(1, -1), ffn_w_down[l].astype(BF16), nctx_t,
                          latents_only=not ctx_out)
        x_parts = (x_all,)
    return x_all
```

```python
import functools
import math

import jax
import jax.numpy as jnp
import numpy as np
from jax import lax
from jax.experimental import pallas as pl
from jax.experimental.pallas import tpu as pltpu

F32 = jnp.float32
BF16 = jnp.bfloat16

EPS = 1e-6
NEG = -1e30
LOG2E = math.log2(math.e)

GRID_W = 64
NA_HEADS = 8
NA_HD = 64
NA_WIN_ROWS = 8
NA_WIN_COLS = 16
NA_WIDTH = NA_HEADS * NA_HD

ML_HEADS = 4
ML_HD = 128
ML_WIDTH = ML_HEADS * ML_HD
ML_CHUNK = 128
ML_GATES = 4 * ML_HEADS

MLA_HEADS = 16
MLA_NOPE = 64
MLA_ROPE = 32
MLA_V = 64
MLA_QK = MLA_NOPE + MLA_ROPE
MLA_Q_RANK = 256
MLA_KV_RANK = 128
ROPE_BASE = 10000.0

FFN_HIDDEN = 2816

LANES = 128
SUBLANES_BF16 = 16
TM = 256
VMEM_LIMIT = 48 * 1024 * 1024


def _cparams(sem):
    return pltpu.CompilerParams(dimension_semantics=sem, vmem_limit_bytes=VMEM_LIMIT)


def _resident(shape):
    nd = len(shape)
    return pl.BlockSpec(shape, lambda *_: (0,) * nd, pipeline_mode=pl.Buffered(1))


def _rms(xf, g):
    ms = jnp.mean(xf * xf, axis=-1, keepdims=True)
    return xf * lax.rsqrt(ms + EPS) * g


def _dot(a, b):
    return jnp.dot(a, b, preferred_element_type=F32)


def _dot_nt(a, b):
    return lax.dot_general(a, b, (((1,), (1,)), ((), ())), preferred_element_type=F32)


def _ada_kernel(c_ref, w_ref, b_ref, o_ref):
    c = c_ref[...]
    s = c * jax.nn.sigmoid(c)
    o_ref[0] = _dot(s.astype(BF16), w_ref[0].astype(BF16)) + b_ref[0]


def _ada_call(cvec, ada_w, ada_b):
    depth, d, n = ada_w.shape
    rows = cvec.shape[0]
    tn = 1024
    return pl.pallas_call(
        _ada_kernel,
        out_shape=jax.ShapeDtypeStruct((depth, rows, n), F32),
        grid=(depth, n // tn),
        in_specs=[
            pl.BlockSpec((rows, d), lambda l, j: (0, 0)),
            pl.BlockSpec((1, d, tn), lambda l, j: (l, 0, j)),
            pl.BlockSpec((1, 1, tn), lambda l, j: (l, 0, j)),
        ],
        out_specs=pl.BlockSpec((1, rows, tn), lambda l, j: (l, 0, j)),
        compiler_params=_cparams(("parallel", "parallel")),
        name="ada_mod",
    )(cvec, ada_w, ada_b.reshape(depth, 1, n))


BB = 4
BB_FFN = 2
BB_SCAN = 2


def _bb(b, want=BB):
    while b % want:
        want //= 2
    return want


def _x_spec(d, bb=1):
    return pl.BlockSpec((bb, TM, d), lambda b, i: (b, i, 0))


def _mod_spec(d6, nctx_t, bb=1):
    return pl.BlockSpec((bb, 1, 1, d6), lambda b, i: (b, jnp.where(i >= nctx_t, 1, 0), 0, 0))


def _modulated(x, m, g, d, which):
    o = 3 * d * which
    return _rms(x, g) * (1.0 + m[:, o + d:o + 2 * d]) + m[:, o:o + d]


def _rows(ref):
    return jnp.concatenate([ref[s] for s in range(ref.shape[0])], axis=0)


def _store_rows(ref, val):
    for s in range(ref.shape[0]):
        ref[s] = val[s * TM:(s + 1) * TM].astype(ref.dtype)


def _modulated_rows(x_ref, mod_ref, g, d, which):
    return jnp.concatenate([_modulated(x_ref[s], mod_ref[s, 0], g, d, which)
                            for s in range(x_ref.shape[0])], axis=0)


class _CtxLatRows:
    def __init__(self, c_ref, l_ref, nctx_t):
        self.c_ref, self.l_ref, self.nctx_t, self.shape = c_ref, l_ref, nctx_t, c_ref.shape

    def __getitem__(self, s):
        return jnp.where(pl.program_id(1) < self.nctx_t, self.c_ref[s], self.l_ref[s])


def _x_rows(refs, nctx_t):
    if nctx_t is None:
        return refs[0], refs[1:]
    return _CtxLatRows(refs[0], refs[1], nctx_t), refs[2:]


def _x_specs(x_parts, nctx_t, bb):
    d = x_parts[0].shape[-1]
    return [_x_spec(d, bb)] if len(x_parts) == 1 else list(_ctx_lat_specs(d, nctx_t, bb))


def _even_in_kernel(*refs, d, split_at):
    x_ref, refs = _x_rows(refs, split_at)
    (mod_ref, ng_ref, w_ref, wg_ref, gb_ref,
     naq_ref, nak_ref, nav_ref, mlqk_ref, mlv_ref, mlo_ref, g_ref) = refs
    hb = _modulated_rows(x_ref, mod_ref, ng_ref[...], d, 0).astype(BF16)

    def seg(lo, hi):
        return _dot(hb, w_ref[:, lo:hi])

    w = NA_WIDTH
    _store_rows(naq_ref, seg(0, w) * (NA_HD ** -0.5 * LOG2E))
    _store_rows(nak_ref, seg(w, 2 * w))
    _store_rows(nav_ref, seg(2 * w, 3 * w))
    o = 3 * w
    _store_rows(mlqk_ref, jnp.concatenate([seg(o, o + ML_WIDTH).astype(BF16),
                                           seg(o + ML_WIDTH, o + 2 * ML_WIDTH).astype(BF16)], axis=-1))
    _store_rows(mlv_ref, seg(o + 2 * ML_WIDTH, o + 3 * ML_WIDTH))
    _store_rows(mlo_ref, seg(o + 3 * ML_WIDTH, o + 4 * ML_WIDTH))
    _store_rows(g_ref, _dot(hb, wg_ref[...]) + gb_ref[...])


def _even_in_call(x_parts, modsel, ng, w_main, w_gate, gate_b, nctx_t):
    b, _, d = x_parts[0].shape
    ttot = sum(p.shape[1] for p in x_parts)
    nt = ttot // TM
    bb = _bb(b)
    n_main = w_main.shape[1]
    row = lambda c: _x_spec(c, bb)
    sds = lambda c, dt: jax.ShapeDtypeStruct((b, ttot, c), dt)
    return pl.pallas_call(
        functools.partial(_even_in_kernel, d=d, split_at=nctx_t if len(x_parts) == 2 else None),
        out_shape=(sds(NA_WIDTH, BF16), sds(NA_WIDTH, BF16), sds(NA_WIDTH, BF16),
                   sds(2 * ML_WIDTH, BF16), sds(ML_WIDTH, BF16), sds(ML_WIDTH, BF16),
                   sds(LANES, F32)),
        grid=(b // bb, nt),
        in_specs=[*_x_specs(x_parts, nctx_t, bb), _mod_spec(6 * d, nctx_t, bb), _resident((1, d)),
                  _resident((d, n_main)), _resident((d, LANES)), _resident((1, LANES))],
        out_specs=(row(NA_WIDTH), row(NA_WIDTH), row(NA_WIDTH), row(2 * ML_WIDTH),
                   row(ML_WIDTH), row(ML_WIDTH), row(LANES)),
        compiler_params=_cparams(("parallel", "parallel")),
        name="even_in_proj",
    )(*x_parts, modsel, ng, w_main, w_gate, gate_b)


def _pair_scores(qp, k_parts, bias_parts):
    lane = lax.broadcasted_iota(jnp.int32, qp.shape, 1)
    scores = []
    for hh in range(2):
        keep = (lane >= NA_HD) if hh else (lane < NA_HD)
        qh = jnp.where(keep, qp, jnp.zeros_like(qp))
        s = []
        for kk, bias in zip(k_parts, bias_parts):
            sp = _dot_nt(qh, kk)
            if bias is not None:
                sp = sp + bias[hh]
            s.append(sp)
        scores.append(s)
    return scores


def _pair_finish(scores, v_parts):
    mq = scores[0][0].shape[0]
    lane = lax.broadcasted_iota(jnp.int32, (mq, LANES), 1)
    v_aug = []
    for vv in v_parts:
        ones_col = jnp.where(lax.broadcasted_iota(jnp.int32, vv.shape, 1) == 0, 1.0, 0.0)
        v_aug.append(jnp.concatenate([vv, ones_col.astype(BF16)], axis=-1))
    outs = []
    for s in scores:
        m = functools.reduce(jnp.maximum, [sp.max(axis=-1, keepdims=True) for sp in s])
        o = jnp.zeros((mq, 2 * LANES), F32)
        for sp, vv in zip(s, v_aug):
            o = o + _dot(jnp.exp2(sp - m).astype(BF16), vv)
        outs.append(o[:, 0:LANES] / o[:, LANES:LANES + 1])
    return jnp.where(lane < NA_HD, outs[0], outs[1])


NA_RG = 4
NA_UB = NA_WIN_ROWS + NA_RG - 1


def _na_union_start(r0, rows):
    return jnp.clip(r0 - NA_WIN_ROWS // 2, 0, rows - NA_UB)


def _na_kernel(q_ref, k_ref, v_ref, bias_ref, o_ref, *, ctx, rows):
    us = _na_union_start(pl.program_id(1) * NA_RG, rows)
    start = pl.multiple_of(ctx + us * GRID_W, GRID_W)
    band = NA_UB * GRID_W
    pairs = [slice(p * LANES, (p + 1) * LANES) for p in range(NA_HEADS // 2)]

    def pair_scores(p):
        cs = pairs[p]
        kb = k_ref[0, pl.ds(start, band), cs]
        kc = k_ref[0, 0:ctx, cs]
        bias = (bias_ref[0, 2 * p], bias_ref[0, 2 * p + 1])
        return _pair_scores(q_ref[0, :, cs], (kb, kc), (bias, None))

    scores = [pair_scores(0)]
    for p, cs in enumerate(pairs):
        if p + 1 < len(pairs):
            scores.append(pair_scores(p + 1))
        vb = v_ref[0, pl.ds(start, band), cs]
        vc = v_ref[0, 0:ctx, cs]
        o_ref[0, :, cs] = _pair_finish(scores[p], (vb, vc)).astype(BF16)


def _na_call(naq, nak, nav, bias_tab, ctx, t):
    b, ttot, w = naq.shape
    rows = t // GRID_W
    groups = rows // NA_RG
    assert rows % NA_RG == 0 and groups >= 3 and ctx % (NA_RG * GRID_W) == 0
    q_off = ctx // (NA_RG * GRID_W)
    mq = NA_RG * GRID_W

    def bias_type(bb, g):
        return (jnp.where(g == 0, 0, jnp.where(g == groups - 1, 2, 1)), 0, 0, 0)

    whole = lambda bb, g: (bb, 0, 0)
    return pl.pallas_call(
        functools.partial(_na_kernel, ctx=ctx, rows=rows),
        out_shape=jax.ShapeDtypeStruct((b, t, w), BF16),
        grid=(b, groups),
        in_specs=[
            pl.BlockSpec((1, mq, w), lambda bb, g: (bb, q_off + g, 0)),
            pl.BlockSpec((1, ttot, w), whole),
            pl.BlockSpec((1, ttot, w), whole),
            pl.BlockSpec((1, NA_HEADS, mq, NA_UB * GRID_W), bias_type),
        ],
        out_specs=pl.BlockSpec((1, mq, w), lambda bb, g: (bb, g, 0)),
        compiler_params=_cparams(("parallel", "arbitrary")),
        name="na_attention",
    )(naq, nak, nav, bias_tab)


def _na_ctx_kernel(q_ref, k_ref, v_ref, o_ref):
    pairs = [slice(p * LANES, (p + 1) * LANES) for p in range(NA_HEADS // 2)]
    scores = [_pair_scores(q_ref[0, :, cs], (k_ref[0, :, cs],), (None,)) for cs in pairs]
    for sc, cs in zip(scores, pairs):
        o_ref[0, :, cs] = _pair_finish(sc, (v_ref[0, :, cs],)).astype(BF16)


def _na_ctx_call(naq, nak, nav, ctx):
    b, _, w = naq.shape
    spec = pl.BlockSpec((1, ctx, w), lambda bb: (bb, 0, 0))
    return pl.pallas_call(
        _na_ctx_kernel,
        out_shape=jax.ShapeDtypeStruct((b, ctx, w), BF16),
        grid=(b,),
        in_specs=[spec, spec, spec],
        out_specs=spec,
        compiler_params=_cparams(("parallel",)),
        name="na_ctx_attention",
    )(naq, nak, nav)


def _na_group_geometry(g, rows):
    r = g * NA_RG + np.arange(NA_RG)[:, None]
    us = int(np.clip(g * NA_RG - NA_WIN_ROWS // 2, 0, rows - NA_UB))
    kr = us + np.arange(NA_UB)[None, :]
    rs = np.clip(r - NA_WIN_ROWS // 2, 0, rows - NA_WIN_ROWS)
    valid = (kr >= rs) & (kr < rs + NA_WIN_ROWS)
    dr_idx = np.clip(kr - r + NA_WIN_ROWS - 1, 0, 2 * NA_WIN_ROWS - 2)
    return dr_idx, valid


def _na_bias_table(rpb, rows):
    groups = rows // NA_RG
    geo = [_na_group_geometry(g, rows) for g in range(groups)]
    for dr_g, valid_g in geo[1:groups - 1]:
        assert (valid_g == geo[1][1]).all() and (dr_g[valid_g] == geo[1][0][valid_g]).all()
    kw = NA_WIN_COLS
    col = jnp.arange(GRID_W)
    cs = jnp.clip(col - kw // 2, 0, GRID_W - kw)
    in_win = (col[None, :] >= cs[:, None]) & (col[None, :] < cs[:, None] + kw)
    pad = GRID_W - kw
    ext = jnp.pad(rpb.astype(F32) * LOG2E, ((0, 0), (0, 0), (pad, pad)), mode="edge")
    rpb_cols = jnp.stack([ext[:, :, GRID_W - 1 - qc:2 * GRID_W - 1 - qc] for qc in range(GRID_W)],
                         axis=2)
    rpb_cols = jnp.where(in_win[None, None], rpb_cols, NEG)
    rpb_t = rpb_cols.transpose(0, 2, 1, 3)

    def masked(n):
        return jnp.full((NA_HEADS, GRID_W, n, GRID_W), NEG, F32)

    types = []
    for g in (0, 1, groups - 1):
        dr_idx, valid = geo[g]
        slabs = []
        for i in range(NA_RG):
            js = np.nonzero(valid[i])[0]
            assert (np.diff(js) == 1).all() and (np.diff(dr_idx[i, js]) == 1).all()
            piece = rpb_t[:, :, dr_idx[i, js[0]]:dr_idx[i, js[-1]] + 1]
            slab = jnp.concatenate([masked(js[0]), piece, masked(NA_UB - 1 - js[-1])], axis=2)
            slabs.append(slab.reshape(NA_HEADS, GRID_W, NA_UB * GRID_W))
        types.append(jnp.stack(slabs, axis=1).reshape(NA_HEADS, NA_RG * GRID_W, NA_UB * GRID_W))
    return jnp.stack(types)


def _halo_specs(c, rows_h, ttot, tile0=0, bb=1):
    per = TM // rows_h
    last = ttot // rows_h - 1
    prev = pl.BlockSpec((bb, rows_h, c), lambda b, i: (b, jnp.maximum((i + tile0) * per - 1, 0), 0))
    nxt = pl.BlockSpec((bb, rows_h, c), lambda b, i: (b, jnp.minimum((i + tile0 + 1) * per, last), 0))
    return prev, nxt


def _halo_valid(nctx_t, nt, tile0=0):
    i = pl.program_id(1) + tile0
    pv = jnp.where((i == 0) | (i == nctx_t), 0.0, 1.0).astype(F32)
    nv = jnp.where((i == nctx_t - 1) | (i == nt - 1), 0.0, 1.0).astype(F32)
    return pv, nv


def _conv3(u_ref, h, cw, cb, lo, hi):
    return (cb
            + u_ref[h - 1:h - 1 + TM, lo:hi] * cw[0:1]
            + u_ref[h:h + TM, lo:hi] * cw[1:2]
            + u_ref[h + 1:h + 1 + TM, lo:hi] * cw[2:3])


def _ml_conv_kernel(x_ref, xp_ref, xn_ref, cw_ref, cb_ref, sc_ref, o_ref, kt_ref, u_ref, *, nctx_t, nt):
    pv, nv = _halo_valid(nctx_t, nt)
    h = SUBLANES_BF16
    c = x_ref.shape[-1]
    for s in range(x_ref.shape[0]):
        ys = []
        for j in range(c // LANES):
            cs = slice(j * LANES, (j + 1) * LANES)
            us_ref = u_ref.at[s, j]
            us_ref[0:h] = xp_ref[s, :, cs].astype(F32) * pv
            us_ref[h:h + TM] = x_ref[s, :, cs].astype(F32)
            us_ref[h + TM:h + TM + h] = xn_ref[s, :, cs].astype(F32) * nv
            y = _conv3(us_ref, h, cw_ref[:, cs], cb_ref[:, cs], 0, LANES)
            ys.append(y * jax.nn.sigmoid(y) * sc_ref[:, cs])
        y = jnp.concatenate(ys, axis=-1)
        o_ref[s] = y.astype(BF16)
        kt_ref[s] = y[:, ML_WIDTH:2 * ML_WIDTH].T.astype(BF16)


def _ml_conv_call(mlqk, conv_w, conv_b, scale, nctx_t):
    b, ttot, c = mlqk.shape
    nt = ttot // TM
    bb = _bb(b)
    prev, nxt = _halo_specs(c, SUBLANES_BF16, ttot, bb=bb)
    return pl.pallas_call(
        functools.partial(_ml_conv_kernel, nctx_t=nctx_t, nt=nt),
        out_shape=(jax.ShapeDtypeStruct((b, ttot, c), BF16),
                   jax.ShapeDtypeStruct((b, ML_WIDTH, ttot), BF16)),
        grid=(b // bb, nt),
        in_specs=[_x_spec(c, bb), prev, nxt, _resident((3, c)), _resident((1, c)), _resident((1, c))],
        out_specs=(_x_spec(c, bb), pl.BlockSpec((bb, ML_WIDTH, TM), lambda g, i: (g, 0, i))),
        scratch_shapes=[pltpu.VMEM((bb, c // LANES, TM + 2 * SUBLANES_BF16, LANES), F32)],
        compiler_params=_cparams(("parallel", "parallel")),
        name="mlstm_conv",
    )(mlqk, mlqk, mlqk, conv_w, conv_b, scale)


def _log_sigmoid(x):
    return jnp.minimum(x, 0.0) - jnp.log(1.0 + jnp.exp(-jnp.abs(x)))


def _split_dot(a, b, data_is_rhs):
    data = b if data_is_rhs else a
    hi = data.astype(BF16)
    lo = (data - hi.astype(F32)).astype(BF16)
    if data_is_rhs:
        return _dot(a, hi) + _dot(a, lo)
    return _dot(hi, b) + _dot(lo, b)


def _mlstm_kernel(qkf_ref, ktf_ref, vf_ref, gf_ref, gtf_ref, qkb_ref, ktb_ref, vb_ref, gb_ref, gtb_ref,
                  hf_ref, hb_ref, ct_ref, m_ref):
    L = ML_CHUNK

    @pl.when(pl.program_id(1) == 0)
    def _():
        ct_ref[...] = jnp.zeros_like(ct_ref)
        m_ref[...] = jnp.zeros_like(m_ref)

    ri = lax.broadcasted_iota(jnp.int32, (L, L), 0)
    ci = lax.broadcasted_iota(jnp.int32, (L, L), 1)
    tril = jnp.where(ri >= ci, 1.0, 0.0).astype(BF16)
    triu = jnp.where(ri <= ci, 1.0, 0.0).astype(BF16)
    ones_col = jnp.ones((L, LANES), BF16)

    dirs = (
        (qkf_ref, ktf_ref, vf_ref, gf_ref, gtf_ref, hf_ref, tril, triu, ri >= ci, L - 1, 0),
        (qkb_ref, ktb_ref, vb_ref, gb_ref, gtb_ref, hb_ref, triu, tril, ri <= ci, 0, 2 * ML_HEADS),
    )
    chains = []
    for e in range(qkf_ref.shape[0]):
        for dnum, (qk_ref, kt_ref, v_ref, g_ref, gt_ref, h_ref, tcol, trow, mask, tot_row, goff) in enumerate(dirs):
            gt = gt_ref[e]
            cum_col_all = _split_dot(tcol, _log_sigmoid(g_ref[e]), True)
            cum_row_all = _split_dot(_log_sigmoid(gt), trow, False)
            for hd in range(ML_HEADS):
                ic = goff + hd
                fc = goff + ML_HEADS + hd
                cs = slice(hd * ML_HD, (hd + 1) * ML_HD)
                j = (2 * e + dnum) * ML_HEADS + hd
                q = qk_ref[e, :, cs]
                k = qk_ref[e, :, ML_WIDTH + hd * ML_HD:ML_WIDTH + (hd + 1) * ML_HD]
                ct = ct_ref[j]
                cum_col = cum_col_all[:, fc:fc + 1]
                chains.append(dict(
                    j=j, e=e, h_ref=h_ref, cs=cs, kt=kt_ref[e, cs, :], ct=ct, mask=mask,
                    v_aug=jnp.concatenate([v_ref[e, :, cs], ones_col], axis=-1),
                    s=_dot_nt(q, k), hq=_dot(q, ct.astype(BF16)),
                    cum_col=cum_col, cum_row=cum_row_all[fc:fc + 1, :],
                    i_row=gt[ic:ic + 1, :], total=cum_col[tot_row:tot_row + 1, :]))
    for ch in chains:
        m_st = m_ref[ch["j"]][0:1, 0:1]
        cum_col, total = ch["cum_col"], ch["total"]
        e_row = ch["i_row"] - ch["cum_row"]
        base = jnp.where(ch["mask"], e_row, NEG)
        m_inter = cum_col + m_st
        m_t = jnp.maximum(m_inter, cum_col + base.max(axis=-1, keepdims=True))
        g_row = total + e_row
        m_new = jnp.maximum(total + m_st, g_row.max(axis=-1, keepdims=True))
        ch.update(base=base, m_inter=m_inter, m_t=m_t, g_row=g_row, m_new=m_new,
                  decay=jnp.exp(total + m_st - m_new))
    for ch in chains:
        ch["kw"] = (ch["kt"].astype(F32) * jnp.exp(ch["g_row"] - ch["m_new"])).astype(BF16)
    for ch in chains:
        ch["u"] = _dot(ch["kw"], ch["v_aug"])
    for ch in chains:
        ch["qk"] = (ch["s"] * jnp.exp(ch["base"] + (ch["cum_col"] - ch["m_t"]))).astype(BF16)
        ch["a"] = jnp.exp(ch["m_inter"] - ch["m_t"])
        ch["floor"] = jnp.exp(-ch["m_t"])
    for ch in chains:
        ch["p"] = _dot(ch["qk"], ch["v_aug"])
    for ch in chains:
        ct_ref[ch["j"]] = ch["decay"] * ch["ct"] + ch["u"]
        m_ref[ch["j"]] = jnp.broadcast_to(ch["m_new"], m_ref.shape[1:])
    for ch in chains:
        hfull = ch["a"] * ch["hq"] + ch["p"]
        den = hfull[:, ML_HD:2 * ML_HD]
        h_out = hfull[:, 0:ML_HD] / jnp.maximum(jnp.abs(den), ch["floor"])
        ch["h_ref"][ch["e"], :, ch["cs"]] = h_out.astype(BF16)


def _mlstm_call(qk, kt, v, gates, gates_t, nctx_c):
    b, ttot, _ = qk.shape
    nc = ttot // ML_CHUNK
    L = ML_CHUNK

    def fwd(bb, c):
        return (bb, c, 0)

    def bwd(bb, c):
        return (bb, jnp.where(c < nctx_c, nctx_c - 1 - c, nc - 1 + nctx_c - c), 0)

    def fwd_t(bb, c):
        return (bb, 0, c)

    def bwd_t(bb, c):
        return (bb, 0, jnp.where(c < nctx_c, nctx_c - 1 - c, nc - 1 + nctx_c - c))

    bb = _bb(b, BB_SCAN)
    ins = []
    for row_map, col_map in ((fwd, fwd_t), (bwd, bwd_t)):
        ins += [pl.BlockSpec((bb, L, 2 * ML_WIDTH), row_map),
                pl.BlockSpec((bb, ML_WIDTH, L), col_map),
                pl.BlockSpec((bb, L, ML_WIDTH), row_map),
                pl.BlockSpec((bb, L, LANES), row_map),
                pl.BlockSpec((bb, ML_GATES, L), col_map)]
    out_sds = jax.ShapeDtypeStruct((b, ttot, ML_WIDTH), BF16)
    return pl.pallas_call(
        _mlstm_kernel,
        out_shape=(out_sds, out_sds),
        grid=(b // bb, nc),
        in_specs=ins,
        out_specs=(pl.BlockSpec((bb, L, ML_WIDTH), fwd), pl.BlockSpec((bb, L, ML_WIDTH), bwd)),
        scratch_shapes=[pltpu.VMEM((bb * 2 * ML_HEADS, ML_HD, 2 * ML_HD), F32),
                        pltpu.VMEM((bb * 2 * ML_HEADS, 8, LANES), F32)],
        compiler_params=_cparams(("parallel", "arbitrary")),
        name="mlstm_scan",
    )(qk, kt, v, gates, gates_t, qk, kt, v, gates, gates_t)


def _out_tail(x_ref, mod_ref, ng_ref, y, o_ref, d):
    for s in range(x_ref.shape[0]):
        m = mod_ref[s, 0]
        o_ref[s] = x_ref[s] + m[:, 2 * d:3 * d] * _rms(y[s * TM:(s + 1) * TM], ng_ref[...])


def _ctx_lat_specs(c, nctx_t, bb=1):
    ctx_spec = pl.BlockSpec((bb, TM, c), lambda b, i: (b, jnp.minimum(i, nctx_t - 1), 0))
    lat_spec = pl.BlockSpec((bb, TM, c), lambda b, i: (b, jnp.maximum(i - nctx_t, 0), 0))
    return ctx_spec, lat_spec


def _ctx_or_lat(c_ref, l_ref, nctx_t):
    return jnp.where(pl.program_id(1) < nctx_t, _rows(c_ref), _rows(l_ref))


def _even_out_kernel(*refs, d, nctx_t, split_at):
    x_ref, refs = _x_rows(refs, split_at)
    mod_ref, ng_ref, nac_ref, nal_ref, hf_ref, hb_ref, op_ref, mlg_ref, w_ref, o_ref = refs
    hs = jax.nn.sigmoid(_rows(op_ref).astype(F32)) * (_rows(hf_ref).astype(F32) + _rows(hb_ref).astype(F32))
    parts = []
    for hd in range(ML_HEADS):
        seg = hs[:, hd * ML_HD:(hd + 1) * ML_HD]
        mu = jnp.mean(seg, axis=-1, keepdims=True)
        cen = seg - mu
        var = jnp.mean(cen * cen, axis=-1, keepdims=True)
        parts.append(cen * lax.rsqrt(var + EPS))
    ml = (jnp.concatenate(parts, axis=-1) * mlg_ref[...]).astype(BF16)
    na = _ctx_or_lat(nac_ref, nal_ref, nctx_t)
    y = _dot(na, w_ref[0:NA_WIDTH, :]) + _dot(ml, w_ref[NA_WIDTH:NA_WIDTH + ML_WIDTH, :])
    _out_tail(x_ref, mod_ref, ng_ref, y, o_ref, d)


def _even_out_call(x_parts, modsel, ng, na_c, na_x, hf, hb, mlo, mlg, w_out, nctx_t):
    b, _, d = x_parts[0].shape
    ttot = sum(p.shape[1] for p in x_parts)
    bb = _bb(b)
    row = lambda c: _x_spec(c, bb)
    return pl.pallas_call(
        functools.partial(_even_out_kernel, d=d, nctx_t=nctx_t,
                          split_at=nctx_t if len(x_parts) == 2 else None),
        out_shape=jax.ShapeDtypeStruct((b, ttot, d), F32),
        grid=(b // bb, ttot // TM),
        in_specs=[*_x_specs(x_parts, nctx_t, bb), _mod_spec(6 * d, nctx_t, bb), _resident((1, d)),
                  *_ctx_lat_specs(NA_WIDTH, nctx_t, bb), row(ML_WIDTH), row(ML_WIDTH), row(ML_WIDTH),
                  _resident((1, ML_WIDTH)), _resident(w_out.shape)],
        out_specs=_x_spec(d, bb),
        compiler_params=_cparams(("parallel", "parallel")),
        name="even_out_proj",
    )(*x_parts, modsel, ng, na_c, na_x, hf, hb, mlo, mlg, w_out)


def _odd_out_kernel(x_ref, mod_ref, ng_ref, ac_ref, al_ref, w_ref, o_ref, *, d, nctx_t):
    a = _ctx_or_lat(ac_ref, al_ref, nctx_t)
    _out_tail(x_ref, mod_ref, ng_ref, _dot(a, w_ref[...]), o_ref, d)


def _odd_out_call(x_all, modsel, ng, att_c, att_x, w_o, nctx_t):
    b, ttot, d = x_all.shape
    bb = _bb(b)
    return pl.pallas_call(
        functools.partial(_odd_out_kernel, d=d, nctx_t=nctx_t),
        out_shape=jax.ShapeDtypeStruct((b, ttot, d), F32),
        grid=(b // bb, ttot // TM),
        in_specs=[_x_spec(d, bb), _mod_spec(6 * d, nctx_t, bb), _resident((1, d)),
                  *_ctx_lat_specs(att_x.shape[-1], nctx_t, bb), _resident(w_o.shape)],
        out_specs=_x_spec(d, bb),
        compiler_params=_cparams(("parallel", "parallel")),
        name="odd_out_proj",
    )(x_all, modsel, ng, att_c, att_x, w_o)


FFN_CK = 256
FFN_CPAD = 8


def _ffn_kernel(x_ref, xp_ref, xn_ref, mod_ref, ng_in_ref, ng_out_ref, wup_ref, cw_ref, cb_ref,
                wdn_ref, o_ref, h_ref, u_ref, *, d, nctx_t, nt, tile0):
    pv, nv = _halo_valid(nctx_t, nt, tile0)
    g_in = ng_in_ref[...]
    hh = SUBLANES_BF16
    nb = x_ref.shape[0]
    assert 2 * nb <= hh
    rid = lax.broadcasted_iota(jnp.int32, (hh, d), 0)
    halo = jnp.zeros((hh, d), F32)
    for e in range(nb):
        m = mod_ref[e, 0]
        hp = _modulated(xp_ref[e, hh - 8:hh], m, g_in, d, 1)[7:8] * pv
        hn = _modulated(xn_ref[e, 0:8], m, g_in, d, 1)[0:1] * nv
        halo = jnp.where(rid == 2 * e, hp, jnp.where(rid == 2 * e + 1, hn, halo))
        h_ref[e * TM:(e + 1) * TM] = _modulated(x_ref[e], m, g_in, d, 1).astype(BF16)
    h_ref[nb * TM:nb * TM + hh] = halo.astype(BF16)
    hcat = h_ref[...]
    ck = FFN_CK
    nslab = ck // LANES
    nchunks = FFN_HIDDEN // ck
    r0 = FFN_CPAD

    def store_u(buf, s, u):
        for e in range(nb):
            u_ref[buf, e, s, r0:r0 + TM, :] = u[e * TM:(e + 1) * TM]
            u_ref[buf, e, s, r0 - 1:r0, :] = u[nb * TM + 2 * e:nb * TM + 2 * e + 1]
            u_ref[buf, e, s, r0 + TM:r0 + TM + 1, :] = u[nb * TM + 2 * e + 1:nb * TM + 2 * e + 2]

    def conv_slab(buf, e, s, col):
        cw = cw_ref[:, col:col + LANES]
        return (cb_ref[:, col:col + LANES]
                + u_ref[buf, e, s, r0 - 1:r0 - 1 + TM, :] * cw[0:1]
                + u_ref[buf, e, s, r0:r0 + TM, :] * cw[1:2]
                + u_ref[buf, e, s, r0 + 1:r0 + 1 + TM, :] * cw[2:3])

    y = jnp.zeros((nb * TM, d), F32)
    for c in range(nchunks + 1):
        if c < nchunks:
            lo = c * ck
            glo = FFN_HIDDEN + lo
            ua = _dot(hcat, wup_ref[:, lo:lo + ck])
            ug = _dot(hcat, wup_ref[:, glo:glo + ck])
            for s in range(nslab):
                store_u(c % 2, s, ua[:, s * LANES:(s + 1) * LANES])
                store_u(c % 2, nslab + s, ug[:, s * LANES:(s + 1) * LANES])
        if c >= 1:
            lo = (c - 1) * ck
            glo = FFN_HIDDEN + lo
            rows = []
            for e in range(nb):
                acts = []
                for s in range(nslab):
                    a = conv_slab((c - 1) % 2, e, s, lo + s * LANES)
                    g = conv_slab((c - 1) % 2, e, nslab + s, glo + s * LANES)
                    acts.append((a * (g * jax.nn.sigmoid(g))).astype(BF16))
                rows.append(jnp.concatenate(acts, axis=-1))
            y = _dot(jnp.concatenate(rows, axis=0), wdn_ref[lo:lo + ck, :]) + y
    for e in range(nb):
        o_ref[e] = x_ref[e] + mod_ref[e, 0][:, 5 * d:6 * d] * _rms(y[e * TM:(e + 1) * TM], ng_out_ref[...])


def _ffn_call(x_all, modsel, ng_in, ng_out, w_up, conv_w, conv_b, w_down, nctx_t, latents_only):
    b, ttot, d = x_all.shape
    nt = ttot // TM
    hh = SUBLANES_BF16
    tile0 = nctx_t if latents_only else 0
    bb = _bb(b, BB_FFN)
    prev, nxt = _halo_specs(d, hh, ttot, tile0, bb)
    return pl.pallas_call(
        functools.partial(_ffn_kernel, d=d, nctx_t=nctx_t, nt=nt, tile0=tile0),
        out_shape=jax.ShapeDtypeStruct((b, ttot - tile0 * TM, d), F32),
        grid=(b // bb, nt - tile0),
        in_specs=[pl.BlockSpec((bb, TM, d), lambda g, i: (g, i + tile0, 0)), prev, nxt,
                  _mod_spec(6 * d, nctx_t - tile0, bb), _resident((1, d)),
                  _resident((1, d)), _resident(w_up.shape), _resident(conv_w.shape),
                  _resident(conv_b.shape), _resident(w_down.shape)],
        out_specs=_x_spec(d, bb),
        scratch_shapes=[pltpu.VMEM((bb * TM + hh, d), BF16),
                        pltpu.VMEM((2, bb, 2 * FFN_CK // LANES, TM + 2 * FFN_CPAD, LANES), F32)],
        compiler_params=_cparams(("parallel", "parallel")),
        name="conv_ffn",
    )(x_all, x_all, x_all, modsel, ng_in, ng_out, w_up, conv_w, conv_b, w_down)


MLA_QK_PAD = LANES
MLA_VA = MLA_V + SUBLANES_BF16


def _mla_in_kernel(x_ref, mod_ref, ng_ref, wdq_ref, qg_ref, wuqt_ref, wdkv_ref, kvg_ref,
                   wuk_ref, wuvt_ref, ropeq_ref, ropek_ref, qt_ref, k_ref, vt_ref, *, d):
    nb = x_ref.shape[0]
    hb = _modulated_rows(x_ref, mod_ref, ng_ref[...], d, 0).astype(BF16)
    cq_pre = _dot(hb, wdq_ref[...])
    ckv = _dot(hb, wdkv_ref[...])
    cq = _rms(cq_pre, qg_ref[...]).astype(BF16)
    cn = _rms(ckv[:, 0:MLA_KV_RANK], kvg_ref[...]).astype(BF16)
    qt_all = _dot_nt(wuqt_ref[...], cq)
    k_all = _dot(cn, wuk_ref[...])
    vt_all = _dot_nt(wuvt_ref[...], cn)
    rq = jnp.concatenate([ropeq_ref[...]] * nb, axis=1)
    cos_r, sin_r, cos_c, sin_c = rq[0:8], rq[8:16], rq[16:24], rq[24:32]
    scale = MLA_QK ** -0.5 * LOG2E
    for hd in range(MLA_HEADS):
        base = hd * MLA_QK_PAD
        nope = qt_all[base:base + MLA_NOPE]
        x1r = qt_all[base + 64:base + 72]
        x2r = qt_all[base + 72:base + 80]
        x1c = qt_all[base + 80:base + 88]
        x2c = qt_all[base + 88:base + 96]
        pad = qt_all[base + 96:base + 128]
        roped = jnp.concatenate([
            nope,
            x1r * cos_r - x2r * sin_r, x1r * sin_r + x2r * cos_r,
            x1c * cos_c - x2c * sin_c, x1c * sin_c + x2c * cos_c,
            pad], axis=0)
        roped = (roped * scale).astype(BF16)
        for e in range(nb):
            qt_ref[e, hd] = roped[:, e * TM:(e + 1) * TM]
    rk = jnp.concatenate([ropek_ref[...]] * nb, axis=0)
    kpe = ckv[:, LANES:2 * LANES] * rk[:, 0:LANES] + ckv[:, 2 * LANES:3 * LANES] * rk[:, LANES:2 * LANES]
    ones_rows = jnp.where(lax.broadcasted_iota(jnp.int32, (MLA_VA - MLA_V, nb * TM), 0) == 0, 1.0, 0.0)
    for hd in range(MLA_HEADS):
        kh = (k_all[:, hd * MLA_QK_PAD:(hd + 1) * MLA_QK_PAD] + kpe).astype(BF16)
        vth = jnp.concatenate([vt_all[hd * MLA_V:(hd + 1) * MLA_V], ones_rows], axis=0).astype(BF16)
        for e in range(nb):
            k_ref[e, hd] = kh[e * TM:(e + 1) * TM]
            vt_ref[e, hd] = vth[:, e * TM:(e + 1) * TM]


def _mla_in_call(x_all, modsel, ng, wdq, qg, wuqt, wdkv, kvg, wuk, wuvt, rope_q, rope_k, nctx_t):
    b, ttot, d = x_all.shape
    nt = ttot // TM
    hn = MLA_HEADS
    bb = _bb(b)
    return pl.pallas_call(
        functools.partial(_mla_in_kernel, d=d),
        out_shape=(jax.ShapeDtypeStruct((b, hn, MLA_QK_PAD, ttot), BF16),
                   jax.ShapeDtypeStruct((b, hn, ttot, MLA_QK_PAD), BF16),
                   jax.ShapeDtypeStruct((b, hn, MLA_VA, ttot), BF16)),
        grid=(b // bb, nt),
        in_specs=[_x_spec(d, bb), _mod_spec(6 * d, nctx_t, bb), _resident((1, d)),
                  _resident(wdq.shape), _resident(qg.shape), _resident(wuqt.shape),
                  _resident(wdkv.shape), _resident(kvg.shape), _resident(wuk.shape),
                  _resident(wuvt.shape),
                  pl.BlockSpec((32, TM), lambda g, i: (0, i)),
                  pl.BlockSpec((TM, 2 * LANES), lambda g, i: (i, 0))],
        out_specs=(pl.BlockSpec((bb, hn, MLA_QK_PAD, TM), lambda g, i: (g, 0, 0, i)),
                   pl.BlockSpec((bb, hn, TM, MLA_QK_PAD), lambda g, i: (g, 0, i, 0)),
                   pl.BlockSpec((bb, hn, MLA_VA, TM), lambda g, i: (g, 0, 0, i))),
        compiler_params=_cparams(("parallel", "parallel")),
        name="mla_in_proj",
    )(x_all, modsel, ng, wdq, qg, wuqt, wdkv, kvg, wuk, wuvt, rope_q, rope_k)


MLA_KC = 256
MLA_LOOKAHEAD = 8
MLA_HPS = 8


def _mla_attn_kernel(qt_ref, k_ref, vt_ref, o_ref, *, kv_len):
    nkc = -(-kv_len // MLA_KC)

    def chunk(c):
        return slice(c * MLA_KC, min((c + 1) * MLA_KC, kv_len))

    items = [(hh, c) for hh in range(MLA_HPS) for c in range(nkc)]
    scores, ms, accs, outs = [], [], [], []
    for i in range(len(items) + MLA_LOOKAHEAD):
        if i < len(items):
            hh, c = items[i]
            scores.append(_dot(k_ref[0, hh, chunk(c), :], qt_ref[0, hh]))
        if i >= MLA_LOOKAHEAD:
            hh, c = items[i - MLA_LOOKAHEAD]
            s = scores[i - MLA_LOOKAHEAD]
            m_c = s.max(axis=0, keepdims=True)
            ms.append(m_c)
            accs.append(_dot(vt_ref[0, hh, :, chunk(c)], jnp.exp2(s - m_c).astype(BF16)))
            if c == nkc - 1:
                m = functools.reduce(jnp.maximum, ms)
                acc = jnp.zeros((MLA_VA, TM), F32)
                for m_c, a_c in zip(ms, accs):
                    acc = acc + jnp.exp2(m_c - m) * a_c
                outs.append(acc[0:MLA_V] / acc[MLA_V:MLA_V + 1])
                ms, accs = [], []
    o_ref[0] = jnp.concatenate(outs, axis=0).T.astype(BF16)


def _mla_attn_call(qt, k, vt, q_tile0, n_qtiles, kv_len):
    b, hn, _, ttot = qt.shape
    return pl.pallas_call(
        functools.partial(_mla_attn_kernel, kv_len=kv_len),
        out_shape=jax.ShapeDtypeStruct((b, n_qtiles * TM, hn * MLA_V), BF16),
        grid=(b, hn // MLA_HPS, n_qtiles),
        in_specs=[
            pl.BlockSpec((1, MLA_HPS, MLA_QK_PAD, TM), lambda bb, hp, qi: (bb, hp, 0, q_tile0 + qi)),
            pl.BlockSpec((1, MLA_HPS, kv_len, MLA_QK_PAD), lambda bb, hp, qi: (bb, hp, 0, 0)),
            pl.BlockSpec((1, MLA_HPS, MLA_VA, kv_len), lambda bb, hp, qi: (bb, hp, 0, 0)),
        ],
        out_specs=pl.BlockSpec((1, TM, MLA_HPS * MLA_V), lambda bb, hp, qi: (bb, qi, hp)),
        compiler_params=_cparams(("parallel", "parallel", "arbitrary")),
        name="mla_attention",
    )(qt, k, vt)


def _rope_tables(ctx, t):
    pos = np.arange(t)
    half = MLA_ROPE // 4
    inv = ROPE_BASE ** (-jnp.arange(half, dtype=F32) / half)
    tabs = []
    for p in (pos // GRID_W, pos % GRID_W):
        ang = jnp.asarray(p, F32)[:, None] * inv
        cos = jnp.concatenate([jnp.ones((ctx, half), F32), jnp.cos(ang)], axis=0)
        sin = jnp.concatenate([jnp.zeros((ctx, half), F32), jnp.sin(ang)], axis=0)
        tabs.append((cos, sin))
    (cr, sr), (cc, sc) = tabs
    rope_q = jnp.concatenate([cr, sr, cc, sc], axis=1).T
    ttot = ctx + t
    zeros = jnp.zeros((ttot, MLA_NOPE), F32)
    tail = jnp.zeros((ttot, LANES - MLA_QK), F32)
    cos_k = jnp.concatenate([zeros, cr, cr, cc, cc, tail], axis=1)
    sin_k = jnp.concatenate([zeros, -sr, sr, -sc, sc, tail], axis=1)
    rope_k = jnp.concatenate([cos_k, sin_k], axis=1)
    return rope_q, rope_k


def _mla_weights(w_uq, w_dkv, w_ukv):
    hn = MLA_HEADS
    q_rank = w_uq.shape[0]
    wq = w_uq.reshape(q_rank, hn, MLA_QK)
    wq = jnp.pad(wq, ((0, 0), (0, 0), (0, MLA_QK_PAD - MLA_QK)))
    wuqt = wq.reshape(q_rank, hn * MLA_QK_PAD).T.astype(BF16)
    d = w_dkv.shape[0]
    kpe = w_dkv[:, MLA_KV_RANK:]
    h8 = MLA_ROPE // 4
    swap = jnp.concatenate([kpe[:, h8:2 * h8], kpe[:, 0:h8], kpe[:, 3 * h8:4 * h8],
                            kpe[:, 2 * h8:3 * h8]], axis=1)

    def place(cols):
        return jnp.concatenate([jnp.zeros((d, MLA_NOPE), F32), cols,
                                jnp.zeros((d, LANES - MLA_QK), F32)], axis=1)

    wdkv = jnp.concatenate([w_dkv[:, :MLA_KV_RANK], place(kpe), place(swap)], axis=1).astype(BF16)
    wkv = w_ukv.reshape(MLA_KV_RANK, hn, MLA_NOPE + MLA_V)
    wuk = jnp.pad(wkv[:, :, :MLA_NOPE], ((0, 0), (0, 0), (0, MLA_QK_PAD - MLA_NOPE)))
    wuk = wuk.reshape(MLA_KV_RANK, hn * MLA_QK_PAD).astype(BF16)
    wuvt = wkv[:, :, MLA_NOPE:].reshape(MLA_KV_RANK, hn * MLA_V).T.astype(BF16)
    return wuqt, wdkv, wuk, wuvt


def kernel(x, c, ctx, c_ctx, ada_w, ada_b, norm_g, ffn_w_up, ffn_conv_w, ffn_conv_b, ffn_w_down,
           ev_w_in, ev_gate_b, ev_conv_w, ev_conv_b, ev_rpb, ev_ml_norm_g, ev_w_out,
           od_w_dq, od_q_norm_g, od_w_uq, od_w_dkv, od_kv_norm_g, od_w_ukv, od_w_o):
    b, t, d = x.shape
    nctx = ctx.shape[1]
    depth = ada_w.shape[0]
    ttot = nctx + t
    assert t % TM == 0 and nctx % TM == 0 and t % GRID_W == 0
    nctx_t = nctx // TM

    x_parts = (ctx, x)

    rows = -(-(b + 1) // 8) * 8
    cvec = jnp.zeros((rows, d), F32).at[:b].set(c).at[b].set(c_ctx)
    mod = _ada_call(cvec, ada_w, ada_b)

    rope_q, rope_k = _rope_tables(nctx, t)
    ml_scale = jnp.concatenate([jnp.full((1, ML_WIDTH), ML_HD ** -0.5, F32),
                                jnp.ones((1, ML_WIDTH), F32)], axis=1)

    for l in range(depth):
        ctx_out = l < depth - 1
        modsel = jnp.stack([jnp.broadcast_to(mod[l, b], (b, 6 * d)), mod[l, :b]], axis=1)
        modsel = modsel.reshape(b, 2, 1, 6 * d)
        ng = norm_g[l].reshape(4, 1, d)
        if l % 2 == 0:
            e = l // 2
            w_in = ev_w_in[e]
            n_main = 3 * NA_WIDTH + 4 * ML_WIDTH
            w_gate = jnp.pad(w_in[:, n_main:], ((0, 0), (0, LANES - ML_GATES))).astype(BF16)
            gate_b = jnp.pad(ev_gate_b[e], (0, LANES - ML_GATES)).reshape(1, LANES)
            naq, nak, nav, mlqk, mlv, mlo, gates = _even_in_call(
                x_parts, modsel, ng[0], w_in[:, :n_main].astype(BF16), w_gate, gate_b, nctx_t)
            na_x = _na_call(naq, nak, nav, _na_bias_table(ev_rpb[e], t // GRID_W), nctx, t)
            na_c = _na_ctx_call(naq, nak, nav, nctx)
            qk, kt = _ml_conv_call(mlqk, ev_conv_w[e], ev_conv_b[e].reshape(1, -1), ml_scale, nctx_t)
            gates_t = jnp.swapaxes(gates[:, :, :ML_GATES], 1, 2)
            hf, hb = _mlstm_call(qk, kt, mlv, gates, gates_t, nctx // ML_CHUNK)
            x_all = _even_out_call(x_parts, modsel, ng[1], na_c, na_x, hf, hb, mlo,
                                   ev_ml_norm_g[e].reshape(1, -1), ev_w_out[e].astype(BF16), nctx_t)
        else:
            (x_all,) = x_parts
            o = l // 2
            wuqt, wdkv, wuk, wuvt = _mla_weights(od_w_uq[o], od_w_dkv[o], od_w_ukv[o])
            qt, kk, vt = _mla_in_call(
                x_all, modsel, ng[0], od_w_dq[o].astype(BF16), od_q_norm_g[o].reshape(1, -1),
                wuqt, wdkv, od_kv_norm_g[o].reshape(1, -1), wuk, wuvt, rope_q, rope_k, nctx_t)
            att_x = _mla_attn_call(qt, kk, vt, nctx_t, t // TM, ttot)
            att_c = _mla_attn_call(qt, kk, vt, 0, nctx_t, nctx) if ctx_out else att_x
            x_all = _odd_out_call(x_all, modsel, ng[1], att_c, att_x, od_w_o[o].astype(BF16), nctx_t)
        x_all = _ffn_call(x_all, modsel, ng[2], ng[3], ffn_w_up[l].astype(BF16), ffn_conv_w[l],
                          ffn_conv_b[l].reshape(1, -1), ffn_w_down[l].astype(BF16), nctx_t,
                          latents_only=not ctx_out)
        x_parts = (x_all,)
    return x_all
```
